```python
import math
import jax, jax.numpy as jnp
from jax import lax
import numpy as np

D_MODEL = 1024
BATCH = 32
SEQ = 2048
DEPTH = 4

HEAD_DIM = 64
N_BRANCHES = 4
BRANCH_WIDTH = D_MODEL // 4

A_HEADS = BRANCH_WIDTH // HEAD_DIM
A_PATTERNS = ((128, 1), (512, 4), (2048, 16))
A_BLOCK = 128
B_HEADS = BRANCH_WIDTH // HEAD_DIM
B_LATENT = D_MODEL // 16
IDX_HEADS = 8
IDX_DIM = 32
TOPK_MAX = 256
TOPK_DIV = 4
Q_BLOCK = 128
C_HEADS = 4
C_KEY_DIM = 32
C_VAL_DIM = BRANCH_WIDTH // C_HEADS
C_GATE_RANK = 16
C_GATE_TAU = 16.0
C_CHUNK = 64
D_HEADS = 4
D_HEAD_SIZE = BRANCH_WIDTH // D_HEADS
D_DECAY_RANK = 16
D_ICLR_RANK = 16
D_GATE_RANK = 32
D_GN_EPS = 64e-5
D_IN_WIDTH = 3 * BRANCH_WIDTH + D_DECAY_RANK + D_ICLR_RANK + D_GATE_RANK
RPB_BUCKETS = 32
RPB_MAX_DIST = 2048
RPB_HEADS = A_HEADS + B_HEADS
MOE_GROUPS = 4
MOE_PER_GROUP = 8
MOE_EXPERTS = MOE_GROUPS * MOE_PER_GROUP
MOE_TOPK = 2
MOE_HIDDEN = 256
DN_ALPHA = (2 * DEPTH) ** 0.25
DN_BETA = (8 * DEPTH) ** -0.25
LN_EPS = 1e-5

IN_SPLITS = (
    ("a_q", A_HEADS * HEAD_DIM), ("a_k", A_HEADS * HEAD_DIM), ("a_v", A_HEADS * HEAD_DIM),
    ("b_q", B_HEADS * B_LATENT), ("b_ckv", B_LATENT), ("b_iq", IDX_HEADS * IDX_DIM),
    ("b_ik", IDX_DIM), ("b_iw", IDX_HEADS),
    ("c_q", C_HEADS * C_KEY_DIM), ("c_k", C_HEADS * C_KEY_DIM), ("c_v", C_HEADS * C_VAL_DIM),
    ("c_a", C_GATE_RANK), ("c_g", BRANCH_WIDTH),
    ("d", D_IN_WIDTH),
    ("gate", N_BRANCHES * D_MODEL),
)
IN_WIDTH = sum(w for _, w in IN_SPLITS)

kernel_name = "hybrid_gated_four_mixer_hmoe_deepnorm"


def _split_in(h):
    out = {}
    off = 0
    for name, w in IN_SPLITS:
        out[name] = h[..., off:off + w]
        off += w
    return out


def _layer_norm(x, g, b):
    xf = x.astype(jnp.float32)
    mu = jnp.mean(xf, -1, keepdims=True)
    var = jnp.mean(jnp.square(xf - mu), -1, keepdims=True)
    return ((xf - mu) * lax.rsqrt(var + LN_EPS) * g + b).astype(x.dtype)


def _rms_norm(x, g, eps=1e-6):
    xf = x.astype(jnp.float32)
    return (xf * lax.rsqrt(jnp.mean(xf * xf, -1, keepdims=True) + eps) * g).astype(x.dtype)


def _t5_bucket(dist):
    exact = RPB_BUCKETS // 2
    d = jnp.maximum(dist, 0)
    df = jnp.maximum(d, 1).astype(jnp.float32)
    large = exact + (jnp.log(df / exact) / math.log(RPB_MAX_DIST / exact)
                     * (RPB_BUCKETS - exact)).astype(jnp.int32)
    large = jnp.minimum(large, RPB_BUCKETS - 1)
    return jnp.where(d < exact, d, large)


def _dilated_pattern(q, k, v, bias_tab, window, dilation):
    bsz, seq, nh, dh = q.shape
    lr = seq // dilation
    span = window // dilation
    nb = -(-lr // A_BLOCK)
    lp = nb * A_BLOCK
    bd = bsz * dilation

    def regroup(z):
        z = z.reshape(bsz, lr, dilation, nh, dh).transpose(0, 2, 1, 3, 4)
        return z.reshape(bd, lr, nh, dh)

    qr, kr, vr = regroup(q), regroup(k), regroup(v)
    qb = jnp.pad(qr, ((0, 0), (0, lp - lr), (0, 0), (0, 0))).reshape(bd, nb, A_BLOCK, nh, dh)

    def band(z):
        zp = jnp.pad(z, ((0, 0), (A_BLOCK, lp - lr), (0, 0), (0, 0)))
        prev = zp[:, :lp].reshape(bd, nb, A_BLOCK, nh, dh)
        cur = zp[:, A_BLOCK:].reshape(bd, nb, A_BLOCK, nh, dh)
        return jnp.concatenate([prev, cur], axis=2)

    kb, vb = band(kr), band(vr)
    qi = jnp.arange(A_BLOCK)[:, None]
    kj = jnp.arange(2 * A_BLOCK)[None, :]
    rdist = qi + A_BLOCK - kj
    in_band = (rdist >= 0) & (rdist <= span)
    bias = jnp.transpose(bias_tab[_t5_bucket(rdist * dilation)], (2, 0, 1))
    kpos = jnp.arange(nb)[:, None] * A_BLOCK - A_BLOCK + kj
    mask = in_band[None] & (kpos >= 0)[:, None, :]

    s = jnp.einsum('bnqhd,bnkhd->bnhqk', qb, kb).astype(jnp.float32) * dh ** -0.5 + bias
    s = jnp.where(mask[None, :, None], s, -jnp.inf)
    m = jnp.max(s, -1, keepdims=True)
    p = jnp.exp(s - m)
    den = jnp.sum(p, -1)
    o = jnp.einsum('bnhqk,bnkhd->bnqhd', p, vb.astype(jnp.float32))
    o = o / jnp.swapaxes(den, 2, 3)[..., None]
    lse = jnp.swapaxes(m[..., 0] + jnp.log(den), 2, 3)
    o = o.reshape(bd, lp, nh, dh)
    lse = lse.reshape(bd, lp, nh)

    def ungroup(z):
        z = z[:, :lr].reshape((bsz, dilation, lr) + z.shape[2:])
        z = jnp.moveaxis(z, 1, 2)
        return z.reshape((bsz, seq) + z.shape[3:])

    return ungroup(o), ungroup(lse)


def _mixer_a(q, k, v, bias_tab):
    bsz, seq, _ = q.shape
    shp = (bsz, seq, A_HEADS, HEAD_DIM)
    q, k, v = q.reshape(shp), k.reshape(shp), v.reshape(shp)
    outs, lses = [], []
    for window, dilation in A_PATTERNS:
        o, lse = _dilated_pattern(q, k, v, bias_tab, window, dilation)
        outs.append(o)
        lses.append(lse)
    wts = jax.nn.softmax(jnp.stack(lses), axis=0)
    o = jnp.sum(jnp.stack(outs) * wts[..., None], axis=0)
    return o.reshape(bsz, seq, A_HEADS * HEAD_DIM).astype(q.dtype)


def _mixer_b(q, ckv, iq, ik, iw, kv_gain, w_uv, bias_tab):
    bsz, seq, _ = q.shape
    keep = min(TOPK_MAX, seq // TOPK_DIV)
    nblk = seq // Q_BLOCK
    q = q.reshape(bsz, seq, B_HEADS, B_LATENT)
    ckv = _rms_norm(ckv, kv_gain)
    iq = iq.reshape(bsz, seq, IDX_HEADS, IDX_DIM)

    def to_blocks(z):
        return jnp.moveaxis(z.reshape((bsz, nblk, Q_BLOCK) + z.shape[2:]), 1, 0)

    def one_block(args):
        bi, qb, iqb, iwb = args
        tpos = bi * Q_BLOCK + jnp.arange(Q_BLOCK)
        rel = jax.nn.relu(jnp.einsum('bqhd,bsd->bqhs', iqb, ik).astype(jnp.float32) * IDX_DIM ** -0.5)
        score = jnp.einsum('bqhs,bqh->bqs', rel, iwb.astype(jnp.float32)) * IDX_HEADS ** -0.5
        admissible = jnp.arange(seq)[None, :] <= tpos[:, None]
        score = jnp.where(admissible[None], score, -jnp.inf)
        _, idx = lax.top_k(score, keep)
        sel = jax.vmap(lambda c, i: c[i])(ckv, idx)
        logits = jnp.einsum('bqhc,bqkc->bqhk', qb, sel).astype(jnp.float32) * B_LATENT ** -0.5
        dist = tpos[None, :, None] - idx
        bias = jnp.moveaxis(bias_tab[_t5_bucket(dist)], -1, 2)
        logits = jnp.where((dist >= 0)[:, :, None, :], logits + bias, -jnp.inf)
        p = jax.nn.softmax(logits, axis=-1)
        return jnp.einsum('bqhk,bqkc->bqhc', p.astype(sel.dtype), sel)

    ob = lax.map(one_block, (jnp.arange(nblk), to_blocks(q), to_blocks(iq), to_blocks(iw)))
    o = jnp.moveaxis(ob, 0, 1).reshape(bsz, seq, B_HEADS, B_LATENT)
    o = jnp.einsum('bshc,hcd->bshd', o, w_uv)
    return o.reshape(bsz, seq, B_HEADS * HEAD_DIM).astype(q.dtype)


def _mixer_c(q, k, v, a_low, og, a_up, a_bias, norm_gain):
    f32 = jnp.float32
    bsz, seq, _ = q.shape
    nc = seq // C_CHUNK
    log_a = jax.nn.log_sigmoid((a_low @ a_up + a_bias).astype(f32)) / C_GATE_TAU

    def chunks(z, dim):
        return z.astype(f32).reshape(bsz, nc, C_CHUNK, C_HEADS, dim)

    qc = chunks(q, C_KEY_DIM) * C_KEY_DIM ** -0.5
    kc = chunks(k, C_KEY_DIM)
    vc = chunks(v, C_VAL_DIM)
    cum = jnp.cumsum(chunks(log_a, C_KEY_DIM), axis=2)
    last = cum[:, :, -1]
    q_dec = qc * jnp.exp(cum)
    k_inv = kc * jnp.exp(-cum)
    causal = jnp.tril(jnp.ones((C_CHUNK, C_CHUNK), bool))
    att = jnp.where(causal, jnp.einsum('bnihd,bnjhd->bnhij', q_dec, k_inv), 0.0)
    o_intra = jnp.einsum('bnhij,bnjhv->bnihv', att, vc)
    upd = jnp.einsum('bnjhd,bnjhv->bnhdv', kc * jnp.exp(last[:, :, None] - cum), vc)
    dec = jnp.exp(last)

    def step(state, inp):
        d, u = inp
        return d[..., None] * state + u, state

    init = jnp.zeros((bsz, C_HEADS, C_KEY_DIM, C_VAL_DIM), f32)
    _, before = lax.scan(step, init, (jnp.moveaxis(dec, 1, 0), jnp.moveaxis(upd, 1, 0)))
    before = jnp.moveaxis(before, 0, 1)
    o = o_intra + jnp.einsum('bnihd,bnhdv->bnihv', q_dec, before)
    o = o.reshape(bsz, seq, C_HEADS, C_VAL_DIM)
    o = o * lax.rsqrt(jnp.mean(o * o, -1, keepdims=True) + 1e-6)
    o = o.reshape(bsz, seq, BRANCH_WIDTH) * norm_gain
    return (jax.nn.silu(og.astype(f32)) * o).astype(q.dtype)


def _token_shift(z, mu):
    prev = jnp.pad(z, ((0, 0), (1, 0), (0, 0)))[:, :-1]
    return z + (prev - z) * mu


def _mixer_d(seg, mu, w0, w2, a0, a2, g2, k_k, k_a, r_k, gn_w, gn_b):
    f32 = jnp.float32
    out_dtype = seg.dtype
    bsz, seq, _ = seg.shape
    bw = BRANCH_WIDTH
    seg = _token_shift(seg, mu).astype(f32)
    r, k, v = seg[..., :bw], seg[..., bw:2 * bw], seg[..., 2 * bw:3 * bw]
    o1 = 3 * bw
    o2 = o1 + D_DECAY_RANK
    o3 = o2 + D_ICLR_RANK
    wl, al, gl = seg[..., o1:o2], seg[..., o2:o3], seg[..., o3:o3 + D_GATE_RANK]
    w_raw = -jax.nn.softplus(-(w0 + jnp.tanh(wl) @ w2)) - 0.5
    w = jnp.exp(-jnp.exp(w_raw))
    a = jax.nn.sigmoid(a0 + al @ a2)
    g = jax.nn.sigmoid(gl) @ g2

    def heads(z):
        return z.astype(f32).reshape(bsz, seq, D_HEADS, D_HEAD_SIZE)

    kk = heads(k * k_k)
    kk = kk / jnp.maximum(jnp.sqrt(jnp.sum(kk * kk, -1, keepdims=True)), 1e-12)
    k = k * (1.0 + (a - 1.0) * k_a)
    rh, kh, vh, wh, ah = heads(r), heads(k), heads(v), heads(w), heads(a)
    a_vec = -kk
    b_vec = kk * ah

    def step(state, inp):
        rt, wt, kt, vt, at, bt = inp
        sa = jnp.einsum('bhvk,bhk->bhv', state, at)
        state = (state * wt[:, :, None, :] + sa[..., None] * bt[:, :, None, :]
                 + vt[..., None] * kt[:, :, None, :])
        return state, jnp.einsum('bhvk,bhk->bhv', state, rt)

    init = jnp.zeros((bsz, D_HEADS, D_HEAD_SIZE, D_HEAD_SIZE), f32)
    xs = tuple(jnp.moveaxis(z, 1, 0) for z in (rh, wh, kh, vh, a_vec, b_vec))
    _, y = lax.scan(step, init, xs)
    y = jnp.moveaxis(y, 0, 1)
    mu_y = jnp.mean(y, -1, keepdims=True)
    var_y = jnp.mean(jnp.square(y - mu_y), -1, keepdims=True)
    y = ((y - mu_y) * lax.rsqrt(var_y + D_GN_EPS)).reshape(bsz, seq, bw) * gn_w + gn_b
    bonus = jnp.sum(rh * kh * r_k.reshape(D_HEADS, D_HEAD_SIZE), -1, keepdims=True) * vh
    y = (y + bonus.reshape(bsz, seq, bw)) * g
    return y.astype(out_dtype)


def _moe(x, wr_g, br_g, wr_e, br_e, w_gate, w_up, w_down):
    f32 = jnp.float32

    def one_seq(xs):
        lg = (xs @ wr_g).astype(f32) + br_g
        pg = jax.nn.softmax(lg, axis=-1)
        ptop, gsel = lax.top_k(pg, 1)
        le = ((xs @ wr_e).astype(f32) + br_e).reshape(-1, MOE_GROUPS, MOE_PER_GROUP)
        le = jnp.take_along_axis(le, gsel[:, :, None], axis=1)[:, 0]
        lv, esel = lax.top_k(le, MOE_TOPK)
        we = jax.nn.softmax(lv, axis=-1) * ptop
        local = jnp.einsum('sk,ske->se', we, jax.nn.one_hot(esel, MOE_PER_GROUP, dtype=f32))
        gate = (jax.nn.one_hot(gsel[:, 0], MOE_GROUPS, dtype=f32)[:, :, None]
                * local[:, None, :]).reshape(-1, MOE_EXPERTS)
        hg = jnp.einsum('sd,edh->seh', xs, w_gate)
        hu = jnp.einsum('sd,edh->seh', xs, w_up)
        h = jax.nn.silu(hg) * hu * gate[..., None].astype(xs.dtype)
        return jnp.einsum('seh,ehd->sd', h, w_down)

    return lax.map(one_seq, x).astype(x.dtype)


def setup_inputs(seed: int = 0) -> dict:
    key = jax.random.key(seed)
    ks = jax.random.split(key, 32)
    f32 = jnp.float32

    def nrm(k, shape, scale):
        return jax.random.normal(k, shape, f32) * scale

    L = DEPTH
    bw = BRANCH_WIDTH
    return {
        "x": nrm(ks[0], (BATCH, SEQ, D_MODEL), 1.0),
        "rpb_table": nrm(ks[1], (RPB_BUCKETS, RPB_HEADS), 0.5),
        "w_in": nrm(ks[2], (L, D_MODEL, IN_WIDTH), D_MODEL ** -0.5),
        "b_kv_gain": 1.0 + nrm(ks[3], (L, B_LATENT), 0.05),
        "b_w_uv": nrm(ks[4], (L, B_HEADS, B_LATENT, HEAD_DIM), B_LATENT ** -0.5),
        "c_a_up": nrm(ks[5], (L, C_GATE_RANK, C_HEADS * C_KEY_DIM), C_GATE_RANK ** -0.5),
        "c_a_bias": nrm(ks[6], (L, C_HEADS * C_KEY_DIM), 0.5),
        "c_norm_gain": 1.0 + nrm(ks[7], (L, bw), 0.05),
        "d_mu": jax.random.uniform(ks[8], (L, D_IN_WIDTH), f32),
        "d_w0": nrm(ks[9], (L, bw), 0.5),
        "d_w2": nrm(ks[10], (L, D_DECAY_RANK, bw), 0.5 * D_DECAY_RANK ** -0.5),
        "d_a0": nrm(ks[11], (L, bw), 0.1),
        "d_a2": nrm(ks[12], (L, D_ICLR_RANK, bw), 0.5 * D_ICLR_RANK ** -0.5),
        "d_g2": nrm(ks[13], (L, D_GATE_RANK, bw), D_GATE_RANK ** -0.5),
        "d_k_k": 0.85 + nrm(ks[14], (L, bw), 0.05),
        "d_k_a": 1.0 + nrm(ks[15], (L, bw), 0.05),
        "d_r_k": nrm(ks[16], (L, bw), 0.1),
        "d_gn_w": 1.0 + nrm(ks[17], (L, bw), 0.05),
        "d_gn_b": nrm(ks[18], (L, bw), 0.01),
        "w_branch": nrm(ks[19], (L, N_BRANCHES, bw, D_MODEL), bw ** -0.5),
        "w_out": nrm(ks[20], (L, D_MODEL, D_MODEL), D_MODEL ** -0.5 * DN_BETA),
        "ln_g": 1.0 + nrm(ks[21], (L, 2, D_MODEL), 0.05),
        "ln_b": nrm(ks[22], (L, 2, D_MODEL), 0.01),
        "router_g": nrm(ks[23], (L, D_MODEL, MOE_GROUPS), D_MODEL ** -0.5),
        "router_g_bias": nrm(ks[24], (L, MOE_GROUPS), 0.01),
        "router_e": nrm(ks[25], (L, D_MODEL, MOE_EXPERTS), D_MODEL ** -0.5),
        "router_e_bias": nrm(ks[26], (L, MOE_EXPERTS), 0.01),
        "moe_w_gate": nrm(ks[27], (L, MOE_EXPERTS, D_MODEL, MOE_HIDDEN), D_MODEL ** -0.5),
        "moe_w_up": nrm(ks[28], (L, MOE_EXPERTS, D_MODEL, MOE_HIDDEN), D_MODEL ** -0.5),
        "moe_w_down": nrm(ks[29], (L, MOE_EXPERTS, MOE_HIDDEN, D_MODEL), MOE_HIDDEN ** -0.5 * DN_BETA),
    }


def reference(x, rpb_table, w_in, b_kv_gain, b_w_uv, c_a_up, c_a_bias, c_norm_gain,
              d_mu, d_w0, d_w2, d_a0, d_a2, d_g2, d_k_k, d_k_a, d_r_k, d_gn_w, d_gn_b,
              w_branch, w_out, ln_g, ln_b, router_g, router_g_bias, router_e, router_e_bias,
              moe_w_gate, moe_w_up, moe_w_down):
    bsz, seq, _ = x.shape
    for l in range(DEPTH):
        p = _split_in(x @ w_in[l])
        ya = _mixer_a(p["a_q"], p["a_k"], p["a_v"], rpb_table[:, :A_HEADS])
        yb = _mixer_b(p["b_q"], p["b_ckv"], p["b_iq"], p["b_ik"], p["b_iw"],
                      b_kv_gain[l], b_w_uv[l], rpb_table[:, A_HEADS:])
        yc = _mixer_c(p["c_q"], p["c_k"], p["c_v"], p["c_a"], p["c_g"],
                      c_a_up[l], c_a_bias[l], c_norm_gain[l])
        yd = _mixer_d(p["d"], d_mu[l], d_w0[l], d_w2[l], d_a0[l], d_a2[l], d_g2[l],
                      d_k_k[l], d_k_a[l], d_r_k[l], d_gn_w[l], d_gn_b[l])
        ys = jnp.stack([ya.astype(x.dtype), yb.astype(x.dtype), yc.astype(x.dtype), yd.astype(x.dtype)], axis=2)
        gates = jax.nn.sigmoid(p["gate"].reshape(bsz, seq, N_BRANCHES, D_MODEL))
        merged = jnp.sum(gates * jnp.einsum('bsnc,ncd->bsnd', ys, w_branch[l]), axis=2)
        x = _layer_norm(DN_ALPHA * x + merged @ w_out[l], ln_g[l, 0], ln_b[l, 0])
        y = _moe(x, router_g[l], router_g_bias[l], router_e[l], router_e_bias[l],
                 moe_w_gate[l], moe_w_up[l], moe_w_down[l])
        x = _layer_norm(DN_ALPHA * x + y, ln_g[l, 1], ln_b[l, 1])
    return x
```

```python
import functools
import math

import jax
import jax.numpy as jnp
import numpy as np
from jax import lax
from jax.experimental import pallas as pl
from jax.experimental.pallas import tpu as pltpu

F32 = jnp.float32
BF16 = jnp.bfloat16
I32 = jnp.int32

D_MODEL = 1024
N_BRANCHES = 4
BRANCH_WIDTH = 256
HEAD_DIM = 64
A_HEADS = 4
A_PATTERNS = ((128, 1), (512, 4), (2048, 16))
A_BLOCK = 128
B_HEADS = 4
B_LATENT = 64
IDX_HEADS = 8
IDX_DIM = 32
TOPK_MAX = 256
TOPK_DIV = 4
Q_BLOCK = 128
C_HEADS = 4
C_KEY_DIM = 32
C_VAL_DIM = 64
C_GATE_RANK = 16
C_GATE_TAU = 16.0
C_CHUNK = 64
D_HEADS = 4
D_HEAD_SIZE = 64
D_DECAY_RANK = 16
D_ICLR_RANK = 16
D_GATE_RANK = 32
D_GN_EPS = 64e-5
D_CHUNK = 64
RPB_BUCKETS = 32
RPB_MAX_DIST = 2048
MOE_GROUPS = 4
MOE_PER_GROUP = 8
MOE_EXPERTS = 32
MOE_HIDDEN = 256
LN_EPS = 1e-5

LANES = 128
NEG = -1e30
VMEM_LIMIT = 56 * 1024 * 1024

NT = (((1,), (1,)), ((), ()))
TN = (((0,), (0,)), ((), ()))

_PIECES = (
    ("a_q", 2), ("a_k", 2), ("a_v", 2), ("b_q", 2), ("b_iq", 2), ("b_ckv4", 2), ("b_ik8", 2),
    ("c_v", 2), ("c_g", 2), ("d_r", 2), ("d_k", 2), ("d_v", 2),
    ("c_q", 1), ("c_k", 1), ("b_iw", 1), ("c_a", 1), ("d_low", 1), ("pad", 1),
)
_PIECE_OFF = {}
_off = 0
for _n, _w in _PIECES:
    _PIECE_OFF[_n] = (_off, _w)
    _off += _w
PROJ_WIDTH = _off * LANES


def _dot(a, b):
    return jnp.dot(a, b, preferred_element_type=F32)


def _dotg(a, b, dims):
    return lax.dot_general(a, b, dims, preferred_element_type=F32)


def _split2(x):
    hi = x.astype(BF16)
    lo = (x - hi.astype(F32)).astype(BF16)
    return hi, lo


def _dot_rhs_exact(x, m):
    hi, lo = _split2(x)
    return _dot(hi, m) + _dot(lo, m)


def _dot_lhs_exact(m, x):
    hi, lo = _split2(x)
    return _dot(m, hi) + _dot(m, lo)


def _dot3(x, w):
    xh, xl = _split2(x)
    wh, wl = _split2(w)
    return _dot(xh, wh) + (_dot(xl, wh) + _dot(xh, wl))


def _sigmoid(x):
    return 1.0 / (1.0 + jnp.exp(-x))


def _softplus(x):
    return jnp.maximum(x, 0.0) + jnp.log(1.0 + jnp.exp(-jnp.abs(x)))


def _iota(shape, axis):
    return lax.broadcasted_iota(I32, shape, axis)


def _group_matrix(n, group, value, dtype=BF16):
    r = _iota((n, n), 0) // group
    c = _iota((n, n), 1) // group
    return jnp.where(r == c, value, 0.0).astype(dtype)


def _layer_norm(z, g, b):
    mu = jnp.mean(z, -1, keepdims=True)
    zc = z - mu
    var = jnp.mean(zc * zc, -1, keepdims=True)
    return zc * lax.rsqrt(var + LN_EPS) * g + b


def _params(*sem):
    return pltpu.CompilerParams(dimension_semantics=sem, vmem_limit_bytes=VMEM_LIMIT)


def _proj_kernel(x_ref, w_ref, o_ref):
    o_ref[...] = _dot(x_ref[...].astype(BF16), w_ref[...])


def _project(x2d, w, tm, tn):
    n, k = x2d.shape
    wt = w.shape[1]
    return pl.pallas_call(
        _proj_kernel,
        grid=(n // tm, wt // tn),
        in_specs=[pl.BlockSpec((tm, k), lambda i, j: (i, 0)),
                  pl.BlockSpec((k, tn), lambda i, j: (0, j))],
        out_specs=pl.BlockSpec((tm, tn), lambda i, j: (i, j)),
        out_shape=jax.ShapeDtypeStruct((n, wt), F32),
        compiler_params=_params("parallel", "parallel"),
        name="proj",
    )(x2d, w)


def _mixer_a_kernel(q0_ref, q1_ref, k0_ref, k1_ref, v0_ref, v1_ref, bias_ref, o_ref, op_ref, lse_ref,
                    *, seq, dils):
    q_refs, k_refs, v_refs = (q0_ref, q1_ref), (k0_ref, k1_ref), (v0_ref, v1_ref)
    ab = A_BLOCK
    lane = _iota((ab, LANES), 1)
    upper = lane >= HEAD_DIM
    col = _iota((ab, 2 * ab), 1)
    scale = HEAD_DIM ** -0.5

    for p, d in enumerate(dils):
        per_res = seq // (d * ab)

        def body(idx, carry, p=p, d=d, per_res=per_res):
            r = idx // per_res
            b = idx % per_res
            start = b * (ab * d) + r
            pstart = jnp.maximum(start - ab * d, r)
            if d == 1:
                rows, prow = pl.ds(pl.multiple_of(start, ab), ab), pl.ds(pl.multiple_of(pstart, ab), ab)
            else:
                rows, prow = pl.ds(start, ab, stride=d), pl.ds(pstart, ab, stride=d)
            first_pen = jnp.where(b == 0, NEG, 0.0)
            pen = jnp.where(col < ab, first_pen, 0.0)
            for hp in range(2):
                q_ref, k_ref, v_ref = q_refs[hp], k_refs[hp], v_refs[hp]
                q = q_ref[rows, :]
                kb = jnp.concatenate([k_ref[prow, :], k_ref[rows, :]], 0).astype(BF16)
                vb = jnp.concatenate([v_ref[prow, :], v_ref[rows, :]], 0).astype(BF16)
                outs, lses = [], []
                for hh in range(2):
                    hm = upper if hh else jnp.logical_not(upper)
                    qm = jnp.where(hm, q, 0.0).astype(BF16)
                    s = _dotg(qm, kb, NT) * scale + bias_ref[p, 2 * hp + hh] + pen
                    mx = jnp.max(s, -1, keepdims=True)
                    e = jnp.exp(s - mx)
                    den = jnp.sum(e, -1, keepdims=True)
                    outs.append(_dot(e.astype(BF16), vb) / den)
                    lses.append(mx + jnp.log(den))
                op_ref[2 * p + hp, rows, :] = jnp.where(upper, outs[1], outs[0])
                lse_ref[2 * p + hp, rows, :] = jnp.where(upper, lses[1], lses[0])
            return carry

        lax.fori_loop(0, seq // ab, body, 0)

    cr = 256

    def combine(c, carry):
        rows = pl.ds(pl.multiple_of(c * cr, cr), cr)
        for hp in range(2):
            ls = [lse_ref[2 * p + hp, rows, :] for p in range(len(dils))]
            mx = functools.reduce(jnp.maximum, ls)
            es = [jnp.exp(l - mx) for l in ls]
            num = functools.reduce(lambda a, b: a + b,
                                   [e * op_ref[2 * p + hp, rows, :] for p, e in enumerate(es)])
            o_ref[rows, hp * LANES:(hp + 1) * LANES] = num / functools.reduce(lambda a, b: a + b, es)
        return carry

    lax.fori_loop(0, seq // cr, combine, 0)


def _mixer_a(h, bias, bsz, seq):
    dils = tuple(d for _, d in A_PATTERNS)
    for w, d in A_PATTERNS:
        assert w // d == A_BLOCK and seq % (d * A_BLOCK) == 0
    npat = len(dils)

    def pieces(name):
        off, w = _PIECE_OFF[name]
        return [pl.BlockSpec((seq, LANES), lambda b, o=off + u: (b, o)) for u in range(w)]

    return pl.pallas_call(
        functools.partial(_mixer_a_kernel, seq=seq, dils=dils),
        grid=(bsz,),
        in_specs=pieces("a_q") + pieces("a_k") + pieces("a_v")
        + [pl.BlockSpec(bias.shape, lambda b: (0, 0, 0, 0))],
        out_specs=pl.BlockSpec((seq, BRANCH_WIDTH), lambda b: (b, 0)),
        out_shape=jax.ShapeDtypeStruct((bsz * seq, BRANCH_WIDTH), F32),
        scratch_shapes=[pltpu.VMEM((2 * npat, seq, LANES), F32),
                        pltpu.VMEM((2 * npat, seq, LANES), F32)],
        compiler_params=_params("parallel"),
        name="mixer_a",
    )(h, h, h, h, h, h, bias)


def _mixer_b_kernel(q_ref, iq_ref, iw_ref, ckv_ref, ik_ref, gain_ref, wuv_ref, bias_ref, o_ref,
                    ckvn_ref, ikb_ref, sc_ref, iqs_ref, qs_ref, *, seq, keep):
    qb = Q_BLOCK
    i = pl.program_id(1)

    @pl.when(i == 0)
    def _():
        gmat = _group_matrix(BRANCH_WIDTH, B_LATENT, 1.0 / B_LATENT)

        def prep(c, carry):
            rows = pl.ds(pl.multiple_of(c * 256, 256), 256)
            x = ckv_ref[rows, :]
            ms = _dot_rhs_exact(x * x, gmat)
            ckvn_ref[rows, :] = (x * lax.rsqrt(ms + 1e-6) * gain_ref[...]).astype(BF16)
            ikb_ref[rows, :] = ik_ref[rows, :].astype(BF16)
            return carry

        lax.fori_loop(0, seq // 256, prep, 0)

    lane = _iota((qb, BRANCH_WIDTH), 1)
    iq = iq_ref[...] * (IDX_DIM ** -0.5)
    for h in range(IDX_HEADS):
        iqs_ref[h * qb:(h + 1) * qb, :] = jnp.where(lane // IDX_DIM == h, iq, 0.0).astype(BF16)
    q = q_ref[...] * (B_LATENT ** -0.5)
    head_of_lane = lane // B_LATENT
    for h in range(B_HEADS):
        qs_ref[h * qb:(h + 1) * qb, :] = jnp.where(head_of_lane == h, q, 0.0).astype(BF16)
    iw = iw_ref[...] * (IDX_HEADS ** -0.5)
    iw_cols = [iw[:, h:h + 1] for h in range(IDX_HEADS)]
    row = _iota((qb, qb), 0)
    col = _iota((qb, qb), 1)
    nkb = i + 1

    def admissible(j):
        return col <= row + jnp.where(j < i, qb, 0)

    def score_body(j, carry):
        kr = pl.ds(pl.multiple_of(j * qb, qb), qb)
        rel = _dotg(iqs_ref[...], ikb_ref[kr, :], NT)
        sc = jnp.maximum(rel[0:qb], 0.0) * iw_cols[0]
        for h in range(1, IDX_HEADS):
            sc = sc + jnp.maximum(rel[h * qb:(h + 1) * qb], 0.0) * iw_cols[h]
        sc_ref[j] = jnp.where(admissible(j), sc, -jnp.inf)
        return carry

    lax.fori_loop(0, nkb, score_body, 0)

    sign = jnp.int32(-2 ** 31)

    def key_to_float(u):
        k = u ^ sign
        bits = k ^ ((k >> 31) & jnp.int32(0x7FFFFFFF))
        return pltpu.bitcast(bits, F32)

    def count(pred):
        def cnt(j, acc):
            return acc + pred(j)
        acc = lax.fori_loop(0, nkb, cnt, jnp.zeros((qb, qb), F32))
        return jnp.sum(acc, -1, keepdims=True)

    def bit_body(t, u):
        cand = u | jnp.left_shift(jnp.int32(1), 31 - t)
        cf = key_to_float(cand)
        c = count(lambda j: jnp.where(sc_ref[j] >= cf, 1.0, 0.0))
        return jnp.where(c >= keep, cand, u)

    u = lax.fori_loop(0, 32, bit_body, jnp.zeros((qb, 1), I32))
    u = jnp.maximum(u ^ sign, jnp.int32(0x007FFFFF - 2 ** 31)) ^ sign
    thr = key_to_float(u)
    need = keep - count(lambda j: jnp.where(sc_ref[j] > thr, 1.0, 0.0))

    nbits = int(math.ceil(math.log2(seq))) + 1

    def cut_body(t, cut):
        cand = cut | jnp.left_shift(jnp.int32(1), nbits - 1 - t)
        c = count(lambda j: jnp.where(sc_ref[j] == thr,
                                      jnp.where(j * qb + col < cand, 1.0, 0.0), 0.0))
        return jnp.where(c <= need, cand, cut)

    cut = lax.fori_loop(0, nbits, cut_body, jnp.zeros((qb, 1), I32))

    def att_body(j, carry):
        ms, ls, acc = carry
        kr = pl.ds(pl.multiple_of(j * qb, qb), qb)
        kc = ckvn_ref[kr, :]
        sc = sc_ref[j]
        sel = jnp.logical_or(sc > thr, jnp.logical_and(sc == thr, j * qb + col < cut))
        sel = jnp.logical_and(sel, admissible(j))
        logits = _dotg(qs_ref[...], kc, NT)
        new_ms, new_ls, ps, alphas = [], [], [], []
        for h in range(B_HEADS):
            s = jnp.where(sel, logits[h * qb:(h + 1) * qb] + bias_ref[i - j, h], NEG)
            m_new = jnp.maximum(ms[h], jnp.max(s, -1, keepdims=True))
            pr = jnp.where(sel, jnp.exp(s - m_new), 0.0)
            alpha = jnp.exp(ms[h] - m_new)
            new_ms.append(m_new)
            new_ls.append(alpha * ls[h] + jnp.sum(pr, -1, keepdims=True))
            ps.append(pr.astype(BF16))
            alphas.append(alpha)
        res = _dot(jnp.concatenate(ps, 0), kc)
        upd = jnp.zeros((qb, BRANCH_WIDTH), F32)
        aw = jnp.zeros((qb, BRANCH_WIDTH), F32)
        for h in range(B_HEADS):
            upd = jnp.where(head_of_lane == h, res[h * qb:(h + 1) * qb], upd)
            aw = jnp.where(head_of_lane == h, alphas[h], aw)
        return tuple(new_ms), tuple(new_ls), acc * aw + upd

    init = (tuple(jnp.full((qb, 1), NEG, F32) for _ in range(B_HEADS)),
            tuple(jnp.zeros((qb, 1), F32) for _ in range(B_HEADS)),
            jnp.zeros((qb, BRANCH_WIDTH), F32))
    _, ls, acc = lax.fori_loop(0, nkb, att_body, init)
    inv = jnp.zeros((qb, BRANCH_WIDTH), F32)
    for h in range(B_HEADS):
        inv = jnp.where(head_of_lane == h, 1.0 / ls[h], inv)
    o_ref[...] = _dot((acc * inv).astype(BF16), wuv_ref[...])


def _mixer_b(h, gain4, wuv_bd, bias, bsz, seq):
    qb = Q_BLOCK
    assert seq % 256 == 0
    nblk = seq // qb
    keep = min(TOPK_MAX, seq // TOPK_DIV)

    def qpiece(name):
        off, w = _PIECE_OFF[name]
        return pl.BlockSpec((qb, w * LANES), lambda b, i, o=off // w: (b * nblk + i, o))

    def kpiece(name):
        off, w = _PIECE_OFF[name]
        return pl.BlockSpec((seq, w * LANES), lambda b, i, o=off // w: (b, o))

    return pl.pallas_call(
        functools.partial(_mixer_b_kernel, seq=seq, keep=keep),
        grid=(bsz, nblk),
        in_specs=[qpiece("b_q"), qpiece("b_iq"), qpiece("b_iw"), kpiece("b_ckv4"), kpiece("b_ik8"),
                  pl.BlockSpec(gain4.shape, lambda b, i: (0, 0)),
                  pl.BlockSpec(wuv_bd.shape, lambda b, i: (0, 0)),
                  pl.BlockSpec(bias.shape, lambda b, i: (0, 0, 0, 0))],
        out_specs=pl.BlockSpec((qb, BRANCH_WIDTH), lambda b, i: (b * nblk + i, 0)),
        out_shape=jax.ShapeDtypeStruct((bsz * seq, BRANCH_WIDTH), F32),
        scratch_shapes=[pltpu.VMEM((seq, BRANCH_WIDTH), BF16),
                        pltpu.VMEM((seq, BRANCH_WIDTH), BF16),
                        pltpu.VMEM((nblk, qb, qb), F32),
                        pltpu.VMEM((IDX_HEADS * qb, BRANCH_WIDTH), BF16),
                        pltpu.VMEM((B_HEADS * qb, BRANCH_WIDTH), BF16)],
        compiler_params=_params("parallel", "arbitrary"),
        name="mixer_b",
    )(h, h, h, h, h, gain4, wuv_bd, bias)


def _mixer_c_kernel(q_ref, k_ref, v_ref, g_ref, a_ref, aup_ref, abias_ref, gain_ref, o_ref, *, seq):
    ch = C_CHUNK
    kw = C_HEADS * C_KEY_DIM
    vw = C_HEADS * C_VAL_DIM
    tri = jnp.where(_iota((ch, ch), 1) <= _iota((ch, ch), 0), 1.0, 0.0).astype(BF16)
    causal = _iota((C_HEADS * ch, ch), 1) <= (_iota((C_HEADS * ch, ch), 0) % ch)
    klane_head = _iota((C_HEADS * ch, kw), 1) // C_KEY_DIM
    krow_head = _iota((C_HEADS * ch, kw), 0) // ch
    vlane_head = _iota((ch, vw), 1) // C_VAL_DIM
    st_mask = (_iota((vw, kw), 0) // C_VAL_DIM) == (_iota((vw, kw), 1) // C_KEY_DIM)
    gmat = _group_matrix(vw, C_VAL_DIM, 1.0 / C_VAL_DIM)
    aup = aup_ref[...]

    def body(c, st):
        rows = pl.ds(pl.multiple_of(c * ch, ch), ch)
        qc = q_ref[rows, :] * (C_KEY_DIM ** -0.5)
        kc = k_ref[rows, :]
        vc = v_ref[rows, :]
        z = _dot3(a_ref[rows, :], aup) + abias_ref[...]
        log_a = -_softplus(-z) / C_GATE_TAU
        cum = _dot_lhs_exact(tri, log_a)
        last = cum[ch - 1:ch, :]
        q_dec = qc * jnp.exp(cum)
        k_inv = (kc * jnp.exp(-cum)).astype(BF16)
        k_dec = (kc * jnp.exp(last - cum)).astype(BF16)
        vb = vc.astype(BF16)
        qd_b = q_dec.astype(BF16)
        q_stack = jnp.where(klane_head == krow_head, jnp.concatenate([q_dec] * C_HEADS, 0), 0.0)
        att = jnp.where(causal, _dotg(q_stack.astype(BF16), k_inv, NT), 0.0)
        full = _dot(att.astype(BF16), vb)
        o = _dotg(qd_b, st.astype(BF16), NT)
        for h in range(C_HEADS):
            o = o + jnp.where(vlane_head == h, full[h * ch:(h + 1) * ch], 0.0)
        upd = _dotg(vb, k_dec, TN)
        st = st * jnp.exp(last) + jnp.where(st_mask, upd, 0.0)
        ms = _dot_rhs_exact(o * o, gmat)
        o = o * lax.rsqrt(ms + 1e-6) * gain_ref[...]
        g = g_ref[rows, :]
        o_ref[rows, :] = g * _sigmoid(g) * o
        return st

    lax.fori_loop(0, seq // ch, body, jnp.zeros((vw, kw), F32))


def _mixer_c(h, aup_pad, abias, gain, bsz, seq):
    assert seq % C_CHUNK == 0

    def piece(name):
        off, w = _PIECE_OFF[name]
        return pl.BlockSpec((seq, w * LANES), lambda b, o=off // w: (b, o))

    def const(a):
        return pl.BlockSpec(a.shape, lambda b: (0, 0))

    return pl.pallas_call(
        functools.partial(_mixer_c_kernel, seq=seq),
        grid=(bsz,),
        in_specs=[piece("c_q"), piece("c_k"), piece("c_v"), piece("c_g"), piece("c_a"),
                  const(aup_pad), const(abias), const(gain)],
        out_specs=pl.BlockSpec((seq, BRANCH_WIDTH), lambda b: (b, 0)),
        out_shape=jax.ShapeDtypeStruct((bsz * seq, BRANCH_WIDTH), F32),
        compiler_params=_params("parallel"),
        name="mixer_c",
    )(h, h, h, h, h, aup_pad, abias, gain)


def _mixer_d_kernel(r_ref, k_ref, v_ref, low_ref, mu_ref, mulow_ref, w2_ref, a2_ref, g2_ref, vec_ref,
                    o_ref, *, seq):
    ch = D_CHUNK
    bw = BRANCH_WIDTH
    nh = D_HEADS
    hs = D_HEAD_SIZE
    r_i = _iota((bw, bw), 0)
    c_i = _iota((bw, bw), 1)
    same = (r_i // hs) == (c_i // hs)
    strict = c_i < r_i
    incl = c_i <= r_i
    eye = r_i == c_i
    ones_bd = _group_matrix(bw, hs, 1.0)
    avg_bd = _group_matrix(bw, hs, 1.0 / hs)
    tri = jnp.where(_iota((ch, ch), 1) <= _iota((ch, ch), 0), 1.0, 0.0).astype(BF16)
    row0 = _iota((ch, bw), 0) == 0
    row0_low = _iota((ch, LANES), 0) == 0
    vec = vec_ref[...]
    w0, a0, k_k, k_a, r_k, gn_w, gn_b = (vec[n:n + 1, :] for n in range(7))
    mu = mu_ref[...]
    w2, a2, g2 = w2_ref[...], a2_ref[...], g2_ref[...]

    def wide(x):
        return jnp.where(same, jnp.concatenate([x] * nh, 0), 0.0)

    def shifted(ref, start, first_row, m):
        cur = ref[pl.ds(start, ch), :]
        last8 = ref[pl.ds(pl.multiple_of(jnp.maximum(start - 8, 0), 8), 8), :]
        prev_row = last8[7:8, :] * jnp.where(start > 0, 1.0, 0.0)
        prev = jnp.where(first_row, prev_row, pltpu.roll(cur, 1, 0))
        return cur + (prev - cur) * m

    def body(c, st):
        start = pl.multiple_of(c * ch, ch)
        r = shifted(r_ref, start, row0, mu[0:1, :])
        k = shifted(k_ref, start, row0, mu[1:2, :])
        v = shifted(v_ref, start, row0, mu[2:3, :])
        low = shifted(low_ref, start, row0_low, mulow_ref[...])
        w_raw = -_softplus(-(w0 + _dot3(jnp.tanh(low), w2))) - 0.5
        lw = -jnp.exp(w_raw)
        a = _sigmoid(a0 + _dot3(low, a2))
        g = _dot3(_sigmoid(low), g2)
        kk = k * k_k
        kk = kk / jnp.maximum(jnp.sqrt(_dot_rhs_exact(kk * kk, ones_bd)), 1e-12)
        k2 = k * (1.0 + (a - 1.0) * k_a)
        bonus = _dot_rhs_exact(r * k2 * r_k, ones_bd) * v
        b = kk * a

        cum = _dot_lhs_exact(tri, lw)
        e_in = jnp.exp(cum)
        e_inv = jnp.exp(-cum)
        g_last = e_in[ch - 1:ch, :]
        xa = wide(-kk * jnp.exp(cum - lw)).astype(BF16)
        xr_f = wide(r * e_in)
        xr = xr_f.astype(BF16)
        yb = wide(b * e_inv).astype(BF16)
        yk = wide(k2 * e_inv).astype(BF16)
        bh = wide(b * e_inv * g_last).astype(BF16)
        kh = wide(k2 * e_inv * g_last).astype(BF16)
        vw = wide(v).astype(BF16)

        a_ab = jnp.where(strict, _dotg(xa, yb, NT), 0.0)
        a_ak = jnp.where(strict, _dotg(xa, yk, NT), 0.0).astype(BF16)
        m_rb = jnp.where(incl, _dotg(xr, yb, NT), 0.0).astype(BF16)
        m_rk = jnp.where(incl, _dotg(xr, yk, NT), 0.0).astype(BF16)

        t_inv = jnp.where(eye, 1.0, a_ab)
        apow = a_ab
        for _ in range(int(math.log2(ch)) - 1):
            ab16 = apow.astype(BF16)
            apow = _dot(ab16, ab16)
            t_inv = t_inv + _dot(t_inv.astype(BF16), apow.astype(BF16))
        t16 = t_inv.astype(BF16)

        atp = _dot(t16, xa).astype(BF16)
        vp = _dot(t16, _dot(a_ak, vw).astype(BF16)).astype(BF16)
        qp = (xr_f + _dot(m_rb, atp)).astype(BF16)
        y0 = _dot(m_rb, vp) + _dot(m_rk, vw)
        gmat = _dotg(bh, atp, TN)
        hmat = _dotg(bh, vp, TN) + _dotg(kh, vw, TN)
        g_col = jnp.sum(jnp.where(eye, g_last, 0.0), -1, keepdims=True)

        st16 = st.astype(BF16)
        y_w = _dot(qp, st16) + y0
        st = g_col * st + _dot(gmat.astype(BF16), st16) + hmat

        y = y_w[0:ch]
        for h in range(1, nh):
            y = y + y_w[h * ch:(h + 1) * ch]
        mean = _dot_rhs_exact(y, avg_bd)
        yc = y - mean
        var = _dot_rhs_exact(yc * yc, avg_bd)
        yn = yc * lax.rsqrt(var + D_GN_EPS) * gn_w + gn_b
        o_ref[pl.ds(start, ch), :] = (yn + bonus) * g
        return st

    lax.fori_loop(0, seq // ch, body, jnp.zeros((bw, bw), F32))


def _mixer_d(h, mu3, mulow, w2p, a2p, g2p, vec, bsz, seq):
    assert seq % D_CHUNK == 0 and D_CHUNK == D_HEAD_SIZE

    def piece(name):
        off, w = _PIECE_OFF[name]
        return pl.BlockSpec((seq, w * LANES), lambda b, o=off // w: (b, o))

    def const(a):
        return pl.BlockSpec(a.shape, lambda b: (0, 0))

    return pl.pallas_call(
        functools.partial(_mixer_d_kernel, seq=seq),
        grid=(bsz,),
        in_specs=[piece("d_r"), piece("d_k"), piece("d_v"), piece("d_low"),
                  const(mu3), const(mulow), const(w2p), const(a2p), const(g2p), const(vec)],
        out_specs=pl.BlockSpec((seq, BRANCH_WIDTH), lambda b: (b, 0)),
        out_shape=jax.ShapeDtypeStruct((bsz * seq, BRANCH_WIDTH), F32),
        compiler_params=_params("parallel"),
        name="mixer_d",
    )(h, h, h, h, mu3, mulow, w2p, a2p, g2p, vec)


def _merge_kernel(x_ref, ya_ref, yb_ref, yc_ref, yd_ref, wg_ref, wb_ref, wo_ref, g_ref, b_ref, o_ref,
                  *, alpha):
    x = x_ref[...]
    xb = x.astype(BF16)
    merged = None
    for n, y_ref in enumerate((ya_ref, yb_ref, yc_ref, yd_ref)):
        gate = _sigmoid(_dot(xb, wg_ref[:, n * D_MODEL:(n + 1) * D_MODEL]))
        term = gate * _dot(y_ref[...].astype(BF16), wb_ref[n])
        merged = term if merged is None else merged + term
    z = alpha * x + _dot(merged.astype(BF16), wo_ref[...])
    o_ref[...] = _layer_norm(z, g_ref[...], b_ref[...])


def _merge(x2d, ys, wg, wb, wo, g, b, alpha, tm):
    n, d = x2d.shape

    def const(a):
        nd = a.ndim
        return pl.BlockSpec(a.shape, lambda i: (0,) * nd)

    yspec = pl.BlockSpec((tm, BRANCH_WIDTH), lambda i: (i, 0))
    return pl.pallas_call(
        functools.partial(_merge_kernel, alpha=alpha),
        grid=(n // tm,),
        in_specs=[pl.BlockSpec((tm, d), lambda i: (i, 0)), yspec, yspec, yspec, yspec,
                  const(wg), const(wb), const(wo), const(g), const(b)],
        out_specs=pl.BlockSpec((tm, d), lambda i: (i, 0)),
        out_shape=jax.ShapeDtypeStruct((n, d), F32),
        compiler_params=_params("parallel"),
        name="merge",
    )(x2d, *ys, wg, wb, wo, g, b)


def _moe_kernel(x_ref, wr_ref, br_ref, wg_ref, wu_ref, wd_ref, g_ref, b_ref, o_ref,
                xb_ref, gate_ref, acc_ref, *, alpha):
    e = pl.program_id(1)
    tm = x_ref.shape[0]
    lane = _iota((tm, LANES), 1)

    @pl.when(e == 0)
    def _():
        x = x_ref[...]
        xb_ref[...] = x.astype(BF16)
        acc_ref[...] = jnp.zeros_like(acc_ref)
        logits = _dot3(x, wr_ref[...]) + br_ref[...]
        is_g = jnp.logical_and(lane >= MOE_EXPERTS, lane < MOE_EXPERTS + MOE_GROUPS)
        lg = jnp.where(is_g, logits, -jnp.inf)
        gmax = jnp.max(lg, -1, keepdims=True)
        ptop = 1.0 / jnp.sum(jnp.exp(lg - gmax), -1, keepdims=True)
        gsel = jnp.min(jnp.where(lg == gmax, lane, 2 * LANES), -1, keepdims=True) - MOE_EXPERTS
        in_group = jnp.logical_and(lane < MOE_EXPERTS, lane // MOE_PER_GROUP == gsel)
        le = jnp.where(in_group, logits, -jnp.inf)
        v1 = jnp.max(le, -1, keepdims=True)
        i1 = jnp.min(jnp.where(le == v1, lane, 2 * LANES), -1, keepdims=True)
        le2 = jnp.where(lane == i1, -jnp.inf, le)
        v2 = jnp.max(le2, -1, keepdims=True)
        i2 = jnp.min(jnp.where(le2 == v2, lane, 2 * LANES), -1, keepdims=True)
        e2 = jnp.exp(v2 - v1)
        w1 = ptop / (1.0 + e2)
        gate_ref[...] = jnp.where(lane == i1, w1, jnp.where(lane == i2, w1 * e2, 0.0))

    gcol = jnp.sum(jnp.where(lane == e, gate_ref[...], 0.0), -1, keepdims=True)
    xb = xb_ref[...]
    hg = _dot(xb, wg_ref[0])
    hu = _dot(xb, wu_ref[0])
    hidden = hg * _sigmoid(hg) * hu * gcol
    acc_ref[...] += _dot(hidden.astype(BF16), wd_ref[0])

    @pl.when(e == pl.num_programs(1) - 1)
    def _():
        z = alpha * x_ref[...] + acc_ref[...]
        o_ref[...] = _layer_norm(z, g_ref[...], b_ref[...])


def _moe(x2d, wr, br, wg, wu, wd, g, b, alpha, tm):
    n, d = x2d.shape
    ne = wg.shape[0]

    def const(a):
        return pl.BlockSpec(a.shape, lambda i, e: (0, 0))

    return pl.pallas_call(
        functools.partial(_moe_kernel, alpha=alpha),
        grid=(n // tm, ne),
        in_specs=[pl.BlockSpec((tm, d), lambda i, e: (i, 0)), const(wr), const(br),
                  pl.BlockSpec((1, d, MOE_HIDDEN), lambda i, e: (e, 0, 0)),
                  pl.BlockSpec((1, d, MOE_HIDDEN), lambda i, e: (e, 0, 0)),
                  pl.BlockSpec((1, MOE_HIDDEN, d), lambda i, e: (e, 0, 0)),
                  const(g), const(b)],
        out_specs=pl.BlockSpec((tm, d), lambda i, e: (i, 0)),
        out_shape=jax.ShapeDtypeStruct((n, d), F32),
        scratch_shapes=[pltpu.VMEM((tm, d), BF16), pltpu.VMEM((tm, LANES), F32),
                        pltpu.VMEM((tm, d), F32)],
        compiler_params=_params("parallel", "arbitrary"),
        name="moe",
    )(x2d, wr, br, wg, wu, wd, g, b)


def _t5_bucket(dist):
    exact = RPB_BUCKETS // 2
    d = jnp.maximum(dist, 0)
    df = jnp.maximum(d, 1).astype(F32)
    large = exact + (jnp.log(df / exact) / math.log(RPB_MAX_DIST / exact)
                     * (RPB_BUCKETS - exact)).astype(I32)
    large = jnp.minimum(large, RPB_BUCKETS - 1)
    return jnp.where(d < exact, d, large)


def _bias_a(tab):
    qi = jnp.arange(A_BLOCK)[:, None]
    kj = jnp.arange(2 * A_BLOCK)[None, :]
    rdist = qi + A_BLOCK - kj
    out = []
    for window, dilation in A_PATTERNS:
        in_band = (rdist >= 0) & (rdist <= window // dilation)
        bias = jnp.transpose(tab[_t5_bucket(rdist * dilation)], (2, 0, 1))
        out.append(jnp.where(in_band[None], bias, NEG))
    return jnp.stack(out).astype(F32)


def _bias_b(tab, nblk):
    qi = jnp.arange(Q_BLOCK)[:, None]
    kj = jnp.arange(Q_BLOCK)[None, :]
    delta = jnp.arange(nblk)[:, None, None] * Q_BLOCK
    bias = tab[_t5_bucket(delta + qi - kj)]
    return jnp.transpose(bias, (0, 3, 1, 2)).astype(F32)


def _in_offsets():
    splits = (("a_q", 256), ("a_k", 256), ("a_v", 256), ("b_q", 256), ("b_ckv", 64), ("b_iq", 256),
              ("b_ik", 32), ("b_iw", 8), ("c_q", 128), ("c_k", 128), ("c_v", 256), ("c_a", 16),
              ("c_g", 256), ("d", 832), ("gate", 4096))
    off, out = 0, {}
    for name, w in splits:
        out[name] = (off, w)
        off += w
    return out


def _proj_columns():
    src = _in_offsets()
    d0 = src["d"][0]
    named = {
        "a_q": np.arange(256) + src["a_q"][0], "a_k": np.arange(256) + src["a_k"][0],
        "a_v": np.arange(256) + src["a_v"][0], "b_q": np.arange(256) + src["b_q"][0],
        "b_iq": np.arange(256) + src["b_iq"][0],
        "b_ckv4": np.tile(np.arange(64) + src["b_ckv"][0], B_HEADS),
        "b_ik8": np.tile(np.arange(32) + src["b_ik"][0], IDX_HEADS),
        "c_v": np.arange(256) + src["c_v"][0], "c_g": np.arange(256) + src["c_g"][0],
        "d_r": np.arange(256) + d0, "d_k": np.arange(256) + d0 + 256, "d_v": np.arange(256) + d0 + 512,
        "c_q": np.arange(128) + src["c_q"][0], "c_k": np.arange(128) + src["c_k"][0],
        "b_iw": np.arange(8) + src["b_iw"][0], "c_a": np.arange(16) + src["c_a"][0],
        "d_low": np.arange(64) + d0 + 768, "pad": np.arange(0),
    }
    cols = []
    for name, w in _PIECES:
        c = named[name]
        cols.append(np.concatenate([c, -np.ones(w * LANES - len(c), np.int64)]))
    return np.concatenate(cols)


def _pad_rows(w, first, total):
    return jnp.zeros((total, w.shape[1]), w.dtype).at[first:first + w.shape[0]].set(w)


def kernel(x, rpb_table, w_in, b_kv_gain, b_w_uv, c_a_up, c_a_bias, c_norm_gain, d_mu, d_w0, d_w2,
           d_a0, d_a2, d_g2, d_k_k, d_k_a, d_r_k, d_gn_w, d_gn_b, w_branch, w_out, ln_g, ln_b,
           router_g, router_g_bias, router_e, router_e_bias, moe_w_gate, moe_w_up, moe_w_down):
    bsz, seq, d_model = x.shape
    depth = w_in.shape[0]
    n = bsz * seq
    alpha = (2 * depth) ** 0.25
    cols = _proj_columns()
    col_idx = jnp.asarray(np.maximum(cols, 0), I32)
    col_valid = jnp.asarray(cols >= 0)
    gate_off = _in_offsets()["gate"][0]
    bias_a = _bias_a(rpb_table[:, :A_HEADS])
    bias_b = _bias_b(rpb_table[:, A_HEADS:], seq // Q_BLOCK)
    tm_proj = math.gcd(n, 1024)
    tm_merge = math.gcd(n, 512)
    tm_moe = math.gcd(n, 1024)

    x2d = x.reshape(n, d_model)
    for l in range(depth):
        w_proj = jnp.where(col_valid[None, :], w_in[l][:, col_idx], 0.0).astype(BF16)
        h = _project(x2d, w_proj, tm_proj, 768)

        ya = _mixer_a(h, bias_a, bsz, seq)

        gain4 = jnp.tile(b_kv_gain[l], B_HEADS)[None, :]
        wuv_bd = jax.scipy.linalg.block_diag(*[b_w_uv[l, hh] for hh in range(B_HEADS)]).astype(BF16)
        yb = _mixer_b(h, gain4, wuv_bd, bias_b, bsz, seq)

        aup_pad = _pad_rows(c_a_up[l], 0, LANES)
        yc = _mixer_c(h, aup_pad, c_a_bias[l][None, :], c_norm_gain[l][None, :], bsz, seq)

        mu = d_mu[l]
        mu3 = mu[:3 * BRANCH_WIDTH].reshape(3, BRANCH_WIDTH)
        mulow = jnp.zeros((1, LANES), F32).at[0, :D_DECAY_RANK + D_ICLR_RANK + D_GATE_RANK].set(
            mu[3 * BRANCH_WIDTH:])
        w2p = _pad_rows(d_w2[l], 0, LANES)
        a2p = _pad_rows(d_a2[l], D_DECAY_RANK, LANES)
        g2p = _pad_rows(d_g2[l], D_DECAY_RANK + D_ICLR_RANK, LANES)
        vec = jnp.stack([d_w0[l], d_a0[l], d_k_k[l], d_k_a[l], d_r_k[l], d_gn_w[l], d_gn_b[l],
                         jnp.zeros_like(d_w0[l])])
        yd = _mixer_d(h, mu3, mulow, w2p, a2p, g2p, vec, bsz, seq)

        wg = w_in[l][:, gate_off:gate_off + N_BRANCHES * d_model].astype(BF16)
        x2d = _merge(x2d, (ya, yb, yc, yd), wg, w_branch[l].astype(BF16), w_out[l].astype(BF16),
                     ln_g[l, 0][None, :], ln_b[l, 0][None, :], alpha, tm_merge)

        wr = jnp.zeros((d_model, LANES), F32)
        wr = wr.at[:, :MOE_EXPERTS].set(router_e[l]).at[:, MOE_EXPERTS:MOE_EXPERTS + MOE_GROUPS].set(
            router_g[l])
        br = jnp.zeros((1, LANES), F32)
        br = br.at[0, :MOE_EXPERTS].set(router_e_bias[l]).at[0, MOE_EXPERTS:MOE_EXPERTS + MOE_GROUPS].set(
            router_g_bias[l])
        x2d = _moe(x2d, wr, br, moe_w_gate[l].astype(BF16), moe_w_up[l].astype(BF16),
                   moe_w_down[l].astype(BF16), ln_g[l, 1][None, :], ln_b[l, 1][None, :], alpha, tm_moe)
    return x2d.reshape(bsz, seq, d_model)
```

```python
import functools
import math

import jax
import jax.numpy as jnp
import numpy as np
from jax import lax
from jax.experimental import pallas as pl
from jax.experimental.pallas import tpu as pltpu

F32 = jnp.float32
BF16 = jnp.bfloat16
I32 = jnp.int32

D_MODEL = 1024
N_BRANCHES = 4
BRANCH_WIDTH = 256
HEAD_DIM = 64
A_HEADS = 4
A_PATTERNS = ((128, 1), (512, 4), (2048, 16))
A_BLOCK = 128
B_HEADS = 4
B_LATENT = 64
IDX_HEADS = 8
IDX_DIM = 32
TOPK_MAX = 256
TOPK_DIV = 4
Q_BLOCK = 128
C_HEADS = 4
C_KEY_DIM = 32
C_VAL_DIM = 64
C_GATE_RANK = 16
C_GATE_TAU = 16.0
C_CHUNK = 64
D_HEADS = 4
D_HEAD_SIZE = 64
D_DECAY_RANK = 16
D_ICLR_RANK = 16
D_GATE_RANK = 32
D_GN_EPS = 64e-5
D_CHUNK = 64
RPB_BUCKETS = 32
RPB_MAX_DIST = 2048
MOE_GROUPS = 4
MOE_PER_GROUP = 8
MOE_EXPERTS = 32
MOE_HIDDEN = 256
LN_EPS = 1e-5

LANES = 128
NEG = -1e30
VMEM_LIMIT = 56 * 1024 * 1024

NT = (((1,), (1,)), ((), ()))
TN = (((0,), (0,)), ((), ()))

_PIECES = (
    ("a_q", 2), ("a_k", 2), ("a_v", 2), ("b_q", 2), ("b_iq", 2), ("b_ckv4", 2), ("b_ik8", 2),
    ("c_v", 2), ("c_g", 2), ("d_r", 2), ("d_k", 2), ("d_v", 2),
    ("c_q", 1), ("c_k", 1), ("b_iw", 1), ("c_a", 1), ("d_low", 1), ("pad", 1),
)
_PIECE_OFF = {}
_off = 0
for _n, _w in _PIECES:
    _PIECE_OFF[_n] = (_off, _w)
    _off += _w
PROJ_WIDTH = _off * LANES


def _dot(a, b):
    return jnp.dot(a, b, preferred_element_type=F32)


def _dotg(a, b, dims):
    return lax.dot_general(a, b, dims, preferred_element_type=F32)


def _split2(x):
    hi = x.astype(BF16)
    lo = (x - hi.astype(F32)).astype(BF16)
    return hi, lo


def _dot_rhs_exact(x, m):
    hi, lo = _split2(x)
    return _dot(hi, m) + _dot(lo, m)


def _dot_lhs_exact(m, x):
    hi, lo = _split2(x)
    return _dot(m, hi) + _dot(m, lo)


def _dot3(x, w):
    xh, xl = _split2(x)
    wh, wl = _split2(w)
    return _dot(xh, wh) + (_dot(xl, wh) + _dot(xh, wl))


def _sigmoid(x):
    return 1.0 / (1.0 + jnp.exp(-x))


def _softplus(x):
    return jnp.maximum(x, 0.0) + jnp.log(1.0 + jnp.exp(-jnp.abs(x)))


def _iota(shape, axis):
    return lax.broadcasted_iota(I32, shape, axis)


def _group_matrix(n, group, value, dtype=BF16):
    r = _iota((n, n), 0) // group
    c = _iota((n, n), 1) // group
    return jnp.where(r == c, value, 0.0).astype(dtype)


def _layer_norm(z, g, b):
    mu = jnp.mean(z, -1, keepdims=True)
    zc = z - mu
    var = jnp.mean(zc * zc, -1, keepdims=True)
    return zc * lax.rsqrt(var + LN_EPS) * g + b


def _params(*sem):
    return pltpu.CompilerParams(dimension_semantics=sem, vmem_limit_bytes=VMEM_LIMIT)


def _proj_kernel(x_ref, w_ref, o_ref):
    o_ref[...] = _dot(x_ref[...].astype(BF16), w_ref[...])


def _project(x2d, w, tm, tn):
    n, k = x2d.shape
    wt = w.shape[1]
    return pl.pallas_call(
        _proj_kernel,
        grid=(n // tm, wt // tn),
        in_specs=[pl.BlockSpec((tm, k), lambda i, j: (i, 0)),
                  pl.BlockSpec((k, tn), lambda i, j: (0, j))],
        out_specs=pl.BlockSpec((tm, tn), lambda i, j: (i, j)),
        out_shape=jax.ShapeDtypeStruct((n, wt), F32),
        compiler_params=_params("parallel", "parallel"),
        name="proj",
    )(x2d, w)


def _mixer_a_kernel(q0_ref, q1_ref, k0_ref, k1_ref, v0_ref, v1_ref, bias_ref, o_ref, op_ref, lse_ref,
                    *, seq, dils):
    q_refs, k_refs, v_refs = (q0_ref, q1_ref), (k0_ref, k1_ref), (v0_ref, v1_ref)
    ab = A_BLOCK
    lane = _iota((ab, LANES), 1)
    upper = lane >= HEAD_DIM
    col = _iota((ab, 2 * ab), 1)
    scale = HEAD_DIM ** -0.5

    for p, d in enumerate(dils):
        per_res = seq // (d * ab)

        def body(idx, carry, p=p, d=d, per_res=per_res):
            r = idx // per_res
            b = idx % per_res
            start = b * (ab * d) + r
            pstart = jnp.maximum(start - ab * d, r)
            if d == 1:
                rows, prow = pl.ds(pl.multiple_of(start, ab), ab), pl.ds(pl.multiple_of(pstart, ab), ab)
            else:
                rows, prow = pl.ds(start, ab, stride=d), pl.ds(pstart, ab, stride=d)
            first_pen = jnp.where(b == 0, NEG, 0.0)
            pen = jnp.where(col < ab, first_pen, 0.0)
            for hp in range(2):
                q_ref, k_ref, v_ref = q_refs[hp], k_refs[hp], v_refs[hp]
                q = q_ref[rows, :]
                kb = jnp.concatenate([k_ref[prow, :], k_ref[rows, :]], 0).astype(BF16)
                vb = jnp.concatenate([v_ref[prow, :], v_ref[rows, :]], 0).astype(BF16)
                outs, lses = [], []
                for hh in range(2):
                    hm = upper if hh else jnp.logical_not(upper)
                    qm = jnp.where(hm, q, 0.0).astype(BF16)
                    s = _dotg(qm, kb, NT) * scale + bias_ref[p, 2 * hp + hh] + pen
                    mx = jnp.max(s, -1, keepdims=True)
                    e = jnp.exp(s - mx)
                    den = jnp.sum(e, -1, keepdims=True)
                    outs.append(_dot(e.astype(BF16), vb) / den)
                    lses.append(mx + jnp.log(den))
                op_ref[2 * p + hp, rows, :] = jnp.where(upper, outs[1], outs[0])
                lse_ref[2 * p + hp, rows, :] = jnp.where(upper, lses[1], lses[0])
            return carry

        lax.fori_loop(0, seq // ab, body, 0)

    cr = 256

    def combine(c, carry):
        rows = pl.ds(pl.multiple_of(c * cr, cr), cr)
        for hp in range(2):
            ls = [lse_ref[2 * p + hp, rows, :] for p in range(len(dils))]
            mx = functools.reduce(jnp.maximum, ls)
            es = [jnp.exp(l - mx) for l in ls]
            num = functools.reduce(lambda a, b: a + b,
                                   [e * op_ref[2 * p + hp, rows, :] for p, e in enumerate(es)])
            o_ref[rows, hp * LANES:(hp + 1) * LANES] = num / functools.reduce(lambda a, b: a + b, es)
        return carry

    lax.fori_loop(0, seq // cr, combine, 0)


def _mixer_a(h, bias, bsz, seq):
    dils = tuple(d for _, d in A_PATTERNS)
    for w, d in A_PATTERNS:
        assert w // d == A_BLOCK and seq % (d * A_BLOCK) == 0
    npat = len(dils)

    def pieces(name):
        off, w = _PIECE_OFF[name]
        return [pl.BlockSpec((seq, LANES), lambda b, o=off + u: (b, o)) for u in range(w)]

    return pl.pallas_call(
        functools.partial(_mixer_a_kernel, seq=seq, dils=dils),
        grid=(bsz,),
        in_specs=pieces("a_q") + pieces("a_k") + pieces("a_v")
        + [pl.BlockSpec(bias.shape, lambda b: (0, 0, 0, 0))],
        out_specs=pl.BlockSpec((seq, BRANCH_WIDTH), lambda b: (b, 0)),
        out_shape=jax.ShapeDtypeStruct((bsz * seq, BRANCH_WIDTH), F32),
        scratch_shapes=[pltpu.VMEM((2 * npat, seq, LANES), F32),
                        pltpu.VMEM((2 * npat, seq, LANES), F32)],
        compiler_params=_params("parallel"),
        name="mixer_a",
    )(h, h, h, h, h, h, bias)


def _mixer_b_kernel(q_ref, iq_ref, iw_ref, ckv_ref, ik_ref, gain_ref, wuvt_ref, bias_ref, o_ref,
                    ckvn_ref, ckvt_ref, ikb_ref, sc_ref, lg_ref, iqs_ref, qs_ref, *, seq, keep):
    qb = Q_BLOCK
    sb = 2 * qb
    i = pl.program_id(1)

    @pl.when(i == 0)
    def _():
        gmat = _group_matrix(BRANCH_WIDTH, B_LATENT, 1.0 / B_LATENT)

        def prep(c, carry):
            rows = pl.ds(pl.multiple_of(c * sb, sb), sb)
            x = ckv_ref[rows, :]
            ms = _dot_rhs_exact(x * x, gmat)
            xn = x * lax.rsqrt(ms + 1e-6) * gain_ref[...]
            ckvn_ref[rows, :] = xn.astype(BF16)
            ckvt_ref[c] = xn.T[0:B_LATENT, :].astype(BF16)
            ikb_ref[rows, :] = ik_ref[rows, :].astype(BF16)
            return carry

        lax.fori_loop(0, seq // sb, prep, 0)

    lane = _iota((qb, BRANCH_WIDTH), 1)
    iq = iq_ref[...] * (IDX_DIM ** -0.5)
    for h in range(IDX_HEADS):
        iqs_ref[h * qb:(h + 1) * qb, :] = jnp.where(lane // IDX_DIM == h, iq, 0.0).astype(BF16)
    q = q_ref[...] * (B_LATENT ** -0.5)
    for h in range(B_HEADS):
        qs_ref[h * qb:(h + 1) * qb, :] = jnp.where(lane // B_LATENT == h, q, 0.0).astype(BF16)
    iw_t = (iw_ref[...] * (IDX_HEADS ** -0.5)).T
    krow = _iota((sb, qb), 0)
    qcol = _iota((sb, qb), 1)
    nsb = (i + 2) // 2

    def fold(x, op):
        parts = [x[r:r + 8] for r in range(0, sb, 8)]
        while len(parts) > 1:
            parts = [op(parts[k], parts[k + 1]) for k in range(0, len(parts), 2)]
        return parts[0]

    def score_body(j, carry):
        kr = pl.ds(pl.multiple_of(j * sb, sb), sb)
        rel = _dotg(ikb_ref[kr, :], iqs_ref[...], NT)
        sc = jnp.maximum(rel[:, 0:qb], 0.0) * iw_t[0:1, :]
        for h in range(1, IDX_HEADS):
            sc = sc + jnp.maximum(rel[:, h * qb:(h + 1) * qb], 0.0) * iw_t[h:h + 1, :]
        sc_ref[j] = jnp.where(krow <= qcol + (i * qb - j * sb), sc, -jnp.inf)
        logits = _dotg(ckvn_ref[kr, :], qs_ref[...], NT)
        d0 = i - 2 * j
        for h in range(B_HEADS):
            lg = logits[:, h * qb:(h + 1) * qb]
            lg_ref[h, j, 0:qb, :] = lg[0:qb] + bias_ref[d0, h]
            lg_ref[h, j, qb:sb, :] = lg[qb:sb] + bias_ref[jnp.maximum(d0 - 1, 0), h]
        return carry

    lax.fori_loop(0, nsb, score_body, 0)

    sign = jnp.int32(-2 ** 31)

    def key_to_float(u):
        k = u ^ sign
        bits = k ^ ((k >> 31) & jnp.int32(0x7FFFFFFF))
        return pltpu.bitcast(bits, F32)

    def count(pred):
        def cnt(j, acc):
            return acc + fold(pred(j), jnp.add)
        acc = lax.fori_loop(0, nsb, cnt, jnp.zeros((8, qb), F32))
        return jnp.sum(acc, 0, keepdims=True)

    def bit_body(t, u):
        cand = u | jnp.left_shift(jnp.int32(1), 31 - t)
        cf = key_to_float(cand)
        c = count(lambda j: jnp.where(sc_ref[j] >= cf, 1.0, 0.0))
        return jnp.where(c >= keep, cand, u)

    u = lax.fori_loop(0, 32, bit_body, jnp.zeros((1, qb), I32))
    u = jnp.maximum(u ^ sign, jnp.int32(0x007FFFFF - 2 ** 31)) ^ sign
    thr = key_to_float(u)
    n_gt = count(lambda j: jnp.where(sc_ref[j] > thr, 1.0, 0.0))
    n_ge = count(lambda j: jnp.where(sc_ref[j] >= thr, 1.0, 0.0))
    need = keep - n_gt

    nbits = int(math.ceil(math.log2(seq))) + 1

    def cut_search():
        def cut_body(t, cut):
            cand = cut | jnp.left_shift(jnp.int32(1), nbits - 1 - t)
            c = count(lambda j: jnp.where(sc_ref[j] == thr,
                                          jnp.where(krow < cand - j * sb, 1.0, 0.0), 0.0))
            return jnp.where(c <= need, cand, cut)

        return lax.fori_loop(0, nbits, cut_body, jnp.zeros((1, qb), I32))

    surplus = jnp.max(n_ge) > keep
    cut = lax.cond(surplus, cut_search, lambda: jnp.full((1, qb), 2 ** nbits - 1, I32))
    cut = jnp.minimum(cut, i * qb + 1 + _iota((1, qb), 1))

    def mask_body(j, mx):
        sc = sc_ref[j]
        sel = jnp.where(sc > thr, 1.0, jnp.where(sc == thr, jnp.where(krow < cut - j * sb, 1.0, 0.0), 0.0))
        out = []
        for h in range(B_HEADS):
            s = jnp.where(sel > 0.5, lg_ref[h, j], NEG)
            lg_ref[h, j] = s
            out.append(jnp.maximum(mx[h], fold(s, jnp.maximum)))
        return tuple(out)

    mx = lax.fori_loop(0, nsb, mask_body, tuple(jnp.full((8, qb), NEG, F32) for _ in range(B_HEADS)))
    ms = [jnp.max(m, 0, keepdims=True) for m in mx]

    def att_body(j, carry):
        ls, acc = carry
        ps, new_ls = [], []
        for h in range(B_HEADS):
            pr = jnp.exp(lg_ref[h, j] - ms[h])
            new_ls.append(ls[h] + fold(pr, jnp.add))
            ps.append(pr.astype(BF16))
        upd = _dot(ckvt_ref[j], jnp.concatenate(ps, 1))
        return tuple(new_ls), acc + upd

    init = (tuple(jnp.zeros((8, qb), F32) for _ in range(B_HEADS)),
            jnp.zeros((B_LATENT, B_HEADS * qb), F32))
    ls, acc = lax.fori_loop(0, nsb, att_body, init)
    ls = [jnp.sum(l, 0, keepdims=True) for l in ls]
    o_t = (acc / jnp.concatenate(ls, 1)).astype(BF16)
    y_t = jnp.concatenate([_dot(wuvt_ref[h], o_t[:, h * qb:(h + 1) * qb]) for h in range(B_HEADS)], 0)
    o_ref[...] = y_t.T


def _mixer_b(h, gain4, wuv_t, bias, bsz, seq):
    qb = Q_BLOCK
    assert seq % 256 == 0
    nblk = seq // qb
    keep = min(TOPK_MAX, seq // TOPK_DIV)

    def qpiece(name):
        off, w = _PIECE_OFF[name]
        return pl.BlockSpec((qb, w * LANES), lambda b, i, o=off // w: (b * nblk + i, o))

    def kpiece(name):
        off, w = _PIECE_OFF[name]
        return pl.BlockSpec((seq, w * LANES), lambda b, i, o=off // w: (b, o))

    return pl.pallas_call(
        functools.partial(_mixer_b_kernel, seq=seq, keep=keep),
        grid=(bsz, nblk),
        in_specs=[qpiece("b_q"), qpiece("b_iq"), qpiece("b_iw"), kpiece("b_ckv4"), kpiece("b_ik8"),
                  pl.BlockSpec(gain4.shape, lambda b, i: (0, 0)),
                  pl.BlockSpec(wuv_t.shape, lambda b, i: (0, 0, 0)),
                  pl.BlockSpec(bias.shape, lambda b, i: (0, 0, 0, 0))],
        out_specs=pl.BlockSpec((qb, BRANCH_WIDTH), lambda b, i: (b * nblk + i, 0)),
        out_shape=jax.ShapeDtypeStruct((bsz * seq, BRANCH_WIDTH), F32),
        scratch_shapes=[pltpu.VMEM((seq, BRANCH_WIDTH), BF16),
                        pltpu.VMEM((nblk // 2, B_LATENT, 2 * qb), BF16),
                        pltpu.VMEM((seq, BRANCH_WIDTH), BF16),
                        pltpu.VMEM((nblk // 2, 2 * qb, qb), F32),
                        pltpu.VMEM((B_HEADS, nblk // 2, 2 * qb, qb), F32),
                        pltpu.VMEM((IDX_HEADS * qb, BRANCH_WIDTH), BF16),
                        pltpu.VMEM((B_HEADS * qb, BRANCH_WIDTH), BF16)],
        compiler_params=_params("parallel", "arbitrary"),
        name="mixer_b",
    )(h, h, h, h, h, gain4, wuv_t, bias)


def _mixer_c_kernel(q_ref, k_ref, v_ref, g_ref, a_ref, aup_ref, abias_ref, gain_ref, o_ref, *, seq):
    ch = C_CHUNK
    kw = C_HEADS * C_KEY_DIM
    vw = C_HEADS * C_VAL_DIM
    tri = jnp.where(_iota((ch, ch), 1) <= _iota((ch, ch), 0), 1.0, 0.0).astype(BF16)
    causal = _iota((C_HEADS * ch, ch), 1) <= (_iota((C_HEADS * ch, ch), 0) % ch)
    klane_head = _iota((C_HEADS * ch, kw), 1) // C_KEY_DIM
    krow_head = _iota((C_HEADS * ch, kw), 0) // ch
    vlane_head = _iota((ch, vw), 1) // C_VAL_DIM
    st_mask = (_iota((vw, kw), 0) // C_VAL_DIM) == (_iota((vw, kw), 1) // C_KEY_DIM)
    gmat = _group_matrix(vw, C_VAL_DIM, 1.0 / C_VAL_DIM)
    aup = aup_ref[...]

    def body(c, st):
        rows = pl.ds(pl.multiple_of(c * ch, ch), ch)
        qc = q_ref[rows, :] * (C_KEY_DIM ** -0.5)
        kc = k_ref[rows, :]
        vc = v_ref[rows, :]
        z = _dot3(a_ref[rows, :], aup) + abias_ref[...]
        log_a = -_softplus(-z) / C_GATE_TAU
        cum = _dot_lhs_exact(tri, log_a)
        last = cum[ch - 1:ch, :]
        q_dec = qc * jnp.exp(cum)
        k_inv = (kc * jnp.exp(-cum)).astype(BF16)
        k_dec = (kc * jnp.exp(last - cum)).astype(BF16)
        vb = vc.astype(BF16)
        qd_b = q_dec.astype(BF16)
        q_stack = jnp.where(klane_head == krow_head, jnp.concatenate([q_dec] * C_HEADS, 0), 0.0)
        att = jnp.where(causal, _dotg(q_stack.astype(BF16), k_inv, NT), 0.0)
        full = _dot(att.astype(BF16), vb)
        o = _dotg(qd_b, st.astype(BF16), NT)
        for h in range(C_HEADS):
            o = o + jnp.where(vlane_head == h, full[h * ch:(h + 1) * ch], 0.0)
        upd = _dotg(vb, k_dec, TN)
        st = st * jnp.exp(last) + jnp.where(st_mask, upd, 0.0)
        ms = _dot_rhs_exact(o * o, gmat)
        o = o * lax.rsqrt(ms + 1e-6) * gain_ref[...]
        g = g_ref[rows, :]
        o_ref[rows, :] = g * _sigmoid(g) * o
        return st

    lax.fori_loop(0, seq // ch, body, jnp.zeros((vw, kw), F32))


def _mixer_c(h, aup_pad, abias, gain, bsz, seq):
    assert seq % C_CHUNK == 0

    def piece(name):
        off, w = _PIECE_OFF[name]
        return pl.BlockSpec((seq, w * LANES), lambda b, o=off // w: (b, o))

    def const(a):
        return pl.BlockSpec(a.shape, lambda b: (0, 0))

    return pl.pallas_call(
        functools.partial(_mixer_c_kernel, seq=seq),
        grid=(bsz,),
        in_specs=[piece("c_q"), piece("c_k"), piece("c_v"), piece("c_g"), piece("c_a"),
                  const(aup_pad), const(abias), const(gain)],
        out_specs=pl.BlockSpec((seq, BRANCH_WIDTH), lambda b: (b, 0)),
        out_shape=jax.ShapeDtypeStruct((bsz * seq, BRANCH_WIDTH), F32),
        compiler_params=_params("parallel"),
        name="mixer_c",
    )(h, h, h, h, h, aup_pad, abias, gain)


def _mixer_d_kernel(r_ref, k_ref, v_ref, low_ref, mu_ref, mulow_ref, w2_ref, a2_ref, g2_ref, vec_ref,
                    o_ref, *, seq):
    ch = D_CHUNK
    bw = BRANCH_WIDTH
    nh = D_HEADS
    hs = D_HEAD_SIZE
    r_i = _iota((bw, bw), 0)
    c_i = _iota((bw, bw), 1)
    same = (r_i // hs) == (c_i // hs)
    strict = c_i < r_i
    incl = c_i <= r_i
    eye = r_i == c_i
    ones_bd = _group_matrix(bw, hs, 1.0)
    avg_bd = _group_matrix(bw, hs, 1.0 / hs)
    tri = jnp.where(_iota((ch, ch), 1) <= _iota((ch, ch), 0), 1.0, 0.0).astype(BF16)
    row0 = _iota((ch, bw), 0) == 0
    row0_low = _iota((ch, LANES), 0) == 0
    vec = vec_ref[...]
    w0, a0, k_k, k_a, r_k, gn_w, gn_b = (vec[n:n + 1, :] for n in range(7))
    mu = mu_ref[...]
    w2, a2, g2 = w2_ref[...], a2_ref[...], g2_ref[...]

    def wide(x):
        return jnp.where(same, jnp.concatenate([x] * nh, 0), 0.0)

    def shifted(ref, start, first_row, m):
        cur = ref[pl.ds(start, ch), :]
        last8 = ref[pl.ds(pl.multiple_of(jnp.maximum(start - 8, 0), 8), 8), :]
        prev_row = last8[7:8, :] * jnp.where(start > 0, 1.0, 0.0)
        prev = jnp.where(first_row, prev_row, pltpu.roll(cur, 1, 0))
        return cur + (prev - cur) * m

    def body(c, st):
        start = pl.multiple_of(c * ch, ch)
        r = shifted(r_ref, start, row0, mu[0:1, :])
        k = shifted(k_ref, start, row0, mu[1:2, :])
        v = shifted(v_ref, start, row0, mu[2:3, :])
        low = shifted(low_ref, start, row0_low, mulow_ref[...])
        w_raw = -_softplus(-(w0 + _dot3(jnp.tanh(low), w2))) - 0.5
        lw = -jnp.exp(w_raw)
        a = _sigmoid(a0 + _dot3(low, a2))
        g = _dot3(_sigmoid(low), g2)
        kk = k * k_k
        kk = kk / jnp.maximum(jnp.sqrt(_dot_rhs_exact(kk * kk, ones_bd)), 1e-12)
        k2 = k * (1.0 + (a - 1.0) * k_a)
        bonus = _dot_rhs_exact(r * k2 * r_k, ones_bd) * v
        b = kk * a

        cum = _dot_lhs_exact(tri, lw)
        e_in = jnp.exp(cum)
        e_inv = jnp.exp(-cum)
        g_last = e_in[ch - 1:ch, :]
        xa = wide(-kk * jnp.exp(cum - lw)).astype(BF16)
        xr_f = wide(r * e_in)
        xr = xr_f.astype(BF16)
        yb = wide(b * e_inv).astype(BF16)
        yk = wide(k2 * e_inv).astype(BF16)
        bh = wide(b * e_inv * g_last).astype(BF16)
        kh = wide(k2 * e_inv * g_last).astype(BF16)
        vw = wide(v).astype(BF16)

        a_ab = jnp.where(strict, _dotg(xa, yb, NT), 0.0)
        a_ak = jnp.where(strict, _dotg(xa, yk, NT), 0.0).astype(BF16)
        m_rb = jnp.where(incl, _dotg(xr, yb, NT), 0.0).astype(BF16)
        m_rk = jnp.where(incl, _dotg(xr, yk, NT), 0.0).astype(BF16)

        t_inv = jnp.where(eye, 1.0, a_ab)
        apow = a_ab
        for _ in range(int(math.log2(ch)) - 1):
            ab16 = apow.astype(BF16)
            apow = _dot(ab16, ab16)
            t_inv = t_inv + _dot(t_inv.astype(BF16), apow.astype(BF16))
        t16 = t_inv.astype(BF16)

        atp = _dot(t16, xa).astype(BF16)
        vp = _dot(t16, _dot(a_ak, vw).astype(BF16)).astype(BF16)
        qp = (xr_f + _dot(m_rb, atp)).astype(BF16)
        y0 = _dot(m_rb, vp) + _dot(m_rk, vw)
        gmat = _dotg(bh, atp, TN)
        hmat = _dotg(bh, vp, TN) + _dotg(kh, vw, TN)
        g_col = jnp.sum(jnp.where(eye, g_last, 0.0), -1, keepdims=True)

        st16 = st.astype(BF16)
        y_w = _dot(qp, st16) + y0
        st = g_col * st + _dot(gmat.astype(BF16), st16) + hmat

        y = y_w[0:ch]
        for h in range(1, nh):
            y = y + y_w[h * ch:(h + 1) * ch]
        mean = _dot_rhs_exact(y, avg_bd)
        yc = y - mean
        var = _dot_rhs_exact(yc * yc, avg_bd)
        yn = yc * lax.rsqrt(var + D_GN_EPS) * gn_w + gn_b
        o_ref[pl.ds(start, ch), :] = (yn + bonus) * g
        return st

    lax.fori_loop(0, seq // ch, body, jnp.zeros((bw, bw), F32))


def _mixer_d(h, mu3, mulow, w2p, a2p, g2p, vec, bsz, seq):
    assert seq % D_CHUNK == 0 and D_CHUNK == D_HEAD_SIZE

    def piece(name):
        off, w = _PIECE_OFF[name]
        return pl.BlockSpec((seq, w * LANES), lambda b, o=off // w: (b, o))

    def const(a):
        return pl.BlockSpec(a.shape, lambda b: (0, 0))

    return pl.pallas_call(
        functools.partial(_mixer_d_kernel, seq=seq),
        grid=(bsz,),
        in_specs=[piece("d_r"), piece("d_k"), piece("d_v"), piece("d_low"),
                  const(mu3), const(mulow), const(w2p), const(a2p), const(g2p), const(vec)],
        out_specs=pl.BlockSpec((seq, BRANCH_WIDTH), lambda b: (b, 0)),
        out_shape=jax.ShapeDtypeStruct((bsz * seq, BRANCH_WIDTH), F32),
        compiler_params=_params("parallel"),
        name="mixer_d",
    )(h, h, h, h, mu3, mulow, w2p, a2p, g2p, vec)


def _merge_kernel(x_ref, ya_ref, yb_ref, yc_ref, yd_ref, wg_ref, wb_ref, wo_ref, g_ref, b_ref, o_ref,
                  *, alpha):
    x = x_ref[...]
    xb = x.astype(BF16)
    merged = None
    for n, y_ref in enumerate((ya_ref, yb_ref, yc_ref, yd_ref)):
        gate = _sigmoid(_dot(xb, wg_ref[:, n * D_MODEL:(n + 1) * D_MODEL]))
        term = gate * _dot(y_ref[...].astype(BF16), wb_ref[n])
        merged = term if merged is None else merged + term
    z = alpha * x + _dot(merged.astype(BF16), wo_ref[...])
    o_ref[...] = _layer_norm(z, g_ref[...], b_ref[...])


def _merge(x2d, ys, wg, wb, wo, g, b, alpha, tm):
    n, d = x2d.shape

    def const(a):
        nd = a.ndim
        return pl.BlockSpec(a.shape, lambda i: (0,) * nd)

    yspec = pl.BlockSpec((tm, BRANCH_WIDTH), lambda i: (i, 0))
    return pl.pallas_call(
        functools.partial(_merge_kernel, alpha=alpha),
        grid=(n // tm,),
        in_specs=[pl.BlockSpec((tm, d), lambda i: (i, 0)), yspec, yspec, yspec, yspec,
                  const(wg), const(wb), const(wo), const(g), const(b)],
        out_specs=pl.BlockSpec((tm, d), lambda i: (i, 0)),
        out_shape=jax.ShapeDtypeStruct((n, d), F32),
        compiler_params=_params("parallel"),
        name="merge",
    )(x2d, *ys, wg, wb, wo, g, b)


def _moe_kernel(x_ref, wr_ref, br_ref, wg_ref, wu_ref, wd_ref, g_ref, b_ref, o_ref,
                xb_ref, gate_ref, acc_ref, *, alpha):
    e = pl.program_id(1)
    tm = x_ref.shape[0]
    lane = _iota((tm, LANES), 1)

    @pl.when(e == 0)
    def _():
        x = x_ref[...]
        xb_ref[...] = x.astype(BF16)
        acc_ref[...] = jnp.zeros_like(acc_ref)
        logits = _dot3(x, wr_ref[...]) + br_ref[...]
        is_g = jnp.logical_and(lane >= MOE_EXPERTS, lane < MOE_EXPERTS + MOE_GROUPS)
        lg = jnp.where(is_g, logits, -jnp.inf)
        gmax = jnp.max(lg, -1, keepdims=True)
        ptop = 1.0 / jnp.sum(jnp.exp(lg - gmax), -1, keepdims=True)
        gsel = jnp.min(jnp.where(lg == gmax, lane, 2 * LANES), -1, keepdims=True) - MOE_EXPERTS
        in_group = jnp.logical_and(lane < MOE_EXPERTS, lane // MOE_PER_GROUP == gsel)
        le = jnp.where(in_group, logits, -jnp.inf)
        v1 = jnp.max(le, -1, keepdims=True)
        i1 = jnp.min(jnp.where(le == v1, lane, 2 * LANES), -1, keepdims=True)
        le2 = jnp.where(lane == i1, -jnp.inf, le)
        v2 = jnp.max(le2, -1, keepdims=True)
        i2 = jnp.min(jnp.where(le2 == v2, lane, 2 * LANES), -1, keepdims=True)
        e2 = jnp.exp(v2 - v1)
        w1 = ptop / (1.0 + e2)
        gate_ref[...] = jnp.where(lane == i1, w1, jnp.where(lane == i2, w1 * e2, 0.0))

    gcol = jnp.sum(jnp.where(lane == e, gate_ref[...], 0.0), -1, keepdims=True)
    xb = xb_ref[...]
    hg = _dot(xb, wg_ref[0])
    hu = _dot(xb, wu_ref[0])
    hidden = hg * _sigmoid(hg) * hu * gcol
    acc_ref[...] += _dot(hidden.astype(BF16), wd_ref[0])

    @pl.when(e == pl.num_programs(1) - 1)
    def _():
        z = alpha * x_ref[...] + acc_ref[...]
        o_ref[...] = _layer_norm(z, g_ref[...], b_ref[...])


def _moe(x2d, wr, br, wg, wu, wd, g, b, alpha, tm):
    n, d = x2d.shape
    ne = wg.shape[0]

    def const(a):
        return pl.BlockSpec(a.shape, lambda i, e: (0, 0))

    return pl.pallas_call(
        functools.partial(_moe_kernel, alpha=alpha),
        grid=(n // tm, ne),
        in_specs=[pl.BlockSpec((tm, d), lambda i, e: (i, 0)), const(wr), const(br),
                  pl.BlockSpec((1, d, MOE_HIDDEN), lambda i, e: (e, 0, 0)),
                  pl.BlockSpec((1, d, MOE_HIDDEN), lambda i, e: (e, 0, 0)),
                  pl.BlockSpec((1, MOE_HIDDEN, d), lambda i, e: (e, 0, 0)),
                  const(g), const(b)],
        out_specs=pl.BlockSpec((tm, d), lambda i, e: (i, 0)),
        out_shape=jax.ShapeDtypeStruct((n, d), F32),
        scratch_shapes=[pltpu.VMEM((tm, d), BF16), pltpu.VMEM((tm, LANES), F32),
                        pltpu.VMEM((tm, d), F32)],
        compiler_params=_params("parallel", "arbitrary"),
        name="moe",
    )(x2d, wr, br, wg, wu, wd, g, b)


def _t5_bucket(dist):
    exact = RPB_BUCKETS // 2
    d = jnp.maximum(dist, 0)
    df = jnp.maximum(d, 1).astype(F32)
    large = exact + (jnp.log(df / exact) / math.log(RPB_MAX_DIST / exact)
                     * (RPB_BUCKETS - exact)).astype(I32)
    large = jnp.minimum(large, RPB_BUCKETS - 1)
    return jnp.where(d < exact, d, large)


def _bias_a(tab):
    qi = jnp.arange(A_BLOCK)[:, None]
    kj = jnp.arange(2 * A_BLOCK)[None, :]
    rdist = qi + A_BLOCK - kj
    out = []
    for window, dilation in A_PATTERNS:
        in_band = (rdist >= 0) & (rdist <= window // dilation)
        bias = jnp.transpose(tab[_t5_bucket(rdist * dilation)], (2, 0, 1))
        out.append(jnp.where(in_band[None], bias, NEG))
    return jnp.stack(out).astype(F32)


def _bias_b(tab, nblk):
    qi = jnp.arange(Q_BLOCK)[:, None]
    kj = jnp.arange(Q_BLOCK)[None, :]
    delta = jnp.arange(nblk)[:, None, None] * Q_BLOCK
    bias = tab[_t5_bucket(delta + qi - kj)]
    return jnp.transpose(bias, (0, 3, 2, 1)).astype(F32)


def _in_offsets():
    splits = (("a_q", 256), ("a_k", 256), ("a_v", 256), ("b_q", 256), ("b_ckv", 64), ("b_iq", 256),
              ("b_ik", 32), ("b_iw", 8), ("c_q", 128), ("c_k", 128), ("c_v", 256), ("c_a", 16),
              ("c_g", 256), ("d", 832), ("gate", 4096))
    off, out = 0, {}
    for name, w in splits:
        out[name] = (off, w)
        off += w
    return out


def _proj_columns():
    src = _in_offsets()
    d0 = src["d"][0]
    named = {
        "a_q": np.arange(256) + src["a_q"][0], "a_k": np.arange(256) + src["a_k"][0],
        "a_v": np.arange(256) + src["a_v"][0], "b_q": np.arange(256) + src["b_q"][0],
        "b_iq": np.arange(256) + src["b_iq"][0],
        "b_ckv4": np.tile(np.arange(64) + src["b_ckv"][0], B_HEADS),
        "b_ik8": np.tile(np.arange(32) + src["b_ik"][0], IDX_HEADS),
        "c_v": np.arange(256) + src["c_v"][0], "c_g": np.arange(256) + src["c_g"][0],
        "d_r": np.arange(256) + d0, "d_k": np.arange(256) + d0 + 256, "d_v": np.arange(256) + d0 + 512,
        "c_q": np.arange(128) + src["c_q"][0], "c_k": np.arange(128) + src["c_k"][0],
        "b_iw": np.arange(8) + src["b_iw"][0], "c_a": np.arange(16) + src["c_a"][0],
        "d_low": np.arange(64) + d0 + 768, "pad": np.arange(0),
    }
    cols = []
    for name, w in _PIECES:
        c = named[name]
        cols.append(np.concatenate([c, -np.ones(w * LANES - len(c), np.int64)]))
    return np.concatenate(cols)


def _proj_weight(w):
    cols = _proj_columns()
    cuts = [0] + [k for k in range(1, len(cols))
                  if (cols[k] < 0) != (cols[k - 1] < 0) or (cols[k] >= 0 and cols[k] != cols[k - 1] + 1)]
    cuts.append(len(cols))
    parts = []
    for a, b in zip(cuts[:-1], cuts[1:]):
        if cols[a] < 0:
            parts.append(jnp.zeros((w.shape[0], b - a), BF16))
        else:
            parts.append(w[:, int(cols[a]):int(cols[a]) + (b - a)].astype(BF16))
    return jnp.concatenate(parts, 1)


def _pad_rows(w, first, total):
    return jnp.zeros((total, w.shape[1]), w.dtype).at[first:first + w.shape[0]].set(w)


def kernel(x, rpb_table, w_in, b_kv_gain, b_w_uv, c_a_up, c_a_bias, c_norm_gain, d_mu, d_w0, d_w2,
           d_a0, d_a2, d_g2, d_k_k, d_k_a, d_r_k, d_gn_w, d_gn_b, w_branch, w_out, ln_g, ln_b,
           router_g, router_g_bias, router_e, router_e_bias, moe_w_gate, moe_w_up, moe_w_down):
    bsz, seq, d_model = x.shape
    depth = w_in.shape[0]
    n = bsz * seq
    alpha = (2 * depth) ** 0.25
    gate_off = _in_offsets()["gate"][0]
    bias_a = _bias_a(rpb_table[:, :A_HEADS])
    bias_b = _bias_b(rpb_table[:, A_HEADS:], seq // Q_BLOCK)
    tm_proj = math.gcd(n, 1024)
    tm_merge = math.gcd(n, 512)
    tm_moe = math.gcd(n, 1024)

    x2d = x.reshape(n, d_model)
    for l in range(depth):
        h = _project(x2d, _proj_weight(w_in[l]), tm_proj, 768)

        ya = _mixer_a(h, bias_a, bsz, seq)

        gain4 = jnp.tile(b_kv_gain[l], B_HEADS)[None, :]
        wuv_t = jnp.transpose(b_w_uv[l], (0, 2, 1)).astype(BF16)
        yb = _mixer_b(h, gain4, wuv_t, bias_b, bsz, seq)

        aup_pad = _pad_rows(c_a_up[l], 0, LANES)
        yc = _mixer_c(h, aup_pad, c_a_bias[l][None, :], c_norm_gain[l][None, :], bsz, seq)

        mu = d_mu[l]
        mu3 = mu[:3 * BRANCH_WIDTH].reshape(3, BRANCH_WIDTH)
        mulow = jnp.zeros((1, LANES), F32).at[0, :D_DECAY_RANK + D_ICLR_RANK + D_GATE_RANK].set(
            mu[3 * BRANCH_WIDTH:])
        w2p = _pad_rows(d_w2[l], 0, LANES)
        a2p = _pad_rows(d_a2[l], D_DECAY_RANK, LANES)
        g2p = _pad_rows(d_g2[l], D_DECAY_RANK + D_ICLR_RANK, LANES)
        vec = jnp.stack([d_w0[l], d_a0[l], d_k_k[l], d_k_a[l], d_r_k[l], d_gn_w[l], d_gn_b[l],
                         jnp.zeros_like(d_w0[l])])
        yd = _mixer_d(h, mu3, mulow, w2p, a2p, g2p, vec, bsz, seq)

        wg = w_in[l][:, gate_off:gate_off + N_BRANCHES * d_model].astype(BF16)
        x2d = _merge(x2d, (ya, yb, yc, yd), wg, w_branch[l].astype(BF16), w_out[l].astype(BF16),
                     ln_g[l, 0][None, :], ln_b[l, 0][None, :], alpha, tm_merge)

        wr = jnp.zeros((d_model, LANES), F32)
        wr = wr.at[:, :MOE_EXPERTS].set(router_e[l]).at[:, MOE_EXPERTS:MOE_EXPERTS + MOE_GROUPS].set(
            router_g[l])
        br = jnp.zeros((1, LANES), F32)
        br = br.at[0, :MOE_EXPERTS].set(router_e_bias[l]).at[0, MOE_EXPERTS:MOE_EXPERTS + MOE_GROUPS].set(
            router_g_bias[l])
        x2d = _moe(x2d, wr, br, moe_w_gate[l].astype(BF16), moe_w_up[l].astype(BF16),
                   moe_w_down[l].astype(BF16), ln_g[l, 1][None, :], ln_b[l, 1][None, :], alpha, tm_moe)
    return x2d.reshape(bsz, seq, d_model)
```

```python
import functools
import math

import jax
import jax.numpy as jnp
import numpy as np
from jax import lax
from jax.experimental import pallas as pl
from jax.experimental.pallas import tpu as pltpu

F32 = jnp.float32
BF16 = jnp.bfloat16
I32 = jnp.int32

D_MODEL = 1024
N_BRANCHES = 4
BRANCH_WIDTH = 256
HEAD_DIM = 64
A_HEADS = 4
A_PATTERNS = ((128, 1), (512, 4), (2048, 16))
A_BLOCK = 128
B_HEADS = 4
B_LATENT = 64
IDX_HEADS = 8
IDX_DIM = 32
TOPK_MAX = 256
TOPK_DIV = 4
Q_BLOCK = 128
C_HEADS = 4
C_KEY_DIM = 32
C_VAL_DIM = 64
C_GATE_RANK = 16
C_GATE_TAU = 16.0
C_CHUNK = 64
D_HEADS = 4
D_HEAD_SIZE = 64
D_DECAY_RANK = 16
D_ICLR_RANK = 16
D_GATE_RANK = 32
D_GN_EPS = 64e-5
D_CHUNK = 64
RPB_BUCKETS = 32
RPB_MAX_DIST = 2048
MOE_GROUPS = 4
MOE_PER_GROUP = 8
MOE_EXPERTS = 32
MOE_HIDDEN = 256
MOE_CHUNK = 128
LN_EPS = 1e-5

LANES = 128
NEG = -1e30
VMEM_LIMIT = 56 * 1024 * 1024

NT = (((1,), (1,)), ((), ()))
TN = (((0,), (0,)), ((), ()))

_PIECES = (
    ("a_q", 2), ("a_k", 2), ("a_v", 2), ("b_q", 2), ("b_iq", 2), ("b_ckv4", 2), ("b_ik8", 2),
    ("c_v", 2), ("c_g", 2), ("d_r", 2), ("d_k", 2), ("d_v", 2),
    ("c_q", 1), ("c_k", 1), ("b_iw", 1), ("c_a", 1), ("d_low", 1), ("pad", 1),
)
_PIECE_OFF = {}
_off = 0
for _n, _w in _PIECES:
    _PIECE_OFF[_n] = (_off, _w)
    _off += _w
PROJ_WIDTH = _off * LANES


def _dot(a, b):
    return jnp.dot(a, b, preferred_element_type=F32)


def _dotg(a, b, dims):
    return lax.dot_general(a, b, dims, preferred_element_type=F32)


def _split2(x):
    hi = x.astype(BF16)
    lo = (x - hi.astype(F32)).astype(BF16)
    return hi, lo


def _dot_rhs_exact(x, m):
    hi, lo = _split2(x)
    return _dot(hi, m) + _dot(lo, m)


def _dot_lhs_exact(m, x):
    hi, lo = _split2(x)
    return _dot(m, hi) + _dot(m, lo)


def _dot3(x, w):
    xh, xl = _split2(x)
    wh, wl = _split2(w)
    return _dot(xh, wh) + (_dot(xl, wh) + _dot(xh, wl))


def _sigmoid(x):
    return 1.0 / (1.0 + jnp.exp(-x))


def _softplus(x):
    return jnp.maximum(x, 0.0) + jnp.log(1.0 + jnp.exp(-jnp.abs(x)))


def _iota(shape, axis):
    return lax.broadcasted_iota(I32, shape, axis)


def _group_matrix(n, group, value, dtype=BF16):
    r = _iota((n, n), 0) // group
    c = _iota((n, n), 1) // group
    return jnp.where(r == c, value, 0.0).astype(dtype)


def _layer_norm(z, g, b):
    mu = jnp.mean(z, -1, keepdims=True)
    zc = z - mu
    var = jnp.mean(zc * zc, -1, keepdims=True)
    return zc * lax.rsqrt(var + LN_EPS) * g + b


def _params(*sem):
    return pltpu.CompilerParams(dimension_semantics=sem, vmem_limit_bytes=VMEM_LIMIT)


def _proj_kernel(x_ref, w_ref, o_ref):
    o_ref[...] = _dot(x_ref[...].astype(BF16), w_ref[...])


def _project(x2d, w, tm, tn):
    n, k = x2d.shape
    wt = w.shape[1]
    return pl.pallas_call(
        _proj_kernel,
        grid=(n // tm, wt // tn),
        in_specs=[pl.BlockSpec((tm, k), lambda i, j: (i, 0)),
                  pl.BlockSpec((k, tn), lambda i, j: (0, j))],
        out_specs=pl.BlockSpec((tm, tn), lambda i, j: (i, j)),
        out_shape=jax.ShapeDtypeStruct((n, wt), F32),
        compiler_params=_params("parallel", "parallel"),
        name="proj",
    )(x2d, w)


def _mixer_a_kernel(q0_ref, q1_ref, k0_ref, k1_ref, v0_ref, v1_ref, bias_ref, o_ref, op_ref, lse_ref,
                    *, seq, dils):
    q_refs, k_refs, v_refs = (q0_ref, q1_ref), (k0_ref, k1_ref), (v0_ref, v1_ref)
    ab = A_BLOCK
    lane = _iota((ab, LANES), 1)
    upper = lane >= HEAD_DIM
    col = _iota((ab, 2 * ab), 1)
    scale = HEAD_DIM ** -0.5

    for p, d in enumerate(dils):
        per_res = seq // (d * ab)

        def body(idx, carry, p=p, d=d, per_res=per_res):
            r = idx // per_res
            b = idx % per_res
            start = b * (ab * d) + r
            pstart = jnp.maximum(start - ab * d, r)
            if d == 1:
                rows, prow = pl.ds(pl.multiple_of(start, ab), ab), pl.ds(pl.multiple_of(pstart, ab), ab)
            else:
                rows, prow = pl.ds(start, ab, stride=d), pl.ds(pstart, ab, stride=d)
            first_pen = jnp.where(b == 0, NEG, 0.0)
            pen = jnp.where(col < ab, first_pen, 0.0)
            for hp in range(2):
                q_ref, k_ref, v_ref = q_refs[hp], k_refs[hp], v_refs[hp]
                q = q_ref[rows, :]
                kb = jnp.concatenate([k_ref[prow, :], k_ref[rows, :]], 0).astype(BF16)
                vb = jnp.concatenate([v_ref[prow, :], v_ref[rows, :]], 0).astype(BF16)
                outs, lses = [], []
                for hh in range(2):
                    hm = upper if hh else jnp.logical_not(upper)
                    qm = jnp.where(hm, q, 0.0).astype(BF16)
                    s = _dotg(qm, kb, NT) * scale + bias_ref[p, 2 * hp + hh] + pen
                    mx = jnp.max(s, -1, keepdims=True)
                    e = jnp.exp(s - mx)
                    den = jnp.sum(e, -1, keepdims=True)
                    outs.append(_dot(e.astype(BF16), vb) / den)
                    lses.append(mx + jnp.log(den))
                op_ref[2 * p + hp, rows, :] = jnp.where(upper, outs[1], outs[0])
                lse_ref[2 * p + hp, rows, :] = jnp.where(upper, lses[1], lses[0])
            return carry

        lax.fori_loop(0, seq // ab, body, 0)

    cr = 256

    def combine(c, carry):
        rows = pl.ds(pl.multiple_of(c * cr, cr), cr)
        for hp in range(2):
            ls = [lse_ref[2 * p + hp, rows, :] for p in range(len(dils))]
            mx = functools.reduce(jnp.maximum, ls)
            es = [jnp.exp(l - mx) for l in ls]
            num = functools.reduce(lambda a, b: a + b,
                                   [e * op_ref[2 * p + hp, rows, :] for p, e in enumerate(es)])
            o_ref[rows, hp * LANES:(hp + 1) * LANES] = num / functools.reduce(lambda a, b: a + b, es)
        return carry

    lax.fori_loop(0, seq // cr, combine, 0)


def _mixer_a(h, bias, bsz, seq):
    dils = tuple(d for _, d in A_PATTERNS)
    for w, d in A_PATTERNS:
        assert w // d == A_BLOCK and seq % (d * A_BLOCK) == 0
    npat = len(dils)

    def pieces(name):
        off, w = _PIECE_OFF[name]
        return [pl.BlockSpec((seq, LANES), lambda b, o=off + u: (b, o)) for u in range(w)]

    return pl.pallas_call(
        functools.partial(_mixer_a_kernel, seq=seq, dils=dils),
        grid=(bsz,),
        in_specs=pieces("a_q") + pieces("a_k") + pieces("a_v")
        + [pl.BlockSpec(bias.shape, lambda b: (0, 0, 0, 0))],
        out_specs=pl.BlockSpec((seq, BRANCH_WIDTH), lambda b: (b, 0)),
        out_shape=jax.ShapeDtypeStruct((bsz * seq, BRANCH_WIDTH), F32),
        scratch_shapes=[pltpu.VMEM((2 * npat, seq, LANES), F32),
                        pltpu.VMEM((2 * npat, seq, LANES), F32)],
        compiler_params=_params("parallel"),
        name="mixer_a",
    )(h, h, h, h, h, h, bias)


def _mixer_b_kernel(q_ref, iq_ref, iw_ref, ckv_ref, ik_ref, gain_ref, wuvt_ref, bias_ref, o_ref,
                    ckvn_ref, ckvt_ref, ikb_ref, sc_ref, lg_ref, iqs_ref, qs_ref, *, seq, keep):
    qb = Q_BLOCK
    sb = 2 * qb
    i = pl.program_id(1)

    @pl.when(i == 0)
    def _():
        gmat = _group_matrix(BRANCH_WIDTH, B_LATENT, 1.0 / B_LATENT)

        def prep(c, carry):
            rows = pl.ds(pl.multiple_of(c * sb, sb), sb)
            x = ckv_ref[rows, :]
            ms = _dot_rhs_exact(x * x, gmat)
            xn = x * lax.rsqrt(ms + 1e-6) * gain_ref[...]
            ckvn_ref[rows, :] = xn.astype(BF16)
            ckvt_ref[c] = xn.T[0:B_LATENT, :].astype(BF16)
            ikb_ref[rows, :] = ik_ref[rows, :].astype(BF16)
            return carry

        lax.fori_loop(0, seq // sb, prep, 0)

    lane = _iota((qb, BRANCH_WIDTH), 1)
    iq = iq_ref[...] * (IDX_DIM ** -0.5)
    for h in range(IDX_HEADS):
        iqs_ref[h * qb:(h + 1) * qb, :] = jnp.where(lane // IDX_DIM == h, iq, 0.0).astype(BF16)
    q = q_ref[...] * (B_LATENT ** -0.5)
    for h in range(B_HEADS):
        qs_ref[h * qb:(h + 1) * qb, :] = jnp.where(lane // B_LATENT == h, q, 0.0).astype(BF16)
    iw_t = (iw_ref[...] * (IDX_HEADS ** -0.5)).T
    krow = _iota((sb, qb), 0)
    qcol = _iota((sb, qb), 1)
    nsb = (i + 2) // 2

    def fold(x, op):
        parts = [x[r:r + 8] for r in range(0, sb, 8)]
        while len(parts) > 1:
            parts = [op(parts[k], parts[k + 1]) for k in range(0, len(parts), 2)]
        return parts[0]

    def score_body(j, carry):
        kr = pl.ds(pl.multiple_of(j * sb, sb), sb)
        rel = _dotg(ikb_ref[kr, :], iqs_ref[...], NT)
        sc = jnp.maximum(rel[:, 0:qb], 0.0) * iw_t[0:1, :]
        for h in range(1, IDX_HEADS):
            sc = sc + jnp.maximum(rel[:, h * qb:(h + 1) * qb], 0.0) * iw_t[h:h + 1, :]
        sc_ref[j] = jnp.where(krow <= qcol + (i * qb - j * sb), sc, -jnp.inf)
        logits = _dotg(ckvn_ref[kr, :], qs_ref[...], NT)
        d0 = i - 2 * j
        for h in range(B_HEADS):
            lg = logits[:, h * qb:(h + 1) * qb]
            lg_ref[h, j, 0:qb, :] = lg[0:qb] + bias_ref[d0, h]
            lg_ref[h, j, qb:sb, :] = lg[qb:sb] + bias_ref[jnp.maximum(d0 - 1, 0), h]
        return carry

    lax.fori_loop(0, nsb, score_body, 0)

    sign = jnp.int32(-2 ** 31)

    def key_to_float(u):
        k = u ^ sign
        bits = k ^ ((k >> 31) & jnp.int32(0x7FFFFFFF))
        return pltpu.bitcast(bits, F32)

    def count(pred):
        def cnt(j, acc):
            return acc + fold(pred(j), jnp.add)
        acc = lax.fori_loop(0, nsb, cnt, jnp.zeros((8, qb), F32))
        return jnp.sum(acc, 0, keepdims=True)

    def bit_body(t, u):
        cand = u | jnp.left_shift(jnp.int32(1), 31 - t)
        cf = key_to_float(cand)
        c = count(lambda j: jnp.where(sc_ref[j] >= cf, 1.0, 0.0))
        return jnp.where(c >= keep, cand, u)

    u = lax.fori_loop(0, 32, bit_body, jnp.zeros((1, qb), I32))
    u = jnp.maximum(u ^ sign, jnp.int32(0x007FFFFF - 2 ** 31)) ^ sign
    thr = key_to_float(u)
    n_gt = count(lambda j: jnp.where(sc_ref[j] > thr, 1.0, 0.0))
    n_ge = count(lambda j: jnp.where(sc_ref[j] >= thr, 1.0, 0.0))
    need = keep - n_gt

    nbits = int(math.ceil(math.log2(seq))) + 1

    def cut_search():
        def cut_body(t, cut):
            cand = cut | jnp.left_shift(jnp.int32(1), nbits - 1 - t)
            c = count(lambda j: jnp.where(sc_ref[j] == thr,
                                          jnp.where(krow < cand - j * sb, 1.0, 0.0), 0.0))
            return jnp.where(c <= need, cand, cut)

        return lax.fori_loop(0, nbits, cut_body, jnp.zeros((1, qb), I32))

    surplus = jnp.max(n_ge) > keep
    cut = lax.cond(surplus, cut_search, lambda: jnp.full((1, qb), 2 ** nbits - 1, I32))
    cut = jnp.minimum(cut, i * qb + 1 + _iota((1, qb), 1))

    def mask_body(j, mx):
        sc = sc_ref[j]
        sel = jnp.where(sc > thr, 1.0, jnp.where(sc == thr, jnp.where(krow < cut - j * sb, 1.0, 0.0), 0.0))
        out = []
        for h in range(B_HEADS):
            s = jnp.where(sel > 0.5, lg_ref[h, j], NEG)
            lg_ref[h, j] = s
            out.append(jnp.maximum(mx[h], fold(s, jnp.maximum)))
        return tuple(out)

    mx = lax.fori_loop(0, nsb, mask_body, tuple(jnp.full((8, qb), NEG, F32) for _ in range(B_HEADS)))
    ms = [jnp.max(m, 0, keepdims=True) for m in mx]

    def att_body(j, carry):
        ls, acc = carry
        ps, new_ls = [], []
        for h in range(B_HEADS):
            pr = jnp.exp(lg_ref[h, j] - ms[h])
            new_ls.append(ls[h] + fold(pr, jnp.add))
            ps.append(pr.astype(BF16))
        upd = _dot(ckvt_ref[j], jnp.concatenate(ps, 1))
        return tuple(new_ls), acc + upd

    init = (tuple(jnp.zeros((8, qb), F32) for _ in range(B_HEADS)),
            jnp.zeros((B_LATENT, B_HEADS * qb), F32))
    ls, acc = lax.fori_loop(0, nsb, att_body, init)
    ls = [jnp.sum(l, 0, keepdims=True) for l in ls]
    o_t = (acc / jnp.concatenate(ls, 1)).astype(BF16)
    y_t = jnp.concatenate([_dot(wuvt_ref[h], o_t[:, h * qb:(h + 1) * qb]) for h in range(B_HEADS)], 0)
    o_ref[...] = y_t.T


def _mixer_b(h, gain4, wuv_t, bias, bsz, seq):
    qb = Q_BLOCK
    assert seq % 256 == 0
    nblk = seq // qb
    keep = min(TOPK_MAX, seq // TOPK_DIV)

    def qpiece(name):
        off, w = _PIECE_OFF[name]
        return pl.BlockSpec((qb, w * LANES), lambda b, i, o=off // w: (b * nblk + i, o))

    def kpiece(name):
        off, w = _PIECE_OFF[name]
        return pl.BlockSpec((seq, w * LANES), lambda b, i, o=off // w: (b, o))

    return pl.pallas_call(
        functools.partial(_mixer_b_kernel, seq=seq, keep=keep),
        grid=(bsz, nblk),
        in_specs=[qpiece("b_q"), qpiece("b_iq"), qpiece("b_iw"), kpiece("b_ckv4"), kpiece("b_ik8"),
                  pl.BlockSpec(gain4.shape, lambda b, i: (0, 0)),
                  pl.BlockSpec(wuv_t.shape, lambda b, i: (0, 0, 0)),
                  pl.BlockSpec(bias.shape, lambda b, i: (0, 0, 0, 0))],
        out_specs=pl.BlockSpec((qb, BRANCH_WIDTH), lambda b, i: (b * nblk + i, 0)),
        out_shape=jax.ShapeDtypeStruct((bsz * seq, BRANCH_WIDTH), F32),
        scratch_shapes=[pltpu.VMEM((seq, BRANCH_WIDTH), BF16),
                        pltpu.VMEM((nblk // 2, B_LATENT, 2 * qb), BF16),
                        pltpu.VMEM((seq, BRANCH_WIDTH), BF16),
                        pltpu.VMEM((nblk // 2, 2 * qb, qb), F32),
                        pltpu.VMEM((B_HEADS, nblk // 2, 2 * qb, qb), F32),
                        pltpu.VMEM((IDX_HEADS * qb, BRANCH_WIDTH), BF16),
                        pltpu.VMEM((B_HEADS * qb, BRANCH_WIDTH), BF16)],
        compiler_params=_params("parallel", "arbitrary"),
        name="mixer_b",
    )(h, h, h, h, h, gain4, wuv_t, bias)


def _mixer_c_kernel(q_ref, k_ref, v_ref, g_ref, a_ref, aup_ref, abias_ref, gain_ref, o_ref, *, seq):
    ch = C_CHUNK
    kw = C_HEADS * C_KEY_DIM
    vw = C_HEADS * C_VAL_DIM
    tri = jnp.where(_iota((ch, ch), 1) <= _iota((ch, ch), 0), 1.0, 0.0).astype(BF16)
    causal = _iota((C_HEADS * ch, ch), 1) <= (_iota((C_HEADS * ch, ch), 0) % ch)
    klane_head = _iota((C_HEADS * ch, kw), 1) // C_KEY_DIM
    krow_head = _iota((C_HEADS * ch, kw), 0) // ch
    vlane_head = _iota((ch, vw), 1) // C_VAL_DIM
    st_mask = (_iota((vw, kw), 0) // C_VAL_DIM) == (_iota((vw, kw), 1) // C_KEY_DIM)
    gmat = _group_matrix(vw, C_VAL_DIM, 1.0 / C_VAL_DIM)
    aup = aup_ref[...]

    def body(c, st):
        rows = pl.ds(pl.multiple_of(c * ch, ch), ch)
        qc = q_ref[rows, :] * (C_KEY_DIM ** -0.5)
        kc = k_ref[rows, :]
        vc = v_ref[rows, :]
        z = _dot3(a_ref[rows, :], aup) + abias_ref[...]
        log_a = -_softplus(-z) / C_GATE_TAU
        cum = _dot_lhs_exact(tri, log_a)
        last = cum[ch - 1:ch, :]
        q_dec = qc * jnp.exp(cum)
        k_inv = (kc * jnp.exp(-cum)).astype(BF16)
        k_dec = (kc * jnp.exp(last - cum)).astype(BF16)
        vb = vc.astype(BF16)
        qd_b = q_dec.astype(BF16)
        q_stack = jnp.where(klane_head == krow_head, jnp.concatenate([q_dec] * C_HEADS, 0), 0.0)
        att = jnp.where(causal, _dotg(q_stack.astype(BF16), k_inv, NT), 0.0)
        full = _dot(att.astype(BF16), vb)
        o = _dotg(qd_b, st.astype(BF16), NT)
        for h in range(C_HEADS):
            o = o + jnp.where(vlane_head == h, full[h * ch:(h + 1) * ch], 0.0)
        upd = _dotg(vb, k_dec, TN)
        st = st * jnp.exp(last) + jnp.where(st_mask, upd, 0.0)
        ms = _dot_rhs_exact(o * o, gmat)
        o = o * lax.rsqrt(ms + 1e-6) * gain_ref[...]
        g = g_ref[rows, :]
        o_ref[rows, :] = g * _sigmoid(g) * o
        return st

    lax.fori_loop(0, seq // ch, body, jnp.zeros((vw, kw), F32))


def _mixer_c(h, aup_pad, abias, gain, bsz, seq):
    assert seq % C_CHUNK == 0

    def piece(name):
        off, w = _PIECE_OFF[name]
        return pl.BlockSpec((seq, w * LANES), lambda b, o=off // w: (b, o))

    def const(a):
        return pl.BlockSpec(a.shape, lambda b: (0, 0))

    return pl.pallas_call(
        functools.partial(_mixer_c_kernel, seq=seq),
        grid=(bsz,),
        in_specs=[piece("c_q"), piece("c_k"), piece("c_v"), piece("c_g"), piece("c_a"),
                  const(aup_pad), const(abias), const(gain)],
        out_specs=pl.BlockSpec((seq, BRANCH_WIDTH), lambda b: (b, 0)),
        out_shape=jax.ShapeDtypeStruct((bsz * seq, BRANCH_WIDTH), F32),
        compiler_params=_params("parallel"),
        name="mixer_c",
    )(h, h, h, h, h, aup_pad, abias, gain)


def _mixer_d_kernel(r_ref, k_ref, v_ref, low_ref, mu_ref, mulow_ref, w2_ref, a2_ref, g2_ref, vec_ref,
                    o_ref, *, seq, nsub):
    ch = D_CHUNK
    bw = BRANCH_WIDTH
    nh = D_HEADS
    hs = D_HEAD_SIZE
    r_i = _iota((bw, bw), 0)
    c_i = _iota((bw, bw), 1)
    same = (r_i // hs) == (c_i // hs)
    strict = c_i < r_i
    incl = c_i <= r_i
    eye = r_i == c_i
    ones_bd = _group_matrix(bw, hs, 1.0)
    avg_bd = _group_matrix(bw, hs, 1.0 / hs)
    rows = nsub * ch
    tr_i = _iota((rows, rows), 0)
    tc_i = _iota((rows, rows), 1)
    tri = jnp.where(jnp.logical_and(tr_i // ch == tc_i // ch, tc_i <= tr_i), 1.0, 0.0).astype(BF16)
    row0 = _iota((rows, bw), 0) == 0
    row0_low = _iota((rows, LANES), 0) == 0
    vec = vec_ref[...]
    w0, a0, k_k, k_a, r_k, gn_w, gn_b = (vec[n:n + 1, :] for n in range(7))
    mu = mu_ref[...]
    w2, a2, g2 = w2_ref[...], a2_ref[...], g2_ref[...]

    def wide(x):
        return jnp.where(same, jnp.concatenate([x] * nh, 0), 0.0)

    def shifted(ref, start, first_row, m):
        cur = ref[pl.ds(start, rows), :]
        last8 = ref[pl.ds(pl.multiple_of(jnp.maximum(start - 8, 0), 8), 8), :]
        prev_row = last8[7:8, :] * jnp.where(start > 0, 1.0, 0.0)
        prev = jnp.where(first_row, prev_row, pltpu.roll(cur, 1, 0))
        return cur + (prev - cur) * m

    def body(c, st):
        start = pl.multiple_of(c * rows, rows)
        r = shifted(r_ref, start, row0, mu[0:1, :])
        k = shifted(k_ref, start, row0, mu[1:2, :])
        v = shifted(v_ref, start, row0, mu[2:3, :])
        low = shifted(low_ref, start, row0_low, mulow_ref[...])
        w_raw = -_softplus(-(w0 + _dot3(jnp.tanh(low), w2))) - 0.5
        lw = -jnp.exp(w_raw)
        a = _sigmoid(a0 + _dot3(low, a2))
        g = _dot3(_sigmoid(low), g2)
        kk = k * k_k
        kk = kk / jnp.maximum(jnp.sqrt(_dot_rhs_exact(kk * kk, ones_bd)), 1e-12)
        k2 = k * (1.0 + (a - 1.0) * k_a)
        bonus = _dot_rhs_exact(r * k2 * r_k, ones_bd) * v
        b = kk * a

        cum = _dot_lhs_exact(tri, lw)
        e_in = jnp.exp(cum)
        e_inv = jnp.exp(-cum)
        a_dec = -kk * jnp.exp(cum - lw)
        r_dec = r * e_in
        b_inv = b * e_inv
        k_inv = k2 * e_inv
        local = [chunk_terms(*(z[s * ch:(s + 1) * ch] for z in (a_dec, r_dec, b_inv, k_inv, v, e_in)))
                 for s in range(nsub)]
        ys = []
        for qp, y0, gmat, hmat, g_col in local:
            st16 = st.astype(BF16)
            y_w = _dot(qp, st16) + y0
            st = g_col * st + _dot(gmat, st16) + hmat
            y = y_w[0:ch]
            for h in range(1, nh):
                y = y + y_w[h * ch:(h + 1) * ch]
            ys.append(y)
        y = jnp.concatenate(ys, 0)
        mean = _dot_rhs_exact(y, avg_bd)
        yc = y - mean
        var = _dot_rhs_exact(yc * yc, avg_bd)
        yn = yc * lax.rsqrt(var + D_GN_EPS) * gn_w + gn_b
        o_ref[pl.ds(start, rows), :] = (yn + bonus) * g
        return st

    def chunk_terms(a_dec, r_dec, b_inv, k_inv, v, e_in):
        g_last = e_in[ch - 1:ch, :]
        xa = wide(a_dec).astype(BF16)
        xr_f = wide(r_dec)
        xr = xr_f.astype(BF16)
        yb = wide(b_inv).astype(BF16)
        yk = wide(k_inv).astype(BF16)
        bh = wide(b_inv * g_last).astype(BF16)
        kh = wide(k_inv * g_last).astype(BF16)
        vw = wide(v).astype(BF16)

        a_ab = jnp.where(strict, _dotg(xa, yb, NT), 0.0)
        a_ak = jnp.where(strict, _dotg(xa, yk, NT), 0.0).astype(BF16)
        m_rb = jnp.where(incl, _dotg(xr, yb, NT), 0.0).astype(BF16)
        m_rk = jnp.where(incl, _dotg(xr, yk, NT), 0.0).astype(BF16)

        t_inv = jnp.where(eye, 1.0, a_ab)
        apow = a_ab
        for _ in range(int(math.log2(ch)) - 1):
            ab16 = apow.astype(BF16)
            apow = _dot(ab16, ab16)
            t_inv = t_inv + _dot(t_inv.astype(BF16), apow.astype(BF16))
        t16 = t_inv.astype(BF16)

        atp = _dot(t16, xa).astype(BF16)
        vp = _dot(t16, _dot(a_ak, vw).astype(BF16)).astype(BF16)
        qp = (xr_f + _dot(m_rb, atp)).astype(BF16)
        y0 = _dot(m_rb, vp) + _dot(m_rk, vw)
        gmat = _dotg(bh, atp, TN).astype(BF16)
        hmat = _dotg(bh, vp, TN) + _dotg(kh, vw, TN)
        g_col = jnp.sum(jnp.where(eye, g_last, 0.0), -1, keepdims=True)
        return qp, y0, gmat, hmat, g_col

    lax.fori_loop(0, seq // rows, body, jnp.zeros((bw, bw), F32))


def _mixer_d(h, mu3, mulow, w2p, a2p, g2p, vec, bsz, seq, nsub=2):
    assert seq % (nsub * D_CHUNK) == 0 and D_CHUNK == D_HEAD_SIZE

    def piece(name):
        off, w = _PIECE_OFF[name]
        return pl.BlockSpec((seq, w * LANES), lambda b, o=off // w: (b, o))

    def const(a):
        return pl.BlockSpec(a.shape, lambda b: (0, 0))

    return pl.pallas_call(
        functools.partial(_mixer_d_kernel, seq=seq, nsub=nsub),
        grid=(bsz,),
        in_specs=[piece("d_r"), piece("d_k"), piece("d_v"), piece("d_low"),
                  const(mu3), const(mulow), const(w2p), const(a2p), const(g2p), const(vec)],
        out_specs=pl.BlockSpec((seq, BRANCH_WIDTH), lambda b: (b, 0)),
        out_shape=jax.ShapeDtypeStruct((bsz * seq, BRANCH_WIDTH), F32),
        compiler_params=_params("parallel"),
        name="mixer_d",
    )(h, h, h, h, mu3, mulow, w2p, a2p, g2p, vec)


def _merge_kernel(x_ref, ya_ref, yb_ref, yc_ref, yd_ref, wg_ref, wb_ref, wo_ref, g_ref, b_ref, o_ref,
                  *, alpha):
    x = x_ref[...]
    xb = x.astype(BF16)
    merged = None
    for n, y_ref in enumerate((ya_ref, yb_ref, yc_ref, yd_ref)):
        gate = _sigmoid(_dot(xb, wg_ref[:, n * D_MODEL:(n + 1) * D_MODEL]))
        term = gate * _dot(y_ref[...].astype(BF16), wb_ref[n])
        merged = term if merged is None else merged + term
    z = alpha * x + _dot(merged.astype(BF16), wo_ref[...])
    o_ref[...] = _layer_norm(z, g_ref[...], b_ref[...])


def _merge(x2d, ys, wg, wb, wo, g, b, alpha, tm):
    n, d = x2d.shape

    def const(a):
        nd = a.ndim
        return pl.BlockSpec(a.shape, lambda i: (0,) * nd)

    yspec = pl.BlockSpec((tm, BRANCH_WIDTH), lambda i: (i, 0))
    return pl.pallas_call(
        functools.partial(_merge_kernel, alpha=alpha),
        grid=(n // tm,),
        in_specs=[pl.BlockSpec((tm, d), lambda i: (i, 0)), yspec, yspec, yspec, yspec,
                  const(wg), const(wb), const(wo), const(g), const(b)],
        out_specs=pl.BlockSpec((tm, d), lambda i: (i, 0)),
        out_shape=jax.ShapeDtypeStruct((n, d), F32),
        compiler_params=_params("parallel"),
        name="merge",
    )(x2d, *ys, wg, wb, wo, g, b)


def _moe_kernel(x_ref, ltri_ref, wr_ref, br_ref, wg_ref, wu_ref, wd_ref, g_ref, b_ref, o_ref,
                xs_ref, gs_ref, ys_ref, pt_ref, seg_ref, *, alpha):
    e = pl.program_id(1)
    tm = x_ref.shape[0]
    nrow = xs_ref.shape[0]
    ck = MOE_CHUNK
    lane = _iota((tm, LANES), 1)

    @pl.when(e == 0)
    def _():
        x = x_ref[...]
        xb = x.astype(BF16)
        ys_ref[...] = jnp.zeros_like(ys_ref)
        logits = _dot3(x, wr_ref[...]) + br_ref[...]
        is_g = jnp.logical_and(lane >= MOE_EXPERTS, lane < MOE_EXPERTS + MOE_GROUPS)
        lg = jnp.where(is_g, logits, -jnp.inf)
        gmax = jnp.max(lg, -1, keepdims=True)
        ptop = 1.0 / jnp.sum(jnp.exp(lg - gmax), -1, keepdims=True)
        gsel = jnp.min(jnp.where(lg == gmax, lane, 2 * LANES), -1, keepdims=True) - MOE_EXPERTS
        in_group = jnp.logical_and(lane < MOE_EXPERTS, lane // MOE_PER_GROUP == gsel)
        le = jnp.where(in_group, logits, -jnp.inf)
        v1 = jnp.max(le, -1, keepdims=True)
        i1 = jnp.min(jnp.where(le == v1, lane, 2 * LANES), -1, keepdims=True)
        le2 = jnp.where(lane == i1, -jnp.inf, le)
        v2 = jnp.max(le2, -1, keepdims=True)
        i2 = jnp.min(jnp.where(le2 == v2, lane, 2 * LANES), -1, keepdims=True)
        e2 = jnp.exp(v2 - v1)
        w1 = ptop / (1.0 + e2)
        gate = jnp.where(lane == i1, w1, jnp.where(lane == i2, w1 * e2, 0.0))

        onehot = jnp.where(lane == gsel, 1.0, 0.0)
        rank = _dot(ltri_ref[...], onehot.astype(BF16))
        cnt = jnp.sum(onehot, 0, keepdims=True).astype(I32)
        nch = jnp.right_shift(cnt + (ck - 1), int(math.log2(ck)))
        lane1 = _iota((1, LANES), 1)
        off = jnp.int32(0)
        off_vec = jnp.zeros((1, LANES), I32)
        for grp in range(MOE_GROUPS):
            n_g = nch[0, grp]
            seg_ref[grp] = off
            seg_ref[MOE_GROUPS + grp] = n_g
            off_vec = jnp.where(lane1 == grp, off, off_vec)
            off = off + n_g * ck
        dest = jnp.sum(onehot * (rank + off_vec.astype(F32)), -1, keepdims=True)
        pt_ref[...] = jnp.where(_iota((tm, nrow), 1) == dest.astype(I32), 1.0, 0.0).astype(BF16)
        dest_row = jnp.broadcast_to(dest, (tm, LANES)).T[0:1, :].astype(I32)
        perm = jnp.where(_iota((nrow, tm), 0) == dest_row, 1.0, 0.0).astype(BF16)
        xs_ref[...] = _dot(perm, xb).astype(BF16)
        gs_ref[...] = _dot_lhs_exact(perm, gate)

    grp = e // MOE_PER_GROUP
    seg_off = seg_ref[grp]
    seg_chunks = seg_ref[MOE_GROUPS + grp]

    def run_expert(start, size):
        rows = pl.ds(pl.multiple_of(start, ck), size)
        xc = xs_ref[rows, :]
        gcol = jnp.sum(jnp.where(_iota((size, LANES), 1) == e, gs_ref[rows, :], 0.0), -1, keepdims=True)
        hg = _dot(xc, wg_ref[0])
        hu = _dot(xc, wu_ref[0])
        hidden = hg * _sigmoid(hg) * hu * gcol
        ys_ref[rows, :] += _dot(hidden.astype(BF16), wd_ref[0])

    def pair(c, carry):
        run_expert(seg_off + c * (2 * ck), 2 * ck)
        return carry

    lax.fori_loop(0, seg_chunks // 2, pair, 0)

    @pl.when(seg_chunks % 2 == 1)
    def _():
        run_expert(seg_off + (seg_chunks - 1) * ck, ck)

    @pl.when(e == pl.num_programs(1) - 1)
    def _():
        z = alpha * x_ref[...] + _dot(pt_ref[...], ys_ref[...].astype(BF16))
        o_ref[...] = _layer_norm(z, g_ref[...], b_ref[...])


def _moe(x2d, wr, br, wg, wu, wd, g, b, alpha, tm):
    n, d = x2d.shape
    ne = wg.shape[0]
    nrow = tm + MOE_GROUPS * MOE_CHUNK
    ltri = jnp.tril(jnp.ones((tm, tm), BF16), -1)

    def const(a):
        return pl.BlockSpec(a.shape, lambda i, e: (0, 0))

    return pl.pallas_call(
        functools.partial(_moe_kernel, alpha=alpha),
        grid=(n // tm, ne),
        in_specs=[pl.BlockSpec((tm, d), lambda i, e: (i, 0)), const(ltri), const(wr), const(br),
                  pl.BlockSpec((1, d, MOE_HIDDEN), lambda i, e: (e, 0, 0)),
                  pl.BlockSpec((1, d, MOE_HIDDEN), lambda i, e: (e, 0, 0)),
                  pl.BlockSpec((1, MOE_HIDDEN, d), lambda i, e: (e, 0, 0)),
                  const(g), const(b)],
        out_specs=pl.BlockSpec((tm, d), lambda i, e: (i, 0)),
        out_shape=jax.ShapeDtypeStruct((n, d), F32),
        scratch_shapes=[pltpu.VMEM((nrow, d), BF16), pltpu.VMEM((nrow, LANES), F32),
                        pltpu.VMEM((nrow, d), F32), pltpu.VMEM((tm, nrow), BF16),
                        pltpu.SMEM((2 * MOE_GROUPS,), I32)],
        compiler_params=_params("parallel", "arbitrary"),
        name="moe",
    )(x2d, ltri, wr, br, wg, wu, wd, g, b)


def _t5_bucket(dist):
    exact = RPB_BUCKETS // 2
    d = jnp.maximum(dist, 0)
    df = jnp.maximum(d, 1).astype(F32)
    large = exact + (jnp.log(df / exact) / math.log(RPB_MAX_DIST / exact)
                     * (RPB_BUCKETS - exact)).astype(I32)
    large = jnp.minimum(large, RPB_BUCKETS - 1)
    return jnp.where(d < exact, d, large)


def _bias_a(tab):
    qi = jnp.arange(A_BLOCK)[:, None]
    kj = jnp.arange(2 * A_BLOCK)[None, :]
    rdist = qi + A_BLOCK - kj
    out = []
    for window, dilation in A_PATTERNS:
        in_band = (rdist >= 0) & (rdist <= window // dilation)
        bias = jnp.transpose(tab[_t5_bucket(rdist * dilation)], (2, 0, 1))
        out.append(jnp.where(in_band[None], bias, NEG))
    return jnp.stack(out).astype(F32)


def _bias_b(tab, nblk):
    qi = jnp.arange(Q_BLOCK)[:, None]
    kj = jnp.arange(Q_BLOCK)[None, :]
    delta = jnp.arange(nblk)[:, None, None] * Q_BLOCK
    bias = tab[_t5_bucket(delta + qi - kj)]
    return jnp.transpose(bias, (0, 3, 2, 1)).astype(F32)


def _in_offsets():
    splits = (("a_q", 256), ("a_k", 256), ("a_v", 256), ("b_q", 256), ("b_ckv", 64), ("b_iq", 256),
              ("b_ik", 32), ("b_iw", 8), ("c_q", 128), ("c_k", 128), ("c_v", 256), ("c_a", 16),
              ("c_g", 256), ("d", 832), ("gate", 4096))
    off, out = 0, {}
    for name, w in splits:
        out[name] = (off, w)
        off += w
    return out


def _proj_columns():
    src = _in_offsets()
    d0 = src["d"][0]
    named = {
        "a_q": np.arange(256) + src["a_q"][0], "a_k": np.arange(256) + src["a_k"][0],
        "a_v": np.arange(256) + src["a_v"][0], "b_q": np.arange(256) + src["b_q"][0],
        "b_iq": np.arange(256) + src["b_iq"][0],
        "b_ckv4": np.tile(np.arange(64) + src["b_ckv"][0], B_HEADS),
        "b_ik8": np.tile(np.arange(32) + src["b_ik"][0], IDX_HEADS),
        "c_v": np.arange(256) + src["c_v"][0], "c_g": np.arange(256) + src["c_g"][0],
        "d_r": np.arange(256) + d0, "d_k": np.arange(256) + d0 + 256, "d_v": np.arange(256) + d0 + 512,
        "c_q": np.arange(128) + src["c_q"][0], "c_k": np.arange(128) + src["c_k"][0],
        "b_iw": np.arange(8) + src["b_iw"][0], "c_a": np.arange(16) + src["c_a"][0],
        "d_low": np.arange(64) + d0 + 768, "pad": np.arange(0),
    }
    cols = []
    for name, w in _PIECES:
        c = named[name]
        cols.append(np.concatenate([c, -np.ones(w * LANES - len(c), np.int64)]))
    return np.concatenate(cols)


def _proj_weight(w):
    cols = _proj_columns()
    cuts = [0] + [k for k in range(1, len(cols))
                  if (cols[k] < 0) != (cols[k - 1] < 0) or (cols[k] >= 0 and cols[k] != cols[k - 1] + 1)]
    cuts.append(len(cols))
    parts = []
    for a, b in zip(cuts[:-1], cuts[1:]):
        if cols[a] < 0:
            parts.append(jnp.zeros((w.shape[0], b - a), BF16))
        else:
            parts.append(w[:, int(cols[a]):int(cols[a]) + (b - a)].astype(BF16))
    return jnp.concatenate(parts, 1)


def _pad_rows(w, first, total):
    return jnp.zeros((total, w.shape[1]), w.dtype).at[first:first + w.shape[0]].set(w)


def kernel(x, rpb_table, w_in, b_kv_gain, b_w_uv, c_a_up, c_a_bias, c_norm_gain, d_mu, d_w0, d_w2,
           d_a0, d_a2, d_g2, d_k_k, d_k_a, d_r_k, d_gn_w, d_gn_b, w_branch, w_out, ln_g, ln_b,
           router_g, router_g_bias, router_e, router_e_bias, moe_w_gate, moe_w_up, moe_w_down):
    bsz, seq, d_model = x.shape
    depth = w_in.shape[0]
    n = bsz * seq
    alpha = (2 * depth) ** 0.25
    gate_off = _in_offsets()["gate"][0]
    bias_a = _bias_a(rpb_table[:, :A_HEADS])
    bias_b = _bias_b(rpb_table[:, A_HEADS:], seq // Q_BLOCK)
    tm_proj = math.gcd(n, 1024)
    tm_merge = math.gcd(n, 512)
    tm_moe = math.gcd(n, 1024)

    x2d = x.reshape(n, d_model)
    for l in range(depth):
        h = _project(x2d, _proj_weight(w_in[l]), tm_proj, 768)

        ya = _mixer_a(h, bias_a, bsz, seq)

        gain4 = jnp.tile(b_kv_gain[l], B_HEADS)[None, :]
        wuv_t = jnp.transpose(b_w_uv[l], (0, 2, 1)).astype(BF16)
        yb = _mixer_b(h, gain4, wuv_t, bias_b, bsz, seq)

        aup_pad = _pad_rows(c_a_up[l], 0, LANES)
        yc = _mixer_c(h, aup_pad, c_a_bias[l][None, :], c_norm_gain[l][None, :], bsz, seq)

        mu = d_mu[l]
        mu3 = mu[:3 * BRANCH_WIDTH].reshape(3, BRANCH_WIDTH)
        mulow = jnp.zeros((1, LANES), F32).at[0, :D_DECAY_RANK + D_ICLR_RANK + D_GATE_RANK].set(
            mu[3 * BRANCH_WIDTH:])
        w2p = _pad_rows(d_w2[l], 0, LANES)
        a2p = _pad_rows(d_a2[l], D_DECAY_RANK, LANES)
        g2p = _pad_rows(d_g2[l], D_DECAY_RANK + D_ICLR_RANK, LANES)
        vec = jnp.stack([d_w0[l], d_a0[l], d_k_k[l], d_k_a[l], d_r_k[l], d_gn_w[l], d_gn_b[l],
                         jnp.zeros_like(d_w0[l])])
        yd = _mixer_d(h, mu3, mulow, w2p, a2p, g2p, vec, bsz, seq)

        wg = w_in[l][:, gate_off:gate_off + N_BRANCHES * d_model].astype(BF16)
        x2d = _merge(x2d, (ya, yb, yc, yd), wg, w_branch[l].astype(BF16), w_out[l].astype(BF16),
                     ln_g[l, 0][None, :], ln_b[l, 0][None, :], alpha, tm_merge)

        wr = jnp.zeros((d_model, LANES), F32)
        wr = wr.at[:, :MOE_EXPERTS].set(router_e[l]).at[:, MOE_EXPERTS:MOE_EXPERTS + MOE_GROUPS].set(
            router_g[l])
        br = jnp.zeros((1, LANES), F32)
        br = br.at[0, :MOE_EXPERTS].set(router_e_bias[l]).at[0, MOE_EXPERTS:MOE_EXPERTS + MOE_GROUPS].set(
            router_g_bias[l])
        x2d = _moe(x2d, wr, br, moe_w_gate[l].astype(BF16), moe_w_up[l].astype(BF16),
                   moe_w_down[l].astype(BF16), ln_g[l, 1][None, :], ln_b[l, 1][None, :], alpha, tm_moe)
    return x2d.reshape(bsz, seq, d_model)
```

```python
import functools
import math

import jax
import jax.numpy as jnp
import numpy as np
from jax import lax
from jax.experimental import pallas as pl
from jax.experimental.pallas import tpu as pltpu

F32 = jnp.float32
BF16 = jnp.bfloat16
I32 = jnp.int32

D_MODEL = 1024
N_BRANCHES = 4
BRANCH_WIDTH = 256
HEAD_DIM = 64
A_HEADS = 4
A_PATTERNS = ((128, 1), (512, 4), (2048, 16))
A_BLOCK = 128
B_HEADS = 4
B_LATENT = 64
IDX_HEADS = 8
IDX_DIM = 32
TOPK_MAX = 256
TOPK_DIV = 4
Q_BLOCK = 128
C_HEADS = 4
C_KEY_DIM = 32
C_VAL_DIM = 64
C_GATE_RANK = 16
C_GATE_TAU = 16.0
C_CHUNK = 64
D_HEADS = 4
D_HEAD_SIZE = 64
D_DECAY_RANK = 16
D_ICLR_RANK = 16
D_GATE_RANK = 32
D_GN_EPS = 64e-5
D_CHUNK = 64
RPB_BUCKETS = 32
RPB_MAX_DIST = 2048
MOE_GROUPS = 4
MOE_PER_GROUP = 8
MOE_EXPERTS = 32
MOE_HIDDEN = 256
MOE_CHUNK = 128
LN_EPS = 1e-5

LANES = 128
NEG = -1e30
VMEM_LIMIT = 56 * 1024 * 1024

NT = (((1,), (1,)), ((), ()))
TN = (((0,), (0,)), ((), ()))

_PIECES = (
    ("a_q", 2), ("a_k", 2), ("a_v", 2), ("b_q", 2), ("b_iq", 2), ("b_ckv4", 2), ("b_ik8", 2),
    ("c_v", 2), ("c_g", 2), ("d_r", 2), ("d_k", 2), ("d_v", 2),
    ("c_q", 1), ("c_k", 1), ("b_iw", 1), ("c_a", 1), ("d_low", 1), ("pad", 1),
)
_PIECE_OFF = {}
_off = 0
for _n, _w in _PIECES:
    _PIECE_OFF[_n] = (_off, _w)
    _off += _w
PROJ_WIDTH = _off * LANES


def _dot(a, b):
    return jnp.dot(a, b, preferred_element_type=F32)


def _dotg(a, b, dims):
    return lax.dot_general(a, b, dims, preferred_element_type=F32)


def _split2(x):
    hi = x.astype(BF16)
    lo = (x - hi.astype(F32)).astype(BF16)
    return hi, lo


def _dot_rhs_exact(x, m):
    hi, lo = _split2(x)
    return _dot(hi, m) + _dot(lo, m)


def _dot_lhs_exact(m, x):
    hi, lo = _split2(x)
    return _dot(m, hi) + _dot(m, lo)


def _dot3(x, w):
    xh, xl = _split2(x)
    wh, wl = _split2(w)
    return _dot(xh, wh) + (_dot(xl, wh) + _dot(xh, wl))


def _sigmoid(x):
    return 1.0 / (1.0 + jnp.exp(-x))


def _softplus(x):
    return jnp.maximum(x, 0.0) + jnp.log(1.0 + jnp.exp(-jnp.abs(x)))


def _iota(shape, axis):
    return lax.broadcasted_iota(I32, shape, axis)


def _group_matrix(n, group, value, dtype=BF16):
    r = _iota((n, n), 0) // group
    c = _iota((n, n), 1) // group
    return jnp.where(r == c, value, 0.0).astype(dtype)


def _layer_norm(z, g, b):
    mu = jnp.mean(z, -1, keepdims=True)
    zc = z - mu
    var = jnp.mean(zc * zc, -1, keepdims=True)
    return zc * lax.rsqrt(var + LN_EPS) * g + b


def _params(*sem):
    return pltpu.CompilerParams(dimension_semantics=sem, vmem_limit_bytes=VMEM_LIMIT)


def _proj_kernel(x_ref, w_ref, o_ref):
    o_ref[...] = _dot(x_ref[...].astype(BF16), w_ref[...])


def _project(x2d, w, tm, tn):
    n, k = x2d.shape
    wt = w.shape[1]
    return pl.pallas_call(
        _proj_kernel,
        grid=(n // tm, wt // tn),
        in_specs=[pl.BlockSpec((tm, k), lambda i, j: (i, 0)),
                  pl.BlockSpec((k, tn), lambda i, j: (0, j))],
        out_specs=pl.BlockSpec((tm, tn), lambda i, j: (i, j)),
        out_shape=jax.ShapeDtypeStruct((n, wt), F32),
        compiler_params=_params("parallel", "parallel"),
        name="proj",
    )(x2d, w)


def _mixer_a_kernel(q0_ref, q1_ref, k0_ref, k1_ref, v0_ref, v1_ref, bias_ref, o_ref, op_ref, lse_ref,
                    *, seq, dils):
    q_refs, k_refs, v_refs = (q0_ref, q1_ref), (k0_ref, k1_ref), (v0_ref, v1_ref)
    ab = A_BLOCK
    lane = _iota((ab, LANES), 1)
    upper = lane >= HEAD_DIM
    col = _iota((ab, 2 * ab), 1)
    scale = HEAD_DIM ** -0.5

    for p, d in enumerate(dils):
        per_res = seq // (d * ab)

        def body(idx, carry, p=p, d=d, per_res=per_res):
            r = idx // per_res
            b = idx % per_res
            start = b * (ab * d) + r
            pstart = jnp.maximum(start - ab * d, r)
            if d == 1:
                rows, prow = pl.ds(pl.multiple_of(start, ab), ab), pl.ds(pl.multiple_of(pstart, ab), ab)
            else:
                rows, prow = pl.ds(start, ab, stride=d), pl.ds(pstart, ab, stride=d)
            first_pen = jnp.where(b == 0, NEG, 0.0)
            pen = jnp.where(col < ab, first_pen, 0.0)
            for hp in range(2):
                q_ref, k_ref, v_ref = q_refs[hp], k_refs[hp], v_refs[hp]
                q = q_ref[rows, :]
                kb = jnp.concatenate([k_ref[prow, :], k_ref[rows, :]], 0).astype(BF16)
                vb = jnp.concatenate([v_ref[prow, :], v_ref[rows, :]], 0).astype(BF16)
                outs, lses = [], []
                for hh in range(2):
                    hm = upper if hh else jnp.logical_not(upper)
                    qm = jnp.where(hm, q, 0.0).astype(BF16)
                    s = _dotg(qm, kb, NT) * scale + bias_ref[p, 2 * hp + hh] + pen
                    mx = jnp.max(s, -1, keepdims=True)
                    e = jnp.exp(s - mx)
                    den = jnp.sum(e, -1, keepdims=True)
                    outs.append(_dot(e.astype(BF16), vb) / den)
                    lses.append(mx + jnp.log(den))
                op_ref[2 * p + hp, rows, :] = jnp.where(upper, outs[1], outs[0])
                lse_ref[2 * p + hp, rows, :] = jnp.where(upper, lses[1], lses[0])
            return carry

        lax.fori_loop(0, seq // ab, body, 0)

    cr = 256

    def combine(c, carry):
        rows = pl.ds(pl.multiple_of(c * cr, cr), cr)
        for hp in range(2):
            ls = [lse_ref[2 * p + hp, rows, :] for p in range(len(dils))]
            mx = functools.reduce(jnp.maximum, ls)
            es = [jnp.exp(l - mx) for l in ls]
            num = functools.reduce(lambda a, b: a + b,
                                   [e * op_ref[2 * p + hp, rows, :] for p, e in enumerate(es)])
            o_ref[rows, hp * LANES:(hp + 1) * LANES] = num / functools.reduce(lambda a, b: a + b, es)
        return carry

    lax.fori_loop(0, seq // cr, combine, 0)


def _mixer_a(h, bias, bsz, seq):
    dils = tuple(d for _, d in A_PATTERNS)
    for w, d in A_PATTERNS:
        assert w // d == A_BLOCK and seq % (d * A_BLOCK) == 0
    npat = len(dils)

    def pieces(name):
        off, w = _PIECE_OFF[name]
        return [pl.BlockSpec((seq, LANES), lambda b, o=off + u: (b, o)) for u in range(w)]

    return pl.pallas_call(
        functools.partial(_mixer_a_kernel, seq=seq, dils=dils),
        grid=(bsz,),
        in_specs=pieces("a_q") + pieces("a_k") + pieces("a_v")
        + [pl.BlockSpec(bias.shape, lambda b: (0, 0, 0, 0))],
        out_specs=pl.BlockSpec((seq, BRANCH_WIDTH), lambda b: (b, 0)),
        out_shape=jax.ShapeDtypeStruct((bsz * seq, BRANCH_WIDTH), F32),
        scratch_shapes=[pltpu.VMEM((2 * npat, seq, LANES), F32),
                        pltpu.VMEM((2 * npat, seq, LANES), F32)],
        compiler_params=_params("parallel"),
        name="mixer_a",
    )(h, h, h, h, h, h, bias)


def _mixer_b_kernel(q_ref, iq_ref, iw_ref, ckv_ref, ik_ref, gain_ref, wuvt_ref, bias_ref, o_ref,
                    ckvn_ref, ckvt_ref, ikb_ref, sc_ref, lg_ref, iqs_ref, qs_ref, *, seq, keep):
    qb = Q_BLOCK
    sb = 2 * qb
    i = pl.program_id(1)

    @pl.when(i == 0)
    def _():
        gmat = _group_matrix(BRANCH_WIDTH, B_LATENT, 1.0 / B_LATENT)

        def prep(c, carry):
            rows = pl.ds(pl.multiple_of(c * sb, sb), sb)
            x = ckv_ref[rows, :]
            ms = _dot_rhs_exact(x * x, gmat)
            xn = x * lax.rsqrt(ms + 1e-6) * gain_ref[...]
            ckvn_ref[rows, :] = xn.astype(BF16)
            ckvt_ref[c] = xn.T[0:B_LATENT, :].astype(BF16)
            ikb_ref[rows, :] = ik_ref[rows, :].astype(BF16)
            return carry

        lax.fori_loop(0, seq // sb, prep, 0)

    lane = _iota((qb, BRANCH_WIDTH), 1)
    iq = iq_ref[...] * (IDX_DIM ** -0.5)
    for h in range(IDX_HEADS):
        iqs_ref[h * qb:(h + 1) * qb, :] = jnp.where(lane // IDX_DIM == h, iq, 0.0).astype(BF16)
    q = q_ref[...] * (B_LATENT ** -0.5)
    for h in range(B_HEADS):
        qs_ref[h * qb:(h + 1) * qb, :] = jnp.where(lane // B_LATENT == h, q, 0.0).astype(BF16)
    iw_t = (iw_ref[...] * (IDX_HEADS ** -0.5)).T
    krow = _iota((sb, qb), 0)
    qcol = _iota((sb, qb), 1)
    nsb = (i + 2) // 2

    def fold(x, op):
        parts = [x[r:r + 8] for r in range(0, sb, 8)]
        while len(parts) > 1:
            parts = [op(parts[k], parts[k + 1]) for k in range(0, len(parts), 2)]
        return parts[0]

    def for_steps(body, init):
        def pair(p, carry):
            return body(2 * p + 1, body(2 * p, carry))

        carry = lax.fori_loop(0, nsb // 2, pair, init)
        return lax.cond(nsb % 2 == 1, lambda c: body(nsb - 1, c), lambda c: c, carry)

    def score_body(j, carry):
        kr = pl.ds(pl.multiple_of(j * sb, sb), sb)
        rel = _dotg(ikb_ref[kr, :], iqs_ref[...], NT)
        sc = jnp.maximum(rel[:, 0:qb], 0.0) * iw_t[0:1, :]
        for h in range(1, IDX_HEADS):
            sc = sc + jnp.maximum(rel[:, h * qb:(h + 1) * qb], 0.0) * iw_t[h:h + 1, :]
        sc_ref[j] = jnp.where(krow <= qcol + (i * qb - j * sb), sc, -jnp.inf)
        logits = _dotg(ckvn_ref[kr, :], qs_ref[...], NT)
        d0 = i - 2 * j
        for h in range(B_HEADS):
            lg = logits[:, h * qb:(h + 1) * qb]
            lg_ref[h, j, 0:qb, :] = lg[0:qb] + bias_ref[d0, h]
            lg_ref[h, j, qb:sb, :] = lg[qb:sb] + bias_ref[jnp.maximum(d0 - 1, 0), h]
        return carry

    for_steps(score_body, 0)

    sign = jnp.int32(-2 ** 31)

    def key_to_float(u):
        k = u ^ sign
        bits = k ^ ((k >> 31) & jnp.int32(0x7FFFFFFF))
        return pltpu.bitcast(bits, F32)

    def count(pred):
        def cnt(j, acc):
            return acc + fold(pred(j), jnp.add)
        acc = for_steps(cnt, jnp.zeros((8, qb), F32))
        return jnp.sum(acc, 0, keepdims=True)

    def bit_body(t, u):
        cand = u | jnp.left_shift(jnp.int32(1), 31 - t)
        cf = key_to_float(cand)
        c = count(lambda j: jnp.where(sc_ref[j] >= cf, 1.0, 0.0))
        return jnp.where(c >= keep, cand, u)

    u = lax.fori_loop(0, 32, bit_body, jnp.zeros((1, qb), I32))
    u = jnp.maximum(u ^ sign, jnp.int32(0x007FFFFF - 2 ** 31)) ^ sign
    thr = key_to_float(u)
    n_gt = count(lambda j: jnp.where(sc_ref[j] > thr, 1.0, 0.0))
    n_ge = count(lambda j: jnp.where(sc_ref[j] >= thr, 1.0, 0.0))
    need = keep - n_gt

    nbits = int(math.ceil(math.log2(seq))) + 1

    def cut_search():
        def cut_body(t, cut):
            cand = cut | jnp.left_shift(jnp.int32(1), nbits - 1 - t)
            c = count(lambda j: jnp.where(sc_ref[j] == thr,
                                          jnp.where(krow < cand - j * sb, 1.0, 0.0), 0.0))
            return jnp.where(c <= need, cand, cut)

        return lax.fori_loop(0, nbits, cut_body, jnp.zeros((1, qb), I32))

    surplus = jnp.max(n_ge) > keep
    cut = lax.cond(surplus, cut_search, lambda: jnp.full((1, qb), 2 ** nbits - 1, I32))
    cut = jnp.minimum(cut, i * qb + 1 + _iota((1, qb), 1))

    def mask_body(j, mx):
        sc = sc_ref[j]
        sel = jnp.where(sc > thr, 1.0, jnp.where(sc == thr, jnp.where(krow < cut - j * sb, 1.0, 0.0), 0.0))
        out = []
        for h in range(B_HEADS):
            s = jnp.where(sel > 0.5, lg_ref[h, j], NEG)
            lg_ref[h, j] = s
            out.append(jnp.maximum(mx[h], fold(s, jnp.maximum)))
        return tuple(out)

    mx = for_steps(mask_body, tuple(jnp.full((8, qb), NEG, F32) for _ in range(B_HEADS)))
    ms = [jnp.max(m, 0, keepdims=True) for m in mx]

    def att_body(j, carry):
        ls, acc = carry
        ps, new_ls = [], []
        for h in range(B_HEADS):
            pr = jnp.exp(lg_ref[h, j] - ms[h])
            new_ls.append(ls[h] + fold(pr, jnp.add))
            ps.append(pr.astype(BF16))
        upd = _dot(ckvt_ref[j], jnp.concatenate(ps, 1))
        return tuple(new_ls), acc + upd

    init = (tuple(jnp.zeros((8, qb), F32) for _ in range(B_HEADS)),
            jnp.zeros((B_LATENT, B_HEADS * qb), F32))
    ls, acc = for_steps(att_body, init)
    ls = [jnp.sum(l, 0, keepdims=True) for l in ls]
    o_t = (acc / jnp.concatenate(ls, 1)).astype(BF16)
    y_t = jnp.concatenate([_dot(wuvt_ref[h], o_t[:, h * qb:(h + 1) * qb]) for h in range(B_HEADS)], 0)
    o_ref[...] = y_t.T


def _mixer_b(h, gain4, wuv_t, bias, bsz, seq):
    qb = Q_BLOCK
    assert seq % 256 == 0
    nblk = seq // qb
    keep = min(TOPK_MAX, seq // TOPK_DIV)

    def qpiece(name):
        off, w = _PIECE_OFF[name]
        return pl.BlockSpec((qb, w * LANES), lambda b, i, o=off // w: (b * nblk + i, o))

    def kpiece(name):
        off, w = _PIECE_OFF[name]
        return pl.BlockSpec((seq, w * LANES), lambda b, i, o=off // w: (b, o))

    return pl.pallas_call(
        functools.partial(_mixer_b_kernel, seq=seq, keep=keep),
        grid=(bsz, nblk),
        in_specs=[qpiece("b_q"), qpiece("b_iq"), qpiece("b_iw"), kpiece("b_ckv4"), kpiece("b_ik8"),
                  pl.BlockSpec(gain4.shape, lambda b, i: (0, 0)),
                  pl.BlockSpec(wuv_t.shape, lambda b, i: (0, 0, 0)),
                  pl.BlockSpec(bias.shape, lambda b, i: (0, 0, 0, 0))],
        out_specs=pl.BlockSpec((qb, BRANCH_WIDTH), lambda b, i: (b * nblk + i, 0)),
        out_shape=jax.ShapeDtypeStruct((bsz * seq, BRANCH_WIDTH), F32),
        scratch_shapes=[pltpu.VMEM((seq, BRANCH_WIDTH), BF16),
                        pltpu.VMEM((nblk // 2, B_LATENT, 2 * qb), BF16),
                        pltpu.VMEM((seq, BRANCH_WIDTH), BF16),
                        pltpu.VMEM((nblk // 2, 2 * qb, qb), F32),
                        pltpu.VMEM((B_HEADS, nblk // 2, 2 * qb, qb), F32),
                        pltpu.VMEM((IDX_HEADS * qb, BRANCH_WIDTH), BF16),
                        pltpu.VMEM((B_HEADS * qb, BRANCH_WIDTH), BF16)],
        compiler_params=_params("parallel", "arbitrary"),
        name="mixer_b",
    )(h, h, h, h, h, gain4, wuv_t, bias)


def _mixer_c_kernel(q_ref, k_ref, v_ref, g_ref, a_ref, aup_ref, abias_ref, gain_ref, o_ref, *, seq):
    ch = C_CHUNK
    kw = C_HEADS * C_KEY_DIM
    vw = C_HEADS * C_VAL_DIM
    tri = jnp.where(_iota((ch, ch), 1) <= _iota((ch, ch), 0), 1.0, 0.0).astype(BF16)
    causal = _iota((C_HEADS * ch, ch), 1) <= (_iota((C_HEADS * ch, ch), 0) % ch)
    klane_head = _iota((C_HEADS * ch, kw), 1) // C_KEY_DIM
    krow_head = _iota((C_HEADS * ch, kw), 0) // ch
    vlane_head = _iota((ch, vw), 1) // C_VAL_DIM
    st_mask = (_iota((vw, kw), 0) // C_VAL_DIM) == (_iota((vw, kw), 1) // C_KEY_DIM)
    gmat = _group_matrix(vw, C_VAL_DIM, 1.0 / C_VAL_DIM)
    aup = aup_ref[...]

    def body(c, st):
        rows = pl.ds(pl.multiple_of(c * ch, ch), ch)
        qc = q_ref[rows, :] * (C_KEY_DIM ** -0.5)
        kc = k_ref[rows, :]
        vc = v_ref[rows, :]
        z = _dot3(a_ref[rows, :], aup) + abias_ref[...]
        log_a = -_softplus(-z) / C_GATE_TAU
        cum = _dot_lhs_exact(tri, log_a)
        last = cum[ch - 1:ch, :]
        q_dec = qc * jnp.exp(cum)
        k_inv = (kc * jnp.exp(-cum)).astype(BF16)
        k_dec = (kc * jnp.exp(last - cum)).astype(BF16)
        vb = vc.astype(BF16)
        qd_b = q_dec.astype(BF16)
        q_stack = jnp.where(klane_head == krow_head, jnp.concatenate([q_dec] * C_HEADS, 0), 0.0)
        att = jnp.where(causal, _dotg(q_stack.astype(BF16), k_inv, NT), 0.0)
        full = _dot(att.astype(BF16), vb)
        o = _dotg(qd_b, st.astype(BF16), NT)
        for h in range(C_HEADS):
            o = o + jnp.where(vlane_head == h, full[h * ch:(h + 1) * ch], 0.0)
        upd = _dotg(vb, k_dec, TN)
        st = st * jnp.exp(last) + jnp.where(st_mask, upd, 0.0)
        ms = _dot_rhs_exact(o * o, gmat)
        o = o * lax.rsqrt(ms + 1e-6) * gain_ref[...]
        g = g_ref[rows, :]
        o_ref[rows, :] = g * _sigmoid(g) * o
        return st

    lax.fori_loop(0, seq // ch, body, jnp.zeros((vw, kw), F32))


def _mixer_c(h, aup_pad, abias, gain, bsz, seq):
    assert seq % C_CHUNK == 0

    def piece(name):
        off, w = _PIECE_OFF[name]
        return pl.BlockSpec((seq, w * LANES), lambda b, o=off // w: (b, o))

    def const(a):
        return pl.BlockSpec(a.shape, lambda b: (0, 0))

    return pl.pallas_call(
        functools.partial(_mixer_c_kernel, seq=seq),
        grid=(bsz,),
        in_specs=[piece("c_q"), piece("c_k"), piece("c_v"), piece("c_g"), piece("c_a"),
                  const(aup_pad), const(abias), const(gain)],
        out_specs=pl.BlockSpec((seq, BRANCH_WIDTH), lambda b: (b, 0)),
        out_shape=jax.ShapeDtypeStruct((bsz * seq, BRANCH_WIDTH), F32),
        compiler_params=_params("parallel"),
        name="mixer_c",
    )(h, h, h, h, h, aup_pad, abias, gain)


def _mixer_d_kernel(r_ref, k_ref, v_ref, low_ref, mu_ref, mulow_ref, w2_ref, a2_ref, g2_ref, vec_ref,
                    o_ref, *, seq, nsub):
    ch = D_CHUNK
    bw = BRANCH_WIDTH
    nh = D_HEADS
    hs = D_HEAD_SIZE
    r_i = _iota((bw, bw), 0)
    c_i = _iota((bw, bw), 1)
    same = (r_i // hs) == (c_i // hs)
    strict = c_i < r_i
    incl = c_i <= r_i
    eye = r_i == c_i
    ones_bd = _group_matrix(bw, hs, 1.0)
    avg_bd = _group_matrix(bw, hs, 1.0 / hs)
    rows = nsub * ch
    tr_i = _iota((rows, rows), 0)
    tc_i = _iota((rows, rows), 1)
    tri = jnp.where(jnp.logical_and(tr_i // ch == tc_i // ch, tc_i <= tr_i), 1.0, 0.0).astype(BF16)
    row0 = _iota((rows, bw), 0) == 0
    row0_low = _iota((rows, LANES), 0) == 0
    vec = vec_ref[...]
    w0, a0, k_k, k_a, r_k, gn_w, gn_b = (vec[n:n + 1, :] for n in range(7))
    mu = mu_ref[...]
    w2, a2, g2 = w2_ref[...], a2_ref[...], g2_ref[...]

    def wide(x):
        return jnp.where(same, jnp.concatenate([x] * nh, 0), 0.0)

    def shifted(ref, start, first_row, m):
        cur = ref[pl.ds(start, rows), :]
        last8 = ref[pl.ds(pl.multiple_of(jnp.maximum(start - 8, 0), 8), 8), :]
        prev_row = last8[7:8, :] * jnp.where(start > 0, 1.0, 0.0)
        prev = jnp.where(first_row, prev_row, pltpu.roll(cur, 1, 0))
        return cur + (prev - cur) * m

    def body(c, st):
        start = pl.multiple_of(c * rows, rows)
        r = shifted(r_ref, start, row0, mu[0:1, :])
        k = shifted(k_ref, start, row0, mu[1:2, :])
        v = shifted(v_ref, start, row0, mu[2:3, :])
        low = shifted(low_ref, start, row0_low, mulow_ref[...])
        w_raw = -_softplus(-(w0 + _dot3(jnp.tanh(low), w2))) - 0.5
        lw = -jnp.exp(w_raw)
        a = _sigmoid(a0 + _dot3(low, a2))
        g = _dot3(_sigmoid(low), g2)
        kk = k * k_k
        kk = kk / jnp.maximum(jnp.sqrt(_dot_rhs_exact(kk * kk, ones_bd)), 1e-12)
        k2 = k * (1.0 + (a - 1.0) * k_a)
        bonus = _dot_rhs_exact(r * k2 * r_k, ones_bd) * v
        b = kk * a

        cum = _dot_lhs_exact(tri, lw)
        e_in = jnp.exp(cum)
        e_inv = jnp.exp(-cum)
        a_dec = -kk * jnp.exp(cum - lw)
        r_dec = r * e_in
        b_inv = b * e_inv
        k_inv = k2 * e_inv
        local = [chunk_terms(*(z[s * ch:(s + 1) * ch] for z in (a_dec, r_dec, b_inv, k_inv, v, e_in)))
                 for s in range(nsub)]
        ys = []
        for qp, y0, gmat, hmat, g_col in local:
            st16 = st.astype(BF16)
            y_w = _dot(qp, st16) + y0
            st = g_col * st + _dot(gmat, st16) + hmat
            y = y_w[0:ch]
            for h in range(1, nh):
                y = y + y_w[h * ch:(h + 1) * ch]
            ys.append(y)
        y = jnp.concatenate(ys, 0)
        mean = _dot_rhs_exact(y, avg_bd)
        yc = y - mean
        var = _dot_rhs_exact(yc * yc, avg_bd)
        yn = yc * lax.rsqrt(var + D_GN_EPS) * gn_w + gn_b
        o_ref[pl.ds(start, rows), :] = (yn + bonus) * g
        return st

    def chunk_terms(a_dec, r_dec, b_inv, k_inv, v, e_in):
        g_last = e_in[ch - 1:ch, :]
        xa = wide(a_dec).astype(BF16)
        xr_f = wide(r_dec)
        xr = xr_f.astype(BF16)
        yb = wide(b_inv).astype(BF16)
        yk = wide(k_inv).astype(BF16)
        bh = wide(b_inv * g_last).astype(BF16)
        kh = wide(k_inv * g_last).astype(BF16)
        vw = wide(v).astype(BF16)

        a_ab = jnp.where(strict, _dotg(xa, yb, NT), 0.0)
        a_ak = jnp.where(strict, _dotg(xa, yk, NT), 0.0).astype(BF16)
        m_rb = jnp.where(incl, _dotg(xr, yb, NT), 0.0).astype(BF16)
        m_rk = jnp.where(incl, _dotg(xr, yk, NT), 0.0).astype(BF16)

        t_inv = jnp.where(eye, 1.0, a_ab)
        apow = a_ab
        for _ in range(int(math.log2(ch)) - 1):
            ab16 = apow.astype(BF16)
            apow = _dot(ab16, ab16)
            t_inv = t_inv + _dot(t_inv.astype(BF16), apow.astype(BF16))
        t16 = t_inv.astype(BF16)

        atp = _dot(t16, xa).astype(BF16)
        vp = _dot(t16, _dot(a_ak, vw).astype(BF16)).astype(BF16)
        qp = (xr_f + _dot(m_rb, atp)).astype(BF16)
        y0 = _dot(m_rb, vp) + _dot(m_rk, vw)
        gmat = _dotg(bh, atp, TN).astype(BF16)
        hmat = _dotg(bh, vp, TN) + _dotg(kh, vw, TN)
        g_col = jnp.sum(jnp.where(eye, g_last, 0.0), -1, keepdims=True)
        return qp, y0, gmat, hmat, g_col

    lax.fori_loop(0, seq // rows, body, jnp.zeros((bw, bw), F32))


def _mixer_d(h, mu3, mulow, w2p, a2p, g2p, vec, bsz, seq, nsub=2):
    assert seq % (nsub * D_CHUNK) == 0 and D_CHUNK == D_HEAD_SIZE

    def piece(name):
        off, w = _PIECE_OFF[name]
        return pl.BlockSpec((seq, w * LANES), lambda b, o=off // w: (b, o))

    def const(a):
        return pl.BlockSpec(a.shape, lambda b: (0, 0))

    return pl.pallas_call(
        functools.partial(_mixer_d_kernel, seq=seq, nsub=nsub),
        grid=(bsz,),
        in_specs=[piece("d_r"), piece("d_k"), piece("d_v"), piece("d_low"),
                  const(mu3), const(mulow), const(w2p), const(a2p), const(g2p), const(vec)],
        out_specs=pl.BlockSpec((seq, BRANCH_WIDTH), lambda b: (b, 0)),
        out_shape=jax.ShapeDtypeStruct((bsz * seq, BRANCH_WIDTH), F32),
        compiler_params=_params("parallel"),
        name="mixer_d",
    )(h, h, h, h, mu3, mulow, w2p, a2p, g2p, vec)


def _merge_kernel(x_ref, ya_ref, yb_ref, yc_ref, yd_ref, wg_ref, wb_ref, wo_ref, g_ref, b_ref, o_ref,
                  *, alpha):
    x = x_ref[...]
    xb = x.astype(BF16)
    merged = None
    for n, y_ref in enumerate((ya_ref, yb_ref, yc_ref, yd_ref)):
        gate = _sigmoid(_dot(xb, wg_ref[:, n * D_MODEL:(n + 1) * D_MODEL]))
        term = gate * _dot(y_ref[...].astype(BF16), wb_ref[n])
        merged = term if merged is None else merged + term
    z = alpha * x + _dot(merged.astype(BF16), wo_ref[...])
    o_ref[...] = _layer_norm(z, g_ref[...], b_ref[...])


def _merge(x2d, ys, wg, wb, wo, g, b, alpha, tm):
    n, d = x2d.shape

    def const(a):
        nd = a.ndim
        return pl.BlockSpec(a.shape, lambda i: (0,) * nd)

    yspec = pl.BlockSpec((tm, BRANCH_WIDTH), lambda i: (i, 0))
    return pl.pallas_call(
        functools.partial(_merge_kernel, alpha=alpha),
        grid=(n // tm,),
        in_specs=[pl.BlockSpec((tm, d), lambda i: (i, 0)), yspec, yspec, yspec, yspec,
                  const(wg), const(wb), const(wo), const(g), const(b)],
        out_specs=pl.BlockSpec((tm, d), lambda i: (i, 0)),
        out_shape=jax.ShapeDtypeStruct((n, d), F32),
        compiler_params=_params("parallel"),
        name="merge",
    )(x2d, *ys, wg, wb, wo, g, b)


def _moe_kernel(x_ref, ltri_ref, wr_ref, br_ref, wg_ref, wu_ref, wd_ref, g_ref, b_ref, o_ref,
                xs_ref, gs_ref, ys_ref, pt_ref, seg_ref, *, alpha):
    e = pl.program_id(1)
    tm = x_ref.shape[0]
    nrow = xs_ref.shape[0]
    ck = MOE_CHUNK
    lane = _iota((tm, LANES), 1)

    @pl.when(e == 0)
    def _():
        x = x_ref[...]
        xb = x.astype(BF16)
        ys_ref[...] = jnp.zeros_like(ys_ref)
        x_lo = (x - xb.astype(F32)).astype(BF16)
        cross = _dot(jnp.concatenate([xb, x_lo], 0), wr_ref[...])
        logits = (cross[0:tm, 0:LANES] + (cross[0:tm, LANES:] + cross[tm:, 0:LANES])) + br_ref[...]
        is_g = jnp.logical_and(lane >= MOE_EXPERTS, lane < MOE_EXPERTS + MOE_GROUPS)
        lg = jnp.where(is_g, logits, -jnp.inf)
        gmax = jnp.max(lg, -1, keepdims=True)
        ptop = 1.0 / jnp.sum(jnp.exp(lg - gmax), -1, keepdims=True)
        gsel = jnp.min(jnp.where(lg == gmax, lane, 2 * LANES), -1, keepdims=True) - MOE_EXPERTS
        in_group = jnp.logical_and(lane < MOE_EXPERTS, lane // MOE_PER_GROUP == gsel)
        le = jnp.where(in_group, logits, -jnp.inf)
        v1 = jnp.max(le, -1, keepdims=True)
        i1 = jnp.min(jnp.where(le == v1, lane, 2 * LANES), -1, keepdims=True)
        le2 = jnp.where(lane == i1, -jnp.inf, le)
        v2 = jnp.max(le2, -1, keepdims=True)
        i2 = jnp.min(jnp.where(le2 == v2, lane, 2 * LANES), -1, keepdims=True)
        e2 = jnp.exp(v2 - v1)
        w1 = ptop / (1.0 + e2)
        gate = jnp.where(lane == i1, w1, jnp.where(lane == i2, w1 * e2, 0.0))

        onehot = jnp.where(lane == gsel, 1.0, 0.0)
        rank = _dot(ltri_ref[...], onehot.astype(BF16))
        cnt = jnp.sum(onehot, 0, keepdims=True).astype(I32)
        nch = jnp.right_shift(cnt + (ck - 1), int(math.log2(ck)))
        lane1 = _iota((1, LANES), 1)
        off = jnp.int32(0)
        off_vec = jnp.zeros((1, LANES), I32)
        for grp in range(MOE_GROUPS):
            n_g = nch[0, grp]
            seg_ref[grp] = off
            seg_ref[MOE_GROUPS + grp] = n_g
            off_vec = jnp.where(lane1 == grp, off, off_vec)
            off = off + n_g * ck
        dest = jnp.sum(onehot * (rank + off_vec.astype(F32)), -1, keepdims=True)
        pt_ref[...] = jnp.where(_iota((tm, nrow), 1) == dest.astype(I32), 1.0, 0.0).astype(BF16)
        dest_row = jnp.broadcast_to(dest, (tm, LANES)).T[0:1, :].astype(I32)
        perm = jnp.where(_iota((nrow, tm), 0) == dest_row, 1.0, 0.0).astype(BF16)
        gate_hi, gate_lo = _split2(gate)
        moved = _dot(perm, jnp.concatenate([xb, gate_hi, gate_lo], 1))
        d = x_ref.shape[1]
        xs_ref[...] = moved[:, 0:d].astype(BF16)
        gs_ref[...] = moved[:, d:d + LANES] + moved[:, d + LANES:]

    grp = e // MOE_PER_GROUP
    seg_off = seg_ref[grp]
    seg_chunks = seg_ref[MOE_GROUPS + grp]

    def run_expert(start, size):
        rows = pl.ds(pl.multiple_of(start, ck), size)
        xc = xs_ref[rows, :]
        gcol = jnp.sum(jnp.where(_iota((size, LANES), 1) == e, gs_ref[rows, :], 0.0), -1, keepdims=True)
        hg = _dot(xc, wg_ref[0])
        hu = _dot(xc, wu_ref[0])
        hidden = hg * _sigmoid(hg) * hu * gcol
        ys_ref[rows, :] += _dot(hidden.astype(BF16), wd_ref[0])

    def pair(c, carry):
        run_expert(seg_off + c * (2 * ck), 2 * ck)
        return carry

    lax.fori_loop(0, seg_chunks // 2, pair, 0)

    @pl.when(seg_chunks % 2 == 1)
    def _():
        run_expert(seg_off + (seg_chunks - 1) * ck, ck)

    @pl.when(e == pl.num_programs(1) - 1)
    def _():
        z = alpha * x_ref[...] + _dot(pt_ref[...], ys_ref[...].astype(BF16))
        o_ref[...] = _layer_norm(z, g_ref[...], b_ref[...])


def _moe(x2d, wr, br, wg, wu, wd, g, b, alpha, tm):
    n, d = x2d.shape
    ne = wg.shape[0]
    nrow = tm + MOE_GROUPS * MOE_CHUNK
    ltri = jnp.tril(jnp.ones((tm, tm), BF16), -1)

    def const(a):
        return pl.BlockSpec(a.shape, lambda i, e: (0, 0))

    return pl.pallas_call(
        functools.partial(_moe_kernel, alpha=alpha),
        grid=(n // tm, ne),
        in_specs=[pl.BlockSpec((tm, d), lambda i, e: (i, 0)), const(ltri), const(wr), const(br),
                  pl.BlockSpec((1, d, MOE_HIDDEN), lambda i, e: (e, 0, 0)),
                  pl.BlockSpec((1, d, MOE_HIDDEN), lambda i, e: (e, 0, 0)),
                  pl.BlockSpec((1, MOE_HIDDEN, d), lambda i, e: (e, 0, 0)),
                  const(g), const(b)],
        out_specs=pl.BlockSpec((tm, d), lambda i, e: (i, 0)),
        out_shape=jax.ShapeDtypeStruct((n, d), F32),
        scratch_shapes=[pltpu.VMEM((nrow, d), BF16), pltpu.VMEM((nrow, LANES), F32),
                        pltpu.VMEM((nrow, d), F32), pltpu.VMEM((tm, nrow), BF16),
                        pltpu.SMEM((2 * MOE_GROUPS,), I32)],
        compiler_params=_params("parallel", "arbitrary"),
        name="moe",
    )(x2d, ltri, wr, br, wg, wu, wd, g, b)


def _t5_bucket(dist):
    exact = RPB_BUCKETS // 2
    d = jnp.maximum(dist, 0)
    df = jnp.maximum(d, 1).astype(F32)
    large = exact + (jnp.log(df / exact) / math.log(RPB_MAX_DIST / exact)
                     * (RPB_BUCKETS - exact)).astype(I32)
    large = jnp.minimum(large, RPB_BUCKETS - 1)
    return jnp.where(d < exact, d, large)


def _rpb_lookup(tab, dist):
    bucket = _t5_bucket(dist)[..., None]
    out = jnp.zeros(dist.shape + (tab.shape[1],), F32)
    for k in range(RPB_BUCKETS):
        out = jnp.where(bucket == k, tab[k], out)
    return out


def _bias_a(tab):
    qi = jnp.arange(A_BLOCK)[:, None]
    kj = jnp.arange(2 * A_BLOCK)[None, :]
    rdist = qi + A_BLOCK - kj
    out = []
    for window, dilation in A_PATTERNS:
        in_band = (rdist >= 0) & (rdist <= window // dilation)
        bias = jnp.transpose(_rpb_lookup(tab, rdist * dilation), (2, 0, 1))
        out.append(jnp.where(in_band[None], bias, NEG))
    return jnp.stack(out).astype(F32)


def _bias_b(tab, nblk):
    qi = jnp.arange(Q_BLOCK)[:, None]
    kj = jnp.arange(Q_BLOCK)[None, :]
    delta = jnp.arange(nblk)[:, None, None] * Q_BLOCK
    bias = _rpb_lookup(tab, delta + qi - kj)
    return jnp.transpose(bias, (0, 3, 2, 1)).astype(F32)


def _in_offsets():
    splits = (("a_q", 256), ("a_k", 256), ("a_v", 256), ("b_q", 256), ("b_ckv", 64), ("b_iq", 256),
              ("b_ik", 32), ("b_iw", 8), ("c_q", 128), ("c_k", 128), ("c_v", 256), ("c_a", 16),
              ("c_g", 256), ("d", 832), ("gate", 4096))
    off, out = 0, {}
    for name, w in splits:
        out[name] = (off, w)
        off += w
    return out


def _proj_columns():
    src = _in_offsets()
    d0 = src["d"][0]
    named = {
        "a_q": np.arange(256) + src["a_q"][0], "a_k": np.arange(256) + src["a_k"][0],
        "a_v": np.arange(256) + src["a_v"][0], "b_q": np.arange(256) + src["b_q"][0],
        "b_iq": np.arange(256) + src["b_iq"][0],
        "b_ckv4": np.tile(np.arange(64) + src["b_ckv"][0], B_HEADS),
        "b_ik8": np.tile(np.arange(32) + src["b_ik"][0], IDX_HEADS),
        "c_v": np.arange(256) + src["c_v"][0], "c_g": np.arange(256) + src["c_g"][0],
        "d_r": np.arange(256) + d0, "d_k": np.arange(256) + d0 + 256, "d_v": np.arange(256) + d0 + 512,
        "c_q": np.arange(128) + src["c_q"][0], "c_k": np.arange(128) + src["c_k"][0],
        "b_iw": np.arange(8) + src["b_iw"][0], "c_a": np.arange(16) + src["c_a"][0],
        "d_low": np.arange(64) + d0 + 768, "pad": np.arange(0),
    }
    cols = []
    for name, w in _PIECES:
        c = named[name]
        cols.append(np.concatenate([c, -np.ones(w * LANES - len(c), np.int64)]))
    return np.concatenate(cols)


def _proj_weight(w):
    cols = _proj_columns()
    cuts = [0] + [k for k in range(1, len(cols))
                  if (cols[k] < 0) != (cols[k - 1] < 0) or (cols[k] >= 0 and cols[k] != cols[k - 1] + 1)]
    cuts.append(len(cols))
    parts = []
    for a, b in zip(cuts[:-1], cuts[1:]):
        if cols[a] < 0:
            parts.append(jnp.zeros((w.shape[0], b - a), BF16))
        else:
            parts.append(w[:, int(cols[a]):int(cols[a]) + (b - a)].astype(BF16))
    return jnp.concatenate(parts, 1)


def _pad_rows(w, first, total):
    return jnp.zeros((total, w.shape[1]), w.dtype).at[first:first + w.shape[0]].set(w)


def kernel(x, rpb_table, w_in, b_kv_gain, b_w_uv, c_a_up, c_a_bias, c_norm_gain, d_mu, d_w0, d_w2,
           d_a0, d_a2, d_g2, d_k_k, d_k_a, d_r_k, d_gn_w, d_gn_b, w_branch, w_out, ln_g, ln_b,
           router_g, router_g_bias, router_e, router_e_bias, moe_w_gate, moe_w_up, moe_w_down):
    bsz, seq, d_model = x.shape
    depth = w_in.shape[0]
    n = bsz * seq
    alpha = (2 * depth) ** 0.25
    gate_off = _in_offsets()["gate"][0]
    bias_a = _bias_a(rpb_table[:, :A_HEADS])
    bias_b = _bias_b(rpb_table[:, A_HEADS:], seq // Q_BLOCK)
    tm_proj = math.gcd(n, 1024)
    tm_merge = math.gcd(n, 512)
    tm_moe = math.gcd(n, 1024)

    x2d = x.reshape(n, d_model)
    for l in range(depth):
        h = _project(x2d, _proj_weight(w_in[l]), tm_proj, 768)

        ya = _mixer_a(h, bias_a, bsz, seq)

        gain4 = jnp.tile(b_kv_gain[l], B_HEADS)[None, :]
        wuv_t = jnp.transpose(b_w_uv[l], (0, 2, 1)).astype(BF16)
        yb = _mixer_b(h, gain4, wuv_t, bias_b, bsz, seq)

        aup_pad = _pad_rows(c_a_up[l], 0, LANES)
        yc = _mixer_c(h, aup_pad, c_a_bias[l][None, :], c_norm_gain[l][None, :], bsz, seq)

        mu = d_mu[l]
        mu3 = mu[:3 * BRANCH_WIDTH].reshape(3, BRANCH_WIDTH)
        mulow = jnp.zeros((1, LANES), F32).at[0, :D_DECAY_RANK + D_ICLR_RANK + D_GATE_RANK].set(
            mu[3 * BRANCH_WIDTH:])
        w2p = _pad_rows(d_w2[l], 0, LANES)
        a2p = _pad_rows(d_a2[l], D_DECAY_RANK, LANES)
        g2p = _pad_rows(d_g2[l], D_DECAY_RANK + D_ICLR_RANK, LANES)
        vec = jnp.stack([d_w0[l], d_a0[l], d_k_k[l], d_k_a[l], d_r_k[l], d_gn_w[l], d_gn_b[l],
                         jnp.zeros_like(d_w0[l])])
        yd = _mixer_d(h, mu3, mulow, w2p, a2p, g2p, vec, bsz, seq)

        wg = w_in[l][:, gate_off:gate_off + N_BRANCHES * d_model].astype(BF16)
        x2d = _merge(x2d, (ya, yb, yc, yd), wg, w_branch[l].astype(BF16), w_out[l].astype(BF16),
                     ln_g[l, 0][None, :], ln_b[l, 0][None, :], alpha, tm_merge)

        wr = jnp.zeros((d_model, LANES), F32)
        wr = wr.at[:, :MOE_EXPERTS].set(router_e[l]).at[:, MOE_EXPERTS:MOE_EXPERTS + MOE_GROUPS].set(
            router_g[l])
        br = jnp.zeros((1, LANES), F32)
        br = br.at[0, :MOE_EXPERTS].set(router_e_bias[l]).at[0, MOE_EXPERTS:MOE_EXPERTS + MOE_GROUPS].set(
            router_g_bias[l])
        wr = jnp.concatenate(_split2(wr), 1)
        x2d = _moe(x2d, wr, br, moe_w_gate[l].astype(BF16), moe_w_up[l].astype(BF16),
                   moe_w_down[l].astype(BF16), ln_g[l, 1][None, :], ln_b[l, 1][None, :], alpha, tm_moe)
    return x2d.reshape(bsz, seq, d_model)
```

```python
import functools
import math

import jax
import jax.numpy as jnp
import numpy as np
from jax import lax
from jax.experimental import pallas as pl
from jax.experimental.pallas import tpu as pltpu

F32 = jnp.float32
BF16 = jnp.bfloat16
I32 = jnp.int32

D_MODEL = 1024
N_BRANCHES = 4
BRANCH_WIDTH = 256
HEAD_DIM = 64
A_HEADS = 4
A_PATTERNS = ((128, 1), (512, 4), (2048, 16))
A_BLOCK = 128
B_HEADS = 4
B_LATENT = 64
IDX_HEADS = 8
IDX_DIM = 32
TOPK_MAX = 256
TOPK_DIV = 4
Q_BLOCK = 128
C_HEADS = 4
C_KEY_DIM = 32
C_VAL_DIM = 64
C_GATE_RANK = 16
C_GATE_TAU = 16.0
C_CHUNK = 64
D_HEADS = 4
D_HEAD_SIZE = 64
D_DECAY_RANK = 16
D_ICLR_RANK = 16
D_GATE_RANK = 32
D_GN_EPS = 64e-5
D_CHUNK = 64
RPB_BUCKETS = 32
RPB_MAX_DIST = 2048
MOE_GROUPS = 4
MOE_PER_GROUP = 8
MOE_EXPERTS = 32
MOE_HIDDEN = 256
MOE_CHUNK = 128
LN_EPS = 1e-5

LANES = 128
NEG = -1e30
VMEM_LIMIT = 56 * 1024 * 1024

NT = (((1,), (1,)), ((), ()))
TN = (((0,), (0,)), ((), ()))

_PIECES = (
    ("a_q", 2), ("a_k", 2), ("a_v", 2), ("b_q", 2), ("b_iq", 2), ("b_ckv4", 2), ("b_ik8", 2),
    ("c_v", 2), ("c_g", 2), ("d_r", 2), ("d_k", 2), ("d_v", 2),
    ("c_q", 1), ("c_k", 1), ("b_iw", 1), ("c_a", 1), ("d_low", 1), ("pad", 1),
)
_PIECE_OFF = {}
_off = 0
for _n, _w in _PIECES:
    _PIECE_OFF[_n] = (_off, _w)
    _off += _w
PROJ_WIDTH = _off * LANES


def _dot(a, b):
    return jnp.dot(a, b, preferred_element_type=F32)


def _dotg(a, b, dims):
    return lax.dot_general(a, b, dims, preferred_element_type=F32)


def _split2(x):
    hi = x.astype(BF16)
    lo = (x - hi.astype(F32)).astype(BF16)
    return hi, lo


def _dot_rhs_exact(x, m):
    hi, lo = _split2(x)
    return _dot(hi, m) + _dot(lo, m)


def _dot_lhs_exact(m, x):
    hi, lo = _split2(x)
    return _dot(m, hi) + _dot(m, lo)


def _dot3(x, w):
    xh, xl = _split2(x)
    wh, wl = _split2(w)
    return _dot(xh, wh) + (_dot(xl, wh) + _dot(xh, wl))


def _sigmoid(x):
    return 1.0 / (1.0 + jnp.exp(-x))


def _softplus(x):
    return jnp.maximum(x, 0.0) + jnp.log(1.0 + jnp.exp(-jnp.abs(x)))


def _iota(shape, axis):
    return lax.broadcasted_iota(I32, shape, axis)


def _group_matrix(n, group, value, dtype=BF16):
    r = _iota((n, n), 0) // group
    c = _iota((n, n), 1) // group
    return jnp.where(r == c, value, 0.0).astype(dtype)


def _layer_norm(z, g, b):
    mu = jnp.mean(z, -1, keepdims=True)
    zc = z - mu
    var = jnp.mean(zc * zc, -1, keepdims=True)
    return zc * lax.rsqrt(var + LN_EPS) * g + b


def _params(*sem):
    return pltpu.CompilerParams(dimension_semantics=sem, vmem_limit_bytes=VMEM_LIMIT)


def _proj_kernel(x_ref, w_ref, o_ref):
    o_ref[...] = _dot(x_ref[...].astype(BF16), w_ref[...])


def _project(x2d, w, tm, tn):
    n, k = x2d.shape
    wt = w.shape[1]
    return pl.pallas_call(
        _proj_kernel,
        grid=(n // tm, wt // tn),
        in_specs=[pl.BlockSpec((tm, k), lambda i, j: (i, 0)),
                  pl.BlockSpec((k, tn), lambda i, j: (0, j))],
        out_specs=pl.BlockSpec((tm, tn), lambda i, j: (i, j)),
        out_shape=jax.ShapeDtypeStruct((n, wt), F32),
        compiler_params=_params("parallel", "parallel"),
        name="proj",
    )(x2d, w)


def _mixer_a_kernel(q0_ref, q1_ref, k0_ref, k1_ref, v0_ref, v1_ref, bias_ref, o_ref, op_ref, lse_ref,
                    *, seq, dils):
    q_refs, k_refs, v_refs = (q0_ref, q1_ref), (k0_ref, k1_ref), (v0_ref, v1_ref)
    ab = A_BLOCK
    lane = _iota((ab, LANES), 1)
    upper = lane >= HEAD_DIM
    col = _iota((ab, 2 * ab), 1)
    scale = HEAD_DIM ** -0.5

    for p, d in enumerate(dils):
        per_res = seq // (d * ab)

        def body(idx, carry, p=p, d=d, per_res=per_res):
            r = idx // per_res
            b = idx % per_res
            start = b * (ab * d) + r
            pstart = jnp.maximum(start - ab * d, r)
            if d == 1:
                rows, prow = pl.ds(pl.multiple_of(start, ab), ab), pl.ds(pl.multiple_of(pstart, ab), ab)
            else:
                rows, prow = pl.ds(start, ab, stride=d), pl.ds(pstart, ab, stride=d)
            first_pen = jnp.where(b == 0, NEG, 0.0)
            pen = jnp.where(col < ab, first_pen, 0.0)
            for hp in range(2):
                q_ref, k_ref, v_ref = q_refs[hp], k_refs[hp], v_refs[hp]
                q = q_ref[rows, :]
                kb = jnp.concatenate([k_ref[prow, :], k_ref[rows, :]], 0).astype(BF16)
                vb = jnp.concatenate([v_ref[prow, :], v_ref[rows, :]], 0).astype(BF16)
                outs, lses = [], []
                for hh in range(2):
                    hm = upper if hh else jnp.logical_not(upper)
                    qm = jnp.where(hm, q, 0.0).astype(BF16)
                    s = _dotg(qm, kb, NT) * scale + bias_ref[p, 2 * hp + hh] + pen
                    mx = jnp.max(s, -1, keepdims=True)
                    e = jnp.exp(s - mx)
                    den = jnp.sum(e, -1, keepdims=True)
                    outs.append(_dot(e.astype(BF16), vb) / den)
                    lses.append(mx + jnp.log(den))
                op_ref[2 * p + hp, rows, :] = jnp.where(upper, outs[1], outs[0])
                lse_ref[2 * p + hp, rows, :] = jnp.where(upper, lses[1], lses[0])
            return carry

        def body_pair(t, carry, body=body):
            return body(2 * t + 1, body(2 * t, carry))

        lax.fori_loop(0, seq // (2 * ab), body_pair, 0)

    cr = 256

    def combine(c, carry):
        rows = pl.ds(pl.multiple_of(c * cr, cr), cr)
        for hp in range(2):
            ls = [lse_ref[2 * p + hp, rows, :] for p in range(len(dils))]
            mx = functools.reduce(jnp.maximum, ls)
            es = [jnp.exp(l - mx) for l in ls]
            num = functools.reduce(lambda a, b: a + b,
                                   [e * op_ref[2 * p + hp, rows, :] for p, e in enumerate(es)])
            o_ref[rows, hp * LANES:(hp + 1) * LANES] = num / functools.reduce(lambda a, b: a + b, es)
        return carry

    lax.fori_loop(0, seq // cr, combine, 0)


def _mixer_a(h, bias, bsz, seq):
    dils = tuple(d for _, d in A_PATTERNS)
    for w, d in A_PATTERNS:
        assert w // d == A_BLOCK and seq % (d * A_BLOCK) == 0
    npat = len(dils)

    def pieces(name):
        off, w = _PIECE_OFF[name]
        return [pl.BlockSpec((seq, LANES), lambda b, o=off + u: (b, o)) for u in range(w)]

    return pl.pallas_call(
        functools.partial(_mixer_a_kernel, seq=seq, dils=dils),
        grid=(bsz,),
        in_specs=pieces("a_q") + pieces("a_k") + pieces("a_v")
        + [pl.BlockSpec(bias.shape, lambda b: (0, 0, 0, 0))],
        out_specs=pl.BlockSpec((seq, BRANCH_WIDTH), lambda b: (b, 0)),
        out_shape=jax.ShapeDtypeStruct((bsz * seq, BRANCH_WIDTH), F32),
        scratch_shapes=[pltpu.VMEM((2 * npat, seq, LANES), F32),
                        pltpu.VMEM((2 * npat, seq, LANES), F32)],
        compiler_params=_params("parallel"),
        name="mixer_a",
    )(h, h, h, h, h, h, bias)


def _mixer_b_kernel(q_ref, iq_ref, iw_ref, ckv_ref, ik_ref, gain_ref, wuvt_ref, bias_ref, o_ref,
                    ckvn_ref, ckvt_ref, ikb_ref, sc_ref, lg_ref, iqs_ref, qs_ref, *, seq, keep):
    qb = Q_BLOCK
    sb = 2 * qb
    i = pl.program_id(1)

    @pl.when(i == 0)
    def _():
        gmat = _group_matrix(BRANCH_WIDTH, B_LATENT, 1.0 / B_LATENT)

        def prep(c, carry):
            rows = pl.ds(pl.multiple_of(c * sb, sb), sb)
            x = ckv_ref[rows, :]
            ms = _dot_rhs_exact(x * x, gmat)
            xn = x * lax.rsqrt(ms + 1e-6) * gain_ref[...]
            ckvn_ref[rows, :] = xn.astype(BF16)
            ckvt_ref[c] = xn.T[0:B_LATENT, :].astype(BF16)
            ikb_ref[rows, :] = ik_ref[rows, :].astype(BF16)
            return carry

        lax.fori_loop(0, seq // sb, prep, 0)

    lane = _iota((qb, BRANCH_WIDTH), 1)
    iq = iq_ref[...] * (IDX_DIM ** -0.5)
    for h in range(IDX_HEADS):
        iqs_ref[h * qb:(h + 1) * qb, :] = jnp.where(lane // IDX_DIM == h, iq, 0.0).astype(BF16)
    q = q_ref[...] * (B_LATENT ** -0.5)
    for h in range(B_HEADS):
        qs_ref[h * qb:(h + 1) * qb, :] = jnp.where(lane // B_LATENT == h, q, 0.0).astype(BF16)
    iw_t = (iw_ref[...] * (IDX_HEADS ** -0.5)).T
    krow = _iota((sb, qb), 0)
    qcol = _iota((sb, qb), 1)
    nsb = (i + 2) // 2

    def fold(x, op):
        parts = [x[r:r + 8] for r in range(0, sb, 8)]
        while len(parts) > 1:
            parts = [op(parts[k], parts[k + 1]) for k in range(0, len(parts), 2)]
        return parts[0]

    def for_steps(body, init):
        def pair(p, carry):
            return body(2 * p + 1, body(2 * p, carry))

        carry = lax.fori_loop(0, nsb // 2, pair, init)
        return lax.cond(nsb % 2 == 1, lambda c: body(nsb - 1, c), lambda c: c, carry)

    def score_body(j, carry):
        kr = pl.ds(pl.multiple_of(j * sb, sb), sb)
        rel = _dotg(ikb_ref[kr, :], iqs_ref[...], NT)
        sc = jnp.maximum(rel[:, 0:qb], 0.0) * iw_t[0:1, :]
        for h in range(1, IDX_HEADS):
            sc = sc + jnp.maximum(rel[:, h * qb:(h + 1) * qb], 0.0) * iw_t[h:h + 1, :]
        sc_ref[j] = jnp.where(krow <= qcol + (i * qb - j * sb), sc, -jnp.inf)
        logits = _dotg(ckvn_ref[kr, :], qs_ref[...], NT)
        d0 = i - 2 * j
        for h in range(B_HEADS):
            lg = logits[:, h * qb:(h + 1) * qb]
            lg_ref[h, j, 0:qb, :] = lg[0:qb] + bias_ref[d0, h]
            lg_ref[h, j, qb:sb, :] = lg[qb:sb] + bias_ref[jnp.maximum(d0 - 1, 0), h]
        return carry

    for_steps(score_body, 0)

    sign = jnp.int32(-2 ** 31)

    def key_to_float(u):
        k = u ^ sign
        bits = k ^ ((k >> 31) & jnp.int32(0x7FFFFFFF))
        return pltpu.bitcast(bits, F32)

    def count(pred):
        def cnt(j, acc):
            return acc + fold(pred(j), jnp.add)
        acc = for_steps(cnt, jnp.zeros((8, qb), F32))
        return jnp.sum(acc, 0, keepdims=True)

    def bit_body(t, u):
        cand = u | jnp.left_shift(jnp.int32(1), 31 - t)
        cf = key_to_float(cand)
        c = count(lambda j: jnp.where(sc_ref[j] >= cf, 1.0, 0.0))
        return jnp.where(c >= keep, cand, u)

    u = lax.fori_loop(0, 32, bit_body, jnp.zeros((1, qb), I32))
    u = jnp.maximum(u ^ sign, jnp.int32(0x007FFFFF - 2 ** 31)) ^ sign
    thr = key_to_float(u)
    n_gt = count(lambda j: jnp.where(sc_ref[j] > thr, 1.0, 0.0))
    n_ge = count(lambda j: jnp.where(sc_ref[j] >= thr, 1.0, 0.0))
    need = keep - n_gt

    nbits = int(math.ceil(math.log2(seq))) + 1

    def cut_search():
        def cut_body(t, cut):
            cand = cut | jnp.left_shift(jnp.int32(1), nbits - 1 - t)
            c = count(lambda j: jnp.where(sc_ref[j] == thr,
                                          jnp.where(krow < cand - j * sb, 1.0, 0.0), 0.0))
            return jnp.where(c <= need, cand, cut)

        return lax.fori_loop(0, nbits, cut_body, jnp.zeros((1, qb), I32))

    surplus = jnp.max(n_ge) > keep
    cut = lax.cond(surplus, cut_search, lambda: jnp.full((1, qb), 2 ** nbits - 1, I32))
    cut = jnp.minimum(cut, i * qb + 1 + _iota((1, qb), 1))

    def mask_body(j, mx):
        sc = sc_ref[j]
        sel = jnp.where(sc > thr, 1.0, jnp.where(sc == thr, jnp.where(krow < cut - j * sb, 1.0, 0.0), 0.0))
        out = []
        for h in range(B_HEADS):
            s = jnp.where(sel > 0.5, lg_ref[h, j], NEG)
            lg_ref[h, j] = s
            out.append(jnp.maximum(mx[h], fold(s, jnp.maximum)))
        return tuple(out)

    mx = for_steps(mask_body, tuple(jnp.full((8, qb), NEG, F32) for _ in range(B_HEADS)))
    ms = [jnp.max(m, 0, keepdims=True) for m in mx]

    def att_body(j, carry):
        ls, acc = carry
        ps, new_ls = [], []
        for h in range(B_HEADS):
            pr = jnp.exp(lg_ref[h, j] - ms[h])
            new_ls.append(ls[h] + fold(pr, jnp.add))
            ps.append(pr.astype(BF16))
        upd = _dot(ckvt_ref[j], jnp.concatenate(ps, 1))
        return tuple(new_ls), acc + upd

    init = (tuple(jnp.zeros((8, qb), F32) for _ in range(B_HEADS)),
            jnp.zeros((B_LATENT, B_HEADS * qb), F32))
    ls, acc = for_steps(att_body, init)
    ls = [jnp.sum(l, 0, keepdims=True) for l in ls]
    o_t = (acc / jnp.concatenate(ls, 1)).astype(BF16)
    y_t = jnp.concatenate([_dot(wuvt_ref[h], o_t[:, h * qb:(h + 1) * qb]) for h in range(B_HEADS)], 0)
    o_ref[...] = y_t.T


def _mixer_b(h, gain4, wuv_t, bias, bsz, seq):
    qb = Q_BLOCK
    assert seq % 256 == 0
    nblk = seq // qb
    keep = min(TOPK_MAX, seq // TOPK_DIV)

    def qpiece(name):
        off, w = _PIECE_OFF[name]
        return pl.BlockSpec((qb, w * LANES), lambda b, i, o=off // w: (b * nblk + i, o))

    def kpiece(name):
        off, w = _PIECE_OFF[name]
        return pl.BlockSpec((seq, w * LANES), lambda b, i, o=off // w: (b, o))

    return pl.pallas_call(
        functools.partial(_mixer_b_kernel, seq=seq, keep=keep),
        grid=(bsz, nblk),
        in_specs=[qpiece("b_q"), qpiece("b_iq"), qpiece("b_iw"), kpiece("b_ckv4"), kpiece("b_ik8"),
                  pl.BlockSpec(gain4.shape, lambda b, i: (0, 0)),
                  pl.BlockSpec(wuv_t.shape, lambda b, i: (0, 0, 0)),
                  pl.BlockSpec(bias.shape, lambda b, i: (0, 0, 0, 0))],
        out_specs=pl.BlockSpec((qb, BRANCH_WIDTH), lambda b, i: (b * nblk + i, 0)),
        out_shape=jax.ShapeDtypeStruct((bsz * seq, BRANCH_WIDTH), F32),
        scratch_shapes=[pltpu.VMEM((seq, BRANCH_WIDTH), BF16),
                        pltpu.VMEM((nblk // 2, B_LATENT, 2 * qb), BF16),
                        pltpu.VMEM((seq, BRANCH_WIDTH), BF16),
                        pltpu.VMEM((nblk // 2, 2 * qb, qb), F32),
                        pltpu.VMEM((B_HEADS, nblk // 2, 2 * qb, qb), F32),
                        pltpu.VMEM((IDX_HEADS * qb, BRANCH_WIDTH), BF16),
                        pltpu.VMEM((B_HEADS * qb, BRANCH_WIDTH), BF16)],
        compiler_params=_params("parallel", "arbitrary"),
        name="mixer_b",
    )(h, h, h, h, h, gain4, wuv_t, bias)


def _mixer_c_kernel(q_ref, k_ref, v_ref, g_ref, a_ref, aup_ref, abias_ref, gain_ref, o_ref, *, seq):
    ch = C_CHUNK
    kw = C_HEADS * C_KEY_DIM
    vw = C_HEADS * C_VAL_DIM
    tri = jnp.where(_iota((ch, ch), 1) <= _iota((ch, ch), 0), 1.0, 0.0).astype(BF16)
    causal = _iota((C_HEADS * ch, ch), 1) <= (_iota((C_HEADS * ch, ch), 0) % ch)
    klane_head = _iota((C_HEADS * ch, kw), 1) // C_KEY_DIM
    krow_head = _iota((C_HEADS * ch, kw), 0) // ch
    vlane_head = _iota((ch, vw), 1) // C_VAL_DIM
    st_mask = (_iota((vw, kw), 0) // C_VAL_DIM) == (_iota((vw, kw), 1) // C_KEY_DIM)
    gmat = _group_matrix(vw, C_VAL_DIM, 1.0 / C_VAL_DIM)
    aup = aup_ref[...]

    def body(c, st):
        rows = pl.ds(pl.multiple_of(c * ch, ch), ch)
        qc = q_ref[rows, :] * (C_KEY_DIM ** -0.5)
        kc = k_ref[rows, :]
        vc = v_ref[rows, :]
        z = _dot3(a_ref[rows, :], aup) + abias_ref[...]
        log_a = -_softplus(-z) / C_GATE_TAU
        cum = _dot_lhs_exact(tri, log_a)
        last = cum[ch - 1:ch, :]
        q_dec = qc * jnp.exp(cum)
        k_inv = (kc * jnp.exp(-cum)).astype(BF16)
        k_dec = (kc * jnp.exp(last - cum)).astype(BF16)
        vb = vc.astype(BF16)
        qd_b = q_dec.astype(BF16)
        q_stack = jnp.where(klane_head == krow_head, jnp.concatenate([q_dec] * C_HEADS, 0), 0.0)
        att = jnp.where(causal, _dotg(q_stack.astype(BF16), k_inv, NT), 0.0)
        full = _dot(att.astype(BF16), vb)
        o = _dotg(qd_b, st.astype(BF16), NT)
        for h in range(C_HEADS):
            o = o + jnp.where(vlane_head == h, full[h * ch:(h + 1) * ch], 0.0)
        upd = _dotg(vb, k_dec, TN)
        st = st * jnp.exp(last) + jnp.where(st_mask, upd, 0.0)
        ms = _dot_rhs_exact(o * o, gmat)
        o = o * lax.rsqrt(ms + 1e-6) * gain_ref[...]
        g = g_ref[rows, :]
        o_ref[rows, :] = g * _sigmoid(g) * o
        return st

    lax.fori_loop(0, seq // ch, body, jnp.zeros((vw, kw), F32))


def _mixer_c(h, aup_pad, abias, gain, bsz, seq):
    assert seq % C_CHUNK == 0

    def piece(name):
        off, w = _PIECE_OFF[name]
        return pl.BlockSpec((seq, w * LANES), lambda b, o=off // w: (b, o))

    def const(a):
        return pl.BlockSpec(a.shape, lambda b: (0, 0))

    return pl.pallas_call(
        functools.partial(_mixer_c_kernel, seq=seq),
        grid=(bsz,),
        in_specs=[piece("c_q"), piece("c_k"), piece("c_v"), piece("c_g"), piece("c_a"),
                  const(aup_pad), const(abias), const(gain)],
        out_specs=pl.BlockSpec((seq, BRANCH_WIDTH), lambda b: (b, 0)),
        out_shape=jax.ShapeDtypeStruct((bsz * seq, BRANCH_WIDTH), F32),
        compiler_params=_params("parallel"),
        name="mixer_c",
    )(h, h, h, h, h, aup_pad, abias, gain)


def _mixer_d_kernel(r_ref, k_ref, v_ref, low_ref, mu_ref, mulow_ref, w2_ref, a2_ref, g2_ref, vec_ref,
                    o_ref, *, seq, nsub):
    ch = D_CHUNK
    bw = BRANCH_WIDTH
    nh = D_HEADS
    hs = D_HEAD_SIZE
    r_i = _iota((bw, bw), 0)
    c_i = _iota((bw, bw), 1)
    same = (r_i // hs) == (c_i // hs)
    strict = c_i < r_i
    incl = c_i <= r_i
    eye = r_i == c_i
    ones_bd = _group_matrix(bw, hs, 1.0)
    avg_bd = _group_matrix(bw, hs, 1.0 / hs)
    rows = nsub * ch
    tr_i = _iota((rows, rows), 0)
    tc_i = _iota((rows, rows), 1)
    tri = jnp.where(jnp.logical_and(tr_i // ch == tc_i // ch, tc_i <= tr_i), 1.0, 0.0).astype(BF16)
    row0 = _iota((rows, bw), 0) == 0
    row0_low = _iota((rows, LANES), 0) == 0
    vec = vec_ref[...]
    w0, a0, k_k, k_a, r_k, gn_w, gn_b = (vec[n:n + 1, :] for n in range(7))
    mu = mu_ref[...]
    w2, a2, g2 = w2_ref[...], a2_ref[...], g2_ref[...]

    def wide(x):
        return jnp.where(same, jnp.concatenate([x] * nh, 0), 0.0)

    def shifted(ref, start, first_row, m):
        cur = ref[pl.ds(start, rows), :]
        last8 = ref[pl.ds(pl.multiple_of(jnp.maximum(start - 8, 0), 8), 8), :]
        prev_row = last8[7:8, :] * jnp.where(start > 0, 1.0, 0.0)
        prev = jnp.where(first_row, prev_row, pltpu.roll(cur, 1, 0))
        return cur + (prev - cur) * m

    def body(c, st):
        start = pl.multiple_of(c * rows, rows)
        r = shifted(r_ref, start, row0, mu[0:1, :])
        k = shifted(k_ref, start, row0, mu[1:2, :])
        v = shifted(v_ref, start, row0, mu[2:3, :])
        low = shifted(low_ref, start, row0_low, mulow_ref[...])
        w_raw = -_softplus(-(w0 + _dot3(jnp.tanh(low), w2))) - 0.5
        lw = -jnp.exp(w_raw)
        a = _sigmoid(a0 + _dot3(low, a2))
        g = _dot3(_sigmoid(low), g2)
        kk = k * k_k
        kk = kk / jnp.maximum(jnp.sqrt(_dot_rhs_exact(kk * kk, ones_bd)), 1e-12)
        k2 = k * (1.0 + (a - 1.0) * k_a)
        bonus = _dot_rhs_exact(r * k2 * r_k, ones_bd) * v
        b = kk * a

        cum = _dot_lhs_exact(tri, lw)
        e_in = jnp.exp(cum)
        e_inv = jnp.exp(-cum)
        a_dec = -kk * jnp.exp(cum - lw)
        r_dec = r * e_in
        b_inv = b * e_inv
        k_inv = k2 * e_inv
        local = [chunk_terms(*(z[s * ch:(s + 1) * ch] for z in (a_dec, r_dec, b_inv, k_inv, v, e_in)))
                 for s in range(nsub)]
        ys = []
        for qp, y0, gmat, hmat, g_col in local:
            st16 = st.astype(BF16)
            y_w = _dot(qp, st16) + y0
            st = g_col * st + _dot(gmat, st16) + hmat
            y = y_w[0:ch]
            for h in range(1, nh):
                y = y + y_w[h * ch:(h + 1) * ch]
            ys.append(y)
        y = jnp.concatenate(ys, 0)
        mean = _dot_rhs_exact(y, avg_bd)
        yc = y - mean
        var = _dot_rhs_exact(yc * yc, avg_bd)
        yn = yc * lax.rsqrt(var + D_GN_EPS) * gn_w + gn_b
        o_ref[pl.ds(start, rows), :] = (yn + bonus) * g
        return st

    def chunk_terms(a_dec, r_dec, b_inv, k_inv, v, e_in):
        g_last = e_in[ch - 1:ch, :]
        xa = wide(a_dec).astype(BF16)
        xr_f = wide(r_dec)
        xr = xr_f.astype(BF16)
        yb = wide(b_inv).astype(BF16)
        yk = wide(k_inv).astype(BF16)
        bh = wide(b_inv * g_last).astype(BF16)
        kh = wide(k_inv * g_last).astype(BF16)
        vw = wide(v).astype(BF16)

        a_ab = jnp.where(strict, _dotg(xa, yb, NT), 0.0)
        a_ak = jnp.where(strict, _dotg(xa, yk, NT), 0.0).astype(BF16)
        m_rb = jnp.where(incl, _dotg(xr, yb, NT), 0.0).astype(BF16)
        m_rk = jnp.where(incl, _dotg(xr, yk, NT), 0.0).astype(BF16)

        t_inv = jnp.where(eye, 1.0, a_ab)
        apow = a_ab
        for _ in range(int(math.log2(ch)) - 1):
            ab16 = apow.astype(BF16)
            apow = _dot(ab16, ab16)
            t_inv = t_inv + _dot(t_inv.astype(BF16), apow.astype(BF16))
        t16 = t_inv.astype(BF16)

        atp = _dot(t16, xa).astype(BF16)
        vp = _dot(t16, _dot(a_ak, vw).astype(BF16)).astype(BF16)
        qp = (xr_f + _dot(m_rb, atp)).astype(BF16)
        y0 = _dot(m_rb, vp) + _dot(m_rk, vw)
        gmat = _dotg(bh, atp, TN).astype(BF16)
        hmat = _dotg(bh, vp, TN) + _dotg(kh, vw, TN)
        g_col = jnp.sum(jnp.where(eye, g_last, 0.0), -1, keepdims=True)
        return qp, y0, gmat, hmat, g_col

    lax.fori_loop(0, seq // rows, body, jnp.zeros((bw, bw), F32))


def _mixer_d(h, mu3, mulow, w2p, a2p, g2p, vec, bsz, seq, nsub=2):
    assert seq % (nsub * D_CHUNK) == 0 and D_CHUNK == D_HEAD_SIZE

    def piece(name):
        off, w = _PIECE_OFF[name]
        return pl.BlockSpec((seq, w * LANES), lambda b, o=off // w: (b, o))

    def const(a):
        return pl.BlockSpec(a.shape, lambda b: (0, 0))

    return pl.pallas_call(
        functools.partial(_mixer_d_kernel, seq=seq, nsub=nsub),
        grid=(bsz,),
        in_specs=[piece("d_r"), piece("d_k"), piece("d_v"), piece("d_low"),
                  const(mu3), const(mulow), const(w2p), const(a2p), const(g2p), const(vec)],
        out_specs=pl.BlockSpec((seq, BRANCH_WIDTH), lambda b: (b, 0)),
        out_shape=jax.ShapeDtypeStruct((bsz * seq, BRANCH_WIDTH), F32),
        compiler_params=_params("parallel"),
        name="mixer_d",
    )(h, h, h, h, mu3, mulow, w2p, a2p, g2p, vec)


def _merge_kernel(x_ref, ya_ref, yb_ref, yc_ref, yd_ref, wg_ref, wb_ref, wo_ref, g_ref, b_ref, o_ref,
                  *, alpha):
    x = x_ref[...]
    xb = x.astype(BF16)
    merged = None
    for n, y_ref in enumerate((ya_ref, yb_ref, yc_ref, yd_ref)):
        gate = _sigmoid(_dot(xb, wg_ref[:, n * D_MODEL:(n + 1) * D_MODEL]))
        term = gate * _dot(y_ref[...].astype(BF16), wb_ref[n])
        merged = term if merged is None else merged + term
    z = alpha * x + _dot(merged.astype(BF16), wo_ref[...])
    o_ref[...] = _layer_norm(z, g_ref[...], b_ref[...])


def _merge(x2d, ys, wg, wb, wo, g, b, alpha, tm):
    n, d = x2d.shape

    def const(a):
        nd = a.ndim
        return pl.BlockSpec(a.shape, lambda i: (0,) * nd)

    yspec = pl.BlockSpec((tm, BRANCH_WIDTH), lambda i: (i, 0))
    return pl.pallas_call(
        functools.partial(_merge_kernel, alpha=alpha),
        grid=(n // tm,),
        in_specs=[pl.BlockSpec((tm, d), lambda i: (i, 0)), yspec, yspec, yspec, yspec,
                  const(wg), const(wb), const(wo), const(g), const(b)],
        out_specs=pl.BlockSpec((tm, d), lambda i: (i, 0)),
        out_shape=jax.ShapeDtypeStruct((n, d), F32),
        compiler_params=_params("parallel"),
        name="merge",
    )(x2d, *ys, wg, wb, wo, g, b)


def _moe_kernel(x_ref, ltri_ref, wr_ref, br_ref, wg_ref, wu_ref, wd_ref, g_ref, b_ref, o_ref,
                xs_ref, gs_ref, ys_ref, pt_ref, seg_ref, *, alpha):
    e = pl.program_id(1)
    tm = x_ref.shape[0]
    nrow = xs_ref.shape[0]
    ck = MOE_CHUNK
    lane = _iota((tm, LANES), 1)

    @pl.when(e == 0)
    def _():
        x = x_ref[...]
        xb = x.astype(BF16)
        ys_ref[...] = jnp.zeros_like(ys_ref)
        x_lo = (x - xb.astype(F32)).astype(BF16)
        cross = _dot(jnp.concatenate([xb, x_lo], 0), wr_ref[...])
        logits = (cross[0:tm, 0:LANES] + (cross[0:tm, LANES:] + cross[tm:, 0:LANES])) + br_ref[...]
        is_g = jnp.logical_and(lane >= MOE_EXPERTS, lane < MOE_EXPERTS + MOE_GROUPS)
        lg = jnp.where(is_g, logits, -jnp.inf)
        gmax = jnp.max(lg, -1, keepdims=True)
        ptop = 1.0 / jnp.sum(jnp.exp(lg - gmax), -1, keepdims=True)
        gsel = jnp.min(jnp.where(lg == gmax, lane, 2 * LANES), -1, keepdims=True) - MOE_EXPERTS
        in_group = jnp.logical_and(lane < MOE_EXPERTS, lane // MOE_PER_GROUP == gsel)
        le = jnp.where(in_group, logits, -jnp.inf)
        v1 = jnp.max(le, -1, keepdims=True)
        i1 = jnp.min(jnp.where(le == v1, lane, 2 * LANES), -1, keepdims=True)
        le2 = jnp.where(lane == i1, -jnp.inf, le)
        v2 = jnp.max(le2, -1, keepdims=True)
        i2 = jnp.min(jnp.where(le2 == v2, lane, 2 * LANES), -1, keepdims=True)
        e2 = jnp.exp(v2 - v1)
        w1 = ptop / (1.0 + e2)
        gate = jnp.where(lane == i1, w1, jnp.where(lane == i2, w1 * e2, 0.0))

        onehot = jnp.where(lane == gsel, 1.0, 0.0)
        rank = _dot(ltri_ref[...], onehot.astype(BF16))
        cnt = jnp.sum(onehot, 0, keepdims=True).astype(I32)
        nch = jnp.right_shift(cnt + (ck - 1), int(math.log2(ck)))
        lane1 = _iota((1, LANES), 1)
        off = jnp.int32(0)
        off_vec = jnp.zeros((1, LANES), I32)
        for grp in range(MOE_GROUPS):
            n_g = nch[0, grp]
            seg_ref[grp] = off
            seg_ref[MOE_GROUPS + grp] = n_g
            off_vec = jnp.where(lane1 == grp, off, off_vec)
            off = off + n_g * ck
        dest = jnp.sum(onehot * (rank + off_vec.astype(F32)), -1, keepdims=True)
        pt_ref[...] = jnp.where(_iota((tm, nrow), 1) == dest.astype(I32), 1.0, 0.0).astype(BF16)
        dest_row = jnp.broadcast_to(dest, (tm, LANES)).T[0:1, :].astype(I32)
        perm = jnp.where(_iota((nrow, tm), 0) == dest_row, 1.0, 0.0).astype(BF16)
        gate_hi, gate_lo = _split2(gate)
        moved = _dot(perm, jnp.concatenate([xb, gate_hi, gate_lo], 1))
        d = x_ref.shape[1]
        xs_ref[...] = moved[:, 0:d].astype(BF16)
        gs_ref[...] = moved[:, d:d + LANES] + moved[:, d + LANES:]

    grp = e // (MOE_PER_GROUP // 2)
    seg_off = seg_ref[grp]
    seg_chunks = seg_ref[MOE_GROUPS + grp]

    def run_experts(start, size):
        rows = pl.ds(pl.multiple_of(start, ck), size)
        xc = xs_ref[rows, :]
        gs = gs_ref[rows, :]
        lane_s = _iota((size, LANES), 1)
        g0 = jnp.sum(jnp.where(lane_s == 2 * e, gs, 0.0), -1, keepdims=True)
        g1 = jnp.sum(jnp.where(lane_s == 2 * e + 1, gs, 0.0), -1, keepdims=True)
        hg = _dot(xc, wg_ref[0])
        hu = _dot(xc, wu_ref[0])
        gcol = jnp.where(_iota((size, 2 * MOE_HIDDEN), 1) < MOE_HIDDEN, g0, g1)
        hidden = hg * _sigmoid(hg) * hu * gcol
        ys_ref[rows, :] += _dot(hidden.astype(BF16), wd_ref[0])

    big = 4

    def quad(c, carry):
        run_experts(seg_off + c * (big * ck), big * ck)
        return carry

    lax.fori_loop(0, seg_chunks // big, quad, 0)
    rem = seg_chunks % big
    rem_start = seg_off + (seg_chunks - rem) * ck
    for r in range(1, big):
        @pl.when(rem == r)
        def _(r=r):
            run_experts(rem_start, r * ck)

    @pl.when(e == pl.num_programs(1) - 1)
    def _():
        z = alpha * x_ref[...] + _dot(pt_ref[...], ys_ref[...].astype(BF16))
        o_ref[...] = _layer_norm(z, g_ref[...], b_ref[...])


def _moe(x2d, wr, br, wg, wu, wd, g, b, alpha, tm):
    n, d = x2d.shape
    ne = wg.shape[0]
    hid = wg.shape[2]
    nrow = tm + MOE_GROUPS * MOE_CHUNK
    ltri = jnp.tril(jnp.ones((tm, tm), BF16), -1)
    wg = wg.reshape(ne // 2, 2, d, hid).transpose(0, 2, 1, 3).reshape(ne // 2, d, 2 * hid)
    wu = wu.reshape(ne // 2, 2, d, hid).transpose(0, 2, 1, 3).reshape(ne // 2, d, 2 * hid)
    wd = wd.reshape(ne // 2, 2 * hid, d)

    def const(a):
        return pl.BlockSpec(a.shape, lambda i, e: (0, 0))

    return pl.pallas_call(
        functools.partial(_moe_kernel, alpha=alpha),
        grid=(n // tm, ne // 2),
        in_specs=[pl.BlockSpec((tm, d), lambda i, e: (i, 0)), const(ltri), const(wr), const(br),
                  pl.BlockSpec((1, d, 2 * hid), lambda i, e: (e, 0, 0)),
                  pl.BlockSpec((1, d, 2 * hid), lambda i, e: (e, 0, 0)),
                  pl.BlockSpec((1, 2 * hid, d), lambda i, e: (e, 0, 0)),
                  const(g), const(b)],
        out_specs=pl.BlockSpec((tm, d), lambda i, e: (i, 0)),
        out_shape=jax.ShapeDtypeStruct((n, d), F32),
        scratch_shapes=[pltpu.VMEM((nrow, d), BF16), pltpu.VMEM((nrow, LANES), F32),
                        pltpu.VMEM((nrow, d), F32), pltpu.VMEM((tm, nrow), BF16),
                        pltpu.SMEM((2 * MOE_GROUPS,), I32)],
        compiler_params=_params("parallel", "arbitrary"),
        name="moe",
    )(x2d, ltri, wr, br, wg, wu, wd, g, b)


def _t5_bucket(dist):
    exact = RPB_BUCKETS // 2
    d = jnp.maximum(dist, 0)
    df = jnp.maximum(d, 1).astype(F32)
    large = exact + (jnp.log(df / exact) / math.log(RPB_MAX_DIST / exact)
                     * (RPB_BUCKETS - exact)).astype(I32)
    large = jnp.minimum(large, RPB_BUCKETS - 1)
    return jnp.where(d < exact, d, large)


def _rpb_lookup(tab, dist):
    bucket = _t5_bucket(dist)[..., None]
    out = jnp.zeros(dist.shape + (tab.shape[1],), F32)
    for k in range(RPB_BUCKETS):
        out = jnp.where(bucket == k, tab[k], out)
    return out


def _bias_a(tab):
    qi = jnp.arange(A_BLOCK)[:, None]
    kj = jnp.arange(2 * A_BLOCK)[None, :]
    rdist = qi + A_BLOCK - kj
    out = []
    for window, dilation in A_PATTERNS:
        in_band = (rdist >= 0) & (rdist <= window // dilation)
        bias = jnp.transpose(_rpb_lookup(tab, rdist * dilation), (2, 0, 1))
        out.append(jnp.where(in_band[None], bias, NEG))
    return jnp.stack(out).astype(F32)


def _bias_b(tab, nblk):
    qi = jnp.arange(Q_BLOCK)[:, None]
    kj = jnp.arange(Q_BLOCK)[None, :]
    delta = jnp.arange(nblk)[:, None, None] * Q_BLOCK
    bias = _rpb_lookup(tab, delta + qi - kj)
    return jnp.transpose(bias, (0, 3, 2, 1)).astype(F32)


def _in_offsets():
    splits = (("a_q", 256), ("a_k", 256), ("a_v", 256), ("b_q", 256), ("b_ckv", 64), ("b_iq", 256),
              ("b_ik", 32), ("b_iw", 8), ("c_q", 128), ("c_k", 128), ("c_v", 256), ("c_a", 16),
              ("c_g", 256), ("d", 832), ("gate", 4096))
    off, out = 0, {}
    for name, w in splits:
        out[name] = (off, w)
        off += w
    return out


def _proj_columns():
    src = _in_offsets()
    d0 = src["d"][0]
    named = {
        "a_q": np.arange(256) + src["a_q"][0], "a_k": np.arange(256) + src["a_k"][0],
        "a_v": np.arange(256) + src["a_v"][0], "b_q": np.arange(256) + src["b_q"][0],
        "b_iq": np.arange(256) + src["b_iq"][0],
        "b_ckv4": np.tile(np.arange(64) + src["b_ckv"][0], B_HEADS),
        "b_ik8": np.tile(np.arange(32) + src["b_ik"][0], IDX_HEADS),
        "c_v": np.arange(256) + src["c_v"][0], "c_g": np.arange(256) + src["c_g"][0],
        "d_r": np.arange(256) + d0, "d_k": np.arange(256) + d0 + 256, "d_v": np.arange(256) + d0 + 512,
        "c_q": np.arange(128) + src["c_q"][0], "c_k": np.arange(128) + src["c_k"][0],
        "b_iw": np.arange(8) + src["b_iw"][0], "c_a": np.arange(16) + src["c_a"][0],
        "d_low": np.arange(64) + d0 + 768, "pad": np.arange(0),
    }
    cols = []
    for name, w in _PIECES:
        c = named[name]
        cols.append(np.concatenate([c, -np.ones(w * LANES - len(c), np.int64)]))
    return np.concatenate(cols)


def _proj_weight(w):
    cols = _proj_columns()
    cuts = [0] + [k for k in range(1, len(cols))
                  if (cols[k] < 0) != (cols[k - 1] < 0) or (cols[k] >= 0 and cols[k] != cols[k - 1] + 1)]
    cuts.append(len(cols))
    parts = []
    for a, b in zip(cuts[:-1], cuts[1:]):
        if cols[a] < 0:
            parts.append(jnp.zeros((w.shape[0], b - a), BF16))
        else:
            parts.append(w[:, int(cols[a]):int(cols[a]) + (b - a)].astype(BF16))
    return jnp.concatenate(parts, 1)


def _pad_rows(w, first, total):
    return jnp.zeros((total, w.shape[1]), w.dtype).at[first:first + w.shape[0]].set(w)


def kernel(x, rpb_table, w_in, b_kv_gain, b_w_uv, c_a_up, c_a_bias, c_norm_gain, d_mu, d_w0, d_w2,
           d_a0, d_a2, d_g2, d_k_k, d_k_a, d_r_k, d_gn_w, d_gn_b, w_branch, w_out, ln_g, ln_b,
           router_g, router_g_bias, router_e, router_e_bias, moe_w_gate, moe_w_up, moe_w_down):
    bsz, seq, d_model = x.shape
    depth = w_in.shape[0]
    n = bsz * seq
    alpha = (2 * depth) ** 0.25
    gate_off = _in_offsets()["gate"][0]
    bias_a = _bias_a(rpb_table[:, :A_HEADS])
    bias_b = _bias_b(rpb_table[:, A_HEADS:], seq // Q_BLOCK)
    tm_proj = math.gcd(n, 1024)
    tm_merge = math.gcd(n, 512)
    tm_moe = math.gcd(n, 1024)

    x2d = x.reshape(n, d_model)
    for l in range(depth):
        h = _project(x2d, _proj_weight(w_in[l]), tm_proj, PROJ_WIDTH // 2)

        ya = _mixer_a(h, bias_a, bsz, seq)

        gain4 = jnp.tile(b_kv_gain[l], B_HEADS)[None, :]
        wuv_t = jnp.transpose(b_w_uv[l], (0, 2, 1)).astype(BF16)
        yb = _mixer_b(h, gain4, wuv_t, bias_b, bsz, seq)

        aup_pad = _pad_rows(c_a_up[l], 0, LANES)
        yc = _mixer_c(h, aup_pad, c_a_bias[l][None, :], c_norm_gain[l][None, :], bsz, seq)

        mu = d_mu[l]
        mu3 = mu[:3 * BRANCH_WIDTH].reshape(3, BRANCH_WIDTH)
        mulow = jnp.zeros((1, LANES), F32).at[0, :D_DECAY_RANK + D_ICLR_RANK + D_GATE_RANK].set(
            mu[3 * BRANCH_WIDTH:])
        w2p = _pad_rows(d_w2[l], 0, LANES)
        a2p = _pad_rows(d_a2[l], D_DECAY_RANK, LANES)
        g2p = _pad_rows(d_g2[l], D_DECAY_RANK + D_ICLR_RANK, LANES)
        vec = jnp.stack([d_w0[l], d_a0[l], d_k_k[l], d_k_a[l], d_r_k[l], d_gn_w[l], d_gn_b[l],
                         jnp.zeros_like(d_w0[l])])
        yd = _mixer_d(h, mu3, mulow, w2p, a2p, g2p, vec, bsz, seq)

        wg = w_in[l][:, gate_off:gate_off + N_BRANCHES * d_model].astype(BF16)
        x2d = _merge(x2d, (ya, yb, yc, yd), wg, w_branch[l].astype(BF16), w_out[l].astype(BF16),
                     ln_g[l, 0][None, :], ln_b[l, 0][None, :], alpha, tm_merge)

        wr = jnp.zeros((d_model, LANES), F32)
        wr = wr.at[:, :MOE_EXPERTS].set(router_e[l]).at[:, MOE_EXPERTS:MOE_EXPERTS + MOE_GROUPS].set(
            router_g[l])
        br = jnp.zeros((1, LANES), F32)
        br = br.at[0, :MOE_EXPERTS].set(router_e_bias[l]).at[0, MOE_EXPERTS:MOE_EXPERTS + MOE_GROUPS].set(
            router_g_bias[l])
        wr = jnp.concatenate(_split2(wr), 1)
        x2d = _moe(x2d, wr, br, moe_w_gate[l].astype(BF16), moe_w_up[l].astype(BF16),
                   moe_w_down[l].astype(BF16), ln_g[l, 1][None, :], ln_b[l, 1][None, :], alpha, tm_moe)
    return x2d.reshape(bsz, seq, d_model)
```

```python
import functools
import math

import jax
import jax.numpy as jnp
import numpy as np
from jax import lax
from jax.experimental import pallas as pl
from jax.experimental.pallas import tpu as pltpu

F32 = jnp.float32
BF16 = jnp.bfloat16
I32 = jnp.int32

D_MODEL = 1024
N_BRANCHES = 4
BRANCH_WIDTH = 256
HEAD_DIM = 64
A_HEADS = 4
A_PATTERNS = ((128, 1), (512, 4), (2048, 16))
A_BLOCK = 128
B_HEADS = 4
B_LATENT = 64
IDX_HEADS = 8
IDX_DIM = 32
TOPK_MAX = 256
TOPK_DIV = 4
Q_BLOCK = 128
C_HEADS = 4
C_KEY_DIM = 32
C_VAL_DIM = 64
C_GATE_RANK = 16
C_GATE_TAU = 16.0
C_CHUNK = 64
D_HEADS = 4
D_HEAD_SIZE = 64
D_DECAY_RANK = 16
D_ICLR_RANK = 16
D_GATE_RANK = 32
D_GN_EPS = 64e-5
D_CHUNK = 64
RPB_BUCKETS = 32
RPB_MAX_DIST = 2048
MOE_GROUPS = 4
MOE_PER_GROUP = 8
MOE_EXPERTS = 32
MOE_HIDDEN = 256
MOE_CHUNK = 128
LN_EPS = 1e-5

LANES = 128
NEG = -1e30
VMEM_LIMIT = 56 * 1024 * 1024

NT = (((1,), (1,)), ((), ()))
TN = (((0,), (0,)), ((), ()))

_PIECES = (
    ("a_q", 2), ("a_k", 2), ("a_v", 2), ("b_q", 2), ("b_iq", 2), ("b_ckv4", 2), ("b_ik8", 2),
    ("c_v", 2), ("c_g", 2), ("d_r", 2), ("d_k", 2), ("d_v", 2),
    ("c_q", 1), ("c_k", 1), ("b_iw", 1), ("c_a", 1), ("d_low", 1), ("pad", 1),
)
_PIECE_OFF = {}
_off = 0
for _n, _w in _PIECES:
    _PIECE_OFF[_n] = (_off, _w)
    _off += _w
PROJ_WIDTH = _off * LANES


def _dot(a, b):
    return jnp.dot(a, b, preferred_element_type=F32)


def _dotg(a, b, dims):
    return lax.dot_general(a, b, dims, preferred_element_type=F32)


def _split2(x):
    hi = x.astype(BF16)
    lo = (x - hi.astype(F32)).astype(BF16)
    return hi, lo


def _dot_rhs_exact(x, m):
    hi, lo = _split2(x)
    return _dot(hi, m) + _dot(lo, m)


def _dot_lhs_exact(m, x):
    hi, lo = _split2(x)
    return _dot(m, hi) + _dot(m, lo)


def _dot3(x, w):
    xh, xl = _split2(x)
    wh, wl = _split2(w)
    return _dot(xh, wh) + (_dot(xl, wh) + _dot(xh, wl))


def _sigmoid(x):
    return 1.0 / (1.0 + jnp.exp(-x))


def _softplus(x):
    return jnp.maximum(x, 0.0) + jnp.log(1.0 + jnp.exp(-jnp.abs(x)))


def _iota(shape, axis):
    return lax.broadcasted_iota(I32, shape, axis)


def _group_matrix(n, group, value, dtype=BF16):
    r = _iota((n, n), 0) // group
    c = _iota((n, n), 1) // group
    return jnp.where(r == c, value, 0.0).astype(dtype)


def _layer_norm(z, g, b):
    mu = jnp.mean(z, -1, keepdims=True)
    zc = z - mu
    var = jnp.mean(zc * zc, -1, keepdims=True)
    return zc * lax.rsqrt(var + LN_EPS) * g + b


def _params(*sem):
    return pltpu.CompilerParams(dimension_semantics=sem, vmem_limit_bytes=VMEM_LIMIT)


def _proj_kernel(x_ref, w_ref, o_ref):
    o_ref[...] = _dot(x_ref[...].astype(BF16), w_ref[...])


def _project(x2d, w, tm, tn):
    n, k = x2d.shape
    wt = w.shape[1]
    return pl.pallas_call(
        _proj_kernel,
        grid=(n // tm, wt // tn),
        in_specs=[pl.BlockSpec((tm, k), lambda i, j: (i, 0)),
                  pl.BlockSpec((k, tn), lambda i, j: (0, j))],
        out_specs=pl.BlockSpec((tm, tn), lambda i, j: (i, j)),
        out_shape=jax.ShapeDtypeStruct((n, wt), F32),
        compiler_params=_params("parallel", "parallel"),
        name="proj",
    )(x2d, w)


def _mixer_a_kernel(q0_ref, q1_ref, k0_ref, k1_ref, v0_ref, v1_ref, bias_ref, o_ref, op_ref, lse_ref,
                    *, seq, dils):
    q_refs, k_refs, v_refs = (q0_ref, q1_ref), (k0_ref, k1_ref), (v0_ref, v1_ref)
    ab = A_BLOCK
    lane = _iota((ab, LANES), 1)
    upper = lane >= HEAD_DIM
    col = _iota((ab, 2 * ab), 1)
    scale = HEAD_DIM ** -0.5

    for p, d in enumerate(dils):
        per_res = seq // (d * ab)

        def body(idx, carry, p=p, d=d, per_res=per_res):
            r = idx // per_res
            b = idx % per_res
            start = b * (ab * d) + r
            pstart = jnp.maximum(start - ab * d, r)
            if d == 1:
                rows, prow = pl.ds(pl.multiple_of(start, ab), ab), pl.ds(pl.multiple_of(pstart, ab), ab)
            else:
                rows, prow = pl.ds(start, ab, stride=d), pl.ds(pstart, ab, stride=d)
            first_pen = jnp.where(b == 0, NEG, 0.0)
            pen = jnp.where(col < ab, first_pen, 0.0)
            for hp in range(2):
                q_ref, k_ref, v_ref = q_refs[hp], k_refs[hp], v_refs[hp]
                q = q_ref[rows, :]
                kb = jnp.concatenate([k_ref[prow, :], k_ref[rows, :]], 0).astype(BF16)
                vb = jnp.concatenate([v_ref[prow, :], v_ref[rows, :]], 0).astype(BF16)
                outs, lses = [], []
                for hh in range(2):
                    hm = upper if hh else jnp.logical_not(upper)
                    qm = jnp.where(hm, q, 0.0).astype(BF16)
                    s = _dotg(qm, kb, NT) * scale + bias_ref[p, 2 * hp + hh] + pen
                    mx = jnp.max(s, -1, keepdims=True)
                    e = jnp.exp(s - mx)
                    den = jnp.sum(e, -1, keepdims=True)
                    outs.append(_dot(e.astype(BF16), vb) / den)
                    lses.append(mx + jnp.log(den))
                op_ref[2 * p + hp, rows, :] = jnp.where(upper, outs[1], outs[0])
                lse_ref[2 * p + hp, rows, :] = jnp.where(upper, lses[1], lses[0])
            return carry

        def body_pair(t, carry, body=body):
            return body(2 * t + 1, body(2 * t, carry))

        lax.fori_loop(0, seq // (2 * ab), body_pair, 0)

    cr = 256

    def combine(c, carry):
        rows = pl.ds(pl.multiple_of(c * cr, cr), cr)
        for hp in range(2):
            ls = [lse_ref[2 * p + hp, rows, :] for p in range(len(dils))]
            mx = functools.reduce(jnp.maximum, ls)
            es = [jnp.exp(l - mx) for l in ls]
            num = functools.reduce(lambda a, b: a + b,
                                   [e * op_ref[2 * p + hp, rows, :] for p, e in enumerate(es)])
            o_ref[rows, hp * LANES:(hp + 1) * LANES] = num / functools.reduce(lambda a, b: a + b, es)
        return carry

    lax.fori_loop(0, seq // cr, combine, 0)


def _mixer_a(h, bias, bsz, seq):
    dils = tuple(d for _, d in A_PATTERNS)
    for w, d in A_PATTERNS:
        assert w // d == A_BLOCK and seq % (d * A_BLOCK) == 0
    npat = len(dils)

    def pieces(name):
        off, w = _PIECE_OFF[name]
        return [pl.BlockSpec((seq, LANES), lambda b, o=off + u: (b, o)) for u in range(w)]

    return pl.pallas_call(
        functools.partial(_mixer_a_kernel, seq=seq, dils=dils),
        grid=(bsz,),
        in_specs=pieces("a_q") + pieces("a_k") + pieces("a_v")
        + [pl.BlockSpec(bias.shape, lambda b: (0, 0, 0, 0))],
        out_specs=pl.BlockSpec((seq, BRANCH_WIDTH), lambda b: (b, 0)),
        out_shape=jax.ShapeDtypeStruct((bsz * seq, BRANCH_WIDTH), F32),
        scratch_shapes=[pltpu.VMEM((2 * npat, seq, LANES), F32),
                        pltpu.VMEM((2 * npat, seq, LANES), F32)],
        compiler_params=_params("parallel"),
        name="mixer_a",
    )(h, h, h, h, h, h, bias)


def _mixer_b_kernel(q_ref, iq_ref, iw_ref, ckv_ref, ik_ref, gain_ref, wuvt_ref, bias_ref, o_ref,
                    ckvn_ref, ckvt_ref, ikb_ref, sc_ref, lg_ref, iqs_ref, qs_ref, *, seq, keep):
    qb = Q_BLOCK
    sb = 2 * qb
    i = pl.program_id(1)

    @pl.when(i == 0)
    def _():
        gmat = _group_matrix(BRANCH_WIDTH, B_LATENT, 1.0 / B_LATENT)

        def prep(c, carry):
            rows = pl.ds(pl.multiple_of(c * sb, sb), sb)
            x = ckv_ref[rows, :]
            ms = _dot_rhs_exact(x * x, gmat)
            xn = x * lax.rsqrt(ms + 1e-6) * gain_ref[...]
            ckvn_ref[rows, :] = xn.astype(BF16)
            ckvt_ref[c] = xn.T[0:B_LATENT, :].astype(BF16)
            ikb_ref[rows, :] = ik_ref[rows, :].astype(BF16)
            return carry

        lax.fori_loop(0, seq // sb, prep, 0)

    lane = _iota((qb, BRANCH_WIDTH), 1)
    iq = iq_ref[...] * (IDX_DIM ** -0.5)
    for h in range(IDX_HEADS):
        iqs_ref[h * qb:(h + 1) * qb, :] = jnp.where(lane // IDX_DIM == h, iq, 0.0).astype(BF16)
    q = q_ref[...] * (B_LATENT ** -0.5)
    for h in range(B_HEADS):
        qs_ref[h * qb:(h + 1) * qb, :] = jnp.where(lane // B_LATENT == h, q, 0.0).astype(BF16)
    iw_t = (iw_ref[...] * (IDX_HEADS ** -0.5)).T
    krow = _iota((sb, qb), 0)
    qcol = _iota((sb, qb), 1)
    nsb = (i + 2) // 2

    def fold(x, op):
        parts = [x[r:r + 8] for r in range(0, sb, 8)]
        while len(parts) > 1:
            parts = [op(parts[k], parts[k + 1]) for k in range(0, len(parts), 2)]
        return parts[0]

    def for_steps(body, init):
        def pair(p, carry):
            return body(2 * p + 1, body(2 * p, carry))

        carry = lax.fori_loop(0, nsb // 2, pair, init)
        return lax.cond(nsb % 2 == 1, lambda c: body(nsb - 1, c), lambda c: c, carry)

    def score_body(j, carry):
        kr = pl.ds(pl.multiple_of(j * sb, sb), sb)
        rel = _dotg(ikb_ref[kr, :], iqs_ref[...], NT)
        sc = jnp.maximum(rel[:, 0:qb], 0.0) * iw_t[0:1, :]
        for h in range(1, IDX_HEADS):
            sc = sc + jnp.maximum(rel[:, h * qb:(h + 1) * qb], 0.0) * iw_t[h:h + 1, :]
        sc_ref[j] = jnp.where(krow <= qcol + (i * qb - j * sb), sc, -jnp.inf)
        logits = _dotg(ckvn_ref[kr, :], qs_ref[...], NT)
        d0 = i - 2 * j
        for h in range(B_HEADS):
            lg = logits[:, h * qb:(h + 1) * qb]
            lg_ref[h, j, 0:qb, :] = lg[0:qb] + bias_ref[d0, h]
            lg_ref[h, j, qb:sb, :] = lg[qb:sb] + bias_ref[jnp.maximum(d0 - 1, 0), h]
        return carry

    for_steps(score_body, 0)

    sign = jnp.int32(-2 ** 31)

    def key_to_float(u):
        k = u ^ sign
        bits = k ^ ((k >> 31) & jnp.int32(0x7FFFFFFF))
        return pltpu.bitcast(bits, F32)

    def count(pred):
        def cnt(j, acc):
            return acc + fold(pred(j), jnp.add)
        acc = for_steps(cnt, jnp.zeros((8, qb), F32))
        return jnp.sum(acc, 0, keepdims=True)

    def bit_body(t, state):
        u, n_u = state
        cand = u | jnp.left_shift(jnp.int32(1), 31 - t)
        cf = key_to_float(cand)
        c = count(lambda j: jnp.where(sc_ref[j] >= cf, 1.0, 0.0))
        take = c >= keep
        return jnp.where(take, cand, u), jnp.where(take, c, n_u)

    first_bits, bit_group = 20, 4
    state = (jnp.zeros((1, qb), I32), (jnp.zeros((1, qb), I32) + nsb * sb).astype(F32))
    u, n_ge = lax.fori_loop(0, first_bits, bit_body, state)

    def more_bits(s):
        t, _, n_u = s
        return jnp.logical_and(t < 32, jnp.max(jnp.abs(n_u - keep)) > 0.0)

    def bit_group_body(s):
        t, u, n_u = s
        for k in range(bit_group):
            u, n_u = bit_body(t + k, (u, n_u))
        return t + bit_group, u, n_u

    _, u, n_ge = lax.while_loop(more_bits, bit_group_body, (jnp.int32(first_bits), u, n_ge))
    u = jnp.maximum(u ^ sign, jnp.int32(0x007FFFFF - 2 ** 31)) ^ sign
    thr = key_to_float(u)
    n_gt = count(lambda j: jnp.where(sc_ref[j] > thr, 1.0, 0.0))
    need = keep - n_gt

    nbits = int(math.ceil(math.log2(seq))) + 1

    def cut_search():
        def cut_body(t, cut):
            cand = cut | jnp.left_shift(jnp.int32(1), nbits - 1 - t)
            c = count(lambda j: jnp.where(sc_ref[j] == thr,
                                          jnp.where(krow < cand - j * sb, 1.0, 0.0), 0.0))
            return jnp.where(c <= need, cand, cut)

        return lax.fori_loop(0, nbits, cut_body, jnp.zeros((1, qb), I32))

    surplus = jnp.max(n_ge) > keep
    cut = lax.cond(surplus, cut_search, lambda: jnp.full((1, qb), 2 ** nbits - 1, I32))
    cut = jnp.minimum(cut, i * qb + 1 + _iota((1, qb), 1))

    def mask_body(j, mx):
        sc = sc_ref[j]
        sel = jnp.where(sc > thr, 1.0, jnp.where(sc == thr, jnp.where(krow < cut - j * sb, 1.0, 0.0), 0.0))
        out = []
        for h in range(B_HEADS):
            s = jnp.where(sel > 0.5, lg_ref[h, j], NEG)
            lg_ref[h, j] = s
            out.append(jnp.maximum(mx[h], fold(s, jnp.maximum)))
        return tuple(out)

    mx = for_steps(mask_body, tuple(jnp.full((8, qb), NEG, F32) for _ in range(B_HEADS)))
    ms = [jnp.max(m, 0, keepdims=True) for m in mx]

    def att_body(j, carry):
        ls, acc = carry
        ps, new_ls = [], []
        for h in range(B_HEADS):
            pr = jnp.exp(lg_ref[h, j] - ms[h])
            new_ls.append(ls[h] + fold(pr, jnp.add))
            ps.append(pr.astype(BF16))
        upd = _dot(ckvt_ref[j], jnp.concatenate(ps, 1))
        return tuple(new_ls), acc + upd

    init = (tuple(jnp.zeros((8, qb), F32) for _ in range(B_HEADS)),
            jnp.zeros((B_LATENT, B_HEADS * qb), F32))
    ls, acc = for_steps(att_body, init)
    ls = [jnp.sum(l, 0, keepdims=True) for l in ls]
    o_t = (acc / jnp.concatenate(ls, 1)).astype(BF16)
    y_t = jnp.concatenate([_dot(wuvt_ref[h], o_t[:, h * qb:(h + 1) * qb]) for h in range(B_HEADS)], 0)
    o_ref[...] = y_t.T


def _mixer_b(h, gain4, wuv_t, bias, bsz, seq):
    qb = Q_BLOCK
    assert seq % 256 == 0
    nblk = seq // qb
    keep = min(TOPK_MAX, seq // TOPK_DIV)

    def qpiece(name):
        off, w = _PIECE_OFF[name]
        return pl.BlockSpec((qb, w * LANES), lambda b, i, o=off // w: (b * nblk + i, o))

    def kpiece(name):
        off, w = _PIECE_OFF[name]
        return pl.BlockSpec((seq, w * LANES), lambda b, i, o=off // w: (b, o))

    return pl.pallas_call(
        functools.partial(_mixer_b_kernel, seq=seq, keep=keep),
        grid=(bsz, nblk),
        in_specs=[qpiece("b_q"), qpiece("b_iq"), qpiece("b_iw"), kpiece("b_ckv4"), kpiece("b_ik8"),
                  pl.BlockSpec(gain4.shape, lambda b, i: (0, 0)),
                  pl.BlockSpec(wuv_t.shape, lambda b, i: (0, 0, 0)),
                  pl.BlockSpec(bias.shape, lambda b, i: (0, 0, 0, 0))],
        out_specs=pl.BlockSpec((qb, BRANCH_WIDTH), lambda b, i: (b * nblk + i, 0)),
        out_shape=jax.ShapeDtypeStruct((bsz * seq, BRANCH_WIDTH), F32),
        scratch_shapes=[pltpu.VMEM((seq, BRANCH_WIDTH), BF16),
                        pltpu.VMEM((nblk // 2, B_LATENT, 2 * qb), BF16),
                        pltpu.VMEM((seq, BRANCH_WIDTH), BF16),
                        pltpu.VMEM((nblk // 2, 2 * qb, qb), F32),
                        pltpu.VMEM((B_HEADS, nblk // 2, 2 * qb, qb), F32),
                        pltpu.VMEM((IDX_HEADS * qb, BRANCH_WIDTH), BF16),
                        pltpu.VMEM((B_HEADS * qb, BRANCH_WIDTH), BF16)],
        compiler_params=_params("parallel", "arbitrary"),
        name="mixer_b",
    )(h, h, h, h, h, gain4, wuv_t, bias)


def _mixer_c_kernel(q_ref, k_ref, v_ref, g_ref, a_ref, aup_ref, abias_ref, gain_ref, o_ref, *, seq):
    ch = C_CHUNK
    kw = C_HEADS * C_KEY_DIM
    vw = C_HEADS * C_VAL_DIM
    tri = jnp.where(_iota((ch, ch), 1) <= _iota((ch, ch), 0), 1.0, 0.0).astype(BF16)
    causal = _iota((C_HEADS * ch, ch), 1) <= (_iota((C_HEADS * ch, ch), 0) % ch)
    klane_head = _iota((C_HEADS * ch, kw), 1) // C_KEY_DIM
    krow_head = _iota((C_HEADS * ch, kw), 0) // ch
    vlane_head = _iota((ch, vw), 1) // C_VAL_DIM
    st_mask = (_iota((vw, kw), 0) // C_VAL_DIM) == (_iota((vw, kw), 1) // C_KEY_DIM)
    gmat = _group_matrix(vw, C_VAL_DIM, 1.0 / C_VAL_DIM)
    aup = aup_ref[...]

    def body(c, st):
        rows = pl.ds(pl.multiple_of(c * ch, ch), ch)
        qc = q_ref[rows, :] * (C_KEY_DIM ** -0.5)
        kc = k_ref[rows, :]
        vc = v_ref[rows, :]
        z = _dot3(a_ref[rows, :], aup) + abias_ref[...]
        log_a = -_softplus(-z) / C_GATE_TAU
        cum = _dot_lhs_exact(tri, log_a)
        last = cum[ch - 1:ch, :]
        q_dec = qc * jnp.exp(cum)
        k_inv = (kc * jnp.exp(-cum)).astype(BF16)
        k_dec = (kc * jnp.exp(last - cum)).astype(BF16)
        vb = vc.astype(BF16)
        qd_b = q_dec.astype(BF16)
        q_stack = jnp.where(klane_head == krow_head, jnp.concatenate([q_dec] * C_HEADS, 0), 0.0)
        att = jnp.where(causal, _dotg(q_stack.astype(BF16), k_inv, NT), 0.0)
        full = _dot(att.astype(BF16), vb)
        o = _dotg(qd_b, st.astype(BF16), NT)
        for h in range(C_HEADS):
            o = o + jnp.where(vlane_head == h, full[h * ch:(h + 1) * ch], 0.0)
        upd = _dotg(vb, k_dec, TN)
        st = st * jnp.exp(last) + jnp.where(st_mask, upd, 0.0)
        ms = _dot_rhs_exact(o * o, gmat)
        o = o * lax.rsqrt(ms + 1e-6) * gain_ref[...]
        g = g_ref[rows, :]
        o_ref[rows, :] = g * _sigmoid(g) * o
        return st

    def body_pair(t, st):
        return body(2 * t + 1, body(2 * t, st))

    lax.fori_loop(0, seq // (2 * ch), body_pair, jnp.zeros((vw, kw), F32))


def _mixer_c(h, aup_pad, abias, gain, bsz, seq):
    assert seq % C_CHUNK == 0

    def piece(name):
        off, w = _PIECE_OFF[name]
        return pl.BlockSpec((seq, w * LANES), lambda b, o=off // w: (b, o))

    def const(a):
        return pl.BlockSpec(a.shape, lambda b: (0, 0))

    return pl.pallas_call(
        functools.partial(_mixer_c_kernel, seq=seq),
        grid=(bsz,),
        in_specs=[piece("c_q"), piece("c_k"), piece("c_v"), piece("c_g"), piece("c_a"),
                  const(aup_pad), const(abias), const(gain)],
        out_specs=pl.BlockSpec((seq, BRANCH_WIDTH), lambda b: (b, 0)),
        out_shape=jax.ShapeDtypeStruct((bsz * seq, BRANCH_WIDTH), F32),
        compiler_params=_params("parallel"),
        name="mixer_c",
    )(h, h, h, h, h, aup_pad, abias, gain)


def _mixer_d_kernel(r_ref, k_ref, v_ref, low_ref, mu_ref, mulow_ref, w2_ref, a2_ref, g2_ref, vec_ref,
                    o_ref, *, seq, nsub):
    ch = D_CHUNK
    bw = BRANCH_WIDTH
    nh = D_HEADS
    hs = D_HEAD_SIZE
    r_i = _iota((bw, bw), 0)
    c_i = _iota((bw, bw), 1)
    same = (r_i // hs) == (c_i // hs)
    strict = c_i < r_i
    incl = c_i <= r_i
    eye = r_i == c_i
    ones_bd = _group_matrix(bw, hs, 1.0)
    avg_bd = _group_matrix(bw, hs, 1.0 / hs)
    rows = nsub * ch
    tr_i = _iota((rows, rows), 0)
    tc_i = _iota((rows, rows), 1)
    tri = jnp.where(jnp.logical_and(tr_i // ch == tc_i // ch, tc_i <= tr_i), 1.0, 0.0).astype(BF16)
    row0 = _iota((rows, bw), 0) == 0
    row0_low = _iota((rows, LANES), 0) == 0
    vec = vec_ref[...]
    w0, a0, k_k, k_a, r_k, gn_w, gn_b = (vec[n:n + 1, :] for n in range(7))
    mu = mu_ref[...]
    w2, a2, g2 = w2_ref[...], a2_ref[...], g2_ref[...]

    def wide(x):
        return jnp.where(same, jnp.concatenate([x] * nh, 0), 0.0)

    def shifted(ref, start, first_row, m):
        cur = ref[pl.ds(start, rows), :]
        last8 = ref[pl.ds(pl.multiple_of(jnp.maximum(start - 8, 0), 8), 8), :]
        prev_row = last8[7:8, :] * jnp.where(start > 0, 1.0, 0.0)
        prev = jnp.where(first_row, prev_row, pltpu.roll(cur, 1, 0))
        return cur + (prev - cur) * m

    def body(c, st):
        start = pl.multiple_of(c * rows, rows)
        r = shifted(r_ref, start, row0, mu[0:1, :])
        k = shifted(k_ref, start, row0, mu[1:2, :])
        v = shifted(v_ref, start, row0, mu[2:3, :])
        low = shifted(low_ref, start, row0_low, mulow_ref[...])
        w_raw = -_softplus(-(w0 + _dot3(jnp.tanh(low), w2))) - 0.5
        lw = -jnp.exp(w_raw)
        a = _sigmoid(a0 + _dot3(low, a2))
        g = _dot3(_sigmoid(low), g2)
        kk = k * k_k
        kk = kk / jnp.maximum(jnp.sqrt(_dot_rhs_exact(kk * kk, ones_bd)), 1e-12)
        k2 = k * (1.0 + (a - 1.0) * k_a)
        bonus = _dot_rhs_exact(r * k2 * r_k, ones_bd) * v
        b = kk * a

        cum = _dot_lhs_exact(tri, lw)
        e_in = jnp.exp(cum)
        e_inv = jnp.exp(-cum)
        a_dec = -kk * jnp.exp(cum - lw)
        r_dec = r * e_in
        b_inv = b * e_inv
        k_inv = k2 * e_inv
        local = [chunk_terms(*(z[s * ch:(s + 1) * ch] for z in (a_dec, r_dec, b_inv, k_inv, v, e_in)))
                 for s in range(nsub)]
        ys = []
        for qp, y0, gmat, hmat, g_col in local:
            st16 = st.astype(BF16)
            y_w = _dot(qp, st16) + y0
            st = g_col * st + _dot(gmat, st16) + hmat
            y = y_w[0:ch]
            for h in range(1, nh):
                y = y + y_w[h * ch:(h + 1) * ch]
            ys.append(y)
        y = jnp.concatenate(ys, 0)
        mean = _dot_rhs_exact(y, avg_bd)
        yc = y - mean
        var = _dot_rhs_exact(yc * yc, avg_bd)
        yn = yc * lax.rsqrt(var + D_GN_EPS) * gn_w + gn_b
        o_ref[pl.ds(start, rows), :] = (yn + bonus) * g
        return st

    def chunk_terms(a_dec, r_dec, b_inv, k_inv, v, e_in):
        g_last = e_in[ch - 1:ch, :]
        xa = wide(a_dec).astype(BF16)
        xr_f = wide(r_dec)
        xr = xr_f.astype(BF16)
        yb = wide(b_inv).astype(BF16)
        yk = wide(k_inv).astype(BF16)
        bh = wide(b_inv * g_last).astype(BF16)
        kh = wide(k_inv * g_last).astype(BF16)
        vw = wide(v).astype(BF16)

        a_ab = jnp.where(strict, _dotg(xa, yb, NT), 0.0)
        a_ak = jnp.where(strict, _dotg(xa, yk, NT), 0.0).astype(BF16)
        m_rb = jnp.where(incl, _dotg(xr, yb, NT), 0.0).astype(BF16)
        m_rk = jnp.where(incl, _dotg(xr, yk, NT), 0.0).astype(BF16)

        t_inv = jnp.where(eye, 1.0, a_ab)
        apow = a_ab
        for _ in range(int(math.log2(ch)) - 1):
            ab16 = apow.astype(BF16)
            apow = _dot(ab16, ab16)
            t_inv = t_inv + _dot(t_inv.astype(BF16), apow.astype(BF16))
        t16 = t_inv.astype(BF16)

        atp = _dot(t16, xa).astype(BF16)
        vp = _dot(t16, _dot(a_ak, vw).astype(BF16)).astype(BF16)
        qp = (xr_f + _dot(m_rb, atp)).astype(BF16)
        y0 = _dot(m_rb, vp) + _dot(m_rk, vw)
        gmat = _dotg(bh, atp, TN).astype(BF16)
        hmat = _dotg(bh, vp, TN) + _dotg(kh, vw, TN)
        g_col = jnp.sum(jnp.where(eye, g_last, 0.0), -1, keepdims=True)
        return qp, y0, gmat, hmat, g_col

    lax.fori_loop(0, seq // rows, body, jnp.zeros((bw, bw), F32))


def _mixer_d(h, mu3, mulow, w2p, a2p, g2p, vec, bsz, seq, nsub=4):
    assert seq % (nsub * D_CHUNK) == 0 and D_CHUNK == D_HEAD_SIZE

    def piece(name):
        off, w = _PIECE_OFF[name]
        return pl.BlockSpec((seq, w * LANES), lambda b, o=off // w: (b, o))

    def const(a):
        return pl.BlockSpec(a.shape, lambda b: (0, 0))

    return pl.pallas_call(
        functools.partial(_mixer_d_kernel, seq=seq, nsub=nsub),
        grid=(bsz,),
        in_specs=[piece("d_r"), piece("d_k"), piece("d_v"), piece("d_low"),
                  const(mu3), const(mulow), const(w2p), const(a2p), const(g2p), const(vec)],
        out_specs=pl.BlockSpec((seq, BRANCH_WIDTH), lambda b: (b, 0)),
        out_shape=jax.ShapeDtypeStruct((bsz * seq, BRANCH_WIDTH), F32),
        compiler_params=_params("parallel"),
        name="mixer_d",
    )(h, h, h, h, mu3, mulow, w2p, a2p, g2p, vec)


def _merge_kernel(x_ref, ya_ref, yb_ref, yc_ref, yd_ref, wg_ref, wb_ref, wo_ref, g_ref, b_ref, o_ref,
                  *, alpha):
    x = x_ref[...]
    xb = x.astype(BF16)
    merged = None
    for n, y_ref in enumerate((ya_ref, yb_ref, yc_ref, yd_ref)):
        gate = _sigmoid(_dot(xb, wg_ref[:, n * D_MODEL:(n + 1) * D_MODEL]))
        term = gate * _dot(y_ref[...].astype(BF16), wb_ref[n])
        merged = term if merged is None else merged + term
    z = alpha * x + _dot(merged.astype(BF16), wo_ref[...])
    o_ref[...] = _layer_norm(z, g_ref[...], b_ref[...])


def _merge(x2d, ys, wg, wb, wo, g, b, alpha, tm):
    n, d = x2d.shape

    def const(a):
        nd = a.ndim
        return pl.BlockSpec(a.shape, lambda i: (0,) * nd)

    yspec = pl.BlockSpec((tm, BRANCH_WIDTH), lambda i: (i, 0))
    return pl.pallas_call(
        functools.partial(_merge_kernel, alpha=alpha),
        grid=(n // tm,),
        in_specs=[pl.BlockSpec((tm, d), lambda i: (i, 0)), yspec, yspec, yspec, yspec,
                  const(wg), const(wb), const(wo), const(g), const(b)],
        out_specs=pl.BlockSpec((tm, d), lambda i: (i, 0)),
        out_shape=jax.ShapeDtypeStruct((n, d), F32),
        compiler_params=_params("parallel"),
        name="merge",
    )(x2d, *ys, wg, wb, wo, g, b)


def _moe_kernel(x_ref, ltri_ref, wr_ref, br_ref, wg_ref, wu_ref, wd_ref, g_ref, b_ref, o_ref,
                xs_ref, gs_ref, ys_ref, pt_ref, seg_ref, *, alpha):
    e = pl.program_id(1)
    tm = x_ref.shape[0]
    nrow = xs_ref.shape[0]
    ck = MOE_CHUNK
    lane = _iota((tm, LANES), 1)

    @pl.when(e == 0)
    def _():
        x = x_ref[...]
        xb = x.astype(BF16)
        ys_ref[...] = jnp.zeros_like(ys_ref)
        x_lo = (x - xb.astype(F32)).astype(BF16)
        cross = _dot(jnp.concatenate([xb, x_lo], 0), wr_ref[...])
        logits = (cross[0:tm, 0:LANES] + (cross[0:tm, LANES:] + cross[tm:, 0:LANES])) + br_ref[...]
        is_g = jnp.logical_and(lane >= MOE_EXPERTS, lane < MOE_EXPERTS + MOE_GROUPS)
        lg = jnp.where(is_g, logits, -jnp.inf)
        gmax = jnp.max(lg, -1, keepdims=True)
        ptop = 1.0 / jnp.sum(jnp.exp(lg - gmax), -1, keepdims=True)
        gsel = jnp.min(jnp.where(lg == gmax, lane, 2 * LANES), -1, keepdims=True) - MOE_EXPERTS
        in_group = jnp.logical_and(lane < MOE_EXPERTS, lane // MOE_PER_GROUP == gsel)
        le = jnp.where(in_group, logits, -jnp.inf)
        v1 = jnp.max(le, -1, keepdims=True)
        i1 = jnp.min(jnp.where(le == v1, lane, 2 * LANES), -1, keepdims=True)
        le2 = jnp.where(lane == i1, -jnp.inf, le)
        v2 = jnp.max(le2, -1, keepdims=True)
        i2 = jnp.min(jnp.where(le2 == v2, lane, 2 * LANES), -1, keepdims=True)
        e2 = jnp.exp(v2 - v1)
        w1 = ptop / (1.0 + e2)
        gate = jnp.where(lane == i1, w1, jnp.where(lane == i2, w1 * e2, 0.0))

        onehot = jnp.where(lane == gsel, 1.0, 0.0)
        rank = _dot(ltri_ref[...], onehot.astype(BF16))
        cnt = jnp.sum(onehot, 0, keepdims=True).astype(I32)
        nch = jnp.right_shift(cnt + (ck - 1), int(math.log2(ck)))
        lane1 = _iota((1, LANES), 1)
        off = jnp.int32(0)
        off_vec = jnp.zeros((1, LANES), I32)
        for grp in range(MOE_GROUPS):
            n_g = nch[0, grp]
            seg_ref[grp] = off
            seg_ref[MOE_GROUPS + grp] = n_g
            off_vec = jnp.where(lane1 == grp, off, off_vec)
            off = off + n_g * ck
        dest = jnp.sum(onehot * (rank + off_vec.astype(F32)), -1, keepdims=True)
        pt_ref[...] = jnp.where(_iota((tm, nrow), 1) == dest.astype(I32), 1.0, 0.0).astype(BF16)
        dest_row = jnp.broadcast_to(dest, (tm, LANES)).T[0:1, :].astype(I32)
        perm = jnp.where(_iota((nrow, tm), 0) == dest_row, 1.0, 0.0).astype(BF16)
        gate_hi, gate_lo = _split2(gate)
        moved = _dot(perm, jnp.concatenate([xb, gate_hi, gate_lo], 1))
        d = x_ref.shape[1]
        xs_ref[...] = moved[:, 0:d].astype(BF16)
        gs_ref[...] = moved[:, d:d + LANES] + moved[:, d + LANES:]

    grp = e // (MOE_PER_GROUP // 2)
    seg_off = seg_ref[grp]
    seg_chunks = seg_ref[MOE_GROUPS + grp]

    def run_experts(start, size):
        rows = pl.ds(pl.multiple_of(start, ck), size)
        xc = xs_ref[rows, :]
        gs = gs_ref[rows, :]
        lane_s = _iota((size, LANES), 1)
        g0 = jnp.sum(jnp.where(lane_s == 2 * e, gs, 0.0), -1, keepdims=True)
        g1 = jnp.sum(jnp.where(lane_s == 2 * e + 1, gs, 0.0), -1, keepdims=True)
        hg = _dot(xc, wg_ref[0])
        hu = _dot(xc, wu_ref[0])
        gcol = jnp.where(_iota((size, 2 * MOE_HIDDEN), 1) < MOE_HIDDEN, g0, g1)
        hidden = hg * _sigmoid(hg) * hu * gcol
        ys_ref[rows, :] += _dot(hidden.astype(BF16), wd_ref[0])

    big = 4

    def quad(c, carry):
        run_experts(seg_off + c * (big * ck), big * ck)
        return carry

    lax.fori_loop(0, seg_chunks // big, quad, 0)
    rem = seg_chunks % big
    rem_start = seg_off + (seg_chunks - rem) * ck
    for r in range(1, big):
        @pl.when(rem == r)
        def _(r=r):
            run_experts(rem_start, r * ck)

    @pl.when(e == pl.num_programs(1) - 1)
    def _():
        z = alpha * x_ref[...] + _dot(pt_ref[...], ys_ref[...].astype(BF16))
        o_ref[...] = _layer_norm(z, g_ref[...], b_ref[...])


def _moe(x2d, wr, br, wg, wu, wd, g, b, alpha, tm):
    n, d = x2d.shape
    ne = wg.shape[0]
    hid = wg.shape[2]
    nrow = tm + MOE_GROUPS * MOE_CHUNK
    ltri = jnp.tril(jnp.ones((tm, tm), BF16), -1)
    wg = wg.reshape(ne // 2, 2, d, hid).transpose(0, 2, 1, 3).reshape(ne // 2, d, 2 * hid)
    wu = wu.reshape(ne // 2, 2, d, hid).transpose(0, 2, 1, 3).reshape(ne // 2, d, 2 * hid)
    wd = wd.reshape(ne // 2, 2 * hid, d)

    def const(a):
        return pl.BlockSpec(a.shape, lambda i, e: (0, 0))

    return pl.pallas_call(
        functools.partial(_moe_kernel, alpha=alpha),
        grid=(n // tm, ne // 2),
        in_specs=[pl.BlockSpec((tm, d), lambda i, e: (i, 0)), const(ltri), const(wr), const(br),
                  pl.BlockSpec((1, d, 2 * hid), lambda i, e: (e, 0, 0)),
                  pl.BlockSpec((1, d, 2 * hid), lambda i, e: (e, 0, 0)),
                  pl.BlockSpec((1, 2 * hid, d), lambda i, e: (e, 0, 0)),
                  const(g), const(b)],
        out_specs=pl.BlockSpec((tm, d), lambda i, e: (i, 0)),
        out_shape=jax.ShapeDtypeStruct((n, d), F32),
        scratch_shapes=[pltpu.VMEM((nrow, d), BF16), pltpu.VMEM((nrow, LANES), F32),
                        pltpu.VMEM((nrow, d), F32), pltpu.VMEM((tm, nrow), BF16),
                        pltpu.SMEM((2 * MOE_GROUPS,), I32)],
        compiler_params=_params("parallel", "arbitrary"),
        name="moe",
    )(x2d, ltri, wr, br, wg, wu, wd, g, b)


def _t5_bucket(dist):
    exact = RPB_BUCKETS // 2
    d = jnp.maximum(dist, 0)
    df = jnp.maximum(d, 1).astype(F32)
    large = exact + (jnp.log(df / exact) / math.log(RPB_MAX_DIST / exact)
                     * (RPB_BUCKETS - exact)).astype(I32)
    large = jnp.minimum(large, RPB_BUCKETS - 1)
    return jnp.where(d < exact, d, large)


def _rpb_lookup(tab, dist):
    bucket = _t5_bucket(dist)[..., None]
    out = jnp.zeros(dist.shape + (tab.shape[1],), F32)
    for k in range(RPB_BUCKETS):
        out = jnp.where(bucket == k, tab[k], out)
    return out


def _bias_a(tab):
    qi = jnp.arange(A_BLOCK)[:, None]
    kj = jnp.arange(2 * A_BLOCK)[None, :]
    rdist = qi + A_BLOCK - kj
    out = []
    for window, dilation in A_PATTERNS:
        in_band = (rdist >= 0) & (rdist <= window // dilation)
        bias = jnp.transpose(_rpb_lookup(tab, rdist * dilation), (2, 0, 1))
        out.append(jnp.where(in_band[None], bias, NEG))
    return jnp.stack(out).astype(F32)


def _bias_b(tab, nblk):
    qi = jnp.arange(Q_BLOCK)[:, None]
    kj = jnp.arange(Q_BLOCK)[None, :]
    delta = jnp.arange(nblk)[:, None, None] * Q_BLOCK
    bias = _rpb_lookup(tab, delta + qi - kj)
    return jnp.transpose(bias, (0, 3, 2, 1)).astype(F32)


def _in_offsets():
    splits = (("a_q", 256), ("a_k", 256), ("a_v", 256), ("b_q", 256), ("b_ckv", 64), ("b_iq", 256),
              ("b_ik", 32), ("b_iw", 8), ("c_q", 128), ("c_k", 128), ("c_v", 256), ("c_a", 16),
              ("c_g", 256), ("d", 832), ("gate", 4096))
    off, out = 0, {}
    for name, w in splits:
        out[name] = (off, w)
        off += w
    return out


def _proj_columns():
    src = _in_offsets()
    d0 = src["d"][0]
    named = {
        "a_q": np.arange(256) + src["a_q"][0], "a_k": np.arange(256) + src["a_k"][0],
        "a_v": np.arange(256) + src["a_v"][0], "b_q": np.arange(256) + src["b_q"][0],
        "b_iq": np.arange(256) + src["b_iq"][0],
        "b_ckv4": np.tile(np.arange(64) + src["b_ckv"][0], B_HEADS),
        "b_ik8": np.tile(np.arange(32) + src["b_ik"][0], IDX_HEADS),
        "c_v": np.arange(256) + src["c_v"][0], "c_g": np.arange(256) + src["c_g"][0],
        "d_r": np.arange(256) + d0, "d_k": np.arange(256) + d0 + 256, "d_v": np.arange(256) + d0 + 512,
        "c_q": np.arange(128) + src["c_q"][0], "c_k": np.arange(128) + src["c_k"][0],
        "b_iw": np.arange(8) + src["b_iw"][0], "c_a": np.arange(16) + src["c_a"][0],
        "d_low": np.arange(64) + d0 + 768, "pad": np.arange(0),
    }
    cols = []
    for name, w in _PIECES:
        c = named[name]
        cols.append(np.concatenate([c, -np.ones(w * LANES - len(c), np.int64)]))
    return np.concatenate(cols)


def _proj_weight(w):
    cols = _proj_columns()
    cuts = [0] + [k for k in range(1, len(cols))
                  if (cols[k] < 0) != (cols[k - 1] < 0) or (cols[k] >= 0 and cols[k] != cols[k - 1] + 1)]
    cuts.append(len(cols))
    parts = []
    for a, b in zip(cuts[:-1], cuts[1:]):
        if cols[a] < 0:
            parts.append(jnp.zeros((w.shape[0], b - a), BF16))
        else:
            parts.append(w[:, int(cols[a]):int(cols[a]) + (b - a)].astype(BF16))
    return jnp.concatenate(parts, 1)


def _pad_rows(w, first, total):
    return jnp.zeros((total, w.shape[1]), w.dtype).at[first:first + w.shape[0]].set(w)


def kernel(x, rpb_table, w_in, b_kv_gain, b_w_uv, c_a_up, c_a_bias, c_norm_gain, d_mu, d_w0, d_w2,
           d_a0, d_a2, d_g2, d_k_k, d_k_a, d_r_k, d_gn_w, d_gn_b, w_branch, w_out, ln_g, ln_b,
           router_g, router_g_bias, router_e, router_e_bias, moe_w_gate, moe_w_up, moe_w_down):
    bsz, seq, d_model = x.shape
    depth = w_in.shape[0]
    n = bsz * seq
    alpha = (2 * depth) ** 0.25
    gate_off = _in_offsets()["gate"][0]
    bias_a = _bias_a(rpb_table[:, :A_HEADS])
    bias_b = _bias_b(rpb_table[:, A_HEADS:], seq // Q_BLOCK)
    tm_proj = math.gcd(n, 1024)
    tm_merge = math.gcd(n, 512)
    tm_moe = math.gcd(n, 1024)

    x2d = x.reshape(n, d_model)
    for l in range(depth):
        h = _project(x2d, _proj_weight(w_in[l]), tm_proj, PROJ_WIDTH // 2)

        ya = _mixer_a(h, bias_a, bsz, seq)

        gain4 = jnp.tile(b_kv_gain[l], B_HEADS)[None, :]
        wuv_t = jnp.transpose(b_w_uv[l], (0, 2, 1)).astype(BF16)
        yb = _mixer_b(h, gain4, wuv_t, bias_b, bsz, seq)

        aup_pad = _pad_rows(c_a_up[l], 0, LANES)
        yc = _mixer_c(h, aup_pad, c_a_bias[l][None, :], c_norm_gain[l][None, :], bsz, seq)

        mu = d_mu[l]
        mu3 = mu[:3 * BRANCH_WIDTH].reshape(3, BRANCH_WIDTH)
        mulow = jnp.zeros((1, LANES), F32).at[0, :D_DECAY_RANK + D_ICLR_RANK + D_GATE_RANK].set(
            mu[3 * BRANCH_WIDTH:])
        w2p = _pad_rows(d_w2[l], 0, LANES)
        a2p = _pad_rows(d_a2[l], D_DECAY_RANK, LANES)
        g2p = _pad_rows(d_g2[l], D_DECAY_RANK + D_ICLR_RANK, LANES)
        vec = jnp.stack([d_w0[l], d_a0[l], d_k_k[l], d_k_a[l], d_r_k[l], d_gn_w[l], d_gn_b[l],
                         jnp.zeros_like(d_w0[l])])
        yd = _mixer_d(h, mu3, mulow, w2p, a2p, g2p, vec, bsz, seq)

        wg = w_in[l][:, gate_off:gate_off + N_BRANCHES * d_model].astype(BF16)
        x2d = _merge(x2d, (ya, yb, yc, yd), wg, w_branch[l].astype(BF16), w_out[l].astype(BF16),
                     ln_g[l, 0][None, :], ln_b[l, 0][None, :], alpha, tm_merge)

        wr = jnp.zeros((d_model, LANES), F32)
        wr = wr.at[:, :MOE_EXPERTS].set(router_e[l]).at[:, MOE_EXPERTS:MOE_EXPERTS + MOE_GROUPS].set(
            router_g[l])
        br = jnp.zeros((1, LANES), F32)
        br = br.at[0, :MOE_EXPERTS].set(router_e_bias[l]).at[0, MOE_EXPERTS:MOE_EXPERTS + MOE_GROUPS].set(
            router_g_bias[l])
        wr = jnp.concatenate(_split2(wr), 1)
        x2d = _moe(x2d, wr, br, moe_w_gate[l].astype(BF16), moe_w_up[l].astype(BF16),
                   moe_w_down[l].astype(BF16), ln_g[l, 1][None, :], ln_b[l, 1][None, :], alpha, tm_moe)
    return x2d.reshape(bsz, seq, d_model)
```

```python
import functools
import math

import jax
import jax.numpy as jnp
import numpy as np
from jax import lax
from jax.experimental import pallas as pl
from jax.experimental.pallas import tpu as pltpu

F32 = jnp.float32
BF16 = jnp.bfloat16
I32 = jnp.int32

D_MODEL = 1024
N_BRANCHES = 4
BRANCH_WIDTH = 256
HEAD_DIM = 64
A_HEADS = 4
A_PATTERNS = ((128, 1), (512, 4), (2048, 16))
A_BLOCK = 128
B_HEADS = 4
B_LATENT = 64
IDX_HEADS = 8
IDX_DIM = 32
TOPK_MAX = 256
TOPK_DIV = 4
Q_BLOCK = 128
C_HEADS = 4
C_KEY_DIM = 32
C_VAL_DIM = 64
C_GATE_RANK = 16
C_GATE_TAU = 16.0
C_CHUNK = 64
D_HEADS = 4
D_HEAD_SIZE = 64
D_DECAY_RANK = 16
D_ICLR_RANK = 16
D_GATE_RANK = 32
D_GN_EPS = 64e-5
D_CHUNK = 64
RPB_BUCKETS = 32
RPB_MAX_DIST = 2048
MOE_GROUPS = 4
MOE_PER_GROUP = 8
MOE_EXPERTS = 32
MOE_HIDDEN = 256
MOE_CHUNK = 128
LN_EPS = 1e-5

LANES = 128
NEG = -1e30
VMEM_LIMIT = 56 * 1024 * 1024

NT = (((1,), (1,)), ((), ()))
TN = (((0,), (0,)), ((), ()))

_PIECES = (
    ("a_q", 2), ("a_k", 2), ("a_v", 2), ("b_q", 2), ("b_iq", 2), ("b_ckv4", 2), ("b_ik8", 2),
    ("c_v", 2), ("c_g", 2), ("d_r", 2), ("d_k", 2), ("d_v", 2),
    ("c_q", 1), ("c_k", 1), ("b_iw", 1), ("c_a", 1), ("d_low", 1), ("pad", 1),
)
_PIECE_OFF = {}
_off = 0
for _n, _w in _PIECES:
    _PIECE_OFF[_n] = (_off, _w)
    _off += _w
PROJ_WIDTH = _off * LANES


def _dot(a, b):
    return jnp.dot(a, b, preferred_element_type=F32)


def _dotg(a, b, dims):
    return lax.dot_general(a, b, dims, preferred_element_type=F32)


def _split2(x):
    hi = x.astype(BF16)
    lo = (x - hi.astype(F32)).astype(BF16)
    return hi, lo


def _dot_rhs_exact(x, m):
    hi, lo = _split2(x)
    return _dot(hi, m) + _dot(lo, m)


def _dot_lhs_exact(m, x):
    hi, lo = _split2(x)
    return _dot(m, hi) + _dot(m, lo)


def _dot3(x, w):
    xh, xl = _split2(x)
    wh, wl = _split2(w)
    return _dot(xh, wh) + (_dot(xl, wh) + _dot(xh, wl))


def _sigmoid(x):
    return 1.0 / (1.0 + jnp.exp(-x))


def _softplus(x):
    return jnp.maximum(x, 0.0) + jnp.log(1.0 + jnp.exp(-jnp.abs(x)))


def _iota(shape, axis):
    return lax.broadcasted_iota(I32, shape, axis)


def _group_matrix(n, group, value, dtype=BF16):
    r = _iota((n, n), 0) // group
    c = _iota((n, n), 1) // group
    return jnp.where(r == c, value, 0.0).astype(dtype)


def _layer_norm(z, g, b):
    mu = jnp.mean(z, -1, keepdims=True)
    zc = z - mu
    var = jnp.mean(zc * zc, -1, keepdims=True)
    return zc * lax.rsqrt(var + LN_EPS) * g + b


def _params(*sem):
    return pltpu.CompilerParams(dimension_semantics=sem, vmem_limit_bytes=VMEM_LIMIT)


def _proj_kernel(x_ref, w_ref, o_ref):
    o_ref[...] = _dot(x_ref[...].astype(BF16), w_ref[...])


def _project(x2d, w, tm, tn):
    n, k = x2d.shape
    wt = w.shape[1]
    return pl.pallas_call(
        _proj_kernel,
        grid=(n // tm, wt // tn),
        in_specs=[pl.BlockSpec((tm, k), lambda i, j: (i, 0)),
                  pl.BlockSpec((k, tn), lambda i, j: (0, j))],
        out_specs=pl.BlockSpec((tm, tn), lambda i, j: (i, j)),
        out_shape=jax.ShapeDtypeStruct((n, wt), F32),
        compiler_params=_params("parallel", "parallel"),
        name="proj",
    )(x2d, w)


def _mixer_a_kernel(q0_ref, q1_ref, k0_ref, k1_ref, v0_ref, v1_ref, bias_ref, o_ref, op_ref, lse_ref,
                    *, seq, dils):
    q_refs, k_refs, v_refs = (q0_ref, q1_ref), (k0_ref, k1_ref), (v0_ref, v1_ref)
    ab = A_BLOCK
    lane = _iota((ab, LANES), 1)
    upper = lane >= HEAD_DIM
    col = _iota((ab, 2 * ab), 1)
    scale = HEAD_DIM ** -0.5

    for p, d in enumerate(dils):
        per_res = seq // (d * ab)

        def body(idx, carry, p=p, d=d, per_res=per_res):
            r = idx // per_res
            b = idx % per_res
            start = b * (ab * d) + r
            pstart = jnp.maximum(start - ab * d, r)
            if d == 1:
                rows, prow = pl.ds(pl.multiple_of(start, ab), ab), pl.ds(pl.multiple_of(pstart, ab), ab)
            else:
                rows, prow = pl.ds(start, ab, stride=d), pl.ds(pstart, ab, stride=d)
            first_pen = jnp.where(b == 0, NEG, 0.0)
            pen = jnp.where(col < ab, first_pen, 0.0)
            for hp in range(2):
                q_ref, k_ref, v_ref = q_refs[hp], k_refs[hp], v_refs[hp]
                q = q_ref[rows, :]
                kb = jnp.concatenate([k_ref[prow, :], k_ref[rows, :]], 0).astype(BF16)
                vb = jnp.concatenate([v_ref[prow, :], v_ref[rows, :]], 0).astype(BF16)
                outs, lses = [], []
                for hh in range(2):
                    hm = upper if hh else jnp.logical_not(upper)
                    qm = jnp.where(hm, q, 0.0).astype(BF16)
                    s = _dotg(qm, kb, NT) * scale + bias_ref[p, 2 * hp + hh] + pen
                    mx = jnp.max(s, -1, keepdims=True)
                    e = jnp.exp(s - mx)
                    den = jnp.sum(e, -1, keepdims=True)
                    outs.append(_dot(e.astype(BF16), vb) / den)
                    lses.append(mx + jnp.log(den))
                op_ref[2 * p + hp, rows, :] = jnp.where(upper, outs[1], outs[0])
                lse_ref[2 * p + hp, rows, :] = jnp.where(upper, lses[1], lses[0])
            return carry

        def body_pair(t, carry, body=body):
            return body(2 * t + 1, body(2 * t, carry))

        lax.fori_loop(0, seq // (2 * ab), body_pair, 0)

    cr = 256

    def combine(c, carry):
        rows = pl.ds(pl.multiple_of(c * cr, cr), cr)
        for hp in range(2):
            ls = [lse_ref[2 * p + hp, rows, :] for p in range(len(dils))]
            mx = functools.reduce(jnp.maximum, ls)
            es = [jnp.exp(l - mx) for l in ls]
            num = functools.reduce(lambda a, b: a + b,
                                   [e * op_ref[2 * p + hp, rows, :] for p, e in enumerate(es)])
            o_ref[rows, hp * LANES:(hp + 1) * LANES] = num / functools.reduce(lambda a, b: a + b, es)
        return carry

    lax.fori_loop(0, seq // cr, combine, 0)


def _mixer_a(h, bias, bsz, seq):
    dils = tuple(d for _, d in A_PATTERNS)
    for w, d in A_PATTERNS:
        assert w // d == A_BLOCK and seq % (d * A_BLOCK) == 0
    npat = len(dils)

    def pieces(name):
        off, w = _PIECE_OFF[name]
        return [pl.BlockSpec((seq, LANES), lambda b, o=off + u: (b, o)) for u in range(w)]

    return pl.pallas_call(
        functools.partial(_mixer_a_kernel, seq=seq, dils=dils),
        grid=(bsz,),
        in_specs=pieces("a_q") + pieces("a_k") + pieces("a_v")
        + [pl.BlockSpec(bias.shape, lambda b: (0, 0, 0, 0))],
        out_specs=pl.BlockSpec((seq, BRANCH_WIDTH), lambda b: (b, 0)),
        out_shape=jax.ShapeDtypeStruct((bsz * seq, BRANCH_WIDTH), F32),
        scratch_shapes=[pltpu.VMEM((2 * npat, seq, LANES), F32),
                        pltpu.VMEM((2 * npat, seq, LANES), F32)],
        compiler_params=_params("parallel"),
        name="mixer_a",
    )(h, h, h, h, h, h, bias)


def _mixer_b_kernel(q_ref, iq_ref, iw_ref, ckv_ref, ik_ref, gain_ref, wuvt_ref, bias_ref, o_ref,
                    ckvn_ref, ckvt_ref, ikb_ref, sc_ref, sc16_ref, lg_ref, iqs_ref, qs_ref, *, seq, keep):
    qb = Q_BLOCK
    sb = 2 * qb
    i = pl.program_id(1)

    @pl.when(i == 0)
    def _():
        gmat = _group_matrix(BRANCH_WIDTH, B_LATENT, 1.0 / B_LATENT)

        def prep(c, carry):
            rows = pl.ds(pl.multiple_of(c * sb, sb), sb)
            x = ckv_ref[rows, :]
            ms = _dot_rhs_exact(x * x, gmat)
            xn = x * lax.rsqrt(ms + 1e-6) * gain_ref[...]
            ckvn_ref[rows, :] = xn.astype(BF16)
            ckvt_ref[c] = xn.T[0:B_LATENT, :].astype(BF16)
            ikb_ref[rows, :] = ik_ref[rows, :].astype(BF16)
            return carry

        lax.fori_loop(0, seq // sb, prep, 0)

    lane = _iota((qb, BRANCH_WIDTH), 1)
    iq = iq_ref[...] * (IDX_DIM ** -0.5)
    for h in range(IDX_HEADS):
        iqs_ref[h * qb:(h + 1) * qb, :] = jnp.where(lane // IDX_DIM == h, iq, 0.0).astype(BF16)
    q = q_ref[...] * (B_LATENT ** -0.5)
    for h in range(B_HEADS):
        qs_ref[h * qb:(h + 1) * qb, :] = jnp.where(lane // B_LATENT == h, q, 0.0).astype(BF16)
    iw_t = (iw_ref[...] * (IDX_HEADS ** -0.5)).T
    krow = _iota((sb, qb), 0)
    qcol = _iota((sb, qb), 1)
    nsb = (i + 2) // 2

    def fold(x, op):
        parts = [x[r:r + 8] for r in range(0, sb, 8)]
        while len(parts) > 1:
            parts = [op(parts[k], parts[k + 1]) for k in range(0, len(parts), 2)]
        return parts[0]

    def for_steps(body, init):
        def pair(p, carry):
            return body(2 * p + 1, body(2 * p, carry))

        carry = lax.fori_loop(0, nsb // 2, pair, init)
        return lax.cond(nsb % 2 == 1, lambda c: body(nsb - 1, c), lambda c: c, carry)

    def score_body(j, carry):
        kr = pl.ds(pl.multiple_of(j * sb, sb), sb)
        rel = _dotg(ikb_ref[kr, :], iqs_ref[...], NT)
        sc = jnp.maximum(rel[:, 0:qb], 0.0) * iw_t[0:1, :]
        for h in range(1, IDX_HEADS):
            sc = sc + jnp.maximum(rel[:, h * qb:(h + 1) * qb], 0.0) * iw_t[h:h + 1, :]
        sc = jnp.where(krow <= qcol + (i * qb - j * sb), sc, -jnp.inf)
        sc_ref[j] = sc
        sc16_ref[j] = pltpu.bitcast(pltpu.bitcast(sc, I32) & jnp.int32(-65536), F32).astype(BF16)
        logits = _dotg(ckvn_ref[kr, :], qs_ref[...], NT)
        d0 = i - 2 * j
        for h in range(B_HEADS):
            lg = logits[:, h * qb:(h + 1) * qb]
            lg_ref[h, j, 0:qb, :] = lg[0:qb] + bias_ref[d0, h]
            lg_ref[h, j, qb:sb, :] = lg[qb:sb] + bias_ref[jnp.maximum(d0 - 1, 0), h]
        return carry

    for_steps(score_body, 0)

    sign = jnp.int32(-2 ** 31)

    def key_to_float(u):
        k = u ^ sign
        bits = k ^ ((k >> 31) & jnp.int32(0x7FFFFFFF))
        return pltpu.bitcast(bits, F32)

    def count(pred):
        def cnt(j, acc):
            return acc + fold(pred(j), jnp.add)
        acc = for_steps(cnt, jnp.zeros((8, qb), F32))
        return jnp.sum(acc, 0, keepdims=True)

    pk = 16

    def count_ge16(cf):
        cf = pltpu.bitcast(pltpu.bitcast(cf, I32) & jnp.int32(-65536), F32)
        c16 = jnp.broadcast_to(cf, (pk, qb)).astype(BF16)
        one, zero = jnp.ones((pk, qb), BF16), jnp.zeros((pk, qb), BF16)

        def cnt(j, acc):
            x = sc16_ref[j]
            parts = [jnp.where(x[r:r + pk] >= c16, one, zero) for r in range(0, sb, pk)]
            while len(parts) > 1:
                parts = [parts[k] + parts[k + 1] for k in range(0, len(parts), 2)]
            return acc + parts[0].astype(F32)

        acc = for_steps(cnt, jnp.zeros((pk, qb), F32))
        return jnp.sum(acc, 0, keepdims=True)

    def bit_body(t, state, packed=False):
        u, n_u = state
        cand = u | jnp.left_shift(jnp.int32(1), 31 - t)
        cf = key_to_float(cand)
        c = count_ge16(cf) if packed else count(lambda j: jnp.where(sc_ref[j] >= cf, 1.0, 0.0))
        take = c >= keep
        return jnp.where(take, cand, u), jnp.where(take, c, n_u)

    def bit_body16(t, state):
        return bit_body(t, state, packed=True)

    first_bits, bit_group = 24, 4
    state = (jnp.zeros((1, qb), I32), (jnp.zeros((1, qb), I32) + nsb * sb).astype(F32))
    state = bit_body16(0, state)
    n_pos = count(lambda j: jnp.where(sc_ref[j] > 0.0, 1.0, 0.0))
    zero_thr = jnp.logical_and(state[0] != 0, n_pos < keep)
    state = lax.fori_loop(1, 16, bit_body16, state)
    u, n_ge = lax.fori_loop(16, first_bits, bit_body, state)

    def more_bits(s):
        t, _, n_u = s
        open_rows = jnp.where(zero_thr, 0.0, jnp.abs(n_u - keep))
        return jnp.logical_and(t < 32, jnp.max(open_rows) > 0.0)

    def bit_group_body(s):
        t, u, n_u = s
        for k in range(bit_group):
            u, n_u = bit_body(t + k, (u, n_u))
        return t + bit_group, u, n_u

    _, u, n_ge = lax.while_loop(more_bits, bit_group_body, (jnp.int32(first_bits), u, n_ge))
    u = jnp.maximum(u ^ sign, jnp.int32(0x007FFFFF - 2 ** 31)) ^ sign
    thr = key_to_float(u)
    n_gt = count(lambda j: jnp.where(sc_ref[j] > thr, 1.0, 0.0))
    need = keep - n_gt

    nbits = int(math.ceil(math.log2(seq))) + 1

    def cut_search():
        def cut_body(t, cut):
            cand = cut | jnp.left_shift(jnp.int32(1), nbits - 1 - t)
            c = count(lambda j: jnp.where(sc_ref[j] == thr,
                                          jnp.where(krow < cand - j * sb, 1.0, 0.0), 0.0))
            return jnp.where(c <= need, cand, cut)

        return lax.fori_loop(0, nbits, cut_body, jnp.zeros((1, qb), I32))

    surplus = jnp.max(n_ge) > keep
    cut = lax.cond(surplus, cut_search, lambda: jnp.full((1, qb), 2 ** nbits - 1, I32))
    cut = jnp.minimum(cut, i * qb + 1 + _iota((1, qb), 1))

    def mask_body(j, mx):
        sc = sc_ref[j]
        sel = jnp.where(sc > thr, 1.0, jnp.where(sc == thr, jnp.where(krow < cut - j * sb, 1.0, 0.0), 0.0))
        out = []
        for h in range(B_HEADS):
            s = jnp.where(sel > 0.5, lg_ref[h, j], NEG)
            lg_ref[h, j] = s
            out.append(jnp.maximum(mx[h], fold(s, jnp.maximum)))
        return tuple(out)

    mx = for_steps(mask_body, tuple(jnp.full((8, qb), NEG, F32) for _ in range(B_HEADS)))
    ms = [jnp.max(m, 0, keepdims=True) for m in mx]

    def att_body(j, carry):
        ls, acc = carry
        ps, new_ls = [], []
        for h in range(B_HEADS):
            pr = jnp.exp(lg_ref[h, j] - ms[h])
            new_ls.append(ls[h] + fold(pr, jnp.add))
            ps.append(pr.astype(BF16))
        upd = _dot(ckvt_ref[j], jnp.concatenate(ps, 1))
        return tuple(new_ls), acc + upd

    init = (tuple(jnp.zeros((8, qb), F32) for _ in range(B_HEADS)),
            jnp.zeros((B_LATENT, B_HEADS * qb), F32))
    ls, acc = for_steps(att_body, init)
    ls = [jnp.sum(l, 0, keepdims=True) for l in ls]
    o_t = (acc / jnp.concatenate(ls, 1)).astype(BF16)
    y_t = jnp.concatenate([_dot(wuvt_ref[h], o_t[:, h * qb:(h + 1) * qb]) for h in range(B_HEADS)], 0)
    o_ref[...] = y_t.T


def _mixer_b(h, gain4, wuv_t, bias, bsz, seq):
    qb = Q_BLOCK
    assert seq % 256 == 0
    nblk = seq // qb
    keep = min(TOPK_MAX, seq // TOPK_DIV)

    def qpiece(name):
        off, w = _PIECE_OFF[name]
        return pl.BlockSpec((qb, w * LANES), lambda b, i, o=off // w: (b * nblk + i, o))

    def kpiece(name):
        off, w = _PIECE_OFF[name]
        return pl.BlockSpec((seq, w * LANES), lambda b, i, o=off // w: (b, o))

    return pl.pallas_call(
        functools.partial(_mixer_b_kernel, seq=seq, keep=keep),
        grid=(bsz, nblk),
        in_specs=[qpiece("b_q"), qpiece("b_iq"), qpiece("b_iw"), kpiece("b_ckv4"), kpiece("b_ik8"),
                  pl.BlockSpec(gain4.shape, lambda b, i: (0, 0)),
                  pl.BlockSpec(wuv_t.shape, lambda b, i: (0, 0, 0)),
                  pl.BlockSpec(bias.shape, lambda b, i: (0, 0, 0, 0))],
        out_specs=pl.BlockSpec((qb, BRANCH_WIDTH), lambda b, i: (b * nblk + i, 0)),
        out_shape=jax.ShapeDtypeStruct((bsz * seq, BRANCH_WIDTH), F32),
        scratch_shapes=[pltpu.VMEM((seq, BRANCH_WIDTH), BF16),
                        pltpu.VMEM((nblk // 2, B_LATENT, 2 * qb), BF16),
                        pltpu.VMEM((seq, BRANCH_WIDTH), BF16),
                        pltpu.VMEM((nblk // 2, 2 * qb, qb), F32),
                        pltpu.VMEM((nblk // 2, 2 * qb, qb), BF16),
                        pltpu.VMEM((B_HEADS, nblk // 2, 2 * qb, qb), F32),
                        pltpu.VMEM((IDX_HEADS * qb, BRANCH_WIDTH), BF16),
                        pltpu.VMEM((B_HEADS * qb, BRANCH_WIDTH), BF16)],
        compiler_params=_params("parallel", "arbitrary"),
        name="mixer_b",
    )(h, h, h, h, h, gain4, wuv_t, bias)


def _mixer_c_kernel(q_ref, k_ref, v_ref, g_ref, a_ref, aup_ref, abias_ref, gain_ref, o_ref, *, seq):
    ch = C_CHUNK
    kw = C_HEADS * C_KEY_DIM
    vw = C_HEADS * C_VAL_DIM
    tri = jnp.where(_iota((ch, ch), 1) <= _iota((ch, ch), 0), 1.0, 0.0).astype(BF16)
    causal = _iota((C_HEADS * ch, ch), 1) <= (_iota((C_HEADS * ch, ch), 0) % ch)
    klane_head = _iota((C_HEADS * ch, kw), 1) // C_KEY_DIM
    krow_head = _iota((C_HEADS * ch, kw), 0) // ch
    vlane_head = _iota((ch, vw), 1) // C_VAL_DIM
    st_mask = (_iota((vw, kw), 0) // C_VAL_DIM) == (_iota((vw, kw), 1) // C_KEY_DIM)
    gmat = _group_matrix(vw, C_VAL_DIM, 1.0 / C_VAL_DIM)
    aup = aup_ref[...]

    def body(c, st):
        rows = pl.ds(pl.multiple_of(c * ch, ch), ch)
        qc = q_ref[rows, :] * (C_KEY_DIM ** -0.5)
        kc = k_ref[rows, :]
        vc = v_ref[rows, :]
        z = _dot3(a_ref[rows, :], aup) + abias_ref[...]
        log_a = -_softplus(-z) / C_GATE_TAU
        cum = _dot_lhs_exact(tri, log_a)
        last = cum[ch - 1:ch, :]
        q_dec = qc * jnp.exp(cum)
        k_inv = (kc * jnp.exp(-cum)).astype(BF16)
        k_dec = (kc * jnp.exp(last - cum)).astype(BF16)
        vb = vc.astype(BF16)
        qd_b = q_dec.astype(BF16)
        q_stack = jnp.where(klane_head == krow_head, jnp.concatenate([q_dec] * C_HEADS, 0), 0.0)
        att = jnp.where(causal, _dotg(q_stack.astype(BF16), k_inv, NT), 0.0)
        full = _dot(att.astype(BF16), vb)
        o = _dotg(qd_b, st.astype(BF16), NT)
        for h in range(C_HEADS):
            o = o + jnp.where(vlane_head == h, full[h * ch:(h + 1) * ch], 0.0)
        upd = _dotg(vb, k_dec, TN)
        st = st * jnp.exp(last) + jnp.where(st_mask, upd, 0.0)
        ms = _dot_rhs_exact(o * o, gmat)
        o = o * lax.rsqrt(ms + 1e-6) * gain_ref[...]
        g = g_ref[rows, :]
        o_ref[rows, :] = g * _sigmoid(g) * o
        return st

    def body_pair(t, st):
        return body(2 * t + 1, body(2 * t, st))

    lax.fori_loop(0, seq // (2 * ch), body_pair, jnp.zeros((vw, kw), F32))


def _mixer_c(h, aup_pad, abias, gain, bsz, seq):
    assert seq % C_CHUNK == 0

    def piece(name):
        off, w = _PIECE_OFF[name]
        return pl.BlockSpec((seq, w * LANES), lambda b, o=off // w: (b, o))

    def const(a):
        return pl.BlockSpec(a.shape, lambda b: (0, 0))

    return pl.pallas_call(
        functools.partial(_mixer_c_kernel, seq=seq),
        grid=(bsz,),
        in_specs=[piece("c_q"), piece("c_k"), piece("c_v"), piece("c_g"), piece("c_a"),
                  const(aup_pad), const(abias), const(gain)],
        out_specs=pl.BlockSpec((seq, BRANCH_WIDTH), lambda b: (b, 0)),
        out_shape=jax.ShapeDtypeStruct((bsz * seq, BRANCH_WIDTH), F32),
        compiler_params=_params("parallel"),
        name="mixer_c",
    )(h, h, h, h, h, aup_pad, abias, gain)


def _mixer_d_kernel(r_ref, k_ref, v_ref, low_ref, mu_ref, mulow_ref, w2_ref, a2_ref, g2_ref, vec_ref,
                    o_ref, *, seq, nsub):
    ch = D_CHUNK
    bw = BRANCH_WIDTH
    nh = D_HEADS
    hs = D_HEAD_SIZE
    r_i = _iota((bw, bw), 0)
    c_i = _iota((bw, bw), 1)
    same = (r_i // hs) == (c_i // hs)
    strict = c_i < r_i
    incl = c_i <= r_i
    eye = r_i == c_i
    ones_bd = _group_matrix(bw, hs, 1.0)
    avg_bd = _group_matrix(bw, hs, 1.0 / hs)
    rows = nsub * ch
    tr_i = _iota((rows, rows), 0)
    tc_i = _iota((rows, rows), 1)
    tri = jnp.where(jnp.logical_and(tr_i // ch == tc_i // ch, tc_i <= tr_i), 1.0, 0.0).astype(BF16)
    row0 = _iota((rows, bw), 0) == 0
    row0_low = _iota((rows, LANES), 0) == 0
    vec = vec_ref[...]
    w0, a0, k_k, k_a, r_k, gn_w, gn_b = (vec[n:n + 1, :] for n in range(7))
    mu = mu_ref[...]
    w2, a2, g2 = w2_ref[...], a2_ref[...], g2_ref[...]

    def wide(x):
        return jnp.where(same, jnp.concatenate([x] * nh, 0), 0.0)

    def shifted(ref, start, first_row, m):
        cur = ref[pl.ds(start, rows), :]
        last8 = ref[pl.ds(pl.multiple_of(jnp.maximum(start - 8, 0), 8), 8), :]
        prev_row = last8[7:8, :] * jnp.where(start > 0, 1.0, 0.0)
        prev = jnp.where(first_row, prev_row, pltpu.roll(cur, 1, 0))
        return cur + (prev - cur) * m

    def body(c, st):
        start = pl.multiple_of(c * rows, rows)
        r = shifted(r_ref, start, row0, mu[0:1, :])
        k = shifted(k_ref, start, row0, mu[1:2, :])
        v = shifted(v_ref, start, row0, mu[2:3, :])
        low = shifted(low_ref, start, row0_low, mulow_ref[...])
        w_raw = -_softplus(-(w0 + _dot3(jnp.tanh(low), w2))) - 0.5
        lw = -jnp.exp(w_raw)
        a = _sigmoid(a0 + _dot3(low, a2))
        g = _dot3(_sigmoid(low), g2)
        kk = k * k_k
        kk = kk / jnp.maximum(jnp.sqrt(_dot_rhs_exact(kk * kk, ones_bd)), 1e-12)
        k2 = k * (1.0 + (a - 1.0) * k_a)
        bonus = _dot_rhs_exact(r * k2 * r_k, ones_bd) * v
        b = kk * a

        cum = _dot_lhs_exact(tri, lw)
        e_in = jnp.exp(cum)
        e_inv = jnp.exp(-cum)
        a_dec = -kk * jnp.exp(cum - lw)
        r_dec = r * e_in
        b_inv = b * e_inv
        k_inv = k2 * e_inv
        local = [chunk_terms(*(z[s * ch:(s + 1) * ch] for z in (a_dec, r_dec, b_inv, k_inv, v, e_in)))
                 for s in range(nsub)]
        ys = []
        for qp, y0, gmat, hmat, g_col in local:
            st16 = st.astype(BF16)
            y_w = _dot(qp, st16) + y0
            st = g_col * st + _dot(gmat, st16) + hmat
            y = y_w[0:ch]
            for h in range(1, nh):
                y = y + y_w[h * ch:(h + 1) * ch]
            ys.append(y)
        y = jnp.concatenate(ys, 0)
        mean = _dot_rhs_exact(y, avg_bd)
        yc = y - mean
        var = _dot_rhs_exact(yc * yc, avg_bd)
        yn = yc * lax.rsqrt(var + D_GN_EPS) * gn_w + gn_b
        o_ref[pl.ds(start, rows), :] = (yn + bonus) * g
        return st

    def chunk_terms(a_dec, r_dec, b_inv, k_inv, v, e_in):
        g_last = e_in[ch - 1:ch, :]
        xa = wide(a_dec).astype(BF16)
        xr_f = wide(r_dec)
        xr = xr_f.astype(BF16)
        yb = wide(b_inv).astype(BF16)
        yk = wide(k_inv).astype(BF16)
        bh = wide(b_inv * g_last).astype(BF16)
        kh = wide(k_inv * g_last).astype(BF16)
        vw = wide(v).astype(BF16)

        a_ab = jnp.where(strict, _dotg(xa, yb, NT), 0.0)
        a_ak = jnp.where(strict, _dotg(xa, yk, NT), 0.0).astype(BF16)
        m_rb = jnp.where(incl, _dotg(xr, yb, NT), 0.0).astype(BF16)
        m_rk = jnp.where(incl, _dotg(xr, yk, NT), 0.0).astype(BF16)

        t_inv = jnp.where(eye, 1.0, a_ab)
        apow = a_ab
        for _ in range(int(math.log2(ch)) - 1):
            ab16 = apow.astype(BF16)
            apow = _dot(ab16, ab16)
            t_inv = t_inv + _dot(t_inv.astype(BF16), apow.astype(BF16))
        t16 = t_inv.astype(BF16)

        atp = _dot(t16, xa).astype(BF16)
        vp = _dot(t16, _dot(a_ak, vw).astype(BF16)).astype(BF16)
        qp = (xr_f + _dot(m_rb, atp)).astype(BF16)
        y0 = _dot(m_rb, vp) + _dot(m_rk, vw)
        gmat = _dotg(bh, atp, TN).astype(BF16)
        hmat = _dotg(bh, vp, TN) + _dotg(kh, vw, TN)
        g_col = jnp.sum(jnp.where(eye, g_last, 0.0), -1, keepdims=True)
        return qp, y0, gmat, hmat, g_col

    lax.fori_loop(0, seq // rows, body, jnp.zeros((bw, bw), F32))


def _mixer_d(h, mu3, mulow, w2p, a2p, g2p, vec, bsz, seq, nsub=4):
    assert seq % (nsub * D_CHUNK) == 0 and D_CHUNK == D_HEAD_SIZE

    def piece(name):
        off, w = _PIECE_OFF[name]
        return pl.BlockSpec((seq, w * LANES), lambda b, o=off // w: (b, o))

    def const(a):
        return pl.BlockSpec(a.shape, lambda b: (0, 0))

    return pl.pallas_call(
        functools.partial(_mixer_d_kernel, seq=seq, nsub=nsub),
        grid=(bsz,),
        in_specs=[piece("d_r"), piece("d_k"), piece("d_v"), piece("d_low"),
                  const(mu3), const(mulow), const(w2p), const(a2p), const(g2p), const(vec)],
        out_specs=pl.BlockSpec((seq, BRANCH_WIDTH), lambda b: (b, 0)),
        out_shape=jax.ShapeDtypeStruct((bsz * seq, BRANCH_WIDTH), F32),
        compiler_params=_params("parallel"),
        name="mixer_d",
    )(h, h, h, h, mu3, mulow, w2p, a2p, g2p, vec)


def _merge_kernel(x_ref, ya_ref, yb_ref, yc_ref, yd_ref, wg_ref, wb_ref, wo_ref, g_ref, b_ref, o_ref,
                  *, alpha):
    x = x_ref[...]
    xb = x.astype(BF16)
    merged = None
    for n, y_ref in enumerate((ya_ref, yb_ref, yc_ref, yd_ref)):
        gate = _sigmoid(_dot(xb, wg_ref[:, n * D_MODEL:(n + 1) * D_MODEL]))
        term = gate * _dot(y_ref[...].astype(BF16), wb_ref[n])
        merged = term if merged is None else merged + term
    z = alpha * x + _dot(merged.astype(BF16), wo_ref[...])
    o_ref[...] = _layer_norm(z, g_ref[...], b_ref[...])


def _merge(x2d, ys, wg, wb, wo, g, b, alpha, tm):
    n, d = x2d.shape

    def const(a):
        nd = a.ndim
        return pl.BlockSpec(a.shape, lambda i: (0,) * nd)

    yspec = pl.BlockSpec((tm, BRANCH_WIDTH), lambda i: (i, 0))
    return pl.pallas_call(
        functools.partial(_merge_kernel, alpha=alpha),
        grid=(n // tm,),
        in_specs=[pl.BlockSpec((tm, d), lambda i: (i, 0)), yspec, yspec, yspec, yspec,
                  const(wg), const(wb), const(wo), const(g), const(b)],
        out_specs=pl.BlockSpec((tm, d), lambda i: (i, 0)),
        out_shape=jax.ShapeDtypeStruct((n, d), F32),
        compiler_params=_params("parallel"),
        name="merge",
    )(x2d, *ys, wg, wb, wo, g, b)


def _moe_kernel(x_ref, ltri_ref, wr_ref, br_ref, wg_ref, wu_ref, wd_ref, g_ref, b_ref, o_ref,
                xs_ref, gs_ref, ys_ref, pt_ref, seg_ref, *, alpha):
    e = pl.program_id(1)
    tm = x_ref.shape[0]
    nrow = xs_ref.shape[0]
    ck = MOE_CHUNK
    lane = _iota((tm, LANES), 1)

    @pl.when(e == 0)
    def _():
        x = x_ref[...]
        xb = x.astype(BF16)
        ys_ref[...] = jnp.zeros_like(ys_ref)
        x_lo = (x - xb.astype(F32)).astype(BF16)
        cross = _dot(jnp.concatenate([xb, x_lo], 0), wr_ref[...])
        logits = (cross[0:tm, 0:LANES] + (cross[0:tm, LANES:] + cross[tm:, 0:LANES])) + br_ref[...]
        is_g = jnp.logical_and(lane >= MOE_EXPERTS, lane < MOE_EXPERTS + MOE_GROUPS)
        lg = jnp.where(is_g, logits, -jnp.inf)
        gmax = jnp.max(lg, -1, keepdims=True)
        ptop = 1.0 / jnp.sum(jnp.exp(lg - gmax), -1, keepdims=True)
        gsel = jnp.min(jnp.where(lg == gmax, lane, 2 * LANES), -1, keepdims=True) - MOE_EXPERTS
        in_group = jnp.logical_and(lane < MOE_EXPERTS, lane // MOE_PER_GROUP == gsel)
        le = jnp.where(in_group, logits, -jnp.inf)
        v1 = jnp.max(le, -1, keepdims=True)
        i1 = jnp.min(jnp.where(le == v1, lane, 2 * LANES), -1, keepdims=True)
        le2 = jnp.where(lane == i1, -jnp.inf, le)
        v2 = jnp.max(le2, -1, keepdims=True)
        i2 = jnp.min(jnp.where(le2 == v2, lane, 2 * LANES), -1, keepdims=True)
        e2 = jnp.exp(v2 - v1)
        w1 = ptop / (1.0 + e2)
        gate = jnp.where(lane == i1, w1, jnp.where(lane == i2, w1 * e2, 0.0))

        onehot = jnp.where(lane == gsel, 1.0, 0.0)
        rank = _dot(ltri_ref[...], onehot.astype(BF16))
        cnt = jnp.sum(onehot, 0, keepdims=True).astype(I32)
        nch = jnp.right_shift(cnt + (ck - 1), int(math.log2(ck)))
        lane1 = _iota((1, LANES), 1)
        off = jnp.int32(0)
        off_vec = jnp.zeros((1, LANES), I32)
        for grp in range(MOE_GROUPS):
            n_g = nch[0, grp]
            seg_ref[grp] = off
            seg_ref[MOE_GROUPS + grp] = n_g
            off_vec = jnp.where(lane1 == grp, off, off_vec)
            off = off + n_g * ck
        dest = jnp.sum(onehot * (rank + off_vec.astype(F32)), -1, keepdims=True)
        pt_ref[...] = jnp.where(_iota((tm, nrow), 1) == dest.astype(I32), 1.0, 0.0).astype(BF16)
        dest_row = jnp.broadcast_to(dest, (tm, LANES)).T[0:1, :].astype(I32)
        perm = jnp.where(_iota((nrow, tm), 0) == dest_row, 1.0, 0.0).astype(BF16)
        gate_hi, gate_lo = _split2(gate)
        moved = _dot(perm, jnp.concatenate([xb, gate_hi, gate_lo], 1))
        d = x_ref.shape[1]
        xs_ref[...] = moved[:, 0:d].astype(BF16)
        gs_ref[...] = moved[:, d:d + LANES] + moved[:, d + LANES:]

    grp = e // (MOE_PER_GROUP // 2)
    seg_off = seg_ref[grp]
    seg_chunks = seg_ref[MOE_GROUPS + grp]

    def run_experts(start, size):
        rows = pl.ds(pl.multiple_of(start, ck), size)
        xc = xs_ref[rows, :]
        gs = gs_ref[rows, :]
        lane_s = _iota((size, LANES), 1)
        g0 = jnp.sum(jnp.where(lane_s == 2 * e, gs, 0.0), -1, keepdims=True)
        g1 = jnp.sum(jnp.where(lane_s == 2 * e + 1, gs, 0.0), -1, keepdims=True)
        hg = _dot(xc, wg_ref[0])
        hu = _dot(xc, wu_ref[0])
        gcol = jnp.where(_iota((size, 2 * MOE_HIDDEN), 1) < MOE_HIDDEN, g0, g1)
        hidden = hg * _sigmoid(hg) * hu * gcol
        ys_ref[rows, :] += _dot(hidden.astype(BF16), wd_ref[0])

    big = 4

    def quad(c, carry):
        run_experts(seg_off + c * (big * ck), big * ck)
        return carry

    lax.fori_loop(0, seg_chunks // big, quad, 0)
    rem = seg_chunks % big
    rem_start = seg_off + (seg_chunks - rem) * ck
    for r in range(1, big):
        @pl.when(rem == r)
        def _(r=r):
            run_experts(rem_start, r * ck)

    @pl.when(e == pl.num_programs(1) - 1)
    def _():
        z = alpha * x_ref[...] + _dot(pt_ref[...], ys_ref[...].astype(BF16))
        o_ref[...] = _layer_norm(z, g_ref[...], b_ref[...])


def _moe(x2d, wr, br, wg, wu, wd, g, b, alpha, tm):
    n, d = x2d.shape
    ne = wg.shape[0]
    hid = wg.shape[2]
    nrow = tm + MOE_GROUPS * MOE_CHUNK
    ltri = jnp.tril(jnp.ones((tm, tm), BF16), -1)
    wg = wg.reshape(ne // 2, 2, d, hid).transpose(0, 2, 1, 3).reshape(ne // 2, d, 2 * hid)
    wu = wu.reshape(ne // 2, 2, d, hid).transpose(0, 2, 1, 3).reshape(ne // 2, d, 2 * hid)
    wd = wd.reshape(ne // 2, 2 * hid, d)

    def const(a):
        return pl.BlockSpec(a.shape, lambda i, e: (0, 0))

    return pl.pallas_call(
        functools.partial(_moe_kernel, alpha=alpha),
        grid=(n // tm, ne // 2),
        in_specs=[pl.BlockSpec((tm, d), lambda i, e: (i, 0)), const(ltri), const(wr), const(br),
                  pl.BlockSpec((1, d, 2 * hid), lambda i, e: (e, 0, 0)),
                  pl.BlockSpec((1, d, 2 * hid), lambda i, e: (e, 0, 0)),
                  pl.BlockSpec((1, 2 * hid, d), lambda i, e: (e, 0, 0)),
                  const(g), const(b)],
        out_specs=pl.BlockSpec((tm, d), lambda i, e: (i, 0)),
        out_shape=jax.ShapeDtypeStruct((n, d), F32),
        scratch_shapes=[pltpu.VMEM((nrow, d), BF16), pltpu.VMEM((nrow, LANES), F32),
                        pltpu.VMEM((nrow, d), F32), pltpu.VMEM((tm, nrow), BF16),
                        pltpu.SMEM((2 * MOE_GROUPS,), I32)],
        compiler_params=_params("parallel", "arbitrary"),
        name="moe",
    )(x2d, ltri, wr, br, wg, wu, wd, g, b)


def _t5_bucket(dist):
    exact = RPB_BUCKETS // 2
    d = jnp.maximum(dist, 0)
    df = jnp.maximum(d, 1).astype(F32)
    large = exact + (jnp.log(df / exact) / math.log(RPB_MAX_DIST / exact)
                     * (RPB_BUCKETS - exact)).astype(I32)
    large = jnp.minimum(large, RPB_BUCKETS - 1)
    return jnp.where(d < exact, d, large)


def _rpb_lookup(tab, dist):
    bucket = _t5_bucket(dist)[..., None]
    out = jnp.zeros(dist.shape + (tab.shape[1],), F32)
    for k in range(RPB_BUCKETS):
        out = jnp.where(bucket == k, tab[k], out)
    return out


def _bias_a(tab):
    qi = jnp.arange(A_BLOCK)[:, None]
    kj = jnp.arange(2 * A_BLOCK)[None, :]
    rdist = qi + A_BLOCK - kj
    out = []
    for window, dilation in A_PATTERNS:
        in_band = (rdist >= 0) & (rdist <= window // dilation)
        bias = jnp.transpose(_rpb_lookup(tab, rdist * dilation), (2, 0, 1))
        out.append(jnp.where(in_band[None], bias, NEG))
    return jnp.stack(out).astype(F32)


def _bias_b(tab, nblk):
    qi = jnp.arange(Q_BLOCK)[:, None]
    kj = jnp.arange(Q_BLOCK)[None, :]
    delta = jnp.arange(nblk)[:, None, None] * Q_BLOCK
    bias = _rpb_lookup(tab, delta + qi - kj)
    return jnp.transpose(bias, (0, 3, 2, 1)).astype(F32)


def _in_offsets():
    splits = (("a_q", 256), ("a_k", 256), ("a_v", 256), ("b_q", 256), ("b_ckv", 64), ("b_iq", 256),
              ("b_ik", 32), ("b_iw", 8), ("c_q", 128), ("c_k", 128), ("c_v", 256), ("c_a", 16),
              ("c_g", 256), ("d", 832), ("gate", 4096))
    off, out = 0, {}
    for name, w in splits:
        out[name] = (off, w)
        off += w
    return out


def _proj_columns():
    src = _in_offsets()
    d0 = src["d"][0]
    named = {
        "a_q": np.arange(256) + src["a_q"][0], "a_k": np.arange(256) + src["a_k"][0],
        "a_v": np.arange(256) + src["a_v"][0], "b_q": np.arange(256) + src["b_q"][0],
        "b_iq": np.arange(256) + src["b_iq"][0],
        "b_ckv4": np.tile(np.arange(64) + src["b_ckv"][0], B_HEADS),
        "b_ik8": np.tile(np.arange(32) + src["b_ik"][0], IDX_HEADS),
        "c_v": np.arange(256) + src["c_v"][0], "c_g": np.arange(256) + src["c_g"][0],
        "d_r": np.arange(256) + d0, "d_k": np.arange(256) + d0 + 256, "d_v": np.arange(256) + d0 + 512,
        "c_q": np.arange(128) + src["c_q"][0], "c_k": np.arange(128) + src["c_k"][0],
        "b_iw": np.arange(8) + src["b_iw"][0], "c_a": np.arange(16) + src["c_a"][0],
        "d_low": np.arange(64) + d0 + 768, "pad": np.arange(0),
    }
    cols = []
    for name, w in _PIECES:
        c = named[name]
        cols.append(np.concatenate([c, -np.ones(w * LANES - len(c), np.int64)]))
    return np.concatenate(cols)


def _proj_weight(w):
    cols = _proj_columns()
    cuts = [0] + [k for k in range(1, len(cols))
                  if (cols[k] < 0) != (cols[k - 1] < 0) or (cols[k] >= 0 and cols[k] != cols[k - 1] + 1)]
    cuts.append(len(cols))
    parts = []
    for a, b in zip(cuts[:-1], cuts[1:]):
        if cols[a] < 0:
            parts.append(jnp.zeros((w.shape[0], b - a), BF16))
        else:
            parts.append(w[:, int(cols[a]):int(cols[a]) + (b - a)].astype(BF16))
    return jnp.concatenate(parts, 1)


def _pad_rows(w, first, total):
    return jnp.zeros((total, w.shape[1]), w.dtype).at[first:first + w.shape[0]].set(w)


def kernel(x, rpb_table, w_in, b_kv_gain, b_w_uv, c_a_up, c_a_bias, c_norm_gain, d_mu, d_w0, d_w2,
           d_a0, d_a2, d_g2, d_k_k, d_k_a, d_r_k, d_gn_w, d_gn_b, w_branch, w_out, ln_g, ln_b,
           router_g, router_g_bias, router_e, router_e_bias, moe_w_gate, moe_w_up, moe_w_down):
    bsz, seq, d_model = x.shape
    depth = w_in.shape[0]
    n = bsz * seq
    alpha = (2 * depth) ** 0.25
    gate_off = _in_offsets()["gate"][0]
    bias_a = _bias_a(rpb_table[:, :A_HEADS])
    bias_b = _bias_b(rpb_table[:, A_HEADS:], seq // Q_BLOCK)
    tm_proj = math.gcd(n, 1024)
    tm_merge = math.gcd(n, 512)
    tm_moe = math.gcd(n, 1024)

    x2d = x.reshape(n, d_model)
    for l in range(depth):
        h = _project(x2d, _proj_weight(w_in[l]), tm_proj, PROJ_WIDTH // 2)

        ya = _mixer_a(h, bias_a, bsz, seq)

        gain4 = jnp.tile(b_kv_gain[l], B_HEADS)[None, :]
        wuv_t = jnp.transpose(b_w_uv[l], (0, 2, 1)).astype(BF16)
        yb = _mixer_b(h, gain4, wuv_t, bias_b, bsz, seq)

        aup_pad = _pad_rows(c_a_up[l], 0, LANES)
        yc = _mixer_c(h, aup_pad, c_a_bias[l][None, :], c_norm_gain[l][None, :], bsz, seq)

        mu = d_mu[l]
        mu3 = mu[:3 * BRANCH_WIDTH].reshape(3, BRANCH_WIDTH)
        mulow = jnp.zeros((1, LANES), F32).at[0, :D_DECAY_RANK + D_ICLR_RANK + D_GATE_RANK].set(
            mu[3 * BRANCH_WIDTH:])
        w2p = _pad_rows(d_w2[l], 0, LANES)
        a2p = _pad_rows(d_a2[l], D_DECAY_RANK, LANES)
        g2p = _pad_rows(d_g2[l], D_DECAY_RANK + D_ICLR_RANK, LANES)
        vec = jnp.stack([d_w0[l], d_a0[l], d_k_k[l], d_k_a[l], d_r_k[l], d_gn_w[l], d_gn_b[l],
                         jnp.zeros_like(d_w0[l])])
        yd = _mixer_d(h, mu3, mulow, w2p, a2p, g2p, vec, bsz, seq)

        wg = w_in[l][:, gate_off:gate_off + N_BRANCHES * d_model].astype(BF16)
        x2d = _merge(x2d, (ya, yb, yc, yd), wg, w_branch[l].astype(BF16), w_out[l].astype(BF16),
                     ln_g[l, 0][None, :], ln_b[l, 0][None, :], alpha, tm_merge)

        wr = jnp.zeros((d_model, LANES), F32)
        wr = wr.at[:, :MOE_EXPERTS].set(router_e[l]).at[:, MOE_EXPERTS:MOE_EXPERTS + MOE_GROUPS].set(
            router_g[l])
        br = jnp.zeros((1, LANES), F32)
        br = br.at[0, :MOE_EXPERTS].set(router_e_bias[l]).at[0, MOE_EXPERTS:MOE_EXPERTS + MOE_GROUPS].set(
            router_g_bias[l])
        wr = jnp.concatenate(_split2(wr), 1)
        x2d = _moe(x2d, wr, br, moe_w_gate[l].astype(BF16), moe_w_up[l].astype(BF16),
                   moe_w_down[l].astype(BF16), ln_g[l, 1][None, :], ln_b[l, 1][None, :], alpha, tm_moe)
    return x2d.reshape(bsz, seq, d_model)
```

```python
import functools
import math

import jax
import jax.numpy as jnp
import numpy as np
from jax import lax
from jax.experimental import pallas as pl
from jax.experimental.pallas import tpu as pltpu

F32 = jnp.float32
BF16 = jnp.bfloat16
I32 = jnp.int32

D_MODEL = 1024
N_BRANCHES = 4
BRANCH_WIDTH = 256
HEAD_DIM = 64
A_HEADS = 4
A_PATTERNS = ((128, 1), (512, 4), (2048, 16))
A_BLOCK = 128
B_HEADS = 4
B_LATENT = 64
IDX_HEADS = 8
IDX_DIM = 32
TOPK_MAX = 256
TOPK_DIV = 4
Q_BLOCK = 128
C_HEADS = 4
C_KEY_DIM = 32
C_VAL_DIM = 64
C_GATE_RANK = 16
C_GATE_TAU = 16.0
C_CHUNK = 64
D_HEADS = 4
D_HEAD_SIZE = 64
D_DECAY_RANK = 16
D_ICLR_RANK = 16
D_GATE_RANK = 32
D_GN_EPS = 64e-5
D_CHUNK = 64
RPB_BUCKETS = 32
RPB_MAX_DIST = 2048
MOE_GROUPS = 4
MOE_PER_GROUP = 8
MOE_EXPERTS = 32
MOE_HIDDEN = 256
MOE_CHUNK = 128
LN_EPS = 1e-5

LANES = 128
NEG = -1e30
VMEM_LIMIT = 56 * 1024 * 1024

NT = (((1,), (1,)), ((), ()))
TN = (((0,), (0,)), ((), ()))

_PIECES = (
    ("a_q", 2), ("a_k", 2), ("a_v", 2), ("b_q", 2), ("b_iq", 2),
    ("c_v", 2), ("c_g", 2), ("d_r", 2), ("d_k", 2), ("d_v", 2),
    ("c_q", 1), ("c_k", 1), ("b_misc", 1), ("c_a", 1), ("d_low", 1),
)
_PIECE_OFF = {}
_off = 0
for _n, _w in _PIECES:
    _PIECE_OFF[_n] = (_off, _w)
    _off += _w
PROJ_WIDTH = _off * LANES


def _dot(a, b):
    return jnp.dot(a, b, preferred_element_type=F32)


def _dotg(a, b, dims):
    return lax.dot_general(a, b, dims, preferred_element_type=F32)


def _split2(x):
    hi = x.astype(BF16)
    lo = (x - hi.astype(F32)).astype(BF16)
    return hi, lo


def _dot_rhs_exact(x, m):
    hi, lo = _split2(x)
    return _dot(hi, m) + _dot(lo, m)


def _dot_lhs_exact(m, x):
    hi, lo = _split2(x)
    return _dot(m, hi) + _dot(m, lo)


def _dot3(x, w):
    xh, xl = _split2(x)
    wh, wl = _split2(w)
    return _dot(xh, wh) + (_dot(xl, wh) + _dot(xh, wl))


def _sigmoid(x):
    return 1.0 / (1.0 + jnp.exp(-x))


def _softplus(x):
    return jnp.maximum(x, 0.0) + jnp.log(1.0 + jnp.exp(-jnp.abs(x)))


def _iota(shape, axis):
    return lax.broadcasted_iota(I32, shape, axis)


def _group_matrix(n, group, value, dtype=BF16):
    r = _iota((n, n), 0) // group
    c = _iota((n, n), 1) // group
    return jnp.where(r == c, value, 0.0).astype(dtype)


def _layer_norm(z, g, b):
    mu = jnp.mean(z, -1, keepdims=True)
    zc = z - mu
    var = jnp.mean(zc * zc, -1, keepdims=True)
    return zc * lax.rsqrt(var + LN_EPS) * g + b


def _params(*sem):
    return pltpu.CompilerParams(dimension_semantics=sem, vmem_limit_bytes=VMEM_LIMIT)


def _proj_kernel(x_ref, w_ref, o_ref):
    o_ref[...] = _dot(x_ref[...].astype(BF16), w_ref[...])


def _project(x2d, w, tm, tn):
    n, k = x2d.shape
    wt = w.shape[1]
    return pl.pallas_call(
        _proj_kernel,
        grid=(n // tm, wt // tn),
        in_specs=[pl.BlockSpec((tm, k), lambda i, j: (i, 0)),
                  pl.BlockSpec((k, tn), lambda i, j: (0, j))],
        out_specs=pl.BlockSpec((tm, tn), lambda i, j: (i, j)),
        out_shape=jax.ShapeDtypeStruct((n, wt), F32),
        compiler_params=_params("parallel", "parallel"),
        name="proj",
    )(x2d, w)


def _mixer_a_kernel(q0_ref, q1_ref, k0_ref, k1_ref, v0_ref, v1_ref, bias_ref, o_ref, op_ref, lse_ref,
                    *, seq, dils):
    q_refs, k_refs, v_refs = (q0_ref, q1_ref), (k0_ref, k1_ref), (v0_ref, v1_ref)
    ab = A_BLOCK
    lane = _iota((ab, LANES), 1)
    upper = lane >= HEAD_DIM
    col = _iota((ab, 2 * ab), 1)
    scale = HEAD_DIM ** -0.5

    for p, d in enumerate(dils):
        per_res = seq // (d * ab)

        def body(idx, carry, p=p, d=d, per_res=per_res):
            r = idx // per_res
            b = idx % per_res
            start = b * (ab * d) + r
            pstart = jnp.maximum(start - ab * d, r)
            if d == 1:
                rows, prow = pl.ds(pl.multiple_of(start, ab), ab), pl.ds(pl.multiple_of(pstart, ab), ab)
            else:
                rows, prow = pl.ds(start, ab, stride=d), pl.ds(pstart, ab, stride=d)
            first_pen = jnp.where(b == 0, NEG, 0.0)
            pen = jnp.where(col < ab, first_pen, 0.0)
            for hp in range(2):
                q_ref, k_ref, v_ref = q_refs[hp], k_refs[hp], v_refs[hp]
                q = q_ref[rows, :]
                kb = jnp.concatenate([k_ref[prow, :], k_ref[rows, :]], 0).astype(BF16)
                vb = jnp.concatenate([v_ref[prow, :], v_ref[rows, :]], 0).astype(BF16)
                outs, lses = [], []
                for hh in range(2):
                    hm = upper if hh else jnp.logical_not(upper)
                    qm = jnp.where(hm, q, 0.0).astype(BF16)
                    s = _dotg(qm, kb, NT) * scale + bias_ref[p, 2 * hp + hh] + pen
                    mx = jnp.max(s, -1, keepdims=True)
                    e = jnp.exp(s - mx)
                    den = jnp.sum(e, -1, keepdims=True)
                    outs.append(_dot(e.astype(BF16), vb) / den)
                    lses.append(mx + jnp.log(den))
                op_ref[2 * p + hp, rows, :] = jnp.where(upper, outs[1], outs[0])
                lse_ref[2 * p + hp, rows, :] = jnp.where(upper, lses[1], lses[0])
            return carry

        def body_pair(t, carry, body=body):
            return body(2 * t + 1, body(2 * t, carry))

        lax.fori_loop(0, seq // (2 * ab), body_pair, 0)

    cr = 256

    def combine(c, carry):
        rows = pl.ds(pl.multiple_of(c * cr, cr), cr)
        for hp in range(2):
            ls = [lse_ref[2 * p + hp, rows, :] for p in range(len(dils))]
            mx = functools.reduce(jnp.maximum, ls)
            es = [jnp.exp(l - mx) for l in ls]
            num = functools.reduce(lambda a, b: a + b,
                                   [e * op_ref[2 * p + hp, rows, :] for p, e in enumerate(es)])
            o_ref[rows, hp * LANES:(hp + 1) * LANES] = num / functools.reduce(lambda a, b: a + b, es)
        return carry

    lax.fori_loop(0, seq // cr, combine, 0)


def _mixer_a(h, bias, bsz, seq):
    dils = tuple(d for _, d in A_PATTERNS)
    for w, d in A_PATTERNS:
        assert w // d == A_BLOCK and seq % (d * A_BLOCK) == 0
    npat = len(dils)

    def pieces(name):
        off, w = _PIECE_OFF[name]
        return [pl.BlockSpec((seq, LANES), lambda b, o=off + u: (b, o)) for u in range(w)]

    return pl.pallas_call(
        functools.partial(_mixer_a_kernel, seq=seq, dils=dils),
        grid=(bsz,),
        in_specs=pieces("a_q") + pieces("a_k") + pieces("a_v")
        + [pl.BlockSpec(bias.shape, lambda b: (0, 0, 0, 0))],
        out_specs=pl.BlockSpec((seq, BRANCH_WIDTH), lambda b: (b, 0)),
        out_shape=jax.ShapeDtypeStruct((bsz * seq, BRANCH_WIDTH), F32),
        scratch_shapes=[pltpu.VMEM((2 * npat, seq, LANES), F32),
                        pltpu.VMEM((2 * npat, seq, LANES), F32)],
        compiler_params=_params("parallel"),
        name="mixer_a",
    )(h, h, h, h, h, h, bias)


def _mixer_b_kernel(q_ref, iq_ref, mq_ref, mk_ref, gain_ref, wuvt_ref, bias_ref, o_ref,
                    ckvn_ref, ckvt_ref, ikb_ref, sc_ref, sc16_ref, lg_ref, iqs_ref, qs_ref, *, seq, keep):
    qb = Q_BLOCK
    sb = 2 * qb
    i = pl.program_id(1)
    iw_lane = B_LATENT + IDX_DIM

    @pl.when(i == 0)
    def _():
        src = _iota((LANES, BRANCH_WIDTH), 0)
        dst = _iota((LANES, BRANCH_WIDTH), 1)
        is_lat = src < B_LATENT
        rep_lat = jnp.where(jnp.logical_and(is_lat, dst % B_LATENT == src), 1.0, 0.0).astype(BF16)
        avg_lat = jnp.where(is_lat, 1.0 / B_LATENT, 0.0).astype(BF16)
        is_ik = jnp.logical_and(src >= B_LATENT, src < iw_lane)
        rep_ik = jnp.where(jnp.logical_and(is_ik, dst % IDX_DIM == src - B_LATENT), 1.0, 0.0).astype(BF16)

        def prep(c, carry):
            rows = pl.ds(pl.multiple_of(c * sb, sb), sb)
            x = mk_ref[rows, :]
            hi, lo = _split2(x)
            lo2 = (x - hi.astype(F32) - lo.astype(F32)).astype(BF16)
            lat = _dot(hi, rep_lat) + (_dot(lo, rep_lat) + _dot(lo2, rep_lat))
            ms = _dot_rhs_exact(x * x, avg_lat)
            xn = lat * lax.rsqrt(ms + 1e-6) * gain_ref[...]
            ckvn_ref[rows, :] = xn.astype(BF16)
            ckvt_ref[c] = xn.T[0:B_LATENT, :].astype(BF16)
            ikb_ref[rows, :] = _dot(hi, rep_ik).astype(BF16)
            return carry

        lax.fori_loop(0, seq // sb, prep, 0)

    lane = _iota((qb, BRANCH_WIDTH), 1)
    iq = iq_ref[...] * (IDX_DIM ** -0.5)
    for h in range(IDX_HEADS):
        iqs_ref[h * qb:(h + 1) * qb, :] = jnp.where(lane // IDX_DIM == h, iq, 0.0).astype(BF16)
    q = q_ref[...] * (B_LATENT ** -0.5)
    for h in range(B_HEADS):
        qs_ref[h * qb:(h + 1) * qb, :] = jnp.where(lane // B_LATENT == h, q, 0.0).astype(BF16)
    iw_t = (mq_ref[...] * (IDX_HEADS ** -0.5)).T[iw_lane:iw_lane + IDX_HEADS, :]
    krow = _iota((sb, qb), 0)
    qcol = _iota((sb, qb), 1)
    nsb = (i + 2) // 2

    def fold(x, op):
        parts = [x[r:r + 8] for r in range(0, sb, 8)]
        while len(parts) > 1:
            parts = [op(parts[k], parts[k + 1]) for k in range(0, len(parts), 2)]
        return parts[0]

    def for_steps(body, init):
        def pair(p, carry):
            return body(2 * p + 1, body(2 * p, carry))

        carry = lax.fori_loop(0, nsb // 2, pair, init)
        return lax.cond(nsb % 2 == 1, lambda c: body(nsb - 1, c), lambda c: c, carry)

    def score_body(j, carry):
        kr = pl.ds(pl.multiple_of(j * sb, sb), sb)
        rel = _dotg(ikb_ref[kr, :], iqs_ref[...], NT)
        sc = jnp.maximum(rel[:, 0:qb], 0.0) * iw_t[0:1, :]
        for h in range(1, IDX_HEADS):
            sc = sc + jnp.maximum(rel[:, h * qb:(h + 1) * qb], 0.0) * iw_t[h:h + 1, :]
        sc = jnp.where(krow <= qcol + (i * qb - j * sb), sc, -jnp.inf)
        sc_ref[j] = sc
        sc16_ref[j] = pltpu.bitcast(pltpu.bitcast(sc, I32) & jnp.int32(-65536), F32).astype(BF16)
        logits = _dotg(ckvn_ref[kr, :], qs_ref[...], NT)
        d0 = i - 2 * j
        for h in range(B_HEADS):
            lg = logits[:, h * qb:(h + 1) * qb]
            lg_ref[h, j, 0:qb, :] = lg[0:qb] + bias_ref[d0, h]
            lg_ref[h, j, qb:sb, :] = lg[qb:sb] + bias_ref[jnp.maximum(d0 - 1, 0), h]
        return carry

    for_steps(score_body, 0)

    sign = jnp.int32(-2 ** 31)

    def key_to_float(u):
        k = u ^ sign
        bits = k ^ ((k >> 31) & jnp.int32(0x7FFFFFFF))
        return pltpu.bitcast(bits, F32)

    def count(pred):
        def cnt(j, acc):
            return acc + fold(pred(j), jnp.add)
        acc = for_steps(cnt, jnp.zeros((8, qb), F32))
        return jnp.sum(acc, 0, keepdims=True)

    pk = 16

    def count_ge16(cf):
        cf = pltpu.bitcast(pltpu.bitcast(cf, I32) & jnp.int32(-65536), F32)
        c16 = jnp.broadcast_to(cf, (pk, qb)).astype(BF16)
        one, zero = jnp.ones((pk, qb), BF16), jnp.zeros((pk, qb), BF16)

        def cnt(j, acc):
            x = sc16_ref[j]
            parts = [jnp.where(x[r:r + pk] >= c16, one, zero) for r in range(0, sb, pk)]
            while len(parts) > 1:
                parts = [parts[k] + parts[k + 1] for k in range(0, len(parts), 2)]
            return acc + parts[0].astype(F32)

        acc = for_steps(cnt, jnp.zeros((pk, qb), F32))
        return jnp.sum(acc, 0, keepdims=True)

    def bit_body(t, state, packed=False):
        u, n_u = state
        cand = u | jnp.left_shift(jnp.int32(1), 31 - t)
        cf = key_to_float(cand)
        c = count_ge16(cf) if packed else count(lambda j: jnp.where(sc_ref[j] >= cf, 1.0, 0.0))
        take = c >= keep
        return jnp.where(take, cand, u), jnp.where(take, c, n_u)

    def bit_body16(t, state):
        return bit_body(t, state, packed=True)

    first_bits, bit_group = 24, 4
    state = (jnp.zeros((1, qb), I32), (jnp.zeros((1, qb), I32) + nsb * sb).astype(F32))
    state = bit_body16(0, state)
    n_pos = count(lambda j: jnp.where(sc_ref[j] > 0.0, 1.0, 0.0))
    zero_thr = jnp.logical_and(state[0] != 0, n_pos < keep)
    state = lax.fori_loop(1, 16, bit_body16, state)
    u, n_ge = lax.fori_loop(16, first_bits, bit_body, state)

    def more_bits(s):
        t, _, n_u = s
        open_rows = jnp.where(zero_thr, 0.0, jnp.abs(n_u - keep))
        return jnp.logical_and(t < 32, jnp.max(open_rows) > 0.0)

    def bit_group_body(s):
        t, u, n_u = s
        for k in range(bit_group):
            u, n_u = bit_body(t + k, (u, n_u))
        return t + bit_group, u, n_u

    _, u, n_ge = lax.while_loop(more_bits, bit_group_body, (jnp.int32(first_bits), u, n_ge))
    u = jnp.maximum(u ^ sign, jnp.int32(0x007FFFFF - 2 ** 31)) ^ sign
    thr = key_to_float(u)
    n_gt = count(lambda j: jnp.where(sc_ref[j] > thr, 1.0, 0.0))
    need = keep - n_gt

    nbits = int(math.ceil(math.log2(seq))) + 1

    def cut_search():
        def cut_body(t, cut):
            cand = cut | jnp.left_shift(jnp.int32(1), nbits - 1 - t)
            c = count(lambda j: jnp.where(sc_ref[j] == thr,
                                          jnp.where(krow < cand - j * sb, 1.0, 0.0), 0.0))
            return jnp.where(c <= need, cand, cut)

        return lax.fori_loop(0, nbits, cut_body, jnp.zeros((1, qb), I32))

    surplus = jnp.max(n_ge) > keep
    cut = lax.cond(surplus, cut_search, lambda: jnp.full((1, qb), 2 ** nbits - 1, I32))
    cut = jnp.minimum(cut, i * qb + 1 + _iota((1, qb), 1))

    def mask_body(j, mx):
        sc = sc_ref[j]
        sel = jnp.where(sc > thr, 1.0, jnp.where(sc == thr, jnp.where(krow < cut - j * sb, 1.0, 0.0), 0.0))
        out = []
        for h in range(B_HEADS):
            s = jnp.where(sel > 0.5, lg_ref[h, j], NEG)
            lg_ref[h, j] = s
            out.append(jnp.maximum(mx[h], fold(s, jnp.maximum)))
        return tuple(out)

    mx = for_steps(mask_body, tuple(jnp.full((8, qb), NEG, F32) for _ in range(B_HEADS)))
    ms = [jnp.max(m, 0, keepdims=True) for m in mx]

    def att_body(j, carry):
        ls, acc = carry
        ps, new_ls = [], []
        for h in range(B_HEADS):
            pr = jnp.exp(lg_ref[h, j] - ms[h])
            new_ls.append(ls[h] + fold(pr, jnp.add))
            ps.append(pr.astype(BF16))
        upd = _dot(ckvt_ref[j], jnp.concatenate(ps, 1))
        return tuple(new_ls), acc + upd

    init = (tuple(jnp.zeros((8, qb), F32) for _ in range(B_HEADS)),
            jnp.zeros((B_LATENT, B_HEADS * qb), F32))
    ls, acc = for_steps(att_body, init)
    ls = [jnp.sum(l, 0, keepdims=True) for l in ls]
    o_t = (acc / jnp.concatenate(ls, 1)).astype(BF16)
    y_t = jnp.concatenate([_dot(wuvt_ref[h], o_t[:, h * qb:(h + 1) * qb]) for h in range(B_HEADS)], 0)
    o_ref[...] = y_t.T


def _mixer_b(h, gain4, wuv_t, bias, bsz, seq):
    qb = Q_BLOCK
    assert seq % 256 == 0
    nblk = seq // qb
    keep = min(TOPK_MAX, seq // TOPK_DIV)

    def qpiece(name):
        off, w = _PIECE_OFF[name]
        return pl.BlockSpec((qb, w * LANES), lambda b, i, o=off // w: (b * nblk + i, o))

    def kpiece(name):
        off, w = _PIECE_OFF[name]
        return pl.BlockSpec((seq, w * LANES), lambda b, i, o=off // w: (b, o))

    return pl.pallas_call(
        functools.partial(_mixer_b_kernel, seq=seq, keep=keep),
        grid=(bsz, nblk),
        in_specs=[qpiece("b_q"), qpiece("b_iq"), qpiece("b_misc"), kpiece("b_misc"),
                  pl.BlockSpec(gain4.shape, lambda b, i: (0, 0)),
                  pl.BlockSpec(wuv_t.shape, lambda b, i: (0, 0, 0)),
                  pl.BlockSpec(bias.shape, lambda b, i: (0, 0, 0, 0))],
        out_specs=pl.BlockSpec((qb, BRANCH_WIDTH), lambda b, i: (b * nblk + i, 0)),
        out_shape=jax.ShapeDtypeStruct((bsz * seq, BRANCH_WIDTH), F32),
        scratch_shapes=[pltpu.VMEM((seq, BRANCH_WIDTH), BF16),
                        pltpu.VMEM((nblk // 2, B_LATENT, 2 * qb), BF16),
                        pltpu.VMEM((seq, BRANCH_WIDTH), BF16),
                        pltpu.VMEM((nblk // 2, 2 * qb, qb), F32),
                        pltpu.VMEM((nblk // 2, 2 * qb, qb), BF16),
                        pltpu.VMEM((B_HEADS, nblk // 2, 2 * qb, qb), F32),
                        pltpu.VMEM((IDX_HEADS * qb, BRANCH_WIDTH), BF16),
                        pltpu.VMEM((B_HEADS * qb, BRANCH_WIDTH), BF16)],
        compiler_params=_params("parallel", "arbitrary"),
        name="mixer_b",
    )(h, h, h, h, gain4, wuv_t, bias)


def _mixer_c_kernel(q_ref, k_ref, v_ref, g_ref, a_ref, aup_ref, abias_ref, gain_ref, o_ref, *, seq):
    ch = C_CHUNK
    kw = C_HEADS * C_KEY_DIM
    vw = C_HEADS * C_VAL_DIM
    tri = jnp.where(_iota((ch, ch), 1) <= _iota((ch, ch), 0), 1.0, 0.0).astype(BF16)
    causal = _iota((C_HEADS * ch, ch), 1) <= (_iota((C_HEADS * ch, ch), 0) % ch)
    klane_head = _iota((C_HEADS * ch, kw), 1) // C_KEY_DIM
    krow_head = _iota((C_HEADS * ch, kw), 0) // ch
    vlane_head = _iota((ch, vw), 1) // C_VAL_DIM
    st_mask = (_iota((vw, kw), 0) // C_VAL_DIM) == (_iota((vw, kw), 1) // C_KEY_DIM)
    gmat = _group_matrix(vw, C_VAL_DIM, 1.0 / C_VAL_DIM)
    aup = aup_ref[...]

    def body(c, st):
        rows = pl.ds(pl.multiple_of(c * ch, ch), ch)
        qc = q_ref[rows, :] * (C_KEY_DIM ** -0.5)
        kc = k_ref[rows, :]
        vc = v_ref[rows, :]
        z = _dot3(a_ref[rows, :], aup) + abias_ref[...]
        log_a = -_softplus(-z) / C_GATE_TAU
        cum = _dot_lhs_exact(tri, log_a)
        last = cum[ch - 1:ch, :]
        q_dec = qc * jnp.exp(cum)
        k_inv = (kc * jnp.exp(-cum)).astype(BF16)
        k_dec = (kc * jnp.exp(last - cum)).astype(BF16)
        vb = vc.astype(BF16)
        qd_b = q_dec.astype(BF16)
        q_stack = jnp.where(klane_head == krow_head, jnp.concatenate([q_dec] * C_HEADS, 0), 0.0)
        att = jnp.where(causal, _dotg(q_stack.astype(BF16), k_inv, NT), 0.0)
        full = _dot(att.astype(BF16), vb)
        o = _dotg(qd_b, st.astype(BF16), NT)
        for h in range(C_HEADS):
            o = o + jnp.where(vlane_head == h, full[h * ch:(h + 1) * ch], 0.0)
        upd = _dotg(vb, k_dec, TN)
        st = st * jnp.exp(last) + jnp.where(st_mask, upd, 0.0)
        ms = _dot_rhs_exact(o * o, gmat)
        o = o * lax.rsqrt(ms + 1e-6) * gain_ref[...]
        g = g_ref[rows, :]
        o_ref[rows, :] = g * _sigmoid(g) * o
        return st

    def body_pair(t, st):
        return body(2 * t + 1, body(2 * t, st))

    lax.fori_loop(0, seq // (2 * ch), body_pair, jnp.zeros((vw, kw), F32))


def _mixer_c(h, aup_pad, abias, gain, bsz, seq):
    assert seq % C_CHUNK == 0

    def piece(name):
        off, w = _PIECE_OFF[name]
        return pl.BlockSpec((seq, w * LANES), lambda b, o=off // w: (b, o))

    def const(a):
        return pl.BlockSpec(a.shape, lambda b: (0, 0))

    return pl.pallas_call(
        functools.partial(_mixer_c_kernel, seq=seq),
        grid=(bsz,),
        in_specs=[piece("c_q"), piece("c_k"), piece("c_v"), piece("c_g"), piece("c_a"),
                  const(aup_pad), const(abias), const(gain)],
        out_specs=pl.BlockSpec((seq, BRANCH_WIDTH), lambda b: (b, 0)),
        out_shape=jax.ShapeDtypeStruct((bsz * seq, BRANCH_WIDTH), F32),
        compiler_params=_params("parallel"),
        name="mixer_c",
    )(h, h, h, h, h, aup_pad, abias, gain)


def _mixer_d_kernel(r_ref, k_ref, v_ref, low_ref, mu_ref, mulow_ref, w2_ref, a2_ref, g2_ref, vec_ref,
                    o_ref, *, seq, nsub):
    ch = D_CHUNK
    bw = BRANCH_WIDTH
    nh = D_HEADS
    hs = D_HEAD_SIZE
    r_i = _iota((bw, bw), 0)
    c_i = _iota((bw, bw), 1)
    same = (r_i // hs) == (c_i // hs)
    strict = c_i < r_i
    incl = c_i <= r_i
    eye = r_i == c_i
    ones_bd = _group_matrix(bw, hs, 1.0)
    avg_bd = _group_matrix(bw, hs, 1.0 / hs)
    rows = nsub * ch
    tr_i = _iota((rows, rows), 0)
    tc_i = _iota((rows, rows), 1)
    tri = jnp.where(jnp.logical_and(tr_i // ch == tc_i // ch, tc_i <= tr_i), 1.0, 0.0).astype(BF16)
    row0 = _iota((rows, bw), 0) == 0
    row0_low = _iota((rows, LANES), 0) == 0
    vec = vec_ref[...]
    w0, a0, k_k, k_a, r_k, gn_w, gn_b = (vec[n:n + 1, :] for n in range(7))
    mu = mu_ref[...]
    w2, a2, g2 = w2_ref[...], a2_ref[...], g2_ref[...]

    def wide(x):
        return jnp.where(same, jnp.concatenate([x] * nh, 0), 0.0)

    def shifted(ref, start, first_row, m):
        cur = ref[pl.ds(start, rows), :]
        last8 = ref[pl.ds(pl.multiple_of(jnp.maximum(start - 8, 0), 8), 8), :]
        prev_row = last8[7:8, :] * jnp.where(start > 0, 1.0, 0.0)
        prev = jnp.where(first_row, prev_row, pltpu.roll(cur, 1, 0))
        return cur + (prev - cur) * m

    def body(c, st):
        start = pl.multiple_of(c * rows, rows)
        r = shifted(r_ref, start, row0, mu[0:1, :])
        k = shifted(k_ref, start, row0, mu[1:2, :])
        v = shifted(v_ref, start, row0, mu[2:3, :])
        low = shifted(low_ref, start, row0_low, mulow_ref[...])
        w_raw = -_softplus(-(w0 + _dot3(jnp.tanh(low), w2))) - 0.5
        lw = -jnp.exp(w_raw)
        a = _sigmoid(a0 + _dot3(low, a2))
        g = _dot3(_sigmoid(low), g2)
        kk = k * k_k
        kk = kk / jnp.maximum(jnp.sqrt(_dot_rhs_exact(kk * kk, ones_bd)), 1e-12)
        k2 = k * (1.0 + (a - 1.0) * k_a)
        bonus = _dot_rhs_exact(r * k2 * r_k, ones_bd) * v
        b = kk * a

        cum = _dot_lhs_exact(tri, lw)
        e_in = jnp.exp(cum)
        e_inv = jnp.exp(-cum)
        a_dec = -kk * jnp.exp(cum - lw)
        r_dec = r * e_in
        b_inv = b * e_inv
        k_inv = k2 * e_inv
        local = [chunk_terms(*(z[s * ch:(s + 1) * ch] for z in (a_dec, r_dec, b_inv, k_inv, v, e_in)))
                 for s in range(nsub)]
        ys = []
        for atp, vp, xr, m_rb, y_kv, bh, h_kv, g_col in local:
            st16 = st.astype(BF16)
            u16 = (_dot(atp, st16) + vp).astype(BF16)
            y_w = _dot(xr, st16) + _dot(m_rb, u16) + y_kv
            st = g_col * st + _dotg(bh, u16, TN) + h_kv
            y = y_w[0:ch]
            for h in range(1, nh):
                y = y + y_w[h * ch:(h + 1) * ch]
            ys.append(y)
        y = jnp.concatenate(ys, 0)
        mean = _dot_rhs_exact(y, avg_bd)
        yc = y - mean
        var = _dot_rhs_exact(yc * yc, avg_bd)
        yn = yc * lax.rsqrt(var + D_GN_EPS) * gn_w + gn_b
        o_ref[pl.ds(start, rows), :] = (yn + bonus) * g
        return st

    def chunk_terms(a_dec, r_dec, b_inv, k_inv, v, e_in):
        g_last = e_in[ch - 1:ch, :]
        xa = wide(a_dec).astype(BF16)
        xr = wide(r_dec).astype(BF16)
        yb = wide(b_inv).astype(BF16)
        yk = wide(k_inv).astype(BF16)
        bh = wide(b_inv * g_last).astype(BF16)
        kh = wide(k_inv * g_last).astype(BF16)
        vw = wide(v).astype(BF16)

        a_ab = jnp.where(strict, _dotg(xa, yb, NT), 0.0)
        a_ak = jnp.where(strict, _dotg(xa, yk, NT), 0.0).astype(BF16)
        m_rb = jnp.where(incl, _dotg(xr, yb, NT), 0.0).astype(BF16)
        m_rk = jnp.where(incl, _dotg(xr, yk, NT), 0.0).astype(BF16)

        t_inv = jnp.where(eye, 1.0, a_ab)
        apow = a_ab
        for _ in range(int(math.log2(ch)) - 1):
            ab16 = apow.astype(BF16)
            apow = _dot(ab16, ab16)
            t_inv = t_inv + _dot(t_inv.astype(BF16), apow.astype(BF16))
        t16 = t_inv.astype(BF16)

        atp = _dot(t16, xa).astype(BF16)
        vp = _dot(t16, _dot(a_ak, vw).astype(BF16))
        y_kv = _dot(m_rk, vw)
        h_kv = _dotg(kh, vw, TN)
        g_col = jnp.sum(jnp.where(eye, g_last, 0.0), -1, keepdims=True)
        return atp, vp, xr, m_rb, y_kv, bh, h_kv, g_col

    lax.fori_loop(0, seq // rows, body, jnp.zeros((bw, bw), F32))


def _mixer_d(h, mu3, mulow, w2p, a2p, g2p, vec, bsz, seq, nsub=4):
    assert seq % (nsub * D_CHUNK) == 0 and D_CHUNK == D_HEAD_SIZE

    def piece(name):
        off, w = _PIECE_OFF[name]
        return pl.BlockSpec((seq, w * LANES), lambda b, o=off // w: (b, o))

    def const(a):
        return pl.BlockSpec(a.shape, lambda b: (0, 0))

    return pl.pallas_call(
        functools.partial(_mixer_d_kernel, seq=seq, nsub=nsub),
        grid=(bsz,),
        in_specs=[piece("d_r"), piece("d_k"), piece("d_v"), piece("d_low"),
                  const(mu3), const(mulow), const(w2p), const(a2p), const(g2p), const(vec)],
        out_specs=pl.BlockSpec((seq, BRANCH_WIDTH), lambda b: (b, 0)),
        out_shape=jax.ShapeDtypeStruct((bsz * seq, BRANCH_WIDTH), F32),
        compiler_params=_params("parallel"),
        name="mixer_d",
    )(h, h, h, h, mu3, mulow, w2p, a2p, g2p, vec)


def _merge_kernel(x_ref, ya_ref, yb_ref, yc_ref, yd_ref, wg_ref, wb_ref, wo_ref, g_ref, b_ref, o_ref,
                  *, alpha):
    x = x_ref[...]
    xb = x.astype(BF16)
    merged = None
    for n, y_ref in enumerate((ya_ref, yb_ref, yc_ref, yd_ref)):
        gate = _sigmoid(_dot(xb, wg_ref[:, n * D_MODEL:(n + 1) * D_MODEL]))
        term = gate * _dot(y_ref[...].astype(BF16), wb_ref[n])
        merged = term if merged is None else merged + term
    z = alpha * x + _dot(merged.astype(BF16), wo_ref[...])
    o_ref[...] = _layer_norm(z, g_ref[...], b_ref[...])


def _merge(x2d, ys, wg, wb, wo, g, b, alpha, tm):
    n, d = x2d.shape

    def const(a):
        nd = a.ndim
        return pl.BlockSpec(a.shape, lambda i: (0,) * nd)

    yspec = pl.BlockSpec((tm, BRANCH_WIDTH), lambda i: (i, 0))
    return pl.pallas_call(
        functools.partial(_merge_kernel, alpha=alpha),
        grid=(n // tm,),
        in_specs=[pl.BlockSpec((tm, d), lambda i: (i, 0)), yspec, yspec, yspec, yspec,
                  const(wg), const(wb), const(wo), const(g), const(b)],
        out_specs=pl.BlockSpec((tm, d), lambda i: (i, 0)),
        out_shape=jax.ShapeDtypeStruct((n, d), F32),
        compiler_params=_params("parallel"),
        name="merge",
    )(x2d, *ys, wg, wb, wo, g, b)


def _moe_kernel(x_ref, ltri_ref, wr_ref, br_ref, wg_ref, wu_ref, wd_ref, g_ref, b_ref, o_ref,
                xs_ref, gs_ref, ys_ref, pt_ref, seg_ref, *, alpha):
    e = pl.program_id(1)
    tm = x_ref.shape[0]
    nrow = xs_ref.shape[0]
    ck = MOE_CHUNK
    lane = _iota((tm, LANES), 1)

    @pl.when(e == 0)
    def _():
        x = x_ref[...]
        xb = x.astype(BF16)
        ys_ref[...] = jnp.zeros_like(ys_ref)
        x_lo = (x - xb.astype(F32)).astype(BF16)
        cross = _dot(jnp.concatenate([xb, x_lo], 0), wr_ref[...])
        logits = (cross[0:tm, 0:LANES] + (cross[0:tm, LANES:] + cross[tm:, 0:LANES])) + br_ref[...]
        is_g = jnp.logical_and(lane >= MOE_EXPERTS, lane < MOE_EXPERTS + MOE_GROUPS)
        lg = jnp.where(is_g, logits, -jnp.inf)
        gmax = jnp.max(lg, -1, keepdims=True)
        ptop = 1.0 / jnp.sum(jnp.exp(lg - gmax), -1, keepdims=True)
        gsel = jnp.min(jnp.where(lg == gmax, lane, 2 * LANES), -1, keepdims=True) - MOE_EXPERTS
        in_group = jnp.logical_and(lane < MOE_EXPERTS, lane // MOE_PER_GROUP == gsel)
        le = jnp.where(in_group, logits, -jnp.inf)
        v1 = jnp.max(le, -1, keepdims=True)
        i1 = jnp.min(jnp.where(le == v1, lane, 2 * LANES), -1, keepdims=True)
        le2 = jnp.where(lane == i1, -jnp.inf, le)
        v2 = jnp.max(le2, -1, keepdims=True)
        i2 = jnp.min(jnp.where(le2 == v2, lane, 2 * LANES), -1, keepdims=True)
        e2 = jnp.exp(v2 - v1)
        w1 = ptop / (1.0 + e2)
        gate = jnp.where(lane == i1, w1, jnp.where(lane == i2, w1 * e2, 0.0))

        onehot = jnp.where(lane == gsel, 1.0, 0.0)
        rank = _dot(ltri_ref[...], onehot.astype(BF16))
        cnt = jnp.sum(onehot, 0, keepdims=True).astype(I32)
        nch = jnp.right_shift(cnt + (ck - 1), int(math.log2(ck)))
        lane1 = _iota((1, LANES), 1)
        off = jnp.int32(0)
        off_vec = jnp.zeros((1, LANES), I32)
        for grp in range(MOE_GROUPS):
            n_g = nch[0, grp]
            seg_ref[grp] = off
            seg_ref[MOE_GROUPS + grp] = n_g
            off_vec = jnp.where(lane1 == grp, off, off_vec)
            off = off + n_g * ck
        dest = jnp.sum(onehot * (rank + off_vec.astype(F32)), -1, keepdims=True)
        pt_ref[...] = jnp.where(_iota((tm, nrow), 1) == dest.astype(I32), 1.0, 0.0).astype(BF16)
        dest_row = jnp.broadcast_to(dest, (tm, LANES)).T[0:1, :].astype(I32)
        perm = jnp.where(_iota((nrow, tm), 0) == dest_row, 1.0, 0.0).astype(BF16)
        gate_hi, gate_lo = _split2(gate)
        moved = _dot(perm, jnp.concatenate([xb, gate_hi, gate_lo], 1))
        d = x_ref.shape[1]
        xs_ref[...] = moved[:, 0:d].astype(BF16)
        gs_ref[...] = moved[:, d:d + LANES] + moved[:, d + LANES:]

    grp = e // (MOE_PER_GROUP // 2)
    seg_off = seg_ref[grp]
    seg_chunks = seg_ref[MOE_GROUPS + grp]

    def run_experts(start, size):
        rows = pl.ds(pl.multiple_of(start, ck), size)
        xc = xs_ref[rows, :]
        gs = gs_ref[rows, :]
        lane_s = _iota((size, LANES), 1)
        g0 = jnp.sum(jnp.where(lane_s == 2 * e, gs, 0.0), -1, keepdims=True)
        g1 = jnp.sum(jnp.where(lane_s == 2 * e + 1, gs, 0.0), -1, keepdims=True)
        hidden = []
        for k, gk in enumerate((g0, g1)):
            hg = _dot(xc, wg_ref[k])
            hu = _dot(xc, wu_ref[k])
            hidden.append((hg * _sigmoid(hg) * hu * gk).astype(BF16))
        wd = wd_ref[...].reshape(2 * MOE_HIDDEN, wd_ref.shape[2])
        ys_ref[rows, :] += _dot(jnp.concatenate(hidden, 1), wd)

    big = 4

    def quad(c, carry):
        run_experts(seg_off + c * (big * ck), big * ck)
        return carry

    lax.fori_loop(0, seg_chunks // big, quad, 0)
    rem = seg_chunks % big
    rem_start = seg_off + (seg_chunks - rem) * ck
    for r in range(1, big):
        @pl.when(rem == r)
        def _(r=r):
            run_experts(rem_start, r * ck)

    @pl.when(e == pl.num_programs(1) - 1)
    def _():
        z = alpha * x_ref[...] + _dot(pt_ref[...], ys_ref[...].astype(BF16))
        o_ref[...] = _layer_norm(z, g_ref[...], b_ref[...])


def _moe(x2d, wr, br, wg, wu, wd, g, b, alpha, tm):
    n, d = x2d.shape
    ne = wg.shape[0]
    hid = wg.shape[2]
    nrow = tm + MOE_GROUPS * MOE_CHUNK
    ltri = jnp.tril(jnp.ones((tm, tm), BF16), -1)

    def const(a):
        return pl.BlockSpec(a.shape, lambda i, e: (0, 0))

    return pl.pallas_call(
        functools.partial(_moe_kernel, alpha=alpha),
        grid=(n // tm, ne // 2),
        in_specs=[pl.BlockSpec((tm, d), lambda i, e: (i, 0)), const(ltri), const(wr), const(br),
                  pl.BlockSpec((2, d, hid), lambda i, e: (e, 0, 0)),
                  pl.BlockSpec((2, d, hid), lambda i, e: (e, 0, 0)),
                  pl.BlockSpec((2, hid, d), lambda i, e: (e, 0, 0)),
                  const(g), const(b)],
        out_specs=pl.BlockSpec((tm, d), lambda i, e: (i, 0)),
        out_shape=jax.ShapeDtypeStruct((n, d), F32),
        scratch_shapes=[pltpu.VMEM((nrow, d), BF16), pltpu.VMEM((nrow, LANES), F32),
                        pltpu.VMEM((nrow, d), F32), pltpu.VMEM((tm, nrow), BF16),
                        pltpu.SMEM((2 * MOE_GROUPS,), I32)],
        compiler_params=_params("parallel", "arbitrary"),
        name="moe",
    )(x2d, ltri, wr, br, wg, wu, wd, g, b)


def _t5_bucket(dist):
    exact = RPB_BUCKETS // 2
    d = jnp.maximum(dist, 0)
    df = jnp.maximum(d, 1).astype(F32)
    large = exact + (jnp.log(df / exact) / math.log(RPB_MAX_DIST / exact)
                     * (RPB_BUCKETS - exact)).astype(I32)
    large = jnp.minimum(large, RPB_BUCKETS - 1)
    return jnp.where(d < exact, d, large)


def _rpb_lookup(tab, dist):
    bucket = _t5_bucket(dist)[..., None]
    out = jnp.zeros(dist.shape + (tab.shape[1],), F32)
    for k in range(RPB_BUCKETS):
        out = jnp.where(bucket == k, tab[k], out)
    return out


def _bias_a(tab):
    qi = jnp.arange(A_BLOCK)[:, None]
    kj = jnp.arange(2 * A_BLOCK)[None, :]
    rdist = qi + A_BLOCK - kj
    out = []
    for window, dilation in A_PATTERNS:
        in_band = (rdist >= 0) & (rdist <= window // dilation)
        bias = jnp.transpose(_rpb_lookup(tab, rdist * dilation), (2, 0, 1))
        out.append(jnp.where(in_band[None], bias, NEG))
    return jnp.stack(out).astype(F32)


def _bias_b(tab, nblk):
    qi = jnp.arange(Q_BLOCK)[:, None]
    kj = jnp.arange(Q_BLOCK)[None, :]
    delta = jnp.arange(nblk)[:, None, None] * Q_BLOCK
    bias = _rpb_lookup(tab, delta + qi - kj)
    return jnp.transpose(bias, (0, 3, 2, 1)).astype(F32)


def _in_offsets():
    splits = (("a_q", 256), ("a_k", 256), ("a_v", 256), ("b_q", 256), ("b_ckv", 64), ("b_iq", 256),
              ("b_ik", 32), ("b_iw", 8), ("c_q", 128), ("c_k", 128), ("c_v", 256), ("c_a", 16),
              ("c_g", 256), ("d", 832), ("gate", 4096))
    off, out = 0, {}
    for name, w in splits:
        out[name] = (off, w)
        off += w
    return out


def _proj_columns():
    src = _in_offsets()
    d0 = src["d"][0]
    named = {
        "a_q": np.arange(256) + src["a_q"][0], "a_k": np.arange(256) + src["a_k"][0],
        "a_v": np.arange(256) + src["a_v"][0], "b_q": np.arange(256) + src["b_q"][0],
        "b_iq": np.arange(256) + src["b_iq"][0],
        "b_misc": np.concatenate([np.arange(B_LATENT) + src["b_ckv"][0], np.arange(IDX_DIM) + src["b_ik"][0],
                                  np.arange(IDX_HEADS) + src["b_iw"][0]]),
        "c_v": np.arange(256) + src["c_v"][0], "c_g": np.arange(256) + src["c_g"][0],
        "d_r": np.arange(256) + d0, "d_k": np.arange(256) + d0 + 256, "d_v": np.arange(256) + d0 + 512,
        "c_q": np.arange(128) + src["c_q"][0], "c_k": np.arange(128) + src["c_k"][0],
        "c_a": np.arange(16) + src["c_a"][0], "d_low": np.arange(64) + d0 + 768,
    }
    cols = []
    for name, w in _PIECES:
        c = named[name]
        cols.append(np.concatenate([c, -np.ones(w * LANES - len(c), np.int64)]))
    return np.concatenate(cols)


def _proj_weight(w):
    cols = _proj_columns()
    cuts = [0] + [k for k in range(1, len(cols))
                  if (cols[k] < 0) != (cols[k - 1] < 0) or (cols[k] >= 0 and cols[k] != cols[k - 1] + 1)]
    cuts.append(len(cols))
    parts = []
    for a, b in zip(cuts[:-1], cuts[1:]):
        if cols[a] < 0:
            parts.append(jnp.zeros((w.shape[0], b - a), BF16))
        else:
            parts.append(w[:, int(cols[a]):int(cols[a]) + (b - a)].astype(BF16))
    return jnp.concatenate(parts, 1)


def _pad_rows(w, first, total):
    return jnp.zeros((total, w.shape[1]), w.dtype).at[first:first + w.shape[0]].set(w)


def kernel(x, rpb_table, w_in, b_kv_gain, b_w_uv, c_a_up, c_a_bias, c_norm_gain, d_mu, d_w0, d_w2,
           d_a0, d_a2, d_g2, d_k_k, d_k_a, d_r_k, d_gn_w, d_gn_b, w_branch, w_out, ln_g, ln_b,
           router_g, router_g_bias, router_e, router_e_bias, moe_w_gate, moe_w_up, moe_w_down):
    bsz, seq, d_model = x.shape
    depth = w_in.shape[0]
    n = bsz * seq
    alpha = (2 * depth) ** 0.25
    gate_off = _in_offsets()["gate"][0]
    bias_a = _bias_a(rpb_table[:, :A_HEADS])
    bias_b = _bias_b(rpb_table[:, A_HEADS:], seq // Q_BLOCK)
    tm_proj = math.gcd(n, 512)
    tm_merge = math.gcd(n, 512)
    tm_moe = math.gcd(n, 1024)

    x2d = x.reshape(n, d_model)
    for l in range(depth):
        h = _project(x2d, _proj_weight(w_in[l]), tm_proj, PROJ_WIDTH)

        ya = _mixer_a(h, bias_a, bsz, seq)

        gain4 = jnp.tile(b_kv_gain[l], B_HEADS)[None, :]
        wuv_t = jnp.transpose(b_w_uv[l], (0, 2, 1)).astype(BF16)
        yb = _mixer_b(h, gain4, wuv_t, bias_b, bsz, seq)

        aup_pad = _pad_rows(c_a_up[l], 0, LANES)
        yc = _mixer_c(h, aup_pad, c_a_bias[l][None, :], c_norm_gain[l][None, :], bsz, seq)

        mu = d_mu[l]
        mu3 = mu[:3 * BRANCH_WIDTH].reshape(3, BRANCH_WIDTH)
        mulow = jnp.zeros((1, LANES), F32).at[0, :D_DECAY_RANK + D_ICLR_RANK + D_GATE_RANK].set(
            mu[3 * BRANCH_WIDTH:])
        w2p = _pad_rows(d_w2[l], 0, LANES)
        a2p = _pad_rows(d_a2[l], D_DECAY_RANK, LANES)
        g2p = _pad_rows(d_g2[l], D_DECAY_RANK + D_ICLR_RANK, LANES)
        vec = jnp.stack([d_w0[l], d_a0[l], d_k_k[l], d_k_a[l], d_r_k[l], d_gn_w[l], d_gn_b[l],
                         jnp.zeros_like(d_w0[l])])
        yd = _mixer_d(h, mu3, mulow, w2p, a2p, g2p, vec, bsz, seq)

        wg = w_in[l][:, gate_off:gate_off + N_BRANCHES * d_model].astype(BF16)
        x2d = _merge(x2d, (ya, yb, yc, yd), wg, w_branch[l].astype(BF16), w_out[l].astype(BF16),
                     ln_g[l, 0][None, :], ln_b[l, 0][None, :], alpha, tm_merge)

        wr = jnp.zeros((d_model, LANES), F32)
        wr = wr.at[:, :MOE_EXPERTS].set(router_e[l]).at[:, MOE_EXPERTS:MOE_EXPERTS + MOE_GROUPS].set(
            router_g[l])
        br = jnp.zeros((1, LANES), F32)
        br = br.at[0, :MOE_EXPERTS].set(router_e_bias[l]).at[0, MOE_EXPERTS:MOE_EXPERTS + MOE_GROUPS].set(
            router_g_bias[l])
        wr = jnp.concatenate(_split2(wr), 1)
        x2d = _moe(x2d, wr, br, moe_w_gate[l].astype(BF16), moe_w_up[l].astype(BF16),
                   moe_w_down[l].astype(BF16), ln_g[l, 1][None, :], ln_b[l, 1][None, :], alpha, tm_moe)
    return x2d.reshape(bsz, seq, d_model)
```

```python
import functools
import math

import jax
import jax.numpy as jnp
import numpy as np
from jax import lax
from jax.experimental import pallas as pl
from jax.experimental.pallas import tpu as pltpu

F32 = jnp.float32
BF16 = jnp.bfloat16
I32 = jnp.int32

D_MODEL = 1024
N_BRANCHES = 4
BRANCH_WIDTH = 256
HEAD_DIM = 64
A_HEADS = 4
A_PATTERNS = ((128, 1), (512, 4), (2048, 16))
A_BLOCK = 128
B_HEADS = 4
B_LATENT = 64
IDX_HEADS = 8
IDX_DIM = 32
TOPK_MAX = 256
TOPK_DIV = 4
Q_BLOCK = 128
C_HEADS = 4
C_KEY_DIM = 32
C_VAL_DIM = 64
C_GATE_RANK = 16
C_GATE_TAU = 16.0
C_CHUNK = 64
D_HEADS = 4
D_HEAD_SIZE = 64
D_DECAY_RANK = 16
D_ICLR_RANK = 16
D_GATE_RANK = 32
D_GN_EPS = 64e-5
D_CHUNK = 64
RPB_BUCKETS = 32
RPB_MAX_DIST = 2048
MOE_GROUPS = 4
MOE_PER_GROUP = 8
MOE_EXPERTS = 32
MOE_HIDDEN = 256
MOE_CHUNK = 128
LN_EPS = 1e-5

LANES = 128
NEG = -1e30
VMEM_LIMIT = 56 * 1024 * 1024

NT = (((1,), (1,)), ((), ()))
TN = (((0,), (0,)), ((), ()))

_PIECES = (
    ("a_q", 2), ("a_k", 2), ("a_v", 2), ("b_q", 2), ("b_iq", 2),
    ("c_v", 2), ("c_g", 2), ("d_r", 2), ("d_k", 2), ("d_v", 2),
    ("c_q", 1), ("c_k", 1), ("b_misc", 1), ("c_a", 1), ("d_low", 1),
)
_PIECE_OFF = {}
_off = 0
for _n, _w in _PIECES:
    _PIECE_OFF[_n] = (_off, _w)
    _off += _w
PROJ_WIDTH = _off * LANES


def _dot(a, b):
    return jnp.dot(a, b, preferred_element_type=F32)


def _dotg(a, b, dims):
    return lax.dot_general(a, b, dims, preferred_element_type=F32)


def _split2(x):
    hi = x.astype(BF16)
    lo = (x - hi.astype(F32)).astype(BF16)
    return hi, lo


def _dot_rhs_exact(x, m):
    hi, lo = _split2(x)
    return _dot(hi, m) + _dot(lo, m)


def _dot_lhs_exact(m, x):
    hi, lo = _split2(x)
    return _dot(m, hi) + _dot(m, lo)


def _dot3(x, w):
    xh, xl = _split2(x)
    wh, wl = _split2(w)
    return _dot(xh, wh) + (_dot(xl, wh) + _dot(xh, wl))


def _sigmoid(x):
    return 1.0 / (1.0 + jnp.exp(-x))


def _softplus(x):
    return jnp.maximum(x, 0.0) + jnp.log(1.0 + jnp.exp(-jnp.abs(x)))


def _iota(shape, axis):
    return lax.broadcasted_iota(I32, shape, axis)


def _group_matrix(n, group, value, dtype=BF16):
    r = _iota((n, n), 0) // group
    c = _iota((n, n), 1) // group
    return jnp.where(r == c, value, 0.0).astype(dtype)


def _layer_norm(z, g, b):
    mu = jnp.mean(z, -1, keepdims=True)
    zc = z - mu
    var = jnp.mean(zc * zc, -1, keepdims=True)
    return zc * lax.rsqrt(var + LN_EPS) * g + b


def _params(*sem):
    return pltpu.CompilerParams(dimension_semantics=sem, vmem_limit_bytes=VMEM_LIMIT)


def _proj_kernel(x_ref, w_ref, o_ref):
    o_ref[...] = _dot(x_ref[...].astype(BF16), w_ref[...])


def _project(x2d, w, tm, tn):
    n, k = x2d.shape
    wt = w.shape[1]
    return pl.pallas_call(
        _proj_kernel,
        grid=(n // tm, wt // tn),
        in_specs=[pl.BlockSpec((tm, k), lambda i, j: (i, 0)),
                  pl.BlockSpec((k, tn), lambda i, j: (0, j))],
        out_specs=pl.BlockSpec((tm, tn), lambda i, j: (i, j)),
        out_shape=jax.ShapeDtypeStruct((n, wt), F32),
        compiler_params=_params("parallel", "parallel"),
        name="proj",
    )(x2d, w)


def _mixer_a_kernel(q0_ref, q1_ref, k0_ref, k1_ref, v0_ref, v1_ref, bias_ref, o_ref, op_ref, lse_ref,
                    *, seq, dils):
    q_refs, k_refs, v_refs = (q0_ref, q1_ref), (k0_ref, k1_ref), (v0_ref, v1_ref)
    ab = A_BLOCK
    lane = _iota((ab, LANES), 1)
    upper = lane >= HEAD_DIM
    col = _iota((ab, 2 * ab), 1)
    scale = HEAD_DIM ** -0.5

    for p, d in enumerate(dils):
        per_res = seq // (d * ab)

        def body(idx, carry, p=p, d=d, per_res=per_res):
            r = idx // per_res
            b = idx % per_res
            start = b * (ab * d) + r
            pstart = jnp.maximum(start - ab * d, r)
            if d == 1:
                rows, prow = pl.ds(pl.multiple_of(start, ab), ab), pl.ds(pl.multiple_of(pstart, ab), ab)
            else:
                rows, prow = pl.ds(start, ab, stride=d), pl.ds(pstart, ab, stride=d)
            first_pen = jnp.where(b == 0, NEG, 0.0)
            pen = jnp.where(col < ab, first_pen, 0.0)
            for hp in range(2):
                q_ref, k_ref, v_ref = q_refs[hp], k_refs[hp], v_refs[hp]
                q = q_ref[rows, :]
                kb = jnp.concatenate([k_ref[prow, :], k_ref[rows, :]], 0).astype(BF16)
                vb = jnp.concatenate([v_ref[prow, :], v_ref[rows, :]], 0).astype(BF16)
                outs, lses = [], []
                for hh in range(2):
                    hm = upper if hh else jnp.logical_not(upper)
                    qm = jnp.where(hm, q, 0.0).astype(BF16)
                    s = _dotg(qm, kb, NT) * scale + bias_ref[p, 2 * hp + hh] + pen
                    mx = jnp.max(s, -1, keepdims=True)
                    e = jnp.exp(s - mx)
                    den = jnp.sum(e, -1, keepdims=True)
                    outs.append(_dot(e.astype(BF16), vb) / den)
                    lses.append(mx + jnp.log(den))
                op_ref[2 * p + hp, rows, :] = jnp.where(upper, outs[1], outs[0])
                lse_ref[2 * p + hp, rows, :] = jnp.where(upper, lses[1], lses[0])
            return carry

        def body_pair(t, carry, body=body):
            return body(2 * t + 1, body(2 * t, carry))

        lax.fori_loop(0, seq // (2 * ab), body_pair, 0)

    cr = 256

    def combine(c, carry):
        rows = pl.ds(pl.multiple_of(c * cr, cr), cr)
        for hp in range(2):
            ls = [lse_ref[2 * p + hp, rows, :] for p in range(len(dils))]
            mx = functools.reduce(jnp.maximum, ls)
            es = [jnp.exp(l - mx) for l in ls]
            num = functools.reduce(lambda a, b: a + b,
                                   [e * op_ref[2 * p + hp, rows, :] for p, e in enumerate(es)])
            o_ref[rows, hp * LANES:(hp + 1) * LANES] = num / functools.reduce(lambda a, b: a + b, es)
        return carry

    lax.fori_loop(0, seq // cr, combine, 0)


def _mixer_a(h, bias, bsz, seq):
    dils = tuple(d for _, d in A_PATTERNS)
    for w, d in A_PATTERNS:
        assert w // d == A_BLOCK and seq % (d * A_BLOCK) == 0
    npat = len(dils)

    def pieces(name):
        off, w = _PIECE_OFF[name]
        return [pl.BlockSpec((seq, LANES), lambda b, o=off + u: (b, o)) for u in range(w)]

    return pl.pallas_call(
        functools.partial(_mixer_a_kernel, seq=seq, dils=dils),
        grid=(bsz,),
        in_specs=pieces("a_q") + pieces("a_k") + pieces("a_v")
        + [pl.BlockSpec(bias.shape, lambda b: (0, 0, 0, 0))],
        out_specs=pl.BlockSpec((seq, BRANCH_WIDTH), lambda b: (b, 0)),
        out_shape=jax.ShapeDtypeStruct((bsz * seq, BRANCH_WIDTH), F32),
        scratch_shapes=[pltpu.VMEM((2 * npat, seq, LANES), F32),
                        pltpu.VMEM((2 * npat, seq, LANES), F32)],
        compiler_params=_params("parallel"),
        name="mixer_a",
    )(h, h, h, h, h, h, bias)


def _mixer_b_kernel(q_ref, iq_ref, mk_ref, gain_ref, wuvt_ref, bias_ref, o_ref,
                    ckvn_ref, ckvt_ref, ikb_ref, sc_ref, sc16_ref, lg_ref, iqs_ref, qs_ref, *, seq, keep):
    qb = Q_BLOCK
    sb = 2 * qb
    i = pl.program_id(1)
    iw_lane = B_LATENT + IDX_DIM

    @pl.when(i == 0)
    def _():
        src = _iota((LANES, BRANCH_WIDTH), 0)
        dst = _iota((LANES, BRANCH_WIDTH), 1)
        is_lat = src < B_LATENT
        rep_lat = jnp.where(jnp.logical_and(is_lat, dst % B_LATENT == src), 1.0, 0.0).astype(BF16)
        avg_lat = jnp.where(is_lat, 1.0 / B_LATENT, 0.0).astype(BF16)
        is_ik = jnp.logical_and(src >= B_LATENT, src < iw_lane)
        rep_ik = jnp.where(jnp.logical_and(is_ik, dst % IDX_DIM == src - B_LATENT), 1.0, 0.0).astype(BF16)

        def prep(c, carry):
            rows = pl.ds(pl.multiple_of(c * sb, sb), sb)
            x = mk_ref[rows, :]
            hi, lo = _split2(x)
            lo2 = (x - hi.astype(F32) - lo.astype(F32)).astype(BF16)
            lat = _dot(hi, rep_lat) + (_dot(lo, rep_lat) + _dot(lo2, rep_lat))
            ms = _dot_rhs_exact(x * x, avg_lat)
            xn = lat * lax.rsqrt(ms + 1e-6) * gain_ref[...]
            ckvn_ref[rows, :] = xn.astype(BF16)
            ckvt_ref[c] = xn.T[0:B_LATENT, :].astype(BF16)
            ikb_ref[rows, :] = _dot(hi, rep_ik).astype(BF16)
            return carry

        lax.fori_loop(0, seq // sb, prep, 0)

    lane = _iota((qb, BRANCH_WIDTH), 1)
    qrows = pl.ds(pl.multiple_of(i * qb, qb), qb)
    iq = iq_ref[qrows, :] * (IDX_DIM ** -0.5)
    for h in range(IDX_HEADS):
        iqs_ref[h * qb:(h + 1) * qb, :] = jnp.where(lane // IDX_DIM == h, iq, 0.0).astype(BF16)
    q = q_ref[qrows, :] * (B_LATENT ** -0.5)
    for h in range(B_HEADS):
        qs_ref[h * qb:(h + 1) * qb, :] = jnp.where(lane // B_LATENT == h, q, 0.0).astype(BF16)
    iw_t = (mk_ref[qrows, :] * (IDX_HEADS ** -0.5)).T[iw_lane:iw_lane + IDX_HEADS, :]
    krow = _iota((sb, qb), 0)
    qcol = _iota((sb, qb), 1)
    nsb = (i + 2) // 2

    def fold(x, op):
        parts = [x[r:r + 8] for r in range(0, sb, 8)]
        while len(parts) > 1:
            parts = [op(parts[k], parts[k + 1]) for k in range(0, len(parts), 2)]
        return parts[0]

    def for_steps(body, init):
        def pair(p, carry):
            return body(2 * p + 1, body(2 * p, carry))

        carry = lax.fori_loop(0, nsb // 2, pair, init)
        return lax.cond(nsb % 2 == 1, lambda c: body(nsb - 1, c), lambda c: c, carry)

    def score_body(j, carry):
        kr = pl.ds(pl.multiple_of(j * sb, sb), sb)
        rel = _dotg(ikb_ref[kr, :], iqs_ref[...], NT)
        sc = jnp.maximum(rel[:, 0:qb], 0.0) * iw_t[0:1, :]
        for h in range(1, IDX_HEADS):
            sc = sc + jnp.maximum(rel[:, h * qb:(h + 1) * qb], 0.0) * iw_t[h:h + 1, :]
        sc = jnp.where(krow <= qcol + (i * qb - j * sb), sc, -jnp.inf)
        sc_ref[j] = sc
        sc16_ref[j] = pltpu.bitcast(pltpu.bitcast(sc, I32) & jnp.int32(-65536), F32).astype(BF16)
        logits = _dotg(ckvn_ref[kr, :], qs_ref[...], NT)
        d0 = i - 2 * j
        for h in range(B_HEADS):
            lg = logits[:, h * qb:(h + 1) * qb]
            lg_ref[h, j, 0:qb, :] = lg[0:qb] + bias_ref[d0, h]
            lg_ref[h, j, qb:sb, :] = lg[qb:sb] + bias_ref[jnp.maximum(d0 - 1, 0), h]
        return carry

    for_steps(score_body, 0)

    sign = jnp.int32(-2 ** 31)

    def key_to_float(u):
        k = u ^ sign
        bits = k ^ ((k >> 31) & jnp.int32(0x7FFFFFFF))
        return pltpu.bitcast(bits, F32)

    def count(pred):
        def cnt(j, acc):
            return acc + fold(pred(j), jnp.add)
        acc = for_steps(cnt, jnp.zeros((8, qb), F32))
        return jnp.sum(acc, 0, keepdims=True)

    pk = 16

    def count_ge16(cf):
        cf = pltpu.bitcast(pltpu.bitcast(cf, I32) & jnp.int32(-65536), F32)
        c16 = jnp.broadcast_to(cf, (pk, qb)).astype(BF16)
        one, zero = jnp.ones((pk, qb), BF16), jnp.zeros((pk, qb), BF16)

        def cnt(j, acc):
            x = sc16_ref[j]
            parts = [jnp.where(x[r:r + pk] >= c16, one, zero) for r in range(0, sb, pk)]
            while len(parts) > 1:
                parts = [parts[k] + parts[k + 1] for k in range(0, len(parts), 2)]
            return acc + parts[0].astype(F32)

        acc = for_steps(cnt, jnp.zeros((pk, qb), F32))
        return jnp.sum(acc, 0, keepdims=True)

    def bit_body(t, state, packed=False):
        u, n_u = state
        cand = u | jnp.left_shift(jnp.int32(1), 31 - t)
        cf = key_to_float(cand)
        c = count_ge16(cf) if packed else count(lambda j: jnp.where(sc_ref[j] >= cf, 1.0, 0.0))
        take = c >= keep
        return jnp.where(take, cand, u), jnp.where(take, c, n_u)

    def bit_body16(t, state):
        return bit_body(t, state, packed=True)

    first_bits, bit_group = 24, 4
    state = (jnp.zeros((1, qb), I32), (jnp.zeros((1, qb), I32) + nsb * sb).astype(F32))
    state = bit_body16(0, state)
    n_pos = count(lambda j: jnp.where(sc_ref[j] > 0.0, 1.0, 0.0))
    zero_thr = jnp.logical_and(state[0] != 0, n_pos < keep)
    state = lax.fori_loop(1, 16, bit_body16, state)
    u, n_ge = lax.fori_loop(16, first_bits, bit_body, state)

    def more_bits(s):
        t, _, n_u = s
        open_rows = jnp.where(zero_thr, 0.0, jnp.abs(n_u - keep))
        return jnp.logical_and(t < 32, jnp.max(open_rows) > 0.0)

    def bit_group_body(s):
        t, u, n_u = s
        for k in range(bit_group):
            u, n_u = bit_body(t + k, (u, n_u))
        return t + bit_group, u, n_u

    _, u, n_ge = lax.while_loop(more_bits, bit_group_body, (jnp.int32(first_bits), u, n_ge))
    u = jnp.maximum(u ^ sign, jnp.int32(0x007FFFFF - 2 ** 31)) ^ sign
    thr = key_to_float(u)
    n_gt = count(lambda j: jnp.where(sc_ref[j] > thr, 1.0, 0.0))
    need = keep - n_gt

    nbits = int(math.ceil(math.log2(seq))) + 1

    def cut_search():
        def cut_body(t, cut):
            cand = cut | jnp.left_shift(jnp.int32(1), nbits - 1 - t)
            c = count(lambda j: jnp.where(sc_ref[j] == thr,
                                          jnp.where(krow < cand - j * sb, 1.0, 0.0), 0.0))
            return jnp.where(c <= need, cand, cut)

        return lax.fori_loop(0, nbits, cut_body, jnp.zeros((1, qb), I32))

    surplus = jnp.max(n_ge) > keep
    cut = lax.cond(surplus, cut_search, lambda: jnp.full((1, qb), 2 ** nbits - 1, I32))
    cut = jnp.minimum(cut, i * qb + 1 + _iota((1, qb), 1))

    def mask_body(j, mx):
        sc = sc_ref[j]
        sel = jnp.where(sc > thr, 1.0, jnp.where(sc == thr, jnp.where(krow < cut - j * sb, 1.0, 0.0), 0.0))
        out = []
        for h in range(B_HEADS):
            s = jnp.where(sel > 0.5, lg_ref[h, j], NEG)
            lg_ref[h, j] = s
            out.append(jnp.maximum(mx[h], fold(s, jnp.maximum)))
        return tuple(out)

    mx = for_steps(mask_body, tuple(jnp.full((8, qb), NEG, F32) for _ in range(B_HEADS)))
    ms = [jnp.max(m, 0, keepdims=True) for m in mx]

    def att_body(j, carry):
        ls, acc = carry
        ps, new_ls = [], []
        for h in range(B_HEADS):
            pr = jnp.exp(lg_ref[h, j] - ms[h])
            new_ls.append(ls[h] + fold(pr, jnp.add))
            ps.append(pr.astype(BF16))
        upd = _dot(ckvt_ref[j], jnp.concatenate(ps, 1))
        return tuple(new_ls), acc + upd

    init = (tuple(jnp.zeros((8, qb), F32) for _ in range(B_HEADS)),
            jnp.zeros((B_LATENT, B_HEADS * qb), F32))
    ls, acc = for_steps(att_body, init)
    ls = [jnp.sum(l, 0, keepdims=True) for l in ls]
    o_t = (acc / jnp.concatenate(ls, 1)).astype(BF16)
    y_t = jnp.concatenate([_dot(wuvt_ref[h], o_t[:, h * qb:(h + 1) * qb]) for h in range(B_HEADS)], 0)
    o_ref[...] = y_t.T


def _mixer_b(h, gain4, wuv_t, bias, bsz, seq):
    qb = Q_BLOCK
    assert seq % 256 == 0
    nblk = seq // qb
    keep = min(TOPK_MAX, seq // TOPK_DIV)

    def kpiece(name):
        off, w = _PIECE_OFF[name]
        return pl.BlockSpec((seq, w * LANES), lambda b, i, o=off // w: (b, o))

    return pl.pallas_call(
        functools.partial(_mixer_b_kernel, seq=seq, keep=keep),
        grid=(bsz, nblk),
        in_specs=[kpiece("b_q"), kpiece("b_iq"), kpiece("b_misc"),
                  pl.BlockSpec(gain4.shape, lambda b, i: (0, 0)),
                  pl.BlockSpec(wuv_t.shape, lambda b, i: (0, 0, 0)),
                  pl.BlockSpec(bias.shape, lambda b, i: (0, 0, 0, 0))],
        out_specs=pl.BlockSpec((qb, BRANCH_WIDTH), lambda b, i: (b * nblk + i, 0)),
        out_shape=jax.ShapeDtypeStruct((bsz * seq, BRANCH_WIDTH), F32),
        scratch_shapes=[pltpu.VMEM((seq, BRANCH_WIDTH), BF16),
                        pltpu.VMEM((nblk // 2, B_LATENT, 2 * qb), BF16),
                        pltpu.VMEM((seq, BRANCH_WIDTH), BF16),
                        pltpu.VMEM((nblk // 2, 2 * qb, qb), F32),
                        pltpu.VMEM((nblk // 2, 2 * qb, qb), BF16),
                        pltpu.VMEM((B_HEADS, nblk // 2, 2 * qb, qb), F32),
                        pltpu.VMEM((IDX_HEADS * qb, BRANCH_WIDTH), BF16),
                        pltpu.VMEM((B_HEADS * qb, BRANCH_WIDTH), BF16)],
        compiler_params=_params("parallel", "arbitrary"),
        name="mixer_b",
    )(h, h, h, gain4, wuv_t, bias)


def _mixer_c_kernel(q_ref, k_ref, v_ref, g_ref, a_ref, aup_ref, abias_ref, gain_ref, o_ref, *, seq):
    ch = C_CHUNK
    kw = C_HEADS * C_KEY_DIM
    vw = C_HEADS * C_VAL_DIM
    tri = jnp.where(_iota((ch, ch), 1) <= _iota((ch, ch), 0), 1.0, 0.0).astype(BF16)
    causal = _iota((C_HEADS * ch, ch), 1) <= (_iota((C_HEADS * ch, ch), 0) % ch)
    klane_head = _iota((C_HEADS * ch, kw), 1) // C_KEY_DIM
    krow_head = _iota((C_HEADS * ch, kw), 0) // ch
    vlane_head = _iota((ch, vw), 1) // C_VAL_DIM
    st_mask = (_iota((vw, kw), 0) // C_VAL_DIM) == (_iota((vw, kw), 1) // C_KEY_DIM)
    gmat = _group_matrix(vw, C_VAL_DIM, 1.0 / C_VAL_DIM)
    aup = aup_ref[...]

    def body(c, st):
        rows = pl.ds(pl.multiple_of(c * ch, ch), ch)
        qc = q_ref[rows, :] * (C_KEY_DIM ** -0.5)
        kc = k_ref[rows, :]
        vc = v_ref[rows, :]
        z = _dot3(a_ref[rows, :], aup) + abias_ref[...]
        log_a = -_softplus(-z) / C_GATE_TAU
        cum = _dot_lhs_exact(tri, log_a)
        last = cum[ch - 1:ch, :]
        q_dec = qc * jnp.exp(cum)
        k_inv = (kc * jnp.exp(-cum)).astype(BF16)
        k_dec = (kc * jnp.exp(last - cum)).astype(BF16)
        vb = vc.astype(BF16)
        qd_b = q_dec.astype(BF16)
        q_stack = jnp.where(klane_head == krow_head, jnp.concatenate([q_dec] * C_HEADS, 0), 0.0)
        att = jnp.where(causal, _dotg(q_stack.astype(BF16), k_inv, NT), 0.0)
        full = _dot(att.astype(BF16), vb)
        o = _dotg(qd_b, st.astype(BF16), NT)
        for h in range(C_HEADS):
            o = o + jnp.where(vlane_head == h, full[h * ch:(h + 1) * ch], 0.0)
        upd = _dotg(vb, k_dec, TN)
        st = st * jnp.exp(last) + jnp.where(st_mask, upd, 0.0)
        ms = _dot_rhs_exact(o * o, gmat)
        o = o * lax.rsqrt(ms + 1e-6) * gain_ref[...]
        g = g_ref[rows, :]
        o_ref[rows, :] = g * _sigmoid(g) * o
        return st

    def body_pair(t, st):
        return body(2 * t + 1, body(2 * t, st))

    lax.fori_loop(0, seq // (2 * ch), body_pair, jnp.zeros((vw, kw), F32))


def _mixer_c(h, aup_pad, abias, gain, bsz, seq):
    assert seq % C_CHUNK == 0

    def piece(name):
        off, w = _PIECE_OFF[name]
        return pl.BlockSpec((seq, w * LANES), lambda b, o=off // w: (b, o))

    def const(a):
        return pl.BlockSpec(a.shape, lambda b: (0, 0))

    return pl.pallas_call(
        functools.partial(_mixer_c_kernel, seq=seq),
        grid=(bsz,),
        in_specs=[piece("c_q"), piece("c_k"), piece("c_v"), piece("c_g"), piece("c_a"),
                  const(aup_pad), const(abias), const(gain)],
        out_specs=pl.BlockSpec((seq, BRANCH_WIDTH), lambda b: (b, 0)),
        out_shape=jax.ShapeDtypeStruct((bsz * seq, BRANCH_WIDTH), F32),
        compiler_params=_params("parallel"),
        name="mixer_c",
    )(h, h, h, h, h, aup_pad, abias, gain)


def _mixer_d_kernel(r_ref, k_ref, v_ref, low_ref, mu_ref, mulow_ref, w2_ref, a2_ref, g2_ref, vec_ref,
                    o_ref, *, seq, nsub):
    ch = D_CHUNK
    bw = BRANCH_WIDTH
    nh = D_HEADS
    hs = D_HEAD_SIZE
    r_i = _iota((bw, bw), 0)
    c_i = _iota((bw, bw), 1)
    same = (r_i // hs) == (c_i // hs)
    strict = c_i < r_i
    incl = c_i <= r_i
    eye = r_i == c_i
    ones_bd = _group_matrix(bw, hs, 1.0)
    avg_bd = _group_matrix(bw, hs, 1.0 / hs)
    rows = nsub * ch
    tr_i = _iota((rows, rows), 0)
    tc_i = _iota((rows, rows), 1)
    tri = jnp.where(jnp.logical_and(tr_i // ch == tc_i // ch, tc_i <= tr_i), 1.0, 0.0).astype(BF16)
    row0 = _iota((rows, bw), 0) == 0
    row0_low = _iota((rows, LANES), 0) == 0
    vec = vec_ref[...]
    w0, a0, k_k, k_a, r_k, gn_w, gn_b = (vec[n:n + 1, :] for n in range(7))
    mu = mu_ref[...]
    w2, a2, g2 = w2_ref[...], a2_ref[...], g2_ref[...]

    def wide(x):
        return jnp.where(same, jnp.concatenate([x] * nh, 0), 0.0)

    def shifted(ref, start, first_row, m):
        cur = ref[pl.ds(start, rows), :]
        last8 = ref[pl.ds(pl.multiple_of(jnp.maximum(start - 8, 0), 8), 8), :]
        prev_row = last8[7:8, :] * jnp.where(start > 0, 1.0, 0.0)
        prev = jnp.where(first_row, prev_row, pltpu.roll(cur, 1, 0))
        return cur + (prev - cur) * m

    def body(c, st):
        start = pl.multiple_of(c * rows, rows)
        r = shifted(r_ref, start, row0, mu[0:1, :])
        k = shifted(k_ref, start, row0, mu[1:2, :])
        v = shifted(v_ref, start, row0, mu[2:3, :])
        low = shifted(low_ref, start, row0_low, mulow_ref[...])
        w_raw = -_softplus(-(w0 + _dot3(jnp.tanh(low), w2))) - 0.5
        lw = -jnp.exp(w_raw)
        a = _sigmoid(a0 + _dot3(low, a2))
        g = _dot3(_sigmoid(low), g2)
        kk = k * k_k
        kk = kk / jnp.maximum(jnp.sqrt(_dot_rhs_exact(kk * kk, ones_bd)), 1e-12)
        k2 = k * (1.0 + (a - 1.0) * k_a)
        bonus = _dot_rhs_exact(r * k2 * r_k, ones_bd) * v
        b = kk * a

        cum = _dot_lhs_exact(tri, lw)
        e_in = jnp.exp(cum)
        e_inv = jnp.exp(-cum)
        a_dec = -kk * jnp.exp(cum - lw)
        r_dec = r * e_in
        b_inv = b * e_inv
        k_inv = k2 * e_inv
        local = [chunk_terms(*(z[s * ch:(s + 1) * ch] for z in (a_dec, r_dec, b_inv, k_inv, v, e_in)))
                 for s in range(nsub)]
        ys = []
        for atp, vp, xr, m_rb, y_kv, bh, h_kv, g_col in local:
            st16 = st.astype(BF16)
            u16 = (_dot(atp, st16) + vp).astype(BF16)
            y_w = _dot(xr, st16) + _dot(m_rb, u16) + y_kv
            st = g_col * st + _dotg(bh, u16, TN) + h_kv
            y = y_w[0:ch]
            for h in range(1, nh):
                y = y + y_w[h * ch:(h + 1) * ch]
            ys.append(y)
        y = jnp.concatenate(ys, 0)
        mean = _dot_rhs_exact(y, avg_bd)
        yc = y - mean
        var = _dot_rhs_exact(yc * yc, avg_bd)
        yn = yc * lax.rsqrt(var + D_GN_EPS) * gn_w + gn_b
        o_ref[pl.ds(start, rows), :] = (yn + bonus) * g
        return st

    def chunk_terms(a_dec, r_dec, b_inv, k_inv, v, e_in):
        g_last = e_in[ch - 1:ch, :]
        xa = wide(a_dec).astype(BF16)
        xr = wide(r_dec).astype(BF16)
        yb = wide(b_inv).astype(BF16)
        yk = wide(k_inv).astype(BF16)
        bh = wide(b_inv * g_last).astype(BF16)
        kh = wide(k_inv * g_last).astype(BF16)
        vw = wide(v).astype(BF16)

        a_ab = jnp.where(strict, _dotg(xa, yb, NT), 0.0)
        a_ak = jnp.where(strict, _dotg(xa, yk, NT), 0.0).astype(BF16)
        m_rb = jnp.where(incl, _dotg(xr, yb, NT), 0.0).astype(BF16)
        m_rk = jnp.where(incl, _dotg(xr, yk, NT), 0.0).astype(BF16)

        t_inv = jnp.where(eye, 1.0, a_ab)
        apow = a_ab
        for _ in range(int(math.log2(ch)) - 1):
            ab16 = apow.astype(BF16)
            apow = _dot(ab16, ab16)
            t_inv = t_inv + _dot(t_inv.astype(BF16), apow.astype(BF16))
        t16 = t_inv.astype(BF16)

        atp = _dot(t16, xa).astype(BF16)
        vp = _dot(t16, _dot(a_ak, vw).astype(BF16))
        y_kv = _dot(m_rk, vw)
        h_kv = _dotg(kh, vw, TN)
        g_col = jnp.sum(jnp.where(eye, g_last, 0.0), -1, keepdims=True)
        return atp, vp, xr, m_rb, y_kv, bh, h_kv, g_col

    lax.fori_loop(0, seq // rows, body, jnp.zeros((bw, bw), F32))


def _mixer_d(h, mu3, mulow, w2p, a2p, g2p, vec, bsz, seq, nsub=4):
    assert seq % (nsub * D_CHUNK) == 0 and D_CHUNK == D_HEAD_SIZE

    def piece(name):
        off, w = _PIECE_OFF[name]
        return pl.BlockSpec((seq, w * LANES), lambda b, o=off // w: (b, o))

    def const(a):
        return pl.BlockSpec(a.shape, lambda b: (0, 0))

    return pl.pallas_call(
        functools.partial(_mixer_d_kernel, seq=seq, nsub=nsub),
        grid=(bsz,),
        in_specs=[piece("d_r"), piece("d_k"), piece("d_v"), piece("d_low"),
                  const(mu3), const(mulow), const(w2p), const(a2p), const(g2p), const(vec)],
        out_specs=pl.BlockSpec((seq, BRANCH_WIDTH), lambda b: (b, 0)),
        out_shape=jax.ShapeDtypeStruct((bsz * seq, BRANCH_WIDTH), F32),
        compiler_params=_params("parallel"),
        name="mixer_d",
    )(h, h, h, h, mu3, mulow, w2p, a2p, g2p, vec)


def _merge_kernel(x_ref, ya_ref, yb_ref, yc_ref, yd_ref, wg_ref, wb_ref, wo_ref, g_ref, b_ref, o_ref,
                  *, alpha):
    x = x_ref[...]
    xb = x.astype(BF16)
    merged = None
    for n, y_ref in enumerate((ya_ref, yb_ref, yc_ref, yd_ref)):
        gate = _sigmoid(_dot(xb, wg_ref[:, n * D_MODEL:(n + 1) * D_MODEL]))
        term = gate * _dot(y_ref[...].astype(BF16), wb_ref[n])
        merged = term if merged is None else merged + term
    z = alpha * x + _dot(merged.astype(BF16), wo_ref[...])
    o_ref[...] = _layer_norm(z, g_ref[...], b_ref[...])


def _merge(x2d, ys, wg, wb, wo, g, b, alpha, tm):
    n, d = x2d.shape

    def const(a):
        nd = a.ndim
        return pl.BlockSpec(a.shape, lambda i: (0,) * nd)

    yspec = pl.BlockSpec((tm, BRANCH_WIDTH), lambda i: (i, 0))
    return pl.pallas_call(
        functools.partial(_merge_kernel, alpha=alpha),
        grid=(n // tm,),
        in_specs=[pl.BlockSpec((tm, d), lambda i: (i, 0)), yspec, yspec, yspec, yspec,
                  const(wg), const(wb), const(wo), const(g), const(b)],
        out_specs=pl.BlockSpec((tm, d), lambda i: (i, 0)),
        out_shape=jax.ShapeDtypeStruct((n, d), F32),
        compiler_params=_params("parallel"),
        name="merge",
    )(x2d, *ys, wg, wb, wo, g, b)


def _moe_kernel(x_ref, ltri_ref, wr_ref, br_ref, wg_ref, wu_ref, wd_ref, g_ref, b_ref, o_ref,
                xs_ref, gs_ref, ys_ref, pt_ref, seg_ref, *, alpha):
    e = pl.program_id(1)
    tm = x_ref.shape[0]
    nrow = xs_ref.shape[0]
    ck = MOE_CHUNK
    lane = _iota((tm, LANES), 1)

    @pl.when(e == 0)
    def _():
        x = x_ref[...]
        xb = x.astype(BF16)
        ys_ref[...] = jnp.zeros_like(ys_ref)
        x_lo = (x - xb.astype(F32)).astype(BF16)
        cross = _dot(jnp.concatenate([xb, x_lo], 0), wr_ref[...])
        logits = (cross[0:tm, 0:LANES] + (cross[0:tm, LANES:] + cross[tm:, 0:LANES])) + br_ref[...]
        is_g = jnp.logical_and(lane >= MOE_EXPERTS, lane < MOE_EXPERTS + MOE_GROUPS)
        lg = jnp.where(is_g, logits, -jnp.inf)
        gmax = jnp.max(lg, -1, keepdims=True)
        ptop = 1.0 / jnp.sum(jnp.exp(lg - gmax), -1, keepdims=True)
        gsel = jnp.min(jnp.where(lg == gmax, lane, 2 * LANES), -1, keepdims=True) - MOE_EXPERTS
        in_group = jnp.logical_and(lane < MOE_EXPERTS, lane // MOE_PER_GROUP == gsel)
        le = jnp.where(in_group, logits, -jnp.inf)
        v1 = jnp.max(le, -1, keepdims=True)
        i1 = jnp.min(jnp.where(le == v1, lane, 2 * LANES), -1, keepdims=True)
        le2 = jnp.where(lane == i1, -jnp.inf, le)
        v2 = jnp.max(le2, -1, keepdims=True)
        i2 = jnp.min(jnp.where(le2 == v2, lane, 2 * LANES), -1, keepdims=True)
        e2 = jnp.exp(v2 - v1)
        w1 = ptop / (1.0 + e2)
        gate = jnp.where(lane == i1, w1, jnp.where(lane == i2, w1 * e2, 0.0))

        onehot = jnp.where(lane == gsel, 1.0, 0.0)
        rank = _dot(ltri_ref[...], onehot.astype(BF16))
        cnt = jnp.sum(onehot, 0, keepdims=True).astype(I32)
        nch = jnp.right_shift(cnt + (ck - 1), int(math.log2(ck)))
        lane1 = _iota((1, LANES), 1)
        off = jnp.int32(0)
        off_vec = jnp.zeros((1, LANES), I32)
        for grp in range(MOE_GROUPS):
            n_g = nch[0, grp]
            seg_ref[grp] = off
            seg_ref[MOE_GROUPS + grp] = n_g
            off_vec = jnp.where(lane1 == grp, off, off_vec)
            off = off + n_g * ck
        dest = jnp.sum(onehot * (rank + off_vec.astype(F32)), -1, keepdims=True)
        pt_ref[...] = jnp.where(_iota((tm, nrow), 1) == dest.astype(I32), 1.0, 0.0).astype(BF16)
        dest_row = jnp.broadcast_to(dest, (tm, LANES)).T[0:1, :].astype(I32)
        perm = jnp.where(_iota((nrow, tm), 0) == dest_row, 1.0, 0.0).astype(BF16)
        gate_hi, gate_lo = _split2(gate)
        moved = _dot(perm, jnp.concatenate([xb, gate_hi, gate_lo], 1))
        d = x_ref.shape[1]
        xs_ref[...] = moved[:, 0:d].astype(BF16)
        gs_ref[...] = moved[:, d:d + LANES] + moved[:, d + LANES:]

    grp = e // (MOE_PER_GROUP // 2)
    seg_off = seg_ref[grp]
    seg_chunks = seg_ref[MOE_GROUPS + grp]

    def run_experts(start, size):
        rows = pl.ds(pl.multiple_of(start, ck), size)
        xc = xs_ref[rows, :]
        gs = gs_ref[rows, :]
        lane_s = _iota((size, LANES), 1)
        g0 = jnp.sum(jnp.where(lane_s == 2 * e, gs, 0.0), -1, keepdims=True)
        g1 = jnp.sum(jnp.where(lane_s == 2 * e + 1, gs, 0.0), -1, keepdims=True)
        hidden = []
        for k, gk in enumerate((g0, g1)):
            hg = _dot(xc, wg_ref[k])
            hu = _dot(xc, wu_ref[k])
            hidden.append((hg * _sigmoid(hg) * hu * gk).astype(BF16))
        wd = wd_ref[...].reshape(2 * MOE_HIDDEN, wd_ref.shape[2])
        ys_ref[rows, :] += _dot(jnp.concatenate(hidden, 1), wd)

    big = 4

    def quad(c, carry):
        run_experts(seg_off + c * (big * ck), big * ck)
        return carry

    lax.fori_loop(0, seg_chunks // big, quad, 0)
    rem = seg_chunks % big
    rem_start = seg_off + (seg_chunks - rem) * ck
    for r in range(1, big):
        @pl.when(rem == r)
        def _(r=r):
            run_experts(rem_start, r * ck)

    @pl.when(e == pl.num_programs(1) - 1)
    def _():
        z = alpha * x_ref[...] + _dot(pt_ref[...], ys_ref[...].astype(BF16))
        o_ref[...] = _layer_norm(z, g_ref[...], b_ref[...])


def _moe(x2d, wr, br, wg, wu, wd, g, b, alpha, tm):
    n, d = x2d.shape
    ne = wg.shape[0]
    hid = wg.shape[2]
    nrow = tm + MOE_GROUPS * MOE_CHUNK
    ltri = jnp.tril(jnp.ones((tm, tm), BF16), -1)

    def const(a):
        return pl.BlockSpec(a.shape, lambda i, e: (0, 0))

    return pl.pallas_call(
        functools.partial(_moe_kernel, alpha=alpha),
        grid=(n // tm, ne // 2),
        in_specs=[pl.BlockSpec((tm, d), lambda i, e: (i, 0)), const(ltri), const(wr), const(br),
                  pl.BlockSpec((2, d, hid), lambda i, e: (e, 0, 0)),
                  pl.BlockSpec((2, d, hid), lambda i, e: (e, 0, 0)),
                  pl.BlockSpec((2, hid, d), lambda i, e: (e, 0, 0)),
                  const(g), const(b)],
        out_specs=pl.BlockSpec((tm, d), lambda i, e: (i, 0)),
        out_shape=jax.ShapeDtypeStruct((n, d), F32),
        scratch_shapes=[pltpu.VMEM((nrow, d), BF16), pltpu.VMEM((nrow, LANES), F32),
                        pltpu.VMEM((nrow, d), F32), pltpu.VMEM((tm, nrow), BF16),
                        pltpu.SMEM((2 * MOE_GROUPS,), I32)],
        compiler_params=_params("parallel", "arbitrary"),
        name="moe",
    )(x2d, ltri, wr, br, wg, wu, wd, g, b)


def _t5_bucket(dist):
    exact = RPB_BUCKETS // 2
    d = jnp.maximum(dist, 0)
    df = jnp.maximum(d, 1).astype(F32)
    large = exact + (jnp.log(df / exact) / math.log(RPB_MAX_DIST / exact)
                     * (RPB_BUCKETS - exact)).astype(I32)
    large = jnp.minimum(large, RPB_BUCKETS - 1)
    return jnp.where(d < exact, d, large)


def _rpb_lookup(tab, dist):
    bucket = _t5_bucket(dist)[..., None]
    out = jnp.zeros(dist.shape + (tab.shape[1],), F32)
    for k in range(RPB_BUCKETS):
        out = jnp.where(bucket == k, tab[k], out)
    return out


def _bias_a(tab):
    qi = jnp.arange(A_BLOCK)[:, None]
    kj = jnp.arange(2 * A_BLOCK)[None, :]
    rdist = qi + A_BLOCK - kj
    out = []
    for window, dilation in A_PATTERNS:
        in_band = (rdist >= 0) & (rdist <= window // dilation)
        bias = jnp.transpose(_rpb_lookup(tab, rdist * dilation), (2, 0, 1))
        out.append(jnp.where(in_band[None], bias, NEG))
    return jnp.stack(out).astype(F32)


def _bias_b(tab, nblk):
    qi = jnp.arange(Q_BLOCK)[:, None]
    kj = jnp.arange(Q_BLOCK)[None, :]
    delta = jnp.arange(nblk)[:, None, None] * Q_BLOCK
    bias = _rpb_lookup(tab, delta + qi - kj)
    return jnp.transpose(bias, (0, 3, 2, 1)).astype(F32)


def _in_offsets():
    splits = (("a_q", 256), ("a_k", 256), ("a_v", 256), ("b_q", 256), ("b_ckv", 64), ("b_iq", 256),
              ("b_ik", 32), ("b_iw", 8), ("c_q", 128), ("c_k", 128), ("c_v", 256), ("c_a", 16),
              ("c_g", 256), ("d", 832), ("gate", 4096))
    off, out = 0, {}
    for name, w in splits:
        out[name] = (off, w)
        off += w
    return out


def _proj_columns():
    src = _in_offsets()
    d0 = src["d"][0]
    named = {
        "a_q": np.arange(256) + src["a_q"][0], "a_k": np.arange(256) + src["a_k"][0],
        "a_v": np.arange(256) + src["a_v"][0], "b_q": np.arange(256) + src["b_q"][0],
        "b_iq": np.arange(256) + src["b_iq"][0],
        "b_misc": np.concatenate([np.arange(B_LATENT) + src["b_ckv"][0], np.arange(IDX_DIM) + src["b_ik"][0],
                                  np.arange(IDX_HEADS) + src["b_iw"][0]]),
        "c_v": np.arange(256) + src["c_v"][0], "c_g": np.arange(256) + src["c_g"][0],
        "d_r": np.arange(256) + d0, "d_k": np.arange(256) + d0 + 256, "d_v": np.arange(256) + d0 + 512,
        "c_q": np.arange(128) + src["c_q"][0], "c_k": np.arange(128) + src["c_k"][0],
        "c_a": np.arange(16) + src["c_a"][0], "d_low": np.arange(64) + d0 + 768,
    }
    cols = []
    for name, w in _PIECES:
        c = named[name]
        cols.append(np.concatenate([c, -np.ones(w * LANES - len(c), np.int64)]))
    return np.concatenate(cols)


def _proj_weight(w):
    cols = _proj_columns()
    cuts = [0] + [k for k in range(1, len(cols))
                  if (cols[k] < 0) != (cols[k - 1] < 0) or (cols[k] >= 0 and cols[k] != cols[k - 1] + 1)]
    cuts.append(len(cols))
    parts = []
    for a, b in zip(cuts[:-1], cuts[1:]):
        if cols[a] < 0:
            parts.append(jnp.zeros((w.shape[0], b - a), BF16))
        else:
            parts.append(w[:, int(cols[a]):int(cols[a]) + (b - a)].astype(BF16))
    return jnp.concatenate(parts, 1)


def _pad_rows(w, first, total):
    return jnp.zeros((total, w.shape[1]), w.dtype).at[first:first + w.shape[0]].set(w)


def kernel(x, rpb_table, w_in, b_kv_gain, b_w_uv, c_a_up, c_a_bias, c_norm_gain, d_mu, d_w0, d_w2,
           d_a0, d_a2, d_g2, d_k_k, d_k_a, d_r_k, d_gn_w, d_gn_b, w_branch, w_out, ln_g, ln_b,
           router_g, router_g_bias, router_e, router_e_bias, moe_w_gate, moe_w_up, moe_w_down):
    bsz, seq, d_model = x.shape
    depth = w_in.shape[0]
    n = bsz * seq
    alpha = (2 * depth) ** 0.25
    gate_off = _in_offsets()["gate"][0]
    bias_a = _bias_a(rpb_table[:, :A_HEADS])
    bias_b = _bias_b(rpb_table[:, A_HEADS:], seq // Q_BLOCK)
    tm_proj = math.gcd(n, 512)
    tm_merge = math.gcd(n, 512)
    tm_moe = math.gcd(n, 1024)

    x2d = x.reshape(n, d_model)
    for l in range(depth):
        h = _project(x2d, _proj_weight(w_in[l]), tm_proj, PROJ_WIDTH)

        ya = _mixer_a(h, bias_a, bsz, seq)

        gain4 = jnp.tile(b_kv_gain[l], B_HEADS)[None, :]
        wuv_t = jnp.transpose(b_w_uv[l], (0, 2, 1)).astype(BF16)
        yb = _mixer_b(h, gain4, wuv_t, bias_b, bsz, seq)

        aup_pad = _pad_rows(c_a_up[l], 0, LANES)
        yc = _mixer_c(h, aup_pad, c_a_bias[l][None, :], c_norm_gain[l][None, :], bsz, seq)

        mu = d_mu[l]
        mu3 = mu[:3 * BRANCH_WIDTH].reshape(3, BRANCH_WIDTH)
        mulow = jnp.zeros((1, LANES), F32).at[0, :D_DECAY_RANK + D_ICLR_RANK + D_GATE_RANK].set(
            mu[3 * BRANCH_WIDTH:])
        w2p = _pad_rows(d_w2[l], 0, LANES)
        a2p = _pad_rows(d_a2[l], D_DECAY_RANK, LANES)
        g2p = _pad_rows(d_g2[l], D_DECAY_RANK + D_ICLR_RANK, LANES)
        vec = jnp.stack([d_w0[l], d_a0[l], d_k_k[l], d_k_a[l], d_r_k[l], d_gn_w[l], d_gn_b[l],
                         jnp.zeros_like(d_w0[l])])
        yd = _mixer_d(h, mu3, mulow, w2p, a2p, g2p, vec, bsz, seq)

        wg = w_in[l][:, gate_off:gate_off + N_BRANCHES * d_model].astype(BF16)
        x2d = _merge(x2d, (ya, yb, yc, yd), wg, w_branch[l].astype(BF16), w_out[l].astype(BF16),
                     ln_g[l, 0][None, :], ln_b[l, 0][None, :], alpha, tm_merge)

        wr = jnp.zeros((d_model, LANES), F32)
        wr = wr.at[:, :MOE_EXPERTS].set(router_e[l]).at[:, MOE_EXPERTS:MOE_EXPERTS + MOE_GROUPS].set(
            router_g[l])
        br = jnp.zeros((1, LANES), F32)
        br = br.at[0, :MOE_EXPERTS].set(router_e_bias[l]).at[0, MOE_EXPERTS:MOE_EXPERTS + MOE_GROUPS].set(
            router_g_bias[l])
        wr = jnp.concatenate(_split2(wr), 1)
        x2d = _moe(x2d, wr, br, moe_w_gate[l].astype(BF16), moe_w_up[l].astype(BF16),
                   moe_w_down[l].astype(BF16), ln_g[l, 1][None, :], ln_b[l, 1][None, :], alpha, tm_moe)
    return x2d.reshape(bsz, seq, d_model)
```

```python
import functools
import math

import jax
import jax.numpy as jnp
import numpy as np
from jax import lax
from jax.experimental import pallas as pl
from jax.experimental.pallas import tpu as pltpu

F32 = jnp.float32
BF16 = jnp.bfloat16
I32 = jnp.int32

D_MODEL = 1024
N_BRANCHES = 4
BRANCH_WIDTH = 256
HEAD_DIM = 64
A_HEADS = 4
A_PATTERNS = ((128, 1), (512, 4), (2048, 16))
A_BLOCK = 128
B_HEADS = 4
B_LATENT = 64
IDX_HEADS = 8
IDX_DIM = 32
TOPK_MAX = 256
TOPK_DIV = 4
Q_BLOCK = 128
C_HEADS = 4
C_KEY_DIM = 32
C_VAL_DIM = 64
C_GATE_RANK = 16
C_GATE_TAU = 16.0
C_CHUNK = 64
D_HEADS = 4
D_HEAD_SIZE = 64
D_DECAY_RANK = 16
D_ICLR_RANK = 16
D_GATE_RANK = 32
D_GN_EPS = 64e-5
D_CHUNK = 64
RPB_BUCKETS = 32
RPB_MAX_DIST = 2048
MOE_GROUPS = 4
MOE_PER_GROUP = 8
MOE_EXPERTS = 32
MOE_HIDDEN = 256
MOE_CHUNK = 128
MOE_STEP_EXPERTS = 4
LN_EPS = 1e-5

LANES = 128
NEG = -1e30
VMEM_LIMIT = 56 * 1024 * 1024

NT = (((1,), (1,)), ((), ()))
TN = (((0,), (0,)), ((), ()))

_PIECES = (
    ("a_q", 2), ("a_k", 2), ("a_v", 2), ("b_q", 2), ("b_iq", 2),
    ("c_v", 2), ("c_g", 2), ("d_r", 2), ("d_k", 2), ("d_v", 2),
    ("c_q", 1), ("c_k", 1), ("b_misc", 1), ("c_a", 1), ("d_low", 1),
)
_PIECE_OFF = {}
_off = 0
for _n, _w in _PIECES:
    _PIECE_OFF[_n] = (_off, _w)
    _off += _w
PROJ_WIDTH = _off * LANES


def _dot(a, b):
    return jnp.dot(a, b, preferred_element_type=F32)


def _dotg(a, b, dims):
    return lax.dot_general(a, b, dims, preferred_element_type=F32)


def _split2(x):
    hi = x.astype(BF16)
    lo = (x - hi.astype(F32)).astype(BF16)
    return hi, lo


def _dot_rhs_exact(x, m):
    hi, lo = _split2(x)
    return _dot(hi, m) + _dot(lo, m)


def _dot_lhs_exact(m, x):
    hi, lo = _split2(x)
    return _dot(m, hi) + _dot(m, lo)


def _dot3(x, w):
    xh, xl = _split2(x)
    wh, wl = _split2(w)
    return _dot(xh, wh) + (_dot(xl, wh) + _dot(xh, wl))


def _sigmoid(x):
    return 1.0 / (1.0 + jnp.exp(-x))


def _softplus(x):
    return jnp.maximum(x, 0.0) + jnp.log(1.0 + jnp.exp(-jnp.abs(x)))


def _iota(shape, axis):
    return lax.broadcasted_iota(I32, shape, axis)


def _group_matrix(n, group, value, dtype=BF16):
    r = _iota((n, n), 0) // group
    c = _iota((n, n), 1) // group
    return jnp.where(r == c, value, 0.0).astype(dtype)


def _layer_norm(z, g, b):
    mu = jnp.mean(z, -1, keepdims=True)
    zc = z - mu
    var = jnp.mean(zc * zc, -1, keepdims=True)
    return zc * lax.rsqrt(var + LN_EPS) * g + b


def _params(*sem):
    return pltpu.CompilerParams(dimension_semantics=sem, vmem_limit_bytes=VMEM_LIMIT)


def _proj_kernel(x_ref, w_ref, o_ref):
    o_ref[...] = _dot(x_ref[...].astype(BF16), w_ref[...])


def _project(x2d, w, tm, tn):
    n, k = x2d.shape
    wt = w.shape[1]
    return pl.pallas_call(
        _proj_kernel,
        grid=(n // tm, wt // tn),
        in_specs=[pl.BlockSpec((tm, k), lambda i, j: (i, 0)),
                  pl.BlockSpec((k, tn), lambda i, j: (0, j))],
        out_specs=pl.BlockSpec((tm, tn), lambda i, j: (i, j)),
        out_shape=jax.ShapeDtypeStruct((n, wt), F32),
        compiler_params=_params("parallel", "parallel"),
        name="proj",
    )(x2d, w)


def _mixer_a_kernel(q0_ref, q1_ref, k0_ref, k1_ref, v0_ref, v1_ref, bias_ref, o_ref, op_ref, lse_ref,
                    *, seq, dils):
    q_refs, k_refs, v_refs = (q0_ref, q1_ref), (k0_ref, k1_ref), (v0_ref, v1_ref)
    ab = A_BLOCK
    lane = _iota((ab, LANES), 1)
    upper = lane >= HEAD_DIM
    col = _iota((ab, 2 * ab), 1)
    scale = HEAD_DIM ** -0.5

    for p, d in enumerate(dils):
        per_res = seq // (d * ab)

        def body(idx, carry, p=p, d=d, per_res=per_res):
            r = idx // per_res
            b = idx % per_res
            start = b * (ab * d) + r
            pstart = jnp.maximum(start - ab * d, r)
            if d == 1:
                rows, prow = pl.ds(pl.multiple_of(start, ab), ab), pl.ds(pl.multiple_of(pstart, ab), ab)
            else:
                rows, prow = pl.ds(start, ab, stride=d), pl.ds(pstart, ab, stride=d)
            first_pen = jnp.where(b == 0, NEG, 0.0)
            pen = jnp.where(col < ab, first_pen, 0.0)
            for hp in range(2):
                q_ref, k_ref, v_ref = q_refs[hp], k_refs[hp], v_refs[hp]
                q = q_ref[rows, :]
                kb = jnp.concatenate([k_ref[prow, :], k_ref[rows, :]], 0).astype(BF16)
                vb = jnp.concatenate([v_ref[prow, :], v_ref[rows, :]], 0).astype(BF16)
                outs, lses = [], []
                for hh in range(2):
                    hm = upper if hh else jnp.logical_not(upper)
                    qm = jnp.where(hm, q, 0.0).astype(BF16)
                    s = _dotg(qm, kb, NT) * scale + bias_ref[p, 2 * hp + hh] + pen
                    mx = jnp.max(s, -1, keepdims=True)
                    e = jnp.exp(s - mx)
                    den = jnp.sum(e, -1, keepdims=True)
                    outs.append(_dot(e.astype(BF16), vb) / den)
                    lses.append(mx + jnp.log(den))
                op_ref[2 * p + hp, rows, :] = jnp.where(upper, outs[1], outs[0])
                lse_ref[2 * p + hp, rows, :] = jnp.where(upper, lses[1], lses[0])
            return carry

        def body_pair(t, carry, body=body):
            return body(2 * t + 1, body(2 * t, carry))

        lax.fori_loop(0, seq // (2 * ab), body_pair, 0)

    cr = 256

    def combine(c, carry):
        rows = pl.ds(pl.multiple_of(c * cr, cr), cr)
        for hp in range(2):
            ls = [lse_ref[2 * p + hp, rows, :] for p in range(len(dils))]
            mx = functools.reduce(jnp.maximum, ls)
            es = [jnp.exp(l - mx) for l in ls]
            num = functools.reduce(lambda a, b: a + b,
                                   [e * op_ref[2 * p + hp, rows, :] for p, e in enumerate(es)])
            o_ref[rows, hp * LANES:(hp + 1) * LANES] = num / functools.reduce(lambda a, b: a + b, es)
        return carry

    lax.fori_loop(0, seq // cr, combine, 0)


def _mixer_a(h, bias, bsz, seq):
    dils = tuple(d for _, d in A_PATTERNS)
    for w, d in A_PATTERNS:
        assert w // d == A_BLOCK and seq % (d * A_BLOCK) == 0
    npat = len(dils)

    def pieces(name):
        off, w = _PIECE_OFF[name]
        return [pl.BlockSpec((seq, LANES), lambda b, o=off + u: (b, o)) for u in range(w)]

    return pl.pallas_call(
        functools.partial(_mixer_a_kernel, seq=seq, dils=dils),
        grid=(bsz,),
        in_specs=pieces("a_q") + pieces("a_k") + pieces("a_v")
        + [pl.BlockSpec(bias.shape, lambda b: (0, 0, 0, 0))],
        out_specs=pl.BlockSpec((seq, BRANCH_WIDTH), lambda b: (b, 0)),
        out_shape=jax.ShapeDtypeStruct((bsz * seq, BRANCH_WIDTH), F32),
        scratch_shapes=[pltpu.VMEM((2 * npat, seq, LANES), F32),
                        pltpu.VMEM((2 * npat, seq, LANES), F32)],
        compiler_params=_params("parallel"),
        name="mixer_a",
    )(h, h, h, h, h, h, bias)


def _mixer_b_kernel(*refs, seq, keep):
    def query_block(i, carry):
        _mixer_b_block(i, *refs, seq=seq, keep=keep)
        return carry

    lax.fori_loop(0, seq // Q_BLOCK, query_block, 0)


def _mixer_b_block(i, q_ref, iq_ref, mk_ref, gain_ref, wuvt_ref, bias_ref, o_ref,
                   ckvn_ref, ckvt_ref, ikb_ref, sc_ref, sc16_ref, lg_ref, iqs_ref, qs_ref, *, seq, keep):
    qb = Q_BLOCK
    sb = 2 * qb
    iw_lane = B_LATENT + IDX_DIM

    @pl.when(i == 0)
    def _():
        src = _iota((LANES, BRANCH_WIDTH), 0)
        dst = _iota((LANES, BRANCH_WIDTH), 1)
        is_lat = src < B_LATENT
        rep_lat = jnp.where(jnp.logical_and(is_lat, dst % B_LATENT == src), 1.0, 0.0).astype(BF16)
        avg_lat = jnp.where(is_lat, 1.0 / B_LATENT, 0.0).astype(BF16)
        is_ik = jnp.logical_and(src >= B_LATENT, src < iw_lane)
        rep_ik = jnp.where(jnp.logical_and(is_ik, dst % IDX_DIM == src - B_LATENT), 1.0, 0.0).astype(BF16)

        def prep(c, carry):
            rows = pl.ds(pl.multiple_of(c * sb, sb), sb)
            x = mk_ref[rows, :]
            hi, lo = _split2(x)
            lo2 = (x - hi.astype(F32) - lo.astype(F32)).astype(BF16)
            lat = _dot(hi, rep_lat) + (_dot(lo, rep_lat) + _dot(lo2, rep_lat))
            ms = _dot_rhs_exact(x * x, avg_lat)
            xn = lat * lax.rsqrt(ms + 1e-6) * gain_ref[...]
            ckvn_ref[rows, :] = xn.astype(BF16)
            ckvt_ref[c] = xn.T[0:B_LATENT, :].astype(BF16)
            ikb_ref[rows, :] = _dot(hi, rep_ik).astype(BF16)
            return carry

        lax.fori_loop(0, seq // sb, prep, 0)

    lane = _iota((qb, BRANCH_WIDTH), 1)
    qrows = pl.ds(pl.multiple_of(i * qb, qb), qb)
    iq = iq_ref[qrows, :] * (IDX_DIM ** -0.5)
    for h in range(IDX_HEADS):
        iqs_ref[h * qb:(h + 1) * qb, :] = jnp.where(lane // IDX_DIM == h, iq, 0.0).astype(BF16)
    q = q_ref[qrows, :] * (B_LATENT ** -0.5)
    for h in range(B_HEADS):
        qs_ref[h * qb:(h + 1) * qb, :] = jnp.where(lane // B_LATENT == h, q, 0.0).astype(BF16)
    iw_t = (mk_ref[qrows, :] * (IDX_HEADS ** -0.5)).T[iw_lane:iw_lane + IDX_HEADS, :]
    krow = _iota((sb, qb), 0)
    qcol = _iota((sb, qb), 1)
    nsb = (i + 2) // 2

    def fold(x, op):
        parts = [x[r:r + 8] for r in range(0, sb, 8)]
        while len(parts) > 1:
            parts = [op(parts[k], parts[k + 1]) for k in range(0, len(parts), 2)]
        return parts[0]

    def for_steps(body, init):
        def pair(p, carry):
            return body(2 * p + 1, body(2 * p, carry))

        carry = lax.fori_loop(0, nsb // 2, pair, init)
        return lax.cond(nsb % 2 == 1, lambda c: body(nsb - 1, c), lambda c: c, carry)

    def score_body(j, carry):
        kr = pl.ds(pl.multiple_of(j * sb, sb), sb)
        rel = _dotg(ikb_ref[kr, :], iqs_ref[...], NT)
        sc = jnp.maximum(rel[:, 0:qb], 0.0) * iw_t[0:1, :]
        for h in range(1, IDX_HEADS):
            sc = sc + jnp.maximum(rel[:, h * qb:(h + 1) * qb], 0.0) * iw_t[h:h + 1, :]
        sc = jnp.where(krow <= qcol + (i * qb - j * sb), sc, -jnp.inf)
        sc_ref[j] = sc
        sc16_ref[j] = pltpu.bitcast(pltpu.bitcast(sc, I32) & jnp.int32(-65536), F32).astype(BF16)
        logits = _dotg(ckvn_ref[kr, :], qs_ref[...], NT)
        d0 = i - 2 * j
        for h in range(B_HEADS):
            lg = logits[:, h * qb:(h + 1) * qb]
            lg_ref[h, j, 0:qb, :] = lg[0:qb] + bias_ref[d0, h]
            lg_ref[h, j, qb:sb, :] = lg[qb:sb] + bias_ref[jnp.maximum(d0 - 1, 0), h]
        return carry

    for_steps(score_body, 0)

    sign = jnp.int32(-2 ** 31)

    def key_to_float(u):
        k = u ^ sign
        bits = k ^ ((k >> 31) & jnp.int32(0x7FFFFFFF))
        return pltpu.bitcast(bits, F32)

    def count(pred):
        def cnt(j, acc):
            return acc + fold(pred(j), jnp.add)
        acc = for_steps(cnt, jnp.zeros((8, qb), F32))
        return jnp.sum(acc, 0, keepdims=True)

    pk = 16

    def count_ge16(cf):
        cf = pltpu.bitcast(pltpu.bitcast(cf, I32) & jnp.int32(-65536), F32)
        c16 = jnp.broadcast_to(cf, (pk, qb)).astype(BF16)
        one, zero = jnp.ones((pk, qb), BF16), jnp.zeros((pk, qb), BF16)

        def cnt(j, acc):
            x = sc16_ref[j]
            parts = [jnp.where(x[r:r + pk] >= c16, one, zero) for r in range(0, sb, pk)]
            while len(parts) > 1:
                parts = [parts[k] + parts[k + 1] for k in range(0, len(parts), 2)]
            return acc + parts[0].astype(F32)

        acc = for_steps(cnt, jnp.zeros((pk, qb), F32))
        return jnp.sum(acc, 0, keepdims=True)

    def bit_body(t, state, packed=False):
        u, n_u = state
        cand = u | jnp.left_shift(jnp.int32(1), 31 - t)
        cf = key_to_float(cand)
        c = count_ge16(cf) if packed else count(lambda j: jnp.where(sc_ref[j] >= cf, 1.0, 0.0))
        take = c >= keep
        return jnp.where(take, cand, u), jnp.where(take, c, n_u)

    def bit_body16(t, state):
        return bit_body(t, state, packed=True)

    first_bits, bit_group = 24, 4
    state = (jnp.zeros((1, qb), I32), (jnp.zeros((1, qb), I32) + nsb * sb).astype(F32))
    state = bit_body16(0, state)
    n_pos = count(lambda j: jnp.where(sc_ref[j] > 0.0, 1.0, 0.0))
    zero_thr = jnp.logical_and(state[0] != 0, n_pos < keep)
    state = lax.fori_loop(1, 16, bit_body16, state)
    u, n_ge = lax.fori_loop(16, first_bits, bit_body, state)

    def more_bits(s):
        t, _, n_u = s
        open_rows = jnp.where(zero_thr, 0.0, jnp.abs(n_u - keep))
        return jnp.logical_and(t < 32, jnp.max(open_rows) > 0.0)

    def bit_group_body(s):
        t, u, n_u = s
        for k in range(bit_group):
            u, n_u = bit_body(t + k, (u, n_u))
        return t + bit_group, u, n_u

    _, u, n_ge = lax.while_loop(more_bits, bit_group_body, (jnp.int32(first_bits), u, n_ge))
    u = jnp.maximum(u ^ sign, jnp.int32(0x007FFFFF - 2 ** 31)) ^ sign
    thr = key_to_float(u)
    n_gt = count(lambda j: jnp.where(sc_ref[j] > thr, 1.0, 0.0))
    need = keep - n_gt

    nbits = int(math.ceil(math.log2(seq))) + 1

    def cut_search():
        def cut_body(t, cut):
            cand = cut | jnp.left_shift(jnp.int32(1), nbits - 1 - t)
            c = count(lambda j: jnp.where(sc_ref[j] == thr,
                                          jnp.where(krow < cand - j * sb, 1.0, 0.0), 0.0))
            return jnp.where(c <= need, cand, cut)

        return lax.fori_loop(0, nbits, cut_body, jnp.zeros((1, qb), I32))

    surplus = jnp.max(n_ge) > keep
    cut = lax.cond(surplus, cut_search, lambda: jnp.full((1, qb), 2 ** nbits - 1, I32))
    cut = jnp.minimum(cut, i * qb + 1 + _iota((1, qb), 1))

    def mask_body(j, mx):
        sc = sc_ref[j]
        sel = jnp.where(sc > thr, 1.0, jnp.where(sc == thr, jnp.where(krow < cut - j * sb, 1.0, 0.0), 0.0))
        out = []
        for h in range(B_HEADS):
            s = jnp.where(sel > 0.5, lg_ref[h, j], NEG)
            lg_ref[h, j] = s
            out.append(jnp.maximum(mx[h], fold(s, jnp.maximum)))
        return tuple(out)

    mx = for_steps(mask_body, tuple(jnp.full((8, qb), NEG, F32) for _ in range(B_HEADS)))
    ms = [jnp.max(m, 0, keepdims=True) for m in mx]

    def att_body(j, carry):
        ls, acc = carry
        ps, new_ls = [], []
        for h in range(B_HEADS):
            pr = jnp.exp(lg_ref[h, j] - ms[h])
            new_ls.append(ls[h] + fold(pr, jnp.add))
            ps.append(pr.astype(BF16))
        upd = _dot(ckvt_ref[j], jnp.concatenate(ps, 1))
        return tuple(new_ls), acc + upd

    init = (tuple(jnp.zeros((8, qb), F32) for _ in range(B_HEADS)),
            jnp.zeros((B_LATENT, B_HEADS * qb), F32))
    ls, acc = for_steps(att_body, init)
    ls = [jnp.sum(l, 0, keepdims=True) for l in ls]
    o_t = (acc / jnp.concatenate(ls, 1)).astype(BF16)
    y_t = jnp.concatenate([_dot(wuvt_ref[h], o_t[:, h * qb:(h + 1) * qb]) for h in range(B_HEADS)], 0)
    o_ref[qrows, :] = y_t.T


def _mixer_b(h, gain4, wuv_t, bias, bsz, seq):
    qb = Q_BLOCK
    assert seq % 256 == 0
    nblk = seq // qb
    keep = min(TOPK_MAX, seq // TOPK_DIV)

    def kpiece(name):
        off, w = _PIECE_OFF[name]
        return pl.BlockSpec((seq, w * LANES), lambda b, o=off // w: (b, o))

    return pl.pallas_call(
        functools.partial(_mixer_b_kernel, seq=seq, keep=keep),
        grid=(bsz,),
        in_specs=[kpiece("b_q"), kpiece("b_iq"), kpiece("b_misc"),
                  pl.BlockSpec(gain4.shape, lambda b: (0, 0)),
                  pl.BlockSpec(wuv_t.shape, lambda b: (0, 0, 0)),
                  pl.BlockSpec(bias.shape, lambda b: (0, 0, 0, 0))],
        out_specs=pl.BlockSpec((seq, BRANCH_WIDTH), lambda b: (b, 0)),
        out_shape=jax.ShapeDtypeStruct((bsz * seq, BRANCH_WIDTH), F32),
        scratch_shapes=[pltpu.VMEM((seq, BRANCH_WIDTH), BF16),
                        pltpu.VMEM((nblk // 2, B_LATENT, 2 * qb), BF16),
                        pltpu.VMEM((seq, BRANCH_WIDTH), BF16),
                        pltpu.VMEM((nblk // 2, 2 * qb, qb), F32),
                        pltpu.VMEM((nblk // 2, 2 * qb, qb), BF16),
                        pltpu.VMEM((B_HEADS, nblk // 2, 2 * qb, qb), F32),
                        pltpu.VMEM((IDX_HEADS * qb, BRANCH_WIDTH), BF16),
                        pltpu.VMEM((B_HEADS * qb, BRANCH_WIDTH), BF16)],
        compiler_params=_params("parallel"),
        name="mixer_b",
    )(h, h, h, gain4, wuv_t, bias)


def _mixer_c_kernel(q_ref, k_ref, v_ref, g_ref, a_ref, aup_ref, abias_ref, gain_ref, o_ref, *, seq):
    ch = C_CHUNK
    kw = C_HEADS * C_KEY_DIM
    vw = C_HEADS * C_VAL_DIM
    tri = jnp.where(_iota((ch, ch), 1) <= _iota((ch, ch), 0), 1.0, 0.0).astype(BF16)
    causal = _iota((C_HEADS * ch, ch), 1) <= (_iota((C_HEADS * ch, ch), 0) % ch)
    klane_head = _iota((C_HEADS * ch, kw), 1) // C_KEY_DIM
    krow_head = _iota((C_HEADS * ch, kw), 0) // ch
    vlane_head = _iota((ch, vw), 1) // C_VAL_DIM
    st_mask = (_iota((vw, kw), 0) // C_VAL_DIM) == (_iota((vw, kw), 1) // C_KEY_DIM)
    gmat = _group_matrix(vw, C_VAL_DIM, 1.0 / C_VAL_DIM)
    aup = aup_ref[...]

    def body(c, st):
        rows = pl.ds(pl.multiple_of(c * ch, ch), ch)
        qc = q_ref[rows, :] * (C_KEY_DIM ** -0.5)
        kc = k_ref[rows, :]
        vc = v_ref[rows, :]
        z = _dot3(a_ref[rows, :], aup) + abias_ref[...]
        log_a = -_softplus(-z) / C_GATE_TAU
        cum = _dot_lhs_exact(tri, log_a)
        last = cum[ch - 1:ch, :]
        q_dec = qc * jnp.exp(cum)
        k_inv = (kc * jnp.exp(-cum)).astype(BF16)
        k_dec = (kc * jnp.exp(last - cum)).astype(BF16)
        vb = vc.astype(BF16)
        qd_b = q_dec.astype(BF16)
        q_stack = jnp.where(klane_head == krow_head, jnp.concatenate([q_dec] * C_HEADS, 0), 0.0)
        att = jnp.where(causal, _dotg(q_stack.astype(BF16), k_inv, NT), 0.0)
        full = _dot(att.astype(BF16), vb)
        o = _dotg(qd_b, st.astype(BF16), NT)
        for h in range(C_HEADS):
            o = o + jnp.where(vlane_head == h, full[h * ch:(h + 1) * ch], 0.0)
        upd = _dotg(vb, k_dec, TN)
        st = st * jnp.exp(last) + jnp.where(st_mask, upd, 0.0)
        ms = _dot_rhs_exact(o * o, gmat)
        o = o * lax.rsqrt(ms + 1e-6) * gain_ref[...]
        g = g_ref[rows, :]
        o_ref[rows, :] = g * _sigmoid(g) * o
        return st

    def body_pair(t, st):
        return body(2 * t + 1, body(2 * t, st))

    lax.fori_loop(0, seq // (2 * ch), body_pair, jnp.zeros((vw, kw), F32))


def _mixer_c(h, aup_pad, abias, gain, bsz, seq):
    assert seq % C_CHUNK == 0

    def piece(name):
        off, w = _PIECE_OFF[name]
        return pl.BlockSpec((seq, w * LANES), lambda b, o=off // w: (b, o))

    def const(a):
        return pl.BlockSpec(a.shape, lambda b: (0, 0))

    return pl.pallas_call(
        functools.partial(_mixer_c_kernel, seq=seq),
        grid=(bsz,),
        in_specs=[piece("c_q"), piece("c_k"), piece("c_v"), piece("c_g"), piece("c_a"),
                  const(aup_pad), const(abias), const(gain)],
        out_specs=pl.BlockSpec((seq, BRANCH_WIDTH), lambda b: (b, 0)),
        out_shape=jax.ShapeDtypeStruct((bsz * seq, BRANCH_WIDTH), F32),
        compiler_params=_params("parallel"),
        name="mixer_c",
    )(h, h, h, h, h, aup_pad, abias, gain)


def _mixer_d_kernel(r_ref, k_ref, v_ref, low_ref, mu_ref, mulow_ref, w2_ref, a2_ref, g2_ref, vec_ref,
                    o_ref, *, seq, nsub):
    ch = D_CHUNK
    bw = BRANCH_WIDTH
    nh = D_HEADS
    hs = D_HEAD_SIZE
    r_i = _iota((bw, bw), 0)
    c_i = _iota((bw, bw), 1)
    same = (r_i // hs) == (c_i // hs)
    strict = c_i < r_i
    incl = c_i <= r_i
    eye = r_i == c_i
    ones_bd = _group_matrix(bw, hs, 1.0)
    avg_bd = _group_matrix(bw, hs, 1.0 / hs)
    rows = nsub * ch
    tr_i = _iota((rows, rows), 0)
    tc_i = _iota((rows, rows), 1)
    tri = jnp.where(jnp.logical_and(tr_i // ch == tc_i // ch, tc_i <= tr_i), 1.0, 0.0).astype(BF16)
    row0 = _iota((rows, bw), 0) == 0
    row0_low = _iota((rows, LANES), 0) == 0
    vec = vec_ref[...]
    w0, a0, k_k, k_a, r_k, gn_w, gn_b = (vec[n:n + 1, :] for n in range(7))
    mu = mu_ref[...]
    w2, a2, g2 = w2_ref[...], a2_ref[...], g2_ref[...]

    def wide(x):
        return jnp.where(same, jnp.concatenate([x] * nh, 0), 0.0)

    def shifted(ref, start, first_row, m):
        cur = ref[pl.ds(start, rows), :]
        last8 = ref[pl.ds(pl.multiple_of(jnp.maximum(start - 8, 0), 8), 8), :]
        prev_row = last8[7:8, :] * jnp.where(start > 0, 1.0, 0.0)
        prev = jnp.where(first_row, prev_row, pltpu.roll(cur, 1, 0))
        return cur + (prev - cur) * m

    def body(c, st):
        start = pl.multiple_of(c * rows, rows)
        r = shifted(r_ref, start, row0, mu[0:1, :])
        k = shifted(k_ref, start, row0, mu[1:2, :])
        v = shifted(v_ref, start, row0, mu[2:3, :])
        low = shifted(low_ref, start, row0_low, mulow_ref[...])
        w_raw = -_softplus(-(w0 + _dot3(jnp.tanh(low), w2))) - 0.5
        lw = -jnp.exp(w_raw)
        a = _sigmoid(a0 + _dot3(low, a2))
        g = _dot3(_sigmoid(low), g2)
        kk = k * k_k
        kk = kk / jnp.maximum(jnp.sqrt(_dot_rhs_exact(kk * kk, ones_bd)), 1e-12)
        k2 = k * (1.0 + (a - 1.0) * k_a)
        bonus = _dot_rhs_exact(r * k2 * r_k, ones_bd) * v
        b = kk * a

        cum = _dot_lhs_exact(tri, lw)
        e_in = jnp.exp(cum)
        e_inv = jnp.exp(-cum)
        a_dec = -kk * jnp.exp(cum - lw)
        r_dec = r * e_in
        b_inv = b * e_inv
        k_inv = k2 * e_inv
        local = [chunk_terms(*(z[s * ch:(s + 1) * ch] for z in (a_dec, r_dec, b_inv, k_inv, v, e_in)))
                 for s in range(nsub)]
        ys = []
        for atp, vp, xr, m_rb, y_kv, bh, h_kv, g_col in local:
            st16 = st.astype(BF16)
            u16 = (_dot(atp, st16) + vp).astype(BF16)
            y_w = _dot(xr, st16) + _dot(m_rb, u16) + y_kv
            st = g_col * st + _dotg(bh, u16, TN) + h_kv
            y = y_w[0:ch]
            for h in range(1, nh):
                y = y + y_w[h * ch:(h + 1) * ch]
            ys.append(y)
        y = jnp.concatenate(ys, 0)
        mean = _dot_rhs_exact(y, avg_bd)
        yc = y - mean
        var = _dot_rhs_exact(yc * yc, avg_bd)
        yn = yc * lax.rsqrt(var + D_GN_EPS) * gn_w + gn_b
        o_ref[pl.ds(start, rows), :] = (yn + bonus) * g
        return st

    def chunk_terms(a_dec, r_dec, b_inv, k_inv, v, e_in):
        g_last = e_in[ch - 1:ch, :]
        xa = wide(a_dec).astype(BF16)
        xr = wide(r_dec).astype(BF16)
        yb = wide(b_inv).astype(BF16)
        yk = wide(k_inv).astype(BF16)
        bh = wide(b_inv * g_last).astype(BF16)
        kh = wide(k_inv * g_last).astype(BF16)
        vw = wide(v).astype(BF16)

        a_ab = jnp.where(strict, _dotg(xa, yb, NT), 0.0)
        a_ak = jnp.where(strict, _dotg(xa, yk, NT), 0.0).astype(BF16)
        m_rb = jnp.where(incl, _dotg(xr, yb, NT), 0.0).astype(BF16)
        m_rk = jnp.where(incl, _dotg(xr, yk, NT), 0.0).astype(BF16)

        t_inv = jnp.where(eye, 1.0, a_ab)
        apow = a_ab
        for _ in range(int(math.log2(ch)) - 1):
            ab16 = apow.astype(BF16)
            apow = _dot(ab16, ab16)
            t_inv = t_inv + _dot(t_inv.astype(BF16), apow.astype(BF16))
        t16 = t_inv.astype(BF16)

        atp = _dot(t16, xa).astype(BF16)
        vp = _dot(t16, _dot(a_ak, vw).astype(BF16))
        y_kv = _dot(m_rk, vw)
        h_kv = _dotg(kh, vw, TN)
        g_col = jnp.sum(jnp.where(eye, g_last, 0.0), -1, keepdims=True)
        return atp, vp, xr, m_rb, y_kv, bh, h_kv, g_col

    lax.fori_loop(0, seq // rows, body, jnp.zeros((bw, bw), F32))


def _mixer_d(h, mu3, mulow, w2p, a2p, g2p, vec, bsz, seq, nsub=4):
    assert seq % (nsub * D_CHUNK) == 0 and D_CHUNK == D_HEAD_SIZE

    def piece(name):
        off, w = _PIECE_OFF[name]
        return pl.BlockSpec((seq, w * LANES), lambda b, o=off // w: (b, o))

    def const(a):
        return pl.BlockSpec(a.shape, lambda b: (0, 0))

    return pl.pallas_call(
        functools.partial(_mixer_d_kernel, seq=seq, nsub=nsub),
        grid=(bsz,),
        in_specs=[piece("d_r"), piece("d_k"), piece("d_v"), piece("d_low"),
                  const(mu3), const(mulow), const(w2p), const(a2p), const(g2p), const(vec)],
        out_specs=pl.BlockSpec((seq, BRANCH_WIDTH), lambda b: (b, 0)),
        out_shape=jax.ShapeDtypeStruct((bsz * seq, BRANCH_WIDTH), F32),
        compiler_params=_params("parallel"),
        name="mixer_d",
    )(h, h, h, h, mu3, mulow, w2p, a2p, g2p, vec)


def _merge_kernel(x_ref, ya_ref, yb_ref, yc_ref, yd_ref, wg_ref, wb_ref, wo_ref, g_ref, b_ref, o_ref,
                  *, alpha):
    x = x_ref[...]
    xb = x.astype(BF16)
    merged = None
    for n, y_ref in enumerate((ya_ref, yb_ref, yc_ref, yd_ref)):
        gate = _sigmoid(_dot(xb, wg_ref[:, n * D_MODEL:(n + 1) * D_MODEL]))
        term = gate * _dot(y_ref[...].astype(BF16), wb_ref[n])
        merged = term if merged is None else merged + term
    z = alpha * x + _dot(merged.astype(BF16), wo_ref[...])
    o_ref[...] = _layer_norm(z, g_ref[...], b_ref[...])


def _merge(x2d, ys, wg, wb, wo, g, b, alpha, tm):
    n, d = x2d.shape

    def const(a):
        nd = a.ndim
        return pl.BlockSpec(a.shape, lambda i: (0,) * nd)

    yspec = pl.BlockSpec((tm, BRANCH_WIDTH), lambda i: (i, 0))
    return pl.pallas_call(
        functools.partial(_merge_kernel, alpha=alpha),
        grid=(n // tm,),
        in_specs=[pl.BlockSpec((tm, d), lambda i: (i, 0)), yspec, yspec, yspec, yspec,
                  const(wg), const(wb), const(wo), const(g), const(b)],
        out_specs=pl.BlockSpec((tm, d), lambda i: (i, 0)),
        out_shape=jax.ShapeDtypeStruct((n, d), F32),
        compiler_params=_params("parallel"),
        name="merge",
    )(x2d, *ys, wg, wb, wo, g, b)


def _moe_kernel(x_ref, ltri_ref, wr_ref, br_ref, wg_ref, wu_ref, wd_ref, g_ref, b_ref, o_ref,
                xs_ref, gs_ref, ys_ref, pt_ref, seg_ref, *, alpha):
    e = pl.program_id(1)
    tm = x_ref.shape[0]
    nrow = xs_ref.shape[0]
    ck = MOE_CHUNK
    lane = _iota((tm, LANES), 1)

    @pl.when(e == 0)
    def _():
        x = x_ref[...]
        xb = x.astype(BF16)
        ys_ref[...] = jnp.zeros_like(ys_ref)
        x_lo = (x - xb.astype(F32)).astype(BF16)
        cross = _dot(jnp.concatenate([xb, x_lo], 0), wr_ref[...])
        logits = (cross[0:tm, 0:LANES] + (cross[0:tm, LANES:] + cross[tm:, 0:LANES])) + br_ref[...]
        is_g = jnp.logical_and(lane >= MOE_EXPERTS, lane < MOE_EXPERTS + MOE_GROUPS)
        lg = jnp.where(is_g, logits, -jnp.inf)
        gmax = jnp.max(lg, -1, keepdims=True)
        ptop = 1.0 / jnp.sum(jnp.exp(lg - gmax), -1, keepdims=True)
        gsel = jnp.min(jnp.where(lg == gmax, lane, 2 * LANES), -1, keepdims=True) - MOE_EXPERTS
        in_group = jnp.logical_and(lane < MOE_EXPERTS, lane // MOE_PER_GROUP == gsel)
        le = jnp.where(in_group, logits, -jnp.inf)
        v1 = jnp.max(le, -1, keepdims=True)
        i1 = jnp.min(jnp.where(le == v1, lane, 2 * LANES), -1, keepdims=True)
        le2 = jnp.where(lane == i1, -jnp.inf, le)
        v2 = jnp.max(le2, -1, keepdims=True)
        i2 = jnp.min(jnp.where(le2 == v2, lane, 2 * LANES), -1, keepdims=True)
        e2 = jnp.exp(v2 - v1)
        w1 = ptop / (1.0 + e2)
        gate = jnp.where(lane == i1, w1, jnp.where(lane == i2, w1 * e2, 0.0))

        onehot = jnp.where(lane == gsel, 1.0, 0.0)
        rank = _dot(ltri_ref[...], onehot.astype(BF16))
        cnt = jnp.sum(onehot, 0, keepdims=True).astype(I32)
        nch = jnp.right_shift(cnt + (ck - 1), int(math.log2(ck)))
        lane1 = _iota((1, LANES), 1)
        off = jnp.int32(0)
        off_vec = jnp.zeros((1, LANES), I32)
        for grp in range(MOE_GROUPS):
            n_g = nch[0, grp]
            seg_ref[grp] = off
            seg_ref[MOE_GROUPS + grp] = n_g
            off_vec = jnp.where(lane1 == grp, off, off_vec)
            off = off + n_g * ck
        dest = jnp.sum(onehot * (rank + off_vec.astype(F32)), -1, keepdims=True)
        pt_ref[...] = jnp.where(_iota((tm, nrow), 1) == dest.astype(I32), 1.0, 0.0).astype(BF16)
        dest_row = jnp.broadcast_to(dest, (tm, LANES)).T[0:1, :].astype(I32)
        perm = jnp.where(_iota((nrow, tm), 0) == dest_row, 1.0, 0.0).astype(BF16)
        gate_hi, gate_lo = _split2(gate)
        moved = _dot(perm, jnp.concatenate([xb, gate_hi, gate_lo], 1))
        d = x_ref.shape[1]
        xs_ref[...] = moved[:, 0:d].astype(BF16)
        gs_ref[...] = moved[:, d:d + LANES] + moved[:, d + LANES:]

    nse = MOE_STEP_EXPERTS
    grp = e // (MOE_PER_GROUP // nse)
    seg_off = seg_ref[grp]
    seg_chunks = seg_ref[MOE_GROUPS + grp]

    def run_experts(start, size):
        rows = pl.ds(pl.multiple_of(start, ck), size)
        xc = xs_ref[rows, :]
        gs = gs_ref[rows, :]
        lane_s = _iota((size, LANES), 1)
        hidden = []
        for k in range(nse):
            gk = jnp.sum(jnp.where(lane_s == nse * e + k, gs, 0.0), -1, keepdims=True)
            hg = _dot(xc, wg_ref[k])
            hu = _dot(xc, wu_ref[k])
            hidden.append((hg * _sigmoid(hg) * hu * gk).astype(BF16))
        wd = wd_ref[...].reshape(nse * MOE_HIDDEN, wd_ref.shape[2])
        ys_ref[rows, :] += _dot(jnp.concatenate(hidden, 1), wd)

    big = 4

    def quad(c, carry):
        run_experts(seg_off + c * (big * ck), big * ck)
        return carry

    lax.fori_loop(0, seg_chunks // big, quad, 0)
    rem = seg_chunks % big
    rem_start = seg_off + (seg_chunks - rem) * ck
    for r in range(1, big):
        @pl.when(rem == r)
        def _(r=r):
            run_experts(rem_start, r * ck)

    @pl.when(e == pl.num_programs(1) - 1)
    def _():
        z = alpha * x_ref[...] + _dot(pt_ref[...], ys_ref[...].astype(BF16))
        o_ref[...] = _layer_norm(z, g_ref[...], b_ref[...])


def _moe(x2d, wr, br, wg, wu, wd, g, b, alpha, tm):
    n, d = x2d.shape
    ne = wg.shape[0]
    hid = wg.shape[2]
    nse = MOE_STEP_EXPERTS
    assert MOE_PER_GROUP % nse == 0 and ne % nse == 0
    nrow = tm + MOE_GROUPS * MOE_CHUNK
    ltri = jnp.tril(jnp.ones((tm, tm), BF16), -1)

    def const(a):
        return pl.BlockSpec(a.shape, lambda i, e: (0, 0))

    return pl.pallas_call(
        functools.partial(_moe_kernel, alpha=alpha),
        grid=(n // tm, ne // nse),
        in_specs=[pl.BlockSpec((tm, d), lambda i, e: (i, 0)), const(ltri), const(wr), const(br),
                  pl.BlockSpec((nse, d, hid), lambda i, e: (e, 0, 0)),
                  pl.BlockSpec((nse, d, hid), lambda i, e: (e, 0, 0)),
                  pl.BlockSpec((nse, hid, d), lambda i, e: (e, 0, 0)),
                  const(g), const(b)],
        out_specs=pl.BlockSpec((tm, d), lambda i, e: (i, 0)),
        out_shape=jax.ShapeDtypeStruct((n, d), F32),
        scratch_shapes=[pltpu.VMEM((nrow, d), BF16), pltpu.VMEM((nrow, LANES), F32),
                        pltpu.VMEM((nrow, d), F32), pltpu.VMEM((tm, nrow), BF16),
                        pltpu.SMEM((2 * MOE_GROUPS,), I32)],
        compiler_params=_params("parallel", "arbitrary"),
        name="moe",
    )(x2d, ltri, wr, br, wg, wu, wd, g, b)


def _t5_bucket(dist):
    exact = RPB_BUCKETS // 2
    d = jnp.maximum(dist, 0)
    df = jnp.maximum(d, 1).astype(F32)
    large = exact + (jnp.log(df / exact) / math.log(RPB_MAX_DIST / exact)
                     * (RPB_BUCKETS - exact)).astype(I32)
    large = jnp.minimum(large, RPB_BUCKETS - 1)
    return jnp.where(d < exact, d, large)


def _rpb_lookup(tab, dist):
    bucket = _t5_bucket(dist)[..., None]
    out = jnp.zeros(dist.shape + (tab.shape[1],), F32)
    for k in range(RPB_BUCKETS):
        out = jnp.where(bucket == k, tab[k], out)
    return out


def _bias_a(tab):
    qi = jnp.arange(A_BLOCK)[:, None]
    kj = jnp.arange(2 * A_BLOCK)[None, :]
    rdist = qi + A_BLOCK - kj
    out = []
    for window, dilation in A_PATTERNS:
        in_band = (rdist >= 0) & (rdist <= window // dilation)
        bias = jnp.transpose(_rpb_lookup(tab, rdist * dilation), (2, 0, 1))
        out.append(jnp.where(in_band[None], bias, NEG))
    return jnp.stack(out).astype(F32)


def _bias_b(tab, nblk):
    qi = jnp.arange(Q_BLOCK)[:, None]
    kj = jnp.arange(Q_BLOCK)[None, :]
    delta = jnp.arange(nblk)[:, None, None] * Q_BLOCK
    bias = _rpb_lookup(tab, delta + qi - kj)
    return jnp.transpose(bias, (0, 3, 2, 1)).astype(F32)


def _in_offsets():
    splits = (("a_q", 256), ("a_k", 256), ("a_v", 256), ("b_q", 256), ("b_ckv", 64), ("b_iq", 256),
              ("b_ik", 32), ("b_iw", 8), ("c_q", 128), ("c_k", 128), ("c_v", 256), ("c_a", 16),
              ("c_g", 256), ("d", 832), ("gate", 4096))
    off, out = 0, {}
    for name, w in splits:
        out[name] = (off, w)
        off += w
    return out


def _proj_columns():
    src = _in_offsets()
    d0 = src["d"][0]
    named = {
        "a_q": np.arange(256) + src["a_q"][0], "a_k": np.arange(256) + src["a_k"][0],
        "a_v": np.arange(256) + src["a_v"][0], "b_q": np.arange(256) + src["b_q"][0],
        "b_iq": np.arange(256) + src["b_iq"][0],
        "b_misc": np.concatenate([np.arange(B_LATENT) + src["b_ckv"][0], np.arange(IDX_DIM) + src["b_ik"][0],
                                  np.arange(IDX_HEADS) + src["b_iw"][0]]),
        "c_v": np.arange(256) + src["c_v"][0], "c_g": np.arange(256) + src["c_g"][0],
        "d_r": np.arange(256) + d0, "d_k": np.arange(256) + d0 + 256, "d_v": np.arange(256) + d0 + 512,
        "c_q": np.arange(128) + src["c_q"][0], "c_k": np.arange(128) + src["c_k"][0],
        "c_a": np.arange(16) + src["c_a"][0], "d_low": np.arange(64) + d0 + 768,
    }
    cols = []
    for name, w in _PIECES:
        c = named[name]
        cols.append(np.concatenate([c, -np.ones(w * LANES - len(c), np.int64)]))
    return np.concatenate(cols)


def _proj_weight(w):
    cols = _proj_columns()
    cuts = [0] + [k for k in range(1, len(cols))
                  if (cols[k] < 0) != (cols[k - 1] < 0) or (cols[k] >= 0 and cols[k] != cols[k - 1] + 1)]
    cuts.append(len(cols))
    parts = []
    for a, b in zip(cuts[:-1], cuts[1:]):
        if cols[a] < 0:
            parts.append(jnp.zeros((w.shape[0], b - a), BF16))
        else:
            parts.append(w[:, int(cols[a]):int(cols[a]) + (b - a)].astype(BF16))
    return jnp.concatenate(parts, 1)


def _pad_rows(w, first, total):
    return jnp.zeros((total, w.shape[1]), w.dtype).at[first:first + w.shape[0]].set(w)


def kernel(x, rpb_table, w_in, b_kv_gain, b_w_uv, c_a_up, c_a_bias, c_norm_gain, d_mu, d_w0, d_w2,
           d_a0, d_a2, d_g2, d_k_k, d_k_a, d_r_k, d_gn_w, d_gn_b, w_branch, w_out, ln_g, ln_b,
           router_g, router_g_bias, router_e, router_e_bias, moe_w_gate, moe_w_up, moe_w_down):
    bsz, seq, d_model = x.shape
    depth = w_in.shape[0]
    n = bsz * seq
    alpha = (2 * depth) ** 0.25
    gate_off = _in_offsets()["gate"][0]
    bias_a = _bias_a(rpb_table[:, :A_HEADS])
    bias_b = _bias_b(rpb_table[:, A_HEADS:], seq // Q_BLOCK)
    tm_proj = math.gcd(n, 512)
    tm_merge = math.gcd(n, 512)
    tm_moe = math.gcd(n, 1024)

    x2d = x.reshape(n, d_model)
    for l in range(depth):
        h = _project(x2d, _proj_weight(w_in[l]), tm_proj, PROJ_WIDTH)

        ya = _mixer_a(h, bias_a, bsz, seq)

        gain4 = jnp.tile(b_kv_gain[l], B_HEADS)[None, :]
        wuv_t = jnp.transpose(b_w_uv[l], (0, 2, 1)).astype(BF16)
        yb = _mixer_b(h, gain4, wuv_t, bias_b, bsz, seq)

        aup_pad = _pad_rows(c_a_up[l], 0, LANES)
        yc = _mixer_c(h, aup_pad, c_a_bias[l][None, :], c_norm_gain[l][None, :], bsz, seq)

        mu = d_mu[l]
        mu3 = mu[:3 * BRANCH_WIDTH].reshape(3, BRANCH_WIDTH)
        mulow = jnp.zeros((1, LANES), F32).at[0, :D_DECAY_RANK + D_ICLR_RANK + D_GATE_RANK].set(
            mu[3 * BRANCH_WIDTH:])
        w2p = _pad_rows(d_w2[l], 0, LANES)
        a2p = _pad_rows(d_a2[l], D_DECAY_RANK, LANES)
        g2p = _pad_rows(d_g2[l], D_DECAY_RANK + D_ICLR_RANK, LANES)
        vec = jnp.stack([d_w0[l], d_a0[l], d_k_k[l], d_k_a[l], d_r_k[l], d_gn_w[l], d_gn_b[l],
                         jnp.zeros_like(d_w0[l])])
        yd = _mixer_d(h, mu3, mulow, w2p, a2p, g2p, vec, bsz, seq)

        wg = w_in[l][:, gate_off:gate_off + N_BRANCHES * d_model].astype(BF16)
        x2d = _merge(x2d, (ya, yb, yc, yd), wg, w_branch[l].astype(BF16), w_out[l].astype(BF16),
                     ln_g[l, 0][None, :], ln_b[l, 0][None, :], alpha, tm_merge)

        wr = jnp.zeros((d_model, LANES), F32)
        wr = wr.at[:, :MOE_EXPERTS].set(router_e[l]).at[:, MOE_EXPERTS:MOE_EXPERTS + MOE_GROUPS].set(
            router_g[l])
        br = jnp.zeros((1, LANES), F32)
        br = br.at[0, :MOE_EXPERTS].set(router_e_bias[l]).at[0, MOE_EXPERTS:MOE_EXPERTS + MOE_GROUPS].set(
            router_g_bias[l])
        wr = jnp.concatenate(_split2(wr), 1)
        x2d = _moe(x2d, wr, br, moe_w_gate[l].astype(BF16), moe_w_up[l].astype(BF16),
                   moe_w_down[l].astype(BF16), ln_g[l, 1][None, :], ln_b[l, 1][None, :], alpha, tm_moe)
    return x2d.reshape(bsz, seq, d_model)
```

```python
import functools
import math

import jax
import jax.numpy as jnp
import numpy as np
from jax import lax
from jax.experimental import pallas as pl
from jax.experimental.pallas import tpu as pltpu

F32 = jnp.float32
BF16 = jnp.bfloat16
I32 = jnp.int32

D_MODEL = 1024
N_BRANCHES = 4
BRANCH_WIDTH = 256
HEAD_DIM = 64
A_HEADS = 4
A_PATTERNS = ((128, 1), (512, 4), (2048, 16))
A_BLOCK = 128
A_UNROLL = 4
B_HEADS = 4
B_LATENT = 64
IDX_HEADS = 8
IDX_DIM = 32
TOPK_MAX = 256
TOPK_DIV = 4
Q_BLOCK = 128
C_HEADS = 4
C_KEY_DIM = 32
C_VAL_DIM = 64
C_GATE_RANK = 16
C_GATE_TAU = 16.0
C_CHUNK = 64
C_UNROLL = 4
D_HEADS = 4
D_HEAD_SIZE = 64
D_DECAY_RANK = 16
D_ICLR_RANK = 16
D_GATE_RANK = 32
D_GN_EPS = 64e-5
D_CHUNK = 64
RPB_BUCKETS = 32
RPB_MAX_DIST = 2048
MOE_GROUPS = 4
MOE_PER_GROUP = 8
MOE_EXPERTS = 32
MOE_HIDDEN = 256
MOE_CHUNK = 128
MOE_STEP_EXPERTS = 4
LN_EPS = 1e-5

LANES = 128
NEG = -1e30
VMEM_LIMIT = 56 * 1024 * 1024

NT = (((1,), (1,)), ((), ()))
TN = (((0,), (0,)), ((), ()))

_PIECES = (
    ("a_q", 2), ("a_k", 2), ("a_v", 2), ("b_q", 2), ("b_iq", 2),
    ("c_v", 2), ("c_g", 2), ("d_r", 2), ("d_k", 2), ("d_v", 2),
    ("c_q", 1), ("c_k", 1), ("b_misc", 1), ("c_a", 1), ("d_low", 1),
)
_PIECE_OFF = {}
_off = 0
for _n, _w in _PIECES:
    _PIECE_OFF[_n] = (_off, _w)
    _off += _w
PROJ_WIDTH = _off * LANES


def _dot(a, b):
    return jnp.dot(a, b, preferred_element_type=F32)


def _dotg(a, b, dims):
    return lax.dot_general(a, b, dims, preferred_element_type=F32)


def _split2(x):
    hi = x.astype(BF16)
    lo = (x - hi.astype(F32)).astype(BF16)
    return hi, lo


def _dot_rhs_exact(x, m):
    hi, lo = _split2(x)
    return _dot(hi, m) + _dot(lo, m)


def _dot_lhs_exact(m, x):
    hi, lo = _split2(x)
    return _dot(m, hi) + _dot(m, lo)


def _dot3(x, w):
    xh, xl = _split2(x)
    wh, wl = _split2(w)
    return _dot(xh, wh) + (_dot(xl, wh) + _dot(xh, wl))


def _sigmoid(x):
    return 1.0 / (1.0 + jnp.exp(-x))


def _softplus(x):
    return jnp.maximum(x, 0.0) + jnp.log(1.0 + jnp.exp(-jnp.abs(x)))


def _iota(shape, axis):
    return lax.broadcasted_iota(I32, shape, axis)


def _group_matrix(n, group, value, dtype=BF16):
    r = _iota((n, n), 0) // group
    c = _iota((n, n), 1) // group
    return jnp.where(r == c, value, 0.0).astype(dtype)


def _layer_norm(z, g, b):
    mu = jnp.mean(z, -1, keepdims=True)
    zc = z - mu
    var = jnp.mean(zc * zc, -1, keepdims=True)
    return zc * lax.rsqrt(var + LN_EPS) * g + b


def _params(*sem):
    return pltpu.CompilerParams(dimension_semantics=sem, vmem_limit_bytes=VMEM_LIMIT)


def _proj_kernel(x_ref, w_ref, o_ref):
    o_ref[...] = _dot(x_ref[...].astype(BF16), w_ref[...])


def _project(x2d, w, tm, tn):
    n, k = x2d.shape
    wt = w.shape[1]
    return pl.pallas_call(
        _proj_kernel,
        grid=(n // tm, wt // tn),
        in_specs=[pl.BlockSpec((tm, k), lambda i, j: (i, 0)),
                  pl.BlockSpec((k, tn), lambda i, j: (0, j))],
        out_specs=pl.BlockSpec((tm, tn), lambda i, j: (i, j)),
        out_shape=jax.ShapeDtypeStruct((n, wt), F32),
        compiler_params=_params("parallel", "parallel"),
        name="proj",
    )(x2d, w)


def _mixer_a_kernel(q0_ref, q1_ref, k0_ref, k1_ref, v0_ref, v1_ref, bias_ref, o_ref, op_ref, lse_ref,
                    *, seq, dils):
    q_refs, k_refs, v_refs = (q0_ref, q1_ref), (k0_ref, k1_ref), (v0_ref, v1_ref)
    ab = A_BLOCK
    lane = _iota((ab, LANES), 1)
    upper = lane >= HEAD_DIM
    col = _iota((ab, 2 * ab), 1)
    scale = HEAD_DIM ** -0.5

    for p, d in enumerate(dils):
        per_res = seq // (d * ab)

        def body(idx, carry, p=p, d=d, per_res=per_res):
            r = idx // per_res
            b = idx % per_res
            start = b * (ab * d) + r
            pstart = jnp.maximum(start - ab * d, r)
            if d == 1:
                rows, prow = pl.ds(pl.multiple_of(start, ab), ab), pl.ds(pl.multiple_of(pstart, ab), ab)
            else:
                rows, prow = pl.ds(start, ab, stride=d), pl.ds(pstart, ab, stride=d)
            first_pen = jnp.where(b == 0, NEG, 0.0)
            pen = jnp.where(col < ab, first_pen, 0.0)
            for hp in range(2):
                q_ref, k_ref, v_ref = q_refs[hp], k_refs[hp], v_refs[hp]
                q = q_ref[rows, :]
                kb = jnp.concatenate([k_ref[prow, :], k_ref[rows, :]], 0).astype(BF16)
                vb = jnp.concatenate([v_ref[prow, :], v_ref[rows, :]], 0).astype(BF16)
                outs, lses = [], []
                for hh in range(2):
                    hm = upper if hh else jnp.logical_not(upper)
                    qm = jnp.where(hm, q, 0.0).astype(BF16)
                    s = _dotg(qm, kb, NT) * scale + bias_ref[p, 2 * hp + hh] + pen
                    mx = jnp.max(s, -1, keepdims=True)
                    e = jnp.exp(s - mx)
                    den = jnp.sum(e, -1, keepdims=True)
                    outs.append(_dot(e.astype(BF16), vb) / den)
                    lses.append(mx + jnp.log(den))
                op_ref[2 * p + hp, rows, :] = jnp.where(upper, outs[1], outs[0])
                lse_ref[2 * p + hp, rows, :] = jnp.where(upper, lses[1], lses[0])
            return carry

        def body_group(t, carry, body=body):
            for k in range(A_UNROLL):
                carry = body(A_UNROLL * t + k, carry)
            return carry

        lax.fori_loop(0, seq // (A_UNROLL * ab), body_group, 0)

    cr = 256

    def combine(c, carry):
        rows = pl.ds(pl.multiple_of(c * cr, cr), cr)
        for hp in range(2):
            ls = [lse_ref[2 * p + hp, rows, :] for p in range(len(dils))]
            mx = functools.reduce(jnp.maximum, ls)
            es = [jnp.exp(l - mx) for l in ls]
            num = functools.reduce(lambda a, b: a + b,
                                   [e * op_ref[2 * p + hp, rows, :] for p, e in enumerate(es)])
            o_ref[rows, hp * LANES:(hp + 1) * LANES] = num / functools.reduce(lambda a, b: a + b, es)
        return carry

    lax.fori_loop(0, seq // cr, combine, 0)


def _mixer_a(h, bias, bsz, seq):
    dils = tuple(d for _, d in A_PATTERNS)
    for w, d in A_PATTERNS:
        assert w // d == A_BLOCK and seq % (d * A_BLOCK) == 0
    npat = len(dils)

    def pieces(name):
        off, w = _PIECE_OFF[name]
        return [pl.BlockSpec((seq, LANES), lambda b, o=off + u: (b, o)) for u in range(w)]

    return pl.pallas_call(
        functools.partial(_mixer_a_kernel, seq=seq, dils=dils),
        grid=(bsz,),
        in_specs=pieces("a_q") + pieces("a_k") + pieces("a_v")
        + [pl.BlockSpec(bias.shape, lambda b: (0, 0, 0, 0))],
        out_specs=pl.BlockSpec((seq, BRANCH_WIDTH), lambda b: (b, 0)),
        out_shape=jax.ShapeDtypeStruct((bsz * seq, BRANCH_WIDTH), F32),
        scratch_shapes=[pltpu.VMEM((2 * npat, seq, LANES), F32),
                        pltpu.VMEM((2 * npat, seq, LANES), F32)],
        compiler_params=_params("parallel"),
        name="mixer_a",
    )(h, h, h, h, h, h, bias)


def _mixer_b_kernel(*refs, seq, keep):
    def query_block(i, carry):
        _mixer_b_block(i, *refs, seq=seq, keep=keep)
        return carry

    lax.fori_loop(0, seq // Q_BLOCK, query_block, 0)


def _mixer_b_block(i, q_ref, iq_ref, mk_ref, gain_ref, wuvt_ref, bias_ref, o_ref,
                   ckvn_ref, ckvt_ref, ikb_ref, sc_ref, sc16_ref, lg_ref, iqs_ref, qs_ref, *, seq, keep):
    qb = Q_BLOCK
    sb = 2 * qb
    iw_lane = B_LATENT + IDX_DIM

    @pl.when(i == 0)
    def _():
        src = _iota((LANES, BRANCH_WIDTH), 0)
        dst = _iota((LANES, BRANCH_WIDTH), 1)
        is_lat = src < B_LATENT
        rep_lat = jnp.where(jnp.logical_and(is_lat, dst % B_LATENT == src), 1.0, 0.0).astype(BF16)
        avg_lat = jnp.where(is_lat, 1.0 / B_LATENT, 0.0).astype(BF16)
        is_ik = jnp.logical_and(src >= B_LATENT, src < iw_lane)
        rep_ik = jnp.where(jnp.logical_and(is_ik, dst % IDX_DIM == src - B_LATENT), 1.0, 0.0).astype(BF16)

        def prep(c, carry):
            rows = pl.ds(pl.multiple_of(c * sb, sb), sb)
            x = mk_ref[rows, :]
            hi, lo = _split2(x)
            lo2 = (x - hi.astype(F32) - lo.astype(F32)).astype(BF16)
            lat = _dot(hi, rep_lat) + (_dot(lo, rep_lat) + _dot(lo2, rep_lat))
            ms = _dot_rhs_exact(x * x, avg_lat)
            xn = lat * lax.rsqrt(ms + 1e-6) * gain_ref[...]
            ckvn_ref[rows, :] = xn.astype(BF16)
            ckvt_ref[c] = xn.T[0:B_LATENT, :].astype(BF16)
            ikb_ref[rows, :] = _dot(hi, rep_ik).astype(BF16)
            return carry

        lax.fori_loop(0, seq // sb, prep, 0)

    lane = _iota((qb, BRANCH_WIDTH), 1)
    qrows = pl.ds(pl.multiple_of(i * qb, qb), qb)
    iq = iq_ref[qrows, :] * (IDX_DIM ** -0.5)
    for h in range(IDX_HEADS):
        iqs_ref[h * qb:(h + 1) * qb, :] = jnp.where(lane // IDX_DIM == h, iq, 0.0).astype(BF16)
    q = q_ref[qrows, :] * (B_LATENT ** -0.5)
    for h in range(B_HEADS):
        qs_ref[h * qb:(h + 1) * qb, :] = jnp.where(lane // B_LATENT == h, q, 0.0).astype(BF16)
    iw_t = (mk_ref[qrows, :] * (IDX_HEADS ** -0.5)).T[iw_lane:iw_lane + IDX_HEADS, :]
    krow = _iota((sb, qb), 0)
    qcol = _iota((sb, qb), 1)
    nsb = (i + 2) // 2

    def fold(x, op):
        parts = [x[r:r + 8] for r in range(0, sb, 8)]
        while len(parts) > 1:
            parts = [op(parts[k], parts[k + 1]) for k in range(0, len(parts), 2)]
        return parts[0]

    def for_steps(body, init):
        def pair(p, carry):
            return body(2 * p + 1, body(2 * p, carry))

        carry = lax.fori_loop(0, nsb // 2, pair, init)
        return lax.cond(nsb % 2 == 1, lambda c: body(nsb - 1, c), lambda c: c, carry)

    def score_body(j, carry):
        kr = pl.ds(pl.multiple_of(j * sb, sb), sb)
        rel = _dotg(ikb_ref[kr, :], iqs_ref[...], NT)
        sc = jnp.maximum(rel[:, 0:qb], 0.0) * iw_t[0:1, :]
        for h in range(1, IDX_HEADS):
            sc = sc + jnp.maximum(rel[:, h * qb:(h + 1) * qb], 0.0) * iw_t[h:h + 1, :]
        sc = jnp.where(krow <= qcol + (i * qb - j * sb), sc, -jnp.inf)
        sc_ref[j] = sc
        sc16_ref[j] = pltpu.bitcast(pltpu.bitcast(sc, I32) & jnp.int32(-65536), F32).astype(BF16)
        logits = _dotg(ckvn_ref[kr, :], qs_ref[...], NT)
        d0 = i - 2 * j
        for h in range(B_HEADS):
            lg = logits[:, h * qb:(h + 1) * qb]
            lg_ref[h, j, 0:qb, :] = lg[0:qb] + bias_ref[d0, h]
            lg_ref[h, j, qb:sb, :] = lg[qb:sb] + bias_ref[jnp.maximum(d0 - 1, 0), h]
        return carry

    for_steps(score_body, 0)

    sign = jnp.int32(-2 ** 31)

    def key_to_float(u):
        k = u ^ sign
        bits = k ^ ((k >> 31) & jnp.int32(0x7FFFFFFF))
        return pltpu.bitcast(bits, F32)

    def count(pred):
        def cnt(j, acc):
            return acc + fold(pred(j), jnp.add)
        acc = for_steps(cnt, jnp.zeros((8, qb), F32))
        return jnp.sum(acc, 0, keepdims=True)

    pk = 16

    def count_ge16(cf):
        cf = pltpu.bitcast(pltpu.bitcast(cf, I32) & jnp.int32(-65536), F32)
        c16 = jnp.broadcast_to(cf, (pk, qb)).astype(BF16)
        one, zero = jnp.ones((pk, qb), BF16), jnp.zeros((pk, qb), BF16)

        def cnt(j, acc):
            x = sc16_ref[j]
            parts = [jnp.where(x[r:r + pk] >= c16, one, zero) for r in range(0, sb, pk)]
            while len(parts) > 1:
                parts = [parts[k] + parts[k + 1] for k in range(0, len(parts), 2)]
            return acc + parts[0].astype(F32)

        acc = for_steps(cnt, jnp.zeros((pk, qb), F32))
        return jnp.sum(acc, 0, keepdims=True)

    def bit_body(t, state, packed=False):
        u, n_u = state
        cand = u | jnp.left_shift(jnp.int32(1), 31 - t)
        cf = key_to_float(cand)
        c = count_ge16(cf) if packed else count(lambda j: jnp.where(sc_ref[j] >= cf, 1.0, 0.0))
        take = c >= keep
        return jnp.where(take, cand, u), jnp.where(take, c, n_u)

    def bit_body16(t, state):
        return bit_body(t, state, packed=True)

    first_bits, bit_group = 24, 4
    state = (jnp.zeros((1, qb), I32), (jnp.zeros((1, qb), I32) + nsb * sb).astype(F32))
    state = bit_body16(0, state)
    n_pos = count(lambda j: jnp.where(sc_ref[j] > 0.0, 1.0, 0.0))
    zero_thr = jnp.logical_and(state[0] != 0, n_pos < keep)
    state = lax.fori_loop(1, 16, bit_body16, state)
    u, n_ge = lax.fori_loop(16, first_bits, bit_body, state)

    def more_bits(s):
        t, _, n_u = s
        open_rows = jnp.where(zero_thr, 0.0, jnp.abs(n_u - keep))
        return jnp.logical_and(t < 32, jnp.max(open_rows) > 0.0)

    def bit_group_body(s):
        t, u, n_u = s
        for k in range(bit_group):
            u, n_u = bit_body(t + k, (u, n_u))
        return t + bit_group, u, n_u

    _, u, n_ge = lax.while_loop(more_bits, bit_group_body, (jnp.int32(first_bits), u, n_ge))
    u = jnp.maximum(u ^ sign, jnp.int32(0x007FFFFF - 2 ** 31)) ^ sign
    thr = key_to_float(u)
    n_gt = count(lambda j: jnp.where(sc_ref[j] > thr, 1.0, 0.0))
    need = keep - n_gt

    nbits = int(math.ceil(math.log2(seq))) + 1

    def cut_search():
        def cut_body(t, cut):
            cand = cut | jnp.left_shift(jnp.int32(1), nbits - 1 - t)
            c = count(lambda j: jnp.where(sc_ref[j] == thr,
                                          jnp.where(krow < cand - j * sb, 1.0, 0.0), 0.0))
            return jnp.where(c <= need, cand, cut)

        return lax.fori_loop(0, nbits, cut_body, jnp.zeros((1, qb), I32))

    surplus = jnp.max(n_ge) > keep
    cut = lax.cond(surplus, cut_search, lambda: jnp.full((1, qb), 2 ** nbits - 1, I32))
    cut = jnp.minimum(cut, i * qb + 1 + _iota((1, qb), 1))

    def mask_body(j, mx):
        sc = sc_ref[j]
        sel = jnp.where(sc > thr, 1.0, jnp.where(sc == thr, jnp.where(krow < cut - j * sb, 1.0, 0.0), 0.0))
        out = []
        for h in range(B_HEADS):
            s = jnp.where(sel > 0.5, lg_ref[h, j], NEG)
            lg_ref[h, j] = s
            out.append(jnp.maximum(mx[h], fold(s, jnp.maximum)))
        return tuple(out)

    mx = for_steps(mask_body, tuple(jnp.full((8, qb), NEG, F32) for _ in range(B_HEADS)))
    ms = [jnp.max(m, 0, keepdims=True) for m in mx]

    def att_body(j, carry):
        ls, acc = carry
        ps, new_ls = [], []
        for h in range(B_HEADS):
            pr = jnp.exp(lg_ref[h, j] - ms[h])
            new_ls.append(ls[h] + fold(pr, jnp.add))
            ps.append(pr.astype(BF16))
        upd = _dot(ckvt_ref[j], jnp.concatenate(ps, 1))
        return tuple(new_ls), acc + upd

    init = (tuple(jnp.zeros((8, qb), F32) for _ in range(B_HEADS)),
            jnp.zeros((B_LATENT, B_HEADS * qb), F32))
    ls, acc = for_steps(att_body, init)
    ls = [jnp.sum(l, 0, keepdims=True) for l in ls]
    o_t = (acc / jnp.concatenate(ls, 1)).astype(BF16)
    y_t = jnp.concatenate([_dot(wuvt_ref[h], o_t[:, h * qb:(h + 1) * qb]) for h in range(B_HEADS)], 0)
    o_ref[qrows, :] = y_t.T


def _mixer_b(h, gain4, wuv_t, bias, bsz, seq):
    qb = Q_BLOCK
    assert seq % 256 == 0
    nblk = seq // qb
    keep = min(TOPK_MAX, seq // TOPK_DIV)

    def kpiece(name):
        off, w = _PIECE_OFF[name]
        return pl.BlockSpec((seq, w * LANES), lambda b, o=off // w: (b, o))

    return pl.pallas_call(
        functools.partial(_mixer_b_kernel, seq=seq, keep=keep),
        grid=(bsz,),
        in_specs=[kpiece("b_q"), kpiece("b_iq"), kpiece("b_misc"),
                  pl.BlockSpec(gain4.shape, lambda b: (0, 0)),
                  pl.BlockSpec(wuv_t.shape, lambda b: (0, 0, 0)),
                  pl.BlockSpec(bias.shape, lambda b: (0, 0, 0, 0))],
        out_specs=pl.BlockSpec((seq, BRANCH_WIDTH), lambda b: (b, 0)),
        out_shape=jax.ShapeDtypeStruct((bsz * seq, BRANCH_WIDTH), F32),
        scratch_shapes=[pltpu.VMEM((seq, BRANCH_WIDTH), BF16),
                        pltpu.VMEM((nblk // 2, B_LATENT, 2 * qb), BF16),
                        pltpu.VMEM((seq, BRANCH_WIDTH), BF16),
                        pltpu.VMEM((nblk // 2, 2 * qb, qb), F32),
                        pltpu.VMEM((nblk // 2, 2 * qb, qb), BF16),
                        pltpu.VMEM((B_HEADS, nblk // 2, 2 * qb, qb), F32),
                        pltpu.VMEM((IDX_HEADS * qb, BRANCH_WIDTH), BF16),
                        pltpu.VMEM((B_HEADS * qb, BRANCH_WIDTH), BF16)],
        compiler_params=_params("parallel"),
        name="mixer_b",
    )(h, h, h, gain4, wuv_t, bias)


def _mixer_c_kernel(q_ref, k_ref, v_ref, g_ref, a_ref, aup_ref, abias_ref, gain_ref, o_ref, *, seq):
    ch = C_CHUNK
    kw = C_HEADS * C_KEY_DIM
    vw = C_HEADS * C_VAL_DIM
    tri = jnp.where(_iota((ch, ch), 1) <= _iota((ch, ch), 0), 1.0, 0.0).astype(BF16)
    causal = _iota((C_HEADS * ch, ch), 1) <= (_iota((C_HEADS * ch, ch), 0) % ch)
    klane_head = _iota((C_HEADS * ch, kw), 1) // C_KEY_DIM
    krow_head = _iota((C_HEADS * ch, kw), 0) // ch
    vlane_head = _iota((ch, vw), 1) // C_VAL_DIM
    st_mask = (_iota((vw, kw), 0) // C_VAL_DIM) == (_iota((vw, kw), 1) // C_KEY_DIM)
    gmat = _group_matrix(vw, C_VAL_DIM, 1.0 / C_VAL_DIM)
    aup = aup_ref[...]

    def body(c, st):
        rows = pl.ds(pl.multiple_of(c * ch, ch), ch)
        qc = q_ref[rows, :] * (C_KEY_DIM ** -0.5)
        kc = k_ref[rows, :]
        vc = v_ref[rows, :]
        z = _dot3(a_ref[rows, :], aup) + abias_ref[...]
        log_a = -_softplus(-z) / C_GATE_TAU
        cum = _dot_lhs_exact(tri, log_a)
        last = cum[ch - 1:ch, :]
        q_dec = qc * jnp.exp(cum)
        k_inv = (kc * jnp.exp(-cum)).astype(BF16)
        k_dec = (kc * jnp.exp(last - cum)).astype(BF16)
        vb = vc.astype(BF16)
        qd_b = q_dec.astype(BF16)
        q_stack = jnp.where(klane_head == krow_head, jnp.concatenate([q_dec] * C_HEADS, 0), 0.0)
        att = jnp.where(causal, _dotg(q_stack.astype(BF16), k_inv, NT), 0.0)
        full = _dot(att.astype(BF16), vb)
        o = _dotg(qd_b, st.astype(BF16), NT)
        for h in range(C_HEADS):
            o = o + jnp.where(vlane_head == h, full[h * ch:(h + 1) * ch], 0.0)
        upd = _dotg(vb, k_dec, TN)
        st = st * jnp.exp(last) + jnp.where(st_mask, upd, 0.0)
        ms = _dot_rhs_exact(o * o, gmat)
        o = o * lax.rsqrt(ms + 1e-6) * gain_ref[...]
        g = g_ref[rows, :]
        o_ref[rows, :] = g * _sigmoid(g) * o
        return st

    def body_group(t, st):
        for k in range(C_UNROLL):
            st = body(C_UNROLL * t + k, st)
        return st

    lax.fori_loop(0, seq // (C_UNROLL * ch), body_group, jnp.zeros((vw, kw), F32))


def _mixer_c(h, aup_pad, abias, gain, bsz, seq):
    assert seq % C_CHUNK == 0

    def piece(name):
        off, w = _PIECE_OFF[name]
        return pl.BlockSpec((seq, w * LANES), lambda b, o=off // w: (b, o))

    def const(a):
        return pl.BlockSpec(a.shape, lambda b: (0, 0))

    return pl.pallas_call(
        functools.partial(_mixer_c_kernel, seq=seq),
        grid=(bsz,),
        in_specs=[piece("c_q"), piece("c_k"), piece("c_v"), piece("c_g"), piece("c_a"),
                  const(aup_pad), const(abias), const(gain)],
        out_specs=pl.BlockSpec((seq, BRANCH_WIDTH), lambda b: (b, 0)),
        out_shape=jax.ShapeDtypeStruct((bsz * seq, BRANCH_WIDTH), F32),
        compiler_params=_params("parallel"),
        name="mixer_c",
    )(h, h, h, h, h, aup_pad, abias, gain)


def _mixer_d_kernel(r_ref, k_ref, v_ref, low_ref, mu_ref, mulow_ref, w2_ref, a2_ref, g2_ref, vec_ref,
                    o_ref, *, seq, nsub):
    ch = D_CHUNK
    bw = BRANCH_WIDTH
    nh = D_HEADS
    hs = D_HEAD_SIZE
    r_i = _iota((bw, bw), 0)
    c_i = _iota((bw, bw), 1)
    same = (r_i // hs) == (c_i // hs)
    strict = c_i < r_i
    incl = c_i <= r_i
    eye = r_i == c_i
    ones_bd = _group_matrix(bw, hs, 1.0)
    avg_bd = _group_matrix(bw, hs, 1.0 / hs)
    rows = nsub * ch
    tr_i = _iota((rows, rows), 0)
    tc_i = _iota((rows, rows), 1)
    tri = jnp.where(jnp.logical_and(tr_i // ch == tc_i // ch, tc_i <= tr_i), 1.0, 0.0).astype(BF16)
    row0 = _iota((rows, bw), 0) == 0
    row0_low = _iota((rows, LANES), 0) == 0
    vec = vec_ref[...]
    w0, a0, k_k, k_a, r_k, gn_w, gn_b = (vec[n:n + 1, :] for n in range(7))
    mu = mu_ref[...]
    w2, a2, g2 = w2_ref[...], a2_ref[...], g2_ref[...]

    def wide(x):
        return jnp.where(same, jnp.concatenate([x] * nh, 0), 0.0)

    def shifted(ref, start, first_row, m):
        cur = ref[pl.ds(start, rows), :]
        last8 = ref[pl.ds(pl.multiple_of(jnp.maximum(start - 8, 0), 8), 8), :]
        prev_row = last8[7:8, :] * jnp.where(start > 0, 1.0, 0.0)
        prev = jnp.where(first_row, prev_row, pltpu.roll(cur, 1, 0))
        return cur + (prev - cur) * m

    def body(c, st):
        start = pl.multiple_of(c * rows, rows)
        r = shifted(r_ref, start, row0, mu[0:1, :])
        k = shifted(k_ref, start, row0, mu[1:2, :])
        v = shifted(v_ref, start, row0, mu[2:3, :])
        low = shifted(low_ref, start, row0_low, mulow_ref[...])
        w_raw = -_softplus(-(w0 + _dot3(jnp.tanh(low), w2))) - 0.5
        lw = -jnp.exp(w_raw)
        a = _sigmoid(a0 + _dot3(low, a2))
        g = _dot3(_sigmoid(low), g2)
        kk = k * k_k
        kk = kk / jnp.maximum(jnp.sqrt(_dot_rhs_exact(kk * kk, ones_bd)), 1e-12)
        k2 = k * (1.0 + (a - 1.0) * k_a)
        bonus = _dot_rhs_exact(r * k2 * r_k, ones_bd) * v
        b = kk * a

        cum = _dot_lhs_exact(tri, lw)
        e_in = jnp.exp(cum)
        e_inv = jnp.exp(-cum)
        a_dec = -kk * jnp.exp(cum - lw)
        r_dec = r * e_in
        b_inv = b * e_inv
        k_inv = k2 * e_inv
        local = [chunk_terms(*(z[s * ch:(s + 1) * ch] for z in (a_dec, r_dec, b_inv, k_inv, v, e_in)))
                 for s in range(nsub)]
        ys = []
        for atp, vp, xr, m_rb, y_kv, bh, h_kv, g_col in local:
            st16 = st.astype(BF16)
            u16 = (_dot(atp, st16) + vp).astype(BF16)
            y_w = _dot(xr, st16) + _dot(m_rb, u16) + y_kv
            st = g_col * st + _dotg(bh, u16, TN) + h_kv
            y = y_w[0:ch]
            for h in range(1, nh):
                y = y + y_w[h * ch:(h + 1) * ch]
            ys.append(y)
        y = jnp.concatenate(ys, 0)
        mean = _dot_rhs_exact(y, avg_bd)
        yc = y - mean
        var = _dot_rhs_exact(yc * yc, avg_bd)
        yn = yc * lax.rsqrt(var + D_GN_EPS) * gn_w + gn_b
        o_ref[pl.ds(start, rows), :] = (yn + bonus) * g
        return st

    def chunk_terms(a_dec, r_dec, b_inv, k_inv, v, e_in):
        g_last = e_in[ch - 1:ch, :]
        xa = wide(a_dec).astype(BF16)
        xr = wide(r_dec).astype(BF16)
        yb = wide(b_inv).astype(BF16)
        yk = wide(k_inv).astype(BF16)
        bh = wide(b_inv * g_last).astype(BF16)
        kh = wide(k_inv * g_last).astype(BF16)
        vw = wide(v).astype(BF16)

        a_ab = jnp.where(strict, _dotg(xa, yb, NT), 0.0)
        a_ak = jnp.where(strict, _dotg(xa, yk, NT), 0.0).astype(BF16)
        m_rb = jnp.where(incl, _dotg(xr, yb, NT), 0.0).astype(BF16)
        m_rk = jnp.where(incl, _dotg(xr, yk, NT), 0.0).astype(BF16)

        t_inv = jnp.where(eye, 1.0, a_ab)
        apow = a_ab
        for _ in range(int(math.log2(ch)) - 1):
            ab16 = apow.astype(BF16)
            apow = _dot(ab16, ab16)
            t_inv = t_inv + _dot(t_inv.astype(BF16), apow.astype(BF16))
        t16 = t_inv.astype(BF16)

        atp = _dot(t16, xa).astype(BF16)
        vp = _dot(t16, _dot(a_ak, vw).astype(BF16))
        y_kv = _dot(m_rk, vw)
        h_kv = _dotg(kh, vw, TN)
        g_col = jnp.sum(jnp.where(eye, g_last, 0.0), -1, keepdims=True)
        return atp, vp, xr, m_rb, y_kv, bh, h_kv, g_col

    lax.fori_loop(0, seq // rows, body, jnp.zeros((bw, bw), F32))


def _mixer_d(h, mu3, mulow, w2p, a2p, g2p, vec, bsz, seq, nsub=4):
    assert seq % (nsub * D_CHUNK) == 0 and D_CHUNK == D_HEAD_SIZE

    def piece(name):
        off, w = _PIECE_OFF[name]
        return pl.BlockSpec((seq, w * LANES), lambda b, o=off // w: (b, o))

    def const(a):
        return pl.BlockSpec(a.shape, lambda b: (0, 0))

    return pl.pallas_call(
        functools.partial(_mixer_d_kernel, seq=seq, nsub=nsub),
        grid=(bsz,),
        in_specs=[piece("d_r"), piece("d_k"), piece("d_v"), piece("d_low"),
                  const(mu3), const(mulow), const(w2p), const(a2p), const(g2p), const(vec)],
        out_specs=pl.BlockSpec((seq, BRANCH_WIDTH), lambda b: (b, 0)),
        out_shape=jax.ShapeDtypeStruct((bsz * seq, BRANCH_WIDTH), F32),
        compiler_params=_params("parallel"),
        name="mixer_d",
    )(h, h, h, h, mu3, mulow, w2p, a2p, g2p, vec)


def _merge_kernel(x_ref, ya_ref, yb_ref, yc_ref, yd_ref, wg_ref, wb_ref, wo_ref, g_ref, b_ref, o_ref,
                  *, alpha):
    x = x_ref[...]
    xb = x.astype(BF16)
    merged = None
    for n, y_ref in enumerate((ya_ref, yb_ref, yc_ref, yd_ref)):
        gate = _sigmoid(_dot(xb, wg_ref[:, n * D_MODEL:(n + 1) * D_MODEL]))
        term = gate * _dot(y_ref[...].astype(BF16), wb_ref[n])
        merged = term if merged is None else merged + term
    z = alpha * x + _dot(merged.astype(BF16), wo_ref[...])
    o_ref[...] = _layer_norm(z, g_ref[...], b_ref[...])


def _merge(x2d, ys, wg, wb, wo, g, b, alpha, tm):
    n, d = x2d.shape

    def const(a):
        nd = a.ndim
        return pl.BlockSpec(a.shape, lambda i: (0,) * nd, pipeline_mode=pl.Buffered(1))

    yspec = pl.BlockSpec((tm, BRANCH_WIDTH), lambda i: (i, 0))
    return pl.pallas_call(
        functools.partial(_merge_kernel, alpha=alpha),
        grid=(n // tm,),
        in_specs=[pl.BlockSpec((tm, d), lambda i: (i, 0)), yspec, yspec, yspec, yspec,
                  const(wg), const(wb), const(wo), const(g), const(b)],
        out_specs=pl.BlockSpec((tm, d), lambda i: (i, 0)),
        out_shape=jax.ShapeDtypeStruct((n, d), F32),
        compiler_params=_params("parallel"),
        name="merge",
    )(x2d, *ys, wg, wb, wo, g, b)


def _moe_kernel(x_ref, ltri_ref, wr_ref, br_ref, wg_ref, wu_ref, wd_ref, g_ref, b_ref, o_ref,
                xs_ref, gs_ref, ys_ref, pt_ref, seg_ref, *, alpha):
    e = pl.program_id(1)
    tm = x_ref.shape[0]
    nrow = xs_ref.shape[0]
    ck = MOE_CHUNK
    lane = _iota((tm, LANES), 1)

    @pl.when(e == 0)
    def _():
        x = x_ref[...]
        xb = x.astype(BF16)
        ys_ref[...] = jnp.zeros_like(ys_ref)
        x_lo = (x - xb.astype(F32)).astype(BF16)
        cross = _dot(jnp.concatenate([xb, x_lo], 0), wr_ref[...])
        logits = (cross[0:tm, 0:LANES] + (cross[0:tm, LANES:] + cross[tm:, 0:LANES])) + br_ref[...]
        is_g = jnp.logical_and(lane >= MOE_EXPERTS, lane < MOE_EXPERTS + MOE_GROUPS)
        lg = jnp.where(is_g, logits, -jnp.inf)
        gmax = jnp.max(lg, -1, keepdims=True)
        ptop = 1.0 / jnp.sum(jnp.exp(lg - gmax), -1, keepdims=True)
        gsel = jnp.min(jnp.where(lg == gmax, lane, 2 * LANES), -1, keepdims=True) - MOE_EXPERTS
        in_group = jnp.logical_and(lane < MOE_EXPERTS, lane // MOE_PER_GROUP == gsel)
        le = jnp.where(in_group, logits, -jnp.inf)
        v1 = jnp.max(le, -1, keepdims=True)
        i1 = jnp.min(jnp.where(le == v1, lane, 2 * LANES), -1, keepdims=True)
        le2 = jnp.where(lane == i1, -jnp.inf, le)
        v2 = jnp.max(le2, -1, keepdims=True)
        i2 = jnp.min(jnp.where(le2 == v2, lane, 2 * LANES), -1, keepdims=True)
        e2 = jnp.exp(v2 - v1)
        w1 = ptop / (1.0 + e2)
        gate = jnp.where(lane == i1, w1, jnp.where(lane == i2, w1 * e2, 0.0))

        onehot = jnp.where(lane == gsel, 1.0, 0.0)
        rank = _dot(ltri_ref[...], onehot.astype(BF16))
        cnt = jnp.sum(onehot, 0, keepdims=True).astype(I32)
        nch = jnp.right_shift(cnt + (ck - 1), int(math.log2(ck)))
        lane1 = _iota((1, LANES), 1)
        off = jnp.int32(0)
        off_vec = jnp.zeros((1, LANES), I32)
        for grp in range(MOE_GROUPS):
            n_g = nch[0, grp]
            seg_ref[grp] = off
            seg_ref[MOE_GROUPS + grp] = n_g
            off_vec = jnp.where(lane1 == grp, off, off_vec)
            off = off + n_g * ck
        dest = jnp.sum(onehot * (rank + off_vec.astype(F32)), -1, keepdims=True)
        pt_ref[...] = jnp.where(_iota((tm, nrow), 1) == dest.astype(I32), 1.0, 0.0).astype(BF16)
        dest_row = jnp.broadcast_to(dest, (tm, LANES)).T[0:1, :].astype(I32)
        perm = jnp.where(_iota((nrow, tm), 0) == dest_row, 1.0, 0.0).astype(BF16)
        gate_hi, gate_lo = _split2(gate)
        moved = _dot(perm, jnp.concatenate([xb, gate_hi, gate_lo], 1))
        d = x_ref.shape[1]
        xs_ref[...] = moved[:, 0:d].astype(BF16)
        gs_ref[...] = moved[:, d:d + LANES] + moved[:, d + LANES:]

    nse = MOE_STEP_EXPERTS
    grp = e // (MOE_PER_GROUP // nse)
    seg_off = seg_ref[grp]
    seg_chunks = seg_ref[MOE_GROUPS + grp]

    def run_experts(start, size):
        rows = pl.ds(pl.multiple_of(start, ck), size)
        xc = xs_ref[rows, :]
        gs = gs_ref[rows, :]
        lane_s = _iota((size, LANES), 1)
        hidden = []
        for k in range(nse):
            gk = jnp.sum(jnp.where(lane_s == nse * e + k, gs, 0.0), -1, keepdims=True)
            hg = _dot(xc, wg_ref[k])
            hu = _dot(xc, wu_ref[k])
            hidden.append((hg * _sigmoid(hg) * hu * gk).astype(BF16))
        wd = wd_ref[...].reshape(nse * MOE_HIDDEN, wd_ref.shape[2])
        ys_ref[rows, :] += _dot(jnp.concatenate(hidden, 1), wd)

    big = 4

    def quad(c, carry):
        run_experts(seg_off + c * (big * ck), big * ck)
        return carry

    lax.fori_loop(0, seg_chunks // big, quad, 0)
    rem = seg_chunks % big
    rem_start = seg_off + (seg_chunks - rem) * ck
    for r in range(1, big):
        @pl.when(rem == r)
        def _(r=r):
            run_experts(rem_start, r * ck)

    @pl.when(e == pl.num_programs(1) - 1)
    def _():
        z = alpha * x_ref[...] + _dot(pt_ref[...], ys_ref[...].astype(BF16))
        o_ref[...] = _layer_norm(z, g_ref[...], b_ref[...])


def _moe(x2d, wr, br, wg, wu, wd, g, b, alpha, tm):
    n, d = x2d.shape
    ne = wg.shape[0]
    hid = wg.shape[2]
    nse = MOE_STEP_EXPERTS
    assert MOE_PER_GROUP % nse == 0 and ne % nse == 0
    nrow = tm + MOE_GROUPS * MOE_CHUNK
    ltri = jnp.tril(jnp.ones((tm, tm), BF16), -1)

    def const(a):
        return pl.BlockSpec(a.shape, lambda i, e: (0, 0), pipeline_mode=pl.Buffered(1))

    return pl.pallas_call(
        functools.partial(_moe_kernel, alpha=alpha),
        grid=(n // tm, ne // nse),
        in_specs=[pl.BlockSpec((tm, d), lambda i, e: (i, 0)), const(ltri), const(wr), const(br),
                  pl.BlockSpec((nse, d, hid), lambda i, e: (e, 0, 0)),
                  pl.BlockSpec((nse, d, hid), lambda i, e: (e, 0, 0)),
                  pl.BlockSpec((nse, hid, d), lambda i, e: (e, 0, 0)),
                  const(g), const(b)],
        out_specs=pl.BlockSpec((tm, d), lambda i, e: (i, 0)),
        out_shape=jax.ShapeDtypeStruct((n, d), F32),
        scratch_shapes=[pltpu.VMEM((nrow, d), BF16), pltpu.VMEM((nrow, LANES), F32),
                        pltpu.VMEM((nrow, d), F32), pltpu.VMEM((tm, nrow), BF16),
                        pltpu.SMEM((2 * MOE_GROUPS,), I32)],
        compiler_params=_params("parallel", "arbitrary"),
        name="moe",
    )(x2d, ltri, wr, br, wg, wu, wd, g, b)


def _t5_bucket(dist):
    exact = RPB_BUCKETS // 2
    d = jnp.maximum(dist, 0)
    df = jnp.maximum(d, 1).astype(F32)
    large = exact + (jnp.log(df / exact) / math.log(RPB_MAX_DIST / exact)
                     * (RPB_BUCKETS - exact)).astype(I32)
    large = jnp.minimum(large, RPB_BUCKETS - 1)
    return jnp.where(d < exact, d, large)


def _rpb_lookup(tab, dist):
    bucket = _t5_bucket(dist)[..., None]
    out = jnp.zeros(dist.shape + (tab.shape[1],), F32)
    for k in range(RPB_BUCKETS):
        out = jnp.where(bucket == k, tab[k], out)
    return out


def _bias_a(tab):
    qi = jnp.arange(A_BLOCK)[:, None]
    kj = jnp.arange(2 * A_BLOCK)[None, :]
    rdist = qi + A_BLOCK - kj
    out = []
    for window, dilation in A_PATTERNS:
        in_band = (rdist >= 0) & (rdist <= window // dilation)
        bias = jnp.transpose(_rpb_lookup(tab, rdist * dilation), (2, 0, 1))
        out.append(jnp.where(in_band[None], bias, NEG))
    return jnp.stack(out).astype(F32)


def _bias_b(tab, nblk):
    qi = jnp.arange(Q_BLOCK)[:, None]
    kj = jnp.arange(Q_BLOCK)[None, :]
    delta = jnp.arange(nblk)[:, None, None] * Q_BLOCK
    bias = _rpb_lookup(tab, delta + qi - kj)
    return jnp.transpose(bias, (0, 3, 2, 1)).astype(F32)


def _in_offsets():
    splits = (("a_q", 256), ("a_k", 256), ("a_v", 256), ("b_q", 256), ("b_ckv", 64), ("b_iq", 256),
              ("b_ik", 32), ("b_iw", 8), ("c_q", 128), ("c_k", 128), ("c_v", 256), ("c_a", 16),
              ("c_g", 256), ("d", 832), ("gate", 4096))
    off, out = 0, {}
    for name, w in splits:
        out[name] = (off, w)
        off += w
    return out


def _proj_columns():
    src = _in_offsets()
    d0 = src["d"][0]
    named = {
        "a_q": np.arange(256) + src["a_q"][0], "a_k": np.arange(256) + src["a_k"][0],
        "a_v": np.arange(256) + src["a_v"][0], "b_q": np.arange(256) + src["b_q"][0],
        "b_iq": np.arange(256) + src["b_iq"][0],
        "b_misc": np.concatenate([np.arange(B_LATENT) + src["b_ckv"][0], np.arange(IDX_DIM) + src["b_ik"][0],
                                  np.arange(IDX_HEADS) + src["b_iw"][0]]),
        "c_v": np.arange(256) + src["c_v"][0], "c_g": np.arange(256) + src["c_g"][0],
        "d_r": np.arange(256) + d0, "d_k": np.arange(256) + d0 + 256, "d_v": np.arange(256) + d0 + 512,
        "c_q": np.arange(128) + src["c_q"][0], "c_k": np.arange(128) + src["c_k"][0],
        "c_a": np.arange(16) + src["c_a"][0], "d_low": np.arange(64) + d0 + 768,
    }
    cols = []
    for name, w in _PIECES:
        c = named[name]
        cols.append(np.concatenate([c, -np.ones(w * LANES - len(c), np.int64)]))
    return np.concatenate(cols)


def _proj_weight(w):
    cols = _proj_columns()
    cuts = [0] + [k for k in range(1, len(cols))
                  if (cols[k] < 0) != (cols[k - 1] < 0) or (cols[k] >= 0 and cols[k] != cols[k - 1] + 1)]
    cuts.append(len(cols))
    parts = []
    for a, b in zip(cuts[:-1], cuts[1:]):
        if cols[a] < 0:
            parts.append(jnp.zeros((w.shape[0], b - a), BF16))
        else:
            parts.append(w[:, int(cols[a]):int(cols[a]) + (b - a)].astype(BF16))
    return jnp.concatenate(parts, 1)


def _pad_rows(w, first, total):
    return jnp.zeros((total, w.shape[1]), w.dtype).at[first:first + w.shape[0]].set(w)


def kernel(x, rpb_table, w_in, b_kv_gain, b_w_uv, c_a_up, c_a_bias, c_norm_gain, d_mu, d_w0, d_w2,
           d_a0, d_a2, d_g2, d_k_k, d_k_a, d_r_k, d_gn_w, d_gn_b, w_branch, w_out, ln_g, ln_b,
           router_g, router_g_bias, router_e, router_e_bias, moe_w_gate, moe_w_up, moe_w_down):
    bsz, seq, d_model = x.shape
    depth = w_in.shape[0]
    n = bsz * seq
    alpha = (2 * depth) ** 0.25
    gate_off = _in_offsets()["gate"][0]
    bias_a = _bias_a(rpb_table[:, :A_HEADS])
    bias_b = _bias_b(rpb_table[:, A_HEADS:], seq // Q_BLOCK)
    tm_proj = math.gcd(n, 512)
    tm_merge = math.gcd(n, 1024)
    tm_moe = math.gcd(n, 1024)

    x2d = x.reshape(n, d_model)
    for l in range(depth):
        h = _project(x2d, _proj_weight(w_in[l]), tm_proj, PROJ_WIDTH)

        ya = _mixer_a(h, bias_a, bsz, seq)

        gain4 = jnp.tile(b_kv_gain[l], B_HEADS)[None, :]
        wuv_t = jnp.transpose(b_w_uv[l], (0, 2, 1)).astype(BF16)
        yb = _mixer_b(h, gain4, wuv_t, bias_b, bsz, seq)

        aup_pad = _pad_rows(c_a_up[l], 0, LANES)
        yc = _mixer_c(h, aup_pad, c_a_bias[l][None, :], c_norm_gain[l][None, :], bsz, seq)

        mu = d_mu[l]
        mu3 = mu[:3 * BRANCH_WIDTH].reshape(3, BRANCH_WIDTH)
        mulow = jnp.zeros((1, LANES), F32).at[0, :D_DECAY_RANK + D_ICLR_RANK + D_GATE_RANK].set(
            mu[3 * BRANCH_WIDTH:])
        w2p = _pad_rows(d_w2[l], 0, LANES)
        a2p = _pad_rows(d_a2[l], D_DECAY_RANK, LANES)
        g2p = _pad_rows(d_g2[l], D_DECAY_RANK + D_ICLR_RANK, LANES)
        vec = jnp.stack([d_w0[l], d_a0[l], d_k_k[l], d_k_a[l], d_r_k[l], d_gn_w[l], d_gn_b[l],
                         jnp.zeros_like(d_w0[l])])
        yd = _mixer_d(h, mu3, mulow, w2p, a2p, g2p, vec, bsz, seq)

        wg = w_in[l][:, gate_off:gate_off + N_BRANCHES * d_model].astype(BF16)
        x2d = _merge(x2d, (ya, yb, yc, yd), wg, w_branch[l].astype(BF16), w_out[l].astype(BF16),
                     ln_g[l, 0][None, :], ln_b[l, 0][None, :], alpha, tm_merge)

        wr = jnp.zeros((d_model, LANES), F32)
        wr = wr.at[:, :MOE_EXPERTS].set(router_e[l]).at[:, MOE_EXPERTS:MOE_EXPERTS + MOE_GROUPS].set(
            router_g[l])
        br = jnp.zeros((1, LANES), F32)
        br = br.at[0, :MOE_EXPERTS].set(router_e_bias[l]).at[0, MOE_EXPERTS:MOE_EXPERTS + MOE_GROUPS].set(
            router_g_bias[l])
        wr = jnp.concatenate(_split2(wr), 1)
        x2d = _moe(x2d, wr, br, moe_w_gate[l].astype(BF16), moe_w_up[l].astype(BF16),
                   moe_w_down[l].astype(BF16), ln_g[l, 1][None, :], ln_b[l, 1][None, :], alpha, tm_moe)
    return x2d.reshape(bsz, seq, d_model)
```

```python
import functools
import math

import jax
import jax.numpy as jnp
import numpy as np
from jax import lax
from jax.experimental import pallas as pl
from jax.experimental.pallas import tpu as pltpu

F32 = jnp.float32
BF16 = jnp.bfloat16
I32 = jnp.int32

D_MODEL = 1024
N_BRANCHES = 4
BRANCH_WIDTH = 256
HEAD_DIM = 64
A_HEADS = 4
A_PATTERNS = ((128, 1), (512, 4), (2048, 16))
A_BLOCK = 128
A_UNROLL = 4
B_HEADS = 4
B_LATENT = 64
IDX_HEADS = 8
IDX_DIM = 32
TOPK_MAX = 256
TOPK_DIV = 4
Q_BLOCK = 128
C_HEADS = 4
C_KEY_DIM = 32
C_VAL_DIM = 64
C_GATE_RANK = 16
C_GATE_TAU = 16.0
C_CHUNK = 64
C_UNROLL = 4
D_HEADS = 4
D_HEAD_SIZE = 64
D_DECAY_RANK = 16
D_ICLR_RANK = 16
D_GATE_RANK = 32
D_GN_EPS = 64e-5
D_CHUNK = 64
RPB_BUCKETS = 32
RPB_MAX_DIST = 2048
MOE_GROUPS = 4
MOE_PER_GROUP = 8
MOE_EXPERTS = 32
MOE_HIDDEN = 256
MOE_CHUNK = 128
MOE_STEP_EXPERTS = 4
LN_EPS = 1e-5

LANES = 128
NEG = -1e30
VMEM_LIMIT = 56 * 1024 * 1024

NT = (((1,), (1,)), ((), ()))
TN = (((0,), (0,)), ((), ()))

_PIECES = (
    ("a_q", 2), ("a_k", 2), ("a_v", 2), ("b_q", 2), ("b_iq", 2),
    ("c_v", 2), ("c_g", 2), ("d_r", 2), ("d_k", 2), ("d_v", 2),
    ("c_q", 1), ("c_k", 1), ("b_misc", 1), ("c_a", 1), ("d_low", 1),
)
_PIECE_OFF = {}
_off = 0
for _n, _w in _PIECES:
    _PIECE_OFF[_n] = (_off, _w)
    _off += _w
PROJ_WIDTH = _off * LANES


def _dot(a, b):
    return jnp.dot(a, b, preferred_element_type=F32)


def _dotg(a, b, dims):
    return lax.dot_general(a, b, dims, preferred_element_type=F32)


def _split2(x):
    hi = x.astype(BF16)
    lo = (x - hi.astype(F32)).astype(BF16)
    return hi, lo


def _dot_rhs_exact(x, m):
    hi, lo = _split2(x)
    return _dot(hi, m) + _dot(lo, m)


def _dot_lhs_exact(m, x):
    hi, lo = _split2(x)
    return _dot(m, hi) + _dot(m, lo)


def _dot3(x, w):
    xh, xl = _split2(x)
    wh, wl = _split2(w)
    return _dot(xh, wh) + (_dot(xl, wh) + _dot(xh, wl))


def _sigmoid(x):
    return 1.0 / (1.0 + jnp.exp(-x))


def _softplus(x):
    return jnp.maximum(x, 0.0) + jnp.log(1.0 + jnp.exp(-jnp.abs(x)))


def _iota(shape, axis):
    return lax.broadcasted_iota(I32, shape, axis)


def _group_matrix(n, group, value, dtype=BF16):
    r = _iota((n, n), 0) // group
    c = _iota((n, n), 1) // group
    return jnp.where(r == c, value, 0.0).astype(dtype)


def _layer_norm(z, g, b):
    mu = jnp.mean(z, -1, keepdims=True)
    zc = z - mu
    var = jnp.mean(zc * zc, -1, keepdims=True)
    return zc * lax.rsqrt(var + LN_EPS) * g + b


def _params(*sem):
    return pltpu.CompilerParams(dimension_semantics=sem, vmem_limit_bytes=VMEM_LIMIT)


def _proj_kernel(x_ref, w_ref, o_ref):
    o_ref[...] = _dot(x_ref[...].astype(BF16), w_ref[...])


def _project(x2d, w, tm, tn):
    n, k = x2d.shape
    wt = w.shape[1]
    return pl.pallas_call(
        _proj_kernel,
        grid=(n // tm, wt // tn),
        in_specs=[pl.BlockSpec((tm, k), lambda i, j: (i, 0)),
                  pl.BlockSpec((k, tn), lambda i, j: (0, j))],
        out_specs=pl.BlockSpec((tm, tn), lambda i, j: (i, j)),
        out_shape=jax.ShapeDtypeStruct((n, wt), F32),
        compiler_params=_params("parallel", "parallel"),
        name="proj",
    )(x2d, w)


def _mixer_a_kernel(q0_ref, q1_ref, k0_ref, k1_ref, v0_ref, v1_ref, bias_ref, o_ref, op_ref, lse_ref,
                    *, seq, dils):
    q_refs, k_refs, v_refs = (q0_ref, q1_ref), (k0_ref, k1_ref), (v0_ref, v1_ref)
    ab = A_BLOCK
    lane = _iota((ab, LANES), 1)
    upper = lane >= HEAD_DIM
    col = _iota((ab, 2 * ab), 1)
    scale = HEAD_DIM ** -0.5

    for p, d in enumerate(dils):
        per_res = seq // (d * ab)

        def body(idx, carry, p=p, d=d, per_res=per_res):
            r = idx // per_res
            b = idx % per_res
            start = b * (ab * d) + r
            pstart = jnp.maximum(start - ab * d, r)
            if d == 1:
                rows, prow = pl.ds(pl.multiple_of(start, ab), ab), pl.ds(pl.multiple_of(pstart, ab), ab)
            else:
                rows, prow = pl.ds(start, ab, stride=d), pl.ds(pstart, ab, stride=d)
            first_pen = jnp.where(b == 0, NEG, 0.0)
            pen = jnp.where(col < ab, first_pen, 0.0)
            for hp in range(2):
                q_ref, k_ref, v_ref = q_refs[hp], k_refs[hp], v_refs[hp]
                q = q_ref[rows, :]
                kb = jnp.concatenate([k_ref[prow, :], k_ref[rows, :]], 0).astype(BF16)
                vb = jnp.concatenate([v_ref[prow, :], v_ref[rows, :]], 0).astype(BF16)
                outs, lses = [], []
                for hh in range(2):
                    hm = upper if hh else jnp.logical_not(upper)
                    qm = jnp.where(hm, q, 0.0).astype(BF16)
                    s = _dotg(qm, kb, NT) * scale + bias_ref[p, 2 * hp + hh] + pen
                    mx = jnp.max(s, -1, keepdims=True)
                    e = jnp.exp(s - mx)
                    den = jnp.sum(e, -1, keepdims=True)
                    outs.append(_dot(e.astype(BF16), vb) / den)
                    lses.append(mx + jnp.log(den))
                op_ref[2 * p + hp, rows, :] = jnp.where(upper, outs[1], outs[0])
                lse_ref[2 * p + hp, rows, :] = jnp.where(upper, lses[1], lses[0])
            return carry

        def body_group(t, carry, body=body):
            for k in range(A_UNROLL):
                carry = body(A_UNROLL * t + k, carry)
            return carry

        lax.fori_loop(0, seq // (A_UNROLL * ab), body_group, 0)

    cr = 256

    def combine(c, carry):
        rows = pl.ds(pl.multiple_of(c * cr, cr), cr)
        for hp in range(2):
            ls = [lse_ref[2 * p + hp, rows, :] for p in range(len(dils))]
            mx = functools.reduce(jnp.maximum, ls)
            es = [jnp.exp(l - mx) for l in ls]
            num = functools.reduce(lambda a, b: a + b,
                                   [e * op_ref[2 * p + hp, rows, :] for p, e in enumerate(es)])
            o_ref[rows, hp * LANES:(hp + 1) * LANES] = num / functools.reduce(lambda a, b: a + b, es)
        return carry

    lax.fori_loop(0, seq // cr, combine, 0)


def _mixer_a(h, bias, bsz, seq):
    dils = tuple(d for _, d in A_PATTERNS)
    for w, d in A_PATTERNS:
        assert w // d == A_BLOCK and seq % (d * A_BLOCK) == 0
    npat = len(dils)

    def pieces(name):
        off, w = _PIECE_OFF[name]
        return [pl.BlockSpec((seq, LANES), lambda b, o=off + u: (b, o)) for u in range(w)]

    return pl.pallas_call(
        functools.partial(_mixer_a_kernel, seq=seq, dils=dils),
        grid=(bsz,),
        in_specs=pieces("a_q") + pieces("a_k") + pieces("a_v")
        + [pl.BlockSpec(bias.shape, lambda b: (0, 0, 0, 0))],
        out_specs=pl.BlockSpec((seq, BRANCH_WIDTH), lambda b: (b, 0)),
        out_shape=jax.ShapeDtypeStruct((bsz * seq, BRANCH_WIDTH), F32),
        scratch_shapes=[pltpu.VMEM((2 * npat, seq, LANES), F32),
                        pltpu.VMEM((2 * npat, seq, LANES), F32)],
        compiler_params=_params("parallel"),
        name="mixer_a",
    )(h, h, h, h, h, h, bias)


def _mixer_b_kernel(*refs, seq, keep):
    def query_block(i, carry):
        _mixer_b_block(i, *refs, seq=seq, keep=keep)
        return carry

    lax.fori_loop(0, seq // Q_BLOCK, query_block, 0)


def _mixer_b_block(i, q_ref, iq_ref, mk_ref, gain_ref, wuvt_ref, bias_ref, o_ref,
                   ckvn_ref, ckvt_ref, ikb_ref, sc_ref, sc16_ref, lg_ref, iqs_ref, qs_ref, *, seq, keep):
    qb = Q_BLOCK
    sb = 2 * qb
    iw_lane = B_LATENT + IDX_DIM

    @pl.when(i == 0)
    def _():
        src = _iota((LANES, BRANCH_WIDTH), 0)
        dst = _iota((LANES, BRANCH_WIDTH), 1)
        is_lat = src < B_LATENT
        rep_lat = jnp.where(jnp.logical_and(is_lat, dst % B_LATENT == src), 1.0, 0.0).astype(BF16)
        avg_lat = jnp.where(is_lat, 1.0 / B_LATENT, 0.0).astype(BF16)
        is_ik = jnp.logical_and(src >= B_LATENT, src < iw_lane)
        rep_ik = jnp.where(jnp.logical_and(is_ik, dst % IDX_DIM == src - B_LATENT), 1.0, 0.0).astype(BF16)
        pick_lat = jnp.where(_iota((B_LATENT, BRANCH_WIDTH), 0) == _iota((B_LATENT, BRANCH_WIDTH), 1),
                             1.0, 0.0).astype(BF16)

        def prep(c, carry):
            rows = pl.ds(pl.multiple_of(c * sb, sb), sb)
            x = mk_ref[rows, :]
            hi, lo = _split2(x)
            lo2 = (x - hi.astype(F32) - lo.astype(F32)).astype(BF16)
            lat = _dot(hi, rep_lat) + (_dot(lo, rep_lat) + _dot(lo2, rep_lat))
            ms = _dot_rhs_exact(x * x, avg_lat)
            xn = lat * lax.rsqrt(ms + 1e-6) * gain_ref[...]
            xn16 = xn.astype(BF16)
            ckvn_ref[rows, :] = xn16
            ckvt_ref[c] = _dotg(pick_lat, xn16, NT).astype(BF16)
            ikb_ref[rows, :] = _dot(hi, rep_ik).astype(BF16)
            return carry

        lax.fori_loop(0, seq // sb, prep, 0)

    lane = _iota((qb, BRANCH_WIDTH), 1)
    qrows = pl.ds(pl.multiple_of(i * qb, qb), qb)
    iq = iq_ref[qrows, :] * (IDX_DIM ** -0.5)
    for h in range(IDX_HEADS):
        iqs_ref[h * qb:(h + 1) * qb, :] = jnp.where(lane // IDX_DIM == h, iq, 0.0).astype(BF16)
    q = q_ref[qrows, :] * (B_LATENT ** -0.5)
    for h in range(B_HEADS):
        qs_ref[h * qb:(h + 1) * qb, :] = jnp.where(lane // B_LATENT == h, q, 0.0).astype(BF16)
    pick_iw = jnp.where(_iota((IDX_HEADS, LANES), 1) == _iota((IDX_HEADS, LANES), 0) + iw_lane,
                        1.0, 0.0).astype(BF16)
    iw = mk_ref[qrows, :] * (IDX_HEADS ** -0.5)
    iw_hi, iw_lo = _split2(iw)
    iw_lo2 = (iw - iw_hi.astype(F32) - iw_lo.astype(F32)).astype(BF16)
    iw_t = _dotg(pick_iw, iw_hi, NT) + (_dotg(pick_iw, iw_lo, NT) + _dotg(pick_iw, iw_lo2, NT))
    krow = _iota((sb, qb), 0)
    qcol = _iota((sb, qb), 1)
    nsb = (i + 2) // 2

    def fold(x, op):
        parts = [x[r:r + 8] for r in range(0, sb, 8)]
        while len(parts) > 1:
            parts = [op(parts[k], parts[k + 1]) for k in range(0, len(parts), 2)]
        return parts[0]

    def for_steps(body, init):
        def pair(p, carry):
            return body(2 * p + 1, body(2 * p, carry))

        carry = lax.fori_loop(0, nsb // 2, pair, init)
        return lax.cond(nsb % 2 == 1, lambda c: body(nsb - 1, c), lambda c: c, carry)

    def score_body(j, carry):
        kr = pl.ds(pl.multiple_of(j * sb, sb), sb)
        rel = _dotg(ikb_ref[kr, :], iqs_ref[...], NT)
        sc = jnp.maximum(rel[:, 0:qb], 0.0) * iw_t[0:1, :]
        for h in range(1, IDX_HEADS):
            sc = sc + jnp.maximum(rel[:, h * qb:(h + 1) * qb], 0.0) * iw_t[h:h + 1, :]
        sc = jnp.where(krow <= qcol + (i * qb - j * sb), sc, -jnp.inf)
        sc_ref[j] = sc
        sc16_ref[j] = pltpu.bitcast(pltpu.bitcast(sc, I32) & jnp.int32(-65536), F32).astype(BF16)
        logits = _dotg(ckvn_ref[kr, :], qs_ref[...], NT)
        d0 = i - 2 * j
        for h in range(B_HEADS):
            lg = logits[:, h * qb:(h + 1) * qb]
            lg_ref[h, j, 0:qb, :] = lg[0:qb] + bias_ref[d0, h]
            lg_ref[h, j, qb:sb, :] = lg[qb:sb] + bias_ref[jnp.maximum(d0 - 1, 0), h]
        return carry

    for_steps(score_body, 0)

    sign = jnp.int32(-2 ** 31)

    def key_to_float(u):
        k = u ^ sign
        bits = k ^ ((k >> 31) & jnp.int32(0x7FFFFFFF))
        return pltpu.bitcast(bits, F32)

    def count(pred):
        def cnt(j, acc):
            return acc + fold(pred(j), jnp.add)
        acc = for_steps(cnt, jnp.zeros((8, qb), F32))
        return jnp.sum(acc, 0, keepdims=True)

    pk = 16

    def count_ge16(cf):
        cf = pltpu.bitcast(pltpu.bitcast(cf, I32) & jnp.int32(-65536), F32)
        c16 = jnp.broadcast_to(cf, (pk, qb)).astype(BF16)
        one, zero = jnp.ones((pk, qb), BF16), jnp.zeros((pk, qb), BF16)

        def cnt(j, acc):
            x = sc16_ref[j]
            parts = [jnp.where(x[r:r + pk] >= c16, one, zero) for r in range(0, sb, pk)]
            while len(parts) > 1:
                parts = [parts[k] + parts[k + 1] for k in range(0, len(parts), 2)]
            return acc + parts[0].astype(F32)

        acc = for_steps(cnt, jnp.zeros((pk, qb), F32))
        return jnp.sum(acc, 0, keepdims=True)

    def bit_body(t, state, packed=False):
        u, n_u = state
        cand = u | jnp.left_shift(jnp.int32(1), 31 - t)
        cf = key_to_float(cand)
        c = count_ge16(cf) if packed else count(lambda j: jnp.where(sc_ref[j] >= cf, 1.0, 0.0))
        take = c >= keep
        return jnp.where(take, cand, u), jnp.where(take, c, n_u)

    def bit_body16(t, state):
        return bit_body(t, state, packed=True)

    first_bits, bit_group = 24, 4
    state = (jnp.zeros((1, qb), I32), (jnp.zeros((1, qb), I32) + nsb * sb).astype(F32))
    state = bit_body16(0, state)
    n_pos = count(lambda j: jnp.where(sc_ref[j] > 0.0, 1.0, 0.0))
    zero_thr = jnp.logical_and(state[0] != 0, n_pos < keep)
    state = lax.fori_loop(1, 16, bit_body16, state)
    u, n_ge = lax.fori_loop(16, first_bits, bit_body, state)

    def more_bits(s):
        t, _, n_u = s
        open_rows = jnp.where(zero_thr, 0.0, jnp.abs(n_u - keep))
        return jnp.logical_and(t < 32, jnp.max(open_rows) > 0.0)

    def bit_group_body(s):
        t, u, n_u = s
        for k in range(bit_group):
            u, n_u = bit_body(t + k, (u, n_u))
        return t + bit_group, u, n_u

    _, u, n_ge = lax.while_loop(more_bits, bit_group_body, (jnp.int32(first_bits), u, n_ge))
    u = jnp.maximum(u ^ sign, jnp.int32(0x007FFFFF - 2 ** 31)) ^ sign
    thr = key_to_float(u)
    n_gt = count(lambda j: jnp.where(sc_ref[j] > thr, 1.0, 0.0))
    need = keep - n_gt

    nbits = int(math.ceil(math.log2(seq))) + 1

    def cut_search():
        def cut_body(t, cut):
            cand = cut | jnp.left_shift(jnp.int32(1), nbits - 1 - t)
            c = count(lambda j: jnp.where(sc_ref[j] == thr,
                                          jnp.where(krow < cand - j * sb, 1.0, 0.0), 0.0))
            return jnp.where(c <= need, cand, cut)

        return lax.fori_loop(0, nbits, cut_body, jnp.zeros((1, qb), I32))

    surplus = jnp.max(n_ge) > keep
    cut = lax.cond(surplus, cut_search, lambda: jnp.full((1, qb), 2 ** nbits - 1, I32))
    cut = jnp.minimum(cut, i * qb + 1 + _iota((1, qb), 1))

    def mask_body(j, mx):
        sc = sc_ref[j]
        sel = jnp.where(sc > thr, 1.0, jnp.where(sc == thr, jnp.where(krow < cut - j * sb, 1.0, 0.0), 0.0))
        out = []
        for h in range(B_HEADS):
            s = jnp.where(sel > 0.5, lg_ref[h, j], NEG)
            lg_ref[h, j] = s
            out.append(jnp.maximum(mx[h], fold(s, jnp.maximum)))
        return tuple(out)

    mx = for_steps(mask_body, tuple(jnp.full((8, qb), NEG, F32) for _ in range(B_HEADS)))
    ms = [jnp.max(m, 0, keepdims=True) for m in mx]

    def att_body(j, carry):
        ls, acc = carry
        ps, new_ls = [], []
        for h in range(B_HEADS):
            pr = jnp.exp(lg_ref[h, j] - ms[h])
            new_ls.append(ls[h] + fold(pr, jnp.add))
            ps.append(pr.astype(BF16))
        upd = _dot(ckvt_ref[j], jnp.concatenate(ps, 1))
        return tuple(new_ls), acc + upd

    init = (tuple(jnp.zeros((8, qb), F32) for _ in range(B_HEADS)),
            jnp.zeros((B_LATENT, B_HEADS * qb), F32))
    ls, acc = for_steps(att_body, init)
    ls = [jnp.sum(l, 0, keepdims=True) for l in ls]
    o_t = (acc / jnp.concatenate(ls, 1)).astype(BF16)
    y_t = jnp.concatenate([_dot(wuvt_ref[h], o_t[:, h * qb:(h + 1) * qb]) for h in range(B_HEADS)], 0)
    eye_q = jnp.where(_iota((qb, qb), 0) == _iota((qb, qb), 1), 1.0, 0.0).astype(BF16)
    o_ref[qrows, :] = _dotg(eye_q, y_t.astype(BF16), NT)


def _mixer_b(h, gain4, wuv_t, bias, bsz, seq):
    qb = Q_BLOCK
    assert seq % 256 == 0
    nblk = seq // qb
    keep = min(TOPK_MAX, seq // TOPK_DIV)

    def kpiece(name):
        off, w = _PIECE_OFF[name]
        return pl.BlockSpec((seq, w * LANES), lambda b, o=off // w: (b, o))

    return pl.pallas_call(
        functools.partial(_mixer_b_kernel, seq=seq, keep=keep),
        grid=(bsz,),
        in_specs=[kpiece("b_q"), kpiece("b_iq"), kpiece("b_misc"),
                  pl.BlockSpec(gain4.shape, lambda b: (0, 0)),
                  pl.BlockSpec(wuv_t.shape, lambda b: (0, 0, 0)),
                  pl.BlockSpec(bias.shape, lambda b: (0, 0, 0, 0))],
        out_specs=pl.BlockSpec((seq, BRANCH_WIDTH), lambda b: (b, 0)),
        out_shape=jax.ShapeDtypeStruct((bsz * seq, BRANCH_WIDTH), F32),
        scratch_shapes=[pltpu.VMEM((seq, BRANCH_WIDTH), BF16),
                        pltpu.VMEM((nblk // 2, B_LATENT, 2 * qb), BF16),
                        pltpu.VMEM((seq, BRANCH_WIDTH), BF16),
                        pltpu.VMEM((nblk // 2, 2 * qb, qb), F32),
                        pltpu.VMEM((nblk // 2, 2 * qb, qb), BF16),
                        pltpu.VMEM((B_HEADS, nblk // 2, 2 * qb, qb), F32),
                        pltpu.VMEM((IDX_HEADS * qb, BRANCH_WIDTH), BF16),
                        pltpu.VMEM((B_HEADS * qb, BRANCH_WIDTH), BF16)],
        compiler_params=_params("parallel"),
        name="mixer_b",
    )(h, h, h, gain4, wuv_t, bias)


def _mixer_c_kernel(q_ref, k_ref, v_ref, g_ref, a_ref, aup_ref, abias_ref, gain_ref, o_ref, *, seq):
    ch = C_CHUNK
    kw = C_HEADS * C_KEY_DIM
    vw = C_HEADS * C_VAL_DIM
    tri = jnp.where(_iota((ch, ch), 1) <= _iota((ch, ch), 0), 1.0, 0.0).astype(BF16)
    causal = _iota((C_HEADS * ch, ch), 1) <= (_iota((C_HEADS * ch, ch), 0) % ch)
    klane_head = _iota((C_HEADS * ch, kw), 1) // C_KEY_DIM
    krow_head = _iota((C_HEADS * ch, kw), 0) // ch
    vlane_head = _iota((ch, vw), 1) // C_VAL_DIM
    st_mask = (_iota((vw, kw), 0) // C_VAL_DIM) == (_iota((vw, kw), 1) // C_KEY_DIM)
    gmat = _group_matrix(vw, C_VAL_DIM, 1.0 / C_VAL_DIM)
    aup = aup_ref[...]

    def body(c, st):
        rows = pl.ds(pl.multiple_of(c * ch, ch), ch)
        qc = q_ref[rows, :] * (C_KEY_DIM ** -0.5)
        kc = k_ref[rows, :]
        vc = v_ref[rows, :]
        z = _dot3(a_ref[rows, :], aup) + abias_ref[...]
        log_a = -_softplus(-z) / C_GATE_TAU
        cum = _dot_lhs_exact(tri, log_a)
        last = cum[ch - 1:ch, :]
        q_dec = qc * jnp.exp(cum)
        k_inv = (kc * jnp.exp(-cum)).astype(BF16)
        k_dec = (kc * jnp.exp(last - cum)).astype(BF16)
        vb = vc.astype(BF16)
        qd_b = q_dec.astype(BF16)
        q_stack = jnp.where(klane_head == krow_head, jnp.concatenate([q_dec] * C_HEADS, 0), 0.0)
        att = jnp.where(causal, _dotg(q_stack.astype(BF16), k_inv, NT), 0.0)
        full = _dot(att.astype(BF16), vb)
        o = _dotg(qd_b, st.astype(BF16), NT)
        for h in range(C_HEADS):
            o = o + jnp.where(vlane_head == h, full[h * ch:(h + 1) * ch], 0.0)
        upd = _dotg(vb, k_dec, TN)
        st = st * jnp.exp(last) + jnp.where(st_mask, upd, 0.0)
        ms = _dot_rhs_exact(o * o, gmat)
        o = o * lax.rsqrt(ms + 1e-6) * gain_ref[...]
        g = g_ref[rows, :]
        o_ref[rows, :] = g * _sigmoid(g) * o
        return st

    def body_group(t, st):
        for k in range(C_UNROLL):
            st = body(C_UNROLL * t + k, st)
        return st

    lax.fori_loop(0, seq // (C_UNROLL * ch), body_group, jnp.zeros((vw, kw), F32))


def _mixer_c(h, aup_pad, abias, gain, bsz, seq):
    assert seq % C_CHUNK == 0

    def piece(name):
        off, w = _PIECE_OFF[name]
        return pl.BlockSpec((seq, w * LANES), lambda b, o=off // w: (b, o))

    def const(a):
        return pl.BlockSpec(a.shape, lambda b: (0, 0))

    return pl.pallas_call(
        functools.partial(_mixer_c_kernel, seq=seq),
        grid=(bsz,),
        in_specs=[piece("c_q"), piece("c_k"), piece("c_v"), piece("c_g"), piece("c_a"),
                  const(aup_pad), const(abias), const(gain)],
        out_specs=pl.BlockSpec((seq, BRANCH_WIDTH), lambda b: (b, 0)),
        out_shape=jax.ShapeDtypeStruct((bsz * seq, BRANCH_WIDTH), F32),
        compiler_params=_params("parallel"),
        name="mixer_c",
    )(h, h, h, h, h, aup_pad, abias, gain)


def _mixer_d_kernel(r_ref, k_ref, v_ref, low_ref, mu_ref, mulow_ref, w2_ref, a2_ref, g2_ref, vec_ref,
                    o_ref, *, seq, nsub):
    ch = D_CHUNK
    bw = BRANCH_WIDTH
    nh = D_HEADS
    hs = D_HEAD_SIZE
    r_i = _iota((bw, bw), 0)
    c_i = _iota((bw, bw), 1)
    same = (r_i // hs) == (c_i // hs)
    strict = c_i < r_i
    incl = c_i <= r_i
    eye = r_i == c_i
    ones_bd = _group_matrix(bw, hs, 1.0)
    avg_bd = _group_matrix(bw, hs, 1.0 / hs)
    rows = nsub * ch
    tr_i = _iota((rows, rows), 0)
    tc_i = _iota((rows, rows), 1)
    tri = jnp.where(jnp.logical_and(tr_i // ch == tc_i // ch, tc_i <= tr_i), 1.0, 0.0).astype(BF16)
    row0 = _iota((rows, bw), 0) == 0
    row0_low = _iota((rows, LANES), 0) == 0
    vec = vec_ref[...]
    w0, a0, k_k, k_a, r_k, gn_w, gn_b = (vec[n:n + 1, :] for n in range(7))
    mu = mu_ref[...]
    w2, a2, g2 = w2_ref[...], a2_ref[...], g2_ref[...]

    def wide(x):
        return jnp.where(same, jnp.concatenate([x] * nh, 0), 0.0)

    def shifted(ref, start, first_row, m):
        cur = ref[pl.ds(start, rows), :]
        last8 = ref[pl.ds(pl.multiple_of(jnp.maximum(start - 8, 0), 8), 8), :]
        prev_row = last8[7:8, :] * jnp.where(start > 0, 1.0, 0.0)
        prev = jnp.where(first_row, prev_row, pltpu.roll(cur, 1, 0))
        return cur + (prev - cur) * m

    def body(c, st):
        start = pl.multiple_of(c * rows, rows)
        r = shifted(r_ref, start, row0, mu[0:1, :])
        k = shifted(k_ref, start, row0, mu[1:2, :])
        v = shifted(v_ref, start, row0, mu[2:3, :])
        low = shifted(low_ref, start, row0_low, mulow_ref[...])
        w_raw = -_softplus(-(w0 + _dot3(jnp.tanh(low), w2))) - 0.5
        lw = -jnp.exp(w_raw)
        a = _sigmoid(a0 + _dot3(low, a2))
        g = _dot3(_sigmoid(low), g2)
        kk = k * k_k
        kk = kk / jnp.maximum(jnp.sqrt(_dot_rhs_exact(kk * kk, ones_bd)), 1e-12)
        k2 = k * (1.0 + (a - 1.0) * k_a)
        bonus = _dot_rhs_exact(r * k2 * r_k, ones_bd) * v
        b = kk * a

        cum = _dot_lhs_exact(tri, lw)
        e_in = jnp.exp(cum)
        e_inv = jnp.exp(-cum)
        a_dec = -kk * jnp.exp(cum - lw)
        r_dec = r * e_in
        b_inv = b * e_inv
        k_inv = k2 * e_inv
        local = [chunk_terms(*(z[s * ch:(s + 1) * ch] for z in (a_dec, r_dec, b_inv, k_inv, v, e_in)))
                 for s in range(nsub)]
        ys = []
        for atp, vp, xr, m_rb, y_kv, bh, h_kv, g_col in local:
            st16 = st.astype(BF16)
            u16 = (_dot(atp, st16) + vp).astype(BF16)
            y_w = _dot(xr, st16) + _dot(m_rb, u16) + y_kv
            st = g_col * st + _dotg(bh, u16, TN) + h_kv
            y = y_w[0:ch]
            for h in range(1, nh):
                y = y + y_w[h * ch:(h + 1) * ch]
            ys.append(y)
        y = jnp.concatenate(ys, 0)
        mean = _dot_rhs_exact(y, avg_bd)
        yc = y - mean
        var = _dot_rhs_exact(yc * yc, avg_bd)
        yn = yc * lax.rsqrt(var + D_GN_EPS) * gn_w + gn_b
        o_ref[pl.ds(start, rows), :] = (yn + bonus) * g
        return st

    def chunk_terms(a_dec, r_dec, b_inv, k_inv, v, e_in):
        g_last = e_in[ch - 1:ch, :]
        xa = wide(a_dec).astype(BF16)
        xr = wide(r_dec).astype(BF16)
        yb = wide(b_inv).astype(BF16)
        yk = wide(k_inv).astype(BF16)
        bh = wide(b_inv * g_last).astype(BF16)
        kh = wide(k_inv * g_last).astype(BF16)
        vw = wide(v).astype(BF16)

        a_ab = jnp.where(strict, _dotg(xa, yb, NT), 0.0)
        a_ak = jnp.where(strict, _dotg(xa, yk, NT), 0.0).astype(BF16)
        m_rb = jnp.where(incl, _dotg(xr, yb, NT), 0.0).astype(BF16)
        m_rk = jnp.where(incl, _dotg(xr, yk, NT), 0.0).astype(BF16)

        t_inv = jnp.where(eye, 1.0, a_ab)
        apow = a_ab
        for _ in range(int(math.log2(ch)) - 1):
            ab16 = apow.astype(BF16)
            apow = _dot(ab16, ab16)
            t_inv = t_inv + _dot(t_inv.astype(BF16), apow.astype(BF16))
        t16 = t_inv.astype(BF16)

        atp = _dot(t16, xa).astype(BF16)
        vp = _dot(t16, _dot(a_ak, vw).astype(BF16))
        y_kv = _dot(m_rk, vw)
        h_kv = _dotg(kh, vw, TN)
        g_col = jnp.sum(jnp.where(eye, g_last, 0.0), -1, keepdims=True)
        return atp, vp, xr, m_rb, y_kv, bh, h_kv, g_col

    lax.fori_loop(0, seq // rows, body, jnp.zeros((bw, bw), F32))


def _mixer_d(h, mu3, mulow, w2p, a2p, g2p, vec, bsz, seq, nsub=4):
    assert seq % (nsub * D_CHUNK) == 0 and D_CHUNK == D_HEAD_SIZE

    def piece(name):
        off, w = _PIECE_OFF[name]
        return pl.BlockSpec((seq, w * LANES), lambda b, o=off // w: (b, o))

    def const(a):
        return pl.BlockSpec(a.shape, lambda b: (0, 0))

    return pl.pallas_call(
        functools.partial(_mixer_d_kernel, seq=seq, nsub=nsub),
        grid=(bsz,),
        in_specs=[piece("d_r"), piece("d_k"), piece("d_v"), piece("d_low"),
                  const(mu3), const(mulow), const(w2p), const(a2p), const(g2p), const(vec)],
        out_specs=pl.BlockSpec((seq, BRANCH_WIDTH), lambda b: (b, 0)),
        out_shape=jax.ShapeDtypeStruct((bsz * seq, BRANCH_WIDTH), F32),
        compiler_params=_params("parallel"),
        name="mixer_d",
    )(h, h, h, h, mu3, mulow, w2p, a2p, g2p, vec)


def _merge_kernel(x_ref, ya_ref, yb_ref, yc_ref, yd_ref, wg_ref, wb_ref, wo_ref, g_ref, b_ref, o_ref,
                  *, alpha):
    x = x_ref[...]
    xb = x.astype(BF16)
    merged = None
    for n, y_ref in enumerate((ya_ref, yb_ref, yc_ref, yd_ref)):
        gate = _sigmoid(_dot(xb, wg_ref[:, n * D_MODEL:(n + 1) * D_MODEL]))
        term = gate * _dot(y_ref[...].astype(BF16), wb_ref[n])
        merged = term if merged is None else merged + term
    z = alpha * x + _dot(merged.astype(BF16), wo_ref[...])
    o_ref[...] = _layer_norm(z, g_ref[...], b_ref[...])


def _merge(x2d, ys, wg, wb, wo, g, b, alpha, tm):
    n, d = x2d.shape

    def const(a):
        nd = a.ndim
        return pl.BlockSpec(a.shape, lambda i: (0,) * nd, pipeline_mode=pl.Buffered(1))

    yspec = pl.BlockSpec((tm, BRANCH_WIDTH), lambda i: (i, 0))
    return pl.pallas_call(
        functools.partial(_merge_kernel, alpha=alpha),
        grid=(n // tm,),
        in_specs=[pl.BlockSpec((tm, d), lambda i: (i, 0)), yspec, yspec, yspec, yspec,
                  const(wg), const(wb), const(wo), const(g), const(b)],
        out_specs=pl.BlockSpec((tm, d), lambda i: (i, 0)),
        out_shape=jax.ShapeDtypeStruct((n, d), F32),
        compiler_params=_params("parallel"),
        name="merge",
    )(x2d, *ys, wg, wb, wo, g, b)


def _moe_kernel(x_ref, ltri_ref, wr_ref, br_ref, wg_ref, wu_ref, wd_ref, g_ref, b_ref, o_ref,
                xs_ref, gs_ref, ys_ref, pt_ref, seg_ref, *, alpha):
    e = pl.program_id(1)
    tm = x_ref.shape[0]
    nrow = xs_ref.shape[0]
    ck = MOE_CHUNK
    lane = _iota((tm, LANES), 1)

    @pl.when(e == 0)
    def _():
        x = x_ref[...]
        xb = x.astype(BF16)
        ys_ref[...] = jnp.zeros_like(ys_ref)
        x_lo = (x - xb.astype(F32)).astype(BF16)
        cross = _dot(jnp.concatenate([xb, x_lo], 0), wr_ref[...])
        logits = (cross[0:tm, 0:LANES] + (cross[0:tm, LANES:] + cross[tm:, 0:LANES])) + br_ref[...]
        is_g = jnp.logical_and(lane >= MOE_EXPERTS, lane < MOE_EXPERTS + MOE_GROUPS)
        lg = jnp.where(is_g, logits, -jnp.inf)
        gmax = jnp.max(lg, -1, keepdims=True)
        ptop = 1.0 / jnp.sum(jnp.exp(lg - gmax), -1, keepdims=True)
        gsel = jnp.min(jnp.where(lg == gmax, lane, 2 * LANES), -1, keepdims=True) - MOE_EXPERTS
        in_group = jnp.logical_and(lane < MOE_EXPERTS, lane // MOE_PER_GROUP == gsel)
        le = jnp.where(in_group, logits, -jnp.inf)
        v1 = jnp.max(le, -1, keepdims=True)
        i1 = jnp.min(jnp.where(le == v1, lane, 2 * LANES), -1, keepdims=True)
        le2 = jnp.where(lane == i1, -jnp.inf, le)
        v2 = jnp.max(le2, -1, keepdims=True)
        i2 = jnp.min(jnp.where(le2 == v2, lane, 2 * LANES), -1, keepdims=True)
        e2 = jnp.exp(v2 - v1)
        w1 = ptop / (1.0 + e2)
        gate = jnp.where(lane == i1, w1, jnp.where(lane == i2, w1 * e2, 0.0))

        onehot = jnp.where(lane == gsel, 1.0, 0.0)
        rank = _dot(ltri_ref[...], onehot.astype(BF16))
        cnt = jnp.sum(onehot, 0, keepdims=True).astype(I32)
        nch = jnp.right_shift(cnt + (ck - 1), int(math.log2(ck)))
        lane1 = _iota((1, LANES), 1)
        off = jnp.int32(0)
        off_vec = jnp.zeros((1, LANES), I32)
        for grp in range(MOE_GROUPS):
            n_g = nch[0, grp]
            seg_ref[grp] = off
            seg_ref[MOE_GROUPS + grp] = n_g
            off_vec = jnp.where(lane1 == grp, off, off_vec)
            off = off + n_g * ck
        dest = jnp.sum(onehot * (rank + off_vec.astype(F32)), -1, keepdims=True)
        pt_ref[...] = jnp.where(_iota((tm, nrow), 1) == dest.astype(I32), 1.0, 0.0).astype(BF16)
        d_hi = jnp.floor(dest * (1.0 / LANES))
        digits = jnp.where(lane == 0, d_hi, jnp.where(lane == 1, dest - d_hi * LANES, 0.0)).astype(BF16)
        pick = jnp.where(_iota((8, LANES), 0) == _iota((8, LANES), 1), 1.0, 0.0).astype(BF16)
        digits_t = _dotg(pick, digits, NT)
        dest_row = (digits_t[0:1, :] * LANES + digits_t[1:2, :]).astype(I32)
        perm = jnp.where(_iota((nrow, tm), 0) == dest_row, 1.0, 0.0).astype(BF16)
        gate_hi, gate_lo = _split2(gate)
        moved = _dot(perm, jnp.concatenate([xb, gate_hi, gate_lo], 1))
        d = x_ref.shape[1]
        xs_ref[...] = moved[:, 0:d].astype(BF16)
        gs_ref[...] = moved[:, d:d + LANES] + moved[:, d + LANES:]

    nse = MOE_STEP_EXPERTS
    grp = e // (MOE_PER_GROUP // nse)
    seg_off = seg_ref[grp]
    seg_chunks = seg_ref[MOE_GROUPS + grp]

    def run_experts(start, size):
        rows = pl.ds(pl.multiple_of(start, ck), size)
        xc = xs_ref[rows, :]
        gs = gs_ref[rows, :]
        lane_s = _iota((size, LANES), 1)
        hidden = []
        for k in range(nse):
            gk = jnp.sum(jnp.where(lane_s == nse * e + k, gs, 0.0), -1, keepdims=True)
            hg = _dot(xc, wg_ref[k])
            hu = _dot(xc, wu_ref[k])
            hidden.append((hg * _sigmoid(hg) * hu * gk).astype(BF16))
        wd = wd_ref[...].reshape(nse * MOE_HIDDEN, wd_ref.shape[2])
        ys_ref[rows, :] += _dot(jnp.concatenate(hidden, 1), wd)

    big = 4

    def quad(c, carry):
        run_experts(seg_off + c * (big * ck), big * ck)
        return carry

    lax.fori_loop(0, seg_chunks // big, quad, 0)
    rem = seg_chunks % big
    rem_start = seg_off + (seg_chunks - rem) * ck
    for r in range(1, big):
        @pl.when(rem == r)
        def _(r=r):
            run_experts(rem_start, r * ck)

    @pl.when(e == pl.num_programs(1) - 1)
    def _():
        z = alpha * x_ref[...] + _dot(pt_ref[...], ys_ref[...].astype(BF16))
        o_ref[...] = _layer_norm(z, g_ref[...], b_ref[...])


def _moe(x2d, wr, br, wg, wu, wd, g, b, alpha, tm):
    n, d = x2d.shape
    ne = wg.shape[0]
    hid = wg.shape[2]
    nse = MOE_STEP_EXPERTS
    assert MOE_PER_GROUP % nse == 0 and ne % nse == 0
    nrow = tm + MOE_GROUPS * MOE_CHUNK
    ltri = jnp.tril(jnp.ones((tm, tm), BF16), -1)

    def const(a):
        return pl.BlockSpec(a.shape, lambda i, e: (0, 0), pipeline_mode=pl.Buffered(1))

    return pl.pallas_call(
        functools.partial(_moe_kernel, alpha=alpha),
        grid=(n // tm, ne // nse),
        in_specs=[pl.BlockSpec((tm, d), lambda i, e: (i, 0)), const(ltri), const(wr), const(br),
                  pl.BlockSpec((nse, d, hid), lambda i, e: (e, 0, 0)),
                  pl.BlockSpec((nse, d, hid), lambda i, e: (e, 0, 0)),
                  pl.BlockSpec((nse, hid, d), lambda i, e: (e, 0, 0)),
                  const(g), const(b)],
        out_specs=pl.BlockSpec((tm, d), lambda i, e: (i, 0)),
        out_shape=jax.ShapeDtypeStruct((n, d), F32),
        scratch_shapes=[pltpu.VMEM((nrow, d), BF16), pltpu.VMEM((nrow, LANES), F32),
                        pltpu.VMEM((nrow, d), F32), pltpu.VMEM((tm, nrow), BF16),
                        pltpu.SMEM((2 * MOE_GROUPS,), I32)],
        compiler_params=_params("parallel", "arbitrary"),
        name="moe",
    )(x2d, ltri, wr, br, wg, wu, wd, g, b)


def _t5_bucket(dist):
    exact = RPB_BUCKETS // 2
    d = jnp.maximum(dist, 0)
    df = jnp.maximum(d, 1).astype(F32)
    large = exact + (jnp.log(df / exact) / math.log(RPB_MAX_DIST / exact)
                     * (RPB_BUCKETS - exact)).astype(I32)
    large = jnp.minimum(large, RPB_BUCKETS - 1)
    return jnp.where(d < exact, d, large)


def _rpb_lookup(tab, dist):
    bucket = _t5_bucket(dist)[..., None]
    out = jnp.zeros(dist.shape + (tab.shape[1],), F32)
    for k in range(RPB_BUCKETS):
        out = jnp.where(bucket == k, tab[k], out)
    return out


def _bias_a(tab):
    qi = jnp.arange(A_BLOCK)[:, None]
    kj = jnp.arange(2 * A_BLOCK)[None, :]
    rdist = qi + A_BLOCK - kj
    out = []
    for window, dilation in A_PATTERNS:
        in_band = (rdist >= 0) & (rdist <= window // dilation)
        bias = jnp.transpose(_rpb_lookup(tab, rdist * dilation), (2, 0, 1))
        out.append(jnp.where(in_band[None], bias, NEG))
    return jnp.stack(out).astype(F32)


def _bias_b(tab, nblk):
    qi = jnp.arange(Q_BLOCK)[:, None]
    kj = jnp.arange(Q_BLOCK)[None, :]
    delta = jnp.arange(nblk)[:, None, None] * Q_BLOCK
    bias = _rpb_lookup(tab, delta + qi - kj)
    return jnp.transpose(bias, (0, 3, 2, 1)).astype(F32)


def _in_offsets():
    splits = (("a_q", 256), ("a_k", 256), ("a_v", 256), ("b_q", 256), ("b_ckv", 64), ("b_iq", 256),
              ("b_ik", 32), ("b_iw", 8), ("c_q", 128), ("c_k", 128), ("c_v", 256), ("c_a", 16),
              ("c_g", 256), ("d", 832), ("gate", 4096))
    off, out = 0, {}
    for name, w in splits:
        out[name] = (off, w)
        off += w
    return out


def _proj_columns():
    src = _in_offsets()
    d0 = src["d"][0]
    named = {
        "a_q": np.arange(256) + src["a_q"][0], "a_k": np.arange(256) + src["a_k"][0],
        "a_v": np.arange(256) + src["a_v"][0], "b_q": np.arange(256) + src["b_q"][0],
        "b_iq": np.arange(256) + src["b_iq"][0],
        "b_misc": np.concatenate([np.arange(B_LATENT) + src["b_ckv"][0], np.arange(IDX_DIM) + src["b_ik"][0],
                                  np.arange(IDX_HEADS) + src["b_iw"][0]]),
        "c_v": np.arange(256) + src["c_v"][0], "c_g": np.arange(256) + src["c_g"][0],
        "d_r": np.arange(256) + d0, "d_k": np.arange(256) + d0 + 256, "d_v": np.arange(256) + d0 + 512,
        "c_q": np.arange(128) + src["c_q"][0], "c_k": np.arange(128) + src["c_k"][0],
        "c_a": np.arange(16) + src["c_a"][0], "d_low": np.arange(64) + d0 + 768,
    }
    cols = []
    for name, w in _PIECES:
        c = named[name]
        cols.append(np.concatenate([c, -np.ones(w * LANES - len(c), np.int64)]))
    return np.concatenate(cols)


def _proj_weight(w):
    cols = _proj_columns()
    cuts = [0] + [k for k in range(1, len(cols))
                  if (cols[k] < 0) != (cols[k - 1] < 0) or (cols[k] >= 0 and cols[k] != cols[k - 1] + 1)]
    cuts.append(len(cols))
    parts = []
    for a, b in zip(cuts[:-1], cuts[1:]):
        if cols[a] < 0:
            parts.append(jnp.zeros((w.shape[0], b - a), BF16))
        else:
            parts.append(w[:, int(cols[a]):int(cols[a]) + (b - a)].astype(BF16))
    return jnp.concatenate(parts, 1)


def _pad_rows(w, first, total):
    return jnp.zeros((total, w.shape[1]), w.dtype).at[first:first + w.shape[0]].set(w)


def kernel(x, rpb_table, w_in, b_kv_gain, b_w_uv, c_a_up, c_a_bias, c_norm_gain, d_mu, d_w0, d_w2,
           d_a0, d_a2, d_g2, d_k_k, d_k_a, d_r_k, d_gn_w, d_gn_b, w_branch, w_out, ln_g, ln_b,
           router_g, router_g_bias, router_e, router_e_bias, moe_w_gate, moe_w_up, moe_w_down):
    bsz, seq, d_model = x.shape
    depth = w_in.shape[0]
    n = bsz * seq
    alpha = (2 * depth) ** 0.25
    gate_off = _in_offsets()["gate"][0]
    bias_a = _bias_a(rpb_table[:, :A_HEADS])
    bias_b = _bias_b(rpb_table[:, A_HEADS:], seq // Q_BLOCK)
    tm_proj = math.gcd(n, 512)
    tm_merge = math.gcd(n, 1024)
    tm_moe = math.gcd(n, 1024)

    x2d = x.reshape(n, d_model)
    for l in range(depth):
        h = _project(x2d, _proj_weight(w_in[l]), tm_proj, PROJ_WIDTH)

        ya = _mixer_a(h, bias_a, bsz, seq)

        gain4 = jnp.tile(b_kv_gain[l], B_HEADS)[None, :]
        wuv_t = jnp.transpose(b_w_uv[l], (0, 2, 1)).astype(BF16)
        yb = _mixer_b(h, gain4, wuv_t, bias_b, bsz, seq)

        aup_pad = _pad_rows(c_a_up[l], 0, LANES)
        yc = _mixer_c(h, aup_pad, c_a_bias[l][None, :], c_norm_gain[l][None, :], bsz, seq)

        mu = d_mu[l]
        mu3 = mu[:3 * BRANCH_WIDTH].reshape(3, BRANCH_WIDTH)
        mulow = jnp.zeros((1, LANES), F32).at[0, :D_DECAY_RANK + D_ICLR_RANK + D_GATE_RANK].set(
            mu[3 * BRANCH_WIDTH:])
        w2p = _pad_rows(d_w2[l], 0, LANES)
        a2p = _pad_rows(d_a2[l], D_DECAY_RANK, LANES)
        g2p = _pad_rows(d_g2[l], D_DECAY_RANK + D_ICLR_RANK, LANES)
        vec = jnp.stack([d_w0[l], d_a0[l], d_k_k[l], d_k_a[l], d_r_k[l], d_gn_w[l], d_gn_b[l],
                         jnp.zeros_like(d_w0[l])])
        yd = _mixer_d(h, mu3, mulow, w2p, a2p, g2p, vec, bsz, seq)

        wg = w_in[l][:, gate_off:gate_off + N_BRANCHES * d_model].astype(BF16)
        x2d = _merge(x2d, (ya, yb, yc, yd), wg, w_branch[l].astype(BF16), w_out[l].astype(BF16),
                     ln_g[l, 0][None, :], ln_b[l, 0][None, :], alpha, tm_merge)

        wr = jnp.zeros((d_model, LANES), F32)
        wr = wr.at[:, :MOE_EXPERTS].set(router_e[l]).at[:, MOE_EXPERTS:MOE_EXPERTS + MOE_GROUPS].set(
            router_g[l])
        br = jnp.zeros((1, LANES), F32)
        br = br.at[0, :MOE_EXPERTS].set(router_e_bias[l]).at[0, MOE_EXPERTS:MOE_EXPERTS + MOE_GROUPS].set(
            router_g_bias[l])
        wr = jnp.concatenate(_split2(wr), 1)
        x2d = _moe(x2d, wr, br, moe_w_gate[l].astype(BF16), moe_w_up[l].astype(BF16),
                   moe_w_down[l].astype(BF16), ln_g[l, 1][None, :], ln_b[l, 1][None, :], alpha, tm_moe)
    return x2d.reshape(bsz, seq, d_model)
```

```python
import functools
import math

import jax
import jax.numpy as jnp
import numpy as np
from jax import lax
from jax.experimental import pallas as pl
from jax.experimental.pallas import tpu as pltpu

F32 = jnp.float32
BF16 = jnp.bfloat16
I32 = jnp.int32

D_MODEL = 1024
N_BRANCHES = 4
BRANCH_WIDTH = 256
HEAD_DIM = 64
A_HEADS = 4
A_PATTERNS = ((128, 1), (512, 4), (2048, 16))
A_BLOCK = 128
A_UNROLL = 4
B_HEADS = 4
B_LATENT = 64
IDX_HEADS = 8
IDX_DIM = 32
TOPK_MAX = 256
TOPK_DIV = 4
Q_BLOCK = 128
B_QUERIES = 256
B_KEYS = 256
C_HEADS = 4
C_KEY_DIM = 32
C_VAL_DIM = 64
C_GATE_RANK = 16
C_GATE_TAU = 16.0
C_CHUNK = 64
C_UNROLL = 4
D_HEADS = 4
D_HEAD_SIZE = 64
D_DECAY_RANK = 16
D_ICLR_RANK = 16
D_GATE_RANK = 32
D_GN_EPS = 64e-5
D_CHUNK = 64
RPB_BUCKETS = 32
RPB_MAX_DIST = 2048
MOE_GROUPS = 4
MOE_PER_GROUP = 8
MOE_EXPERTS = 32
MOE_HIDDEN = 256
MOE_CHUNK = 128
MOE_STEP_EXPERTS = 4
LN_EPS = 1e-5

LANES = 128
NEG = -1e30
VMEM_LIMIT = 56 * 1024 * 1024

NT = (((1,), (1,)), ((), ()))
TN = (((0,), (0,)), ((), ()))

_PIECES = (
    ("a_q", 2), ("a_k", 2), ("a_v", 2), ("b_q", 2), ("b_iq", 2),
    ("c_v", 2), ("c_g", 2), ("d_r", 2), ("d_k", 2), ("d_v", 2),
    ("c_q", 1), ("c_k", 1), ("b_misc", 1), ("c_a", 1), ("d_low", 1),
)
_PIECE_OFF = {}
_off = 0
for _n, _w in _PIECES:
    _PIECE_OFF[_n] = (_off, _w)
    _off += _w
PROJ_WIDTH = _off * LANES


def _dot(a, b):
    return jnp.dot(a, b, preferred_element_type=F32)


def _dotg(a, b, dims):
    return lax.dot_general(a, b, dims, preferred_element_type=F32)


def _split2(x):
    hi = x.astype(BF16)
    lo = (x - hi.astype(F32)).astype(BF16)
    return hi, lo


def _dot_rhs_exact(x, m):
    hi, lo = _split2(x)
    return _dot(hi, m) + _dot(lo, m)


def _dot_lhs_exact(m, x):
    hi, lo = _split2(x)
    return _dot(m, hi) + _dot(m, lo)


def _dot3(x, w):
    xh, xl = _split2(x)
    wh, wl = _split2(w)
    return _dot(xh, wh) + (_dot(xl, wh) + _dot(xh, wl))


def _sigmoid(x):
    return 1.0 / (1.0 + jnp.exp(-x))


def _softplus(x):
    return jnp.maximum(x, 0.0) + jnp.log(1.0 + jnp.exp(-jnp.abs(x)))


def _iota(shape, axis):
    return lax.broadcasted_iota(I32, shape, axis)


def _group_matrix(n, group, value, dtype=BF16):
    r = _iota((n, n), 0) // group
    c = _iota((n, n), 1) // group
    return jnp.where(r == c, value, 0.0).astype(dtype)


def _layer_norm(z, g, b):
    mu = jnp.mean(z, -1, keepdims=True)
    zc = z - mu
    var = jnp.mean(zc * zc, -1, keepdims=True)
    return zc * lax.rsqrt(var + LN_EPS) * g + b


def _params(*sem):
    return pltpu.CompilerParams(dimension_semantics=sem, vmem_limit_bytes=VMEM_LIMIT)


def _proj_kernel(x_ref, w_ref, o_ref):
    o_ref[...] = _dot(x_ref[...].astype(BF16), w_ref[...])


def _project(x2d, w, tm, tn):
    n, k = x2d.shape
    wt = w.shape[1]
    return pl.pallas_call(
        _proj_kernel,
        grid=(n // tm, wt // tn),
        in_specs=[pl.BlockSpec((tm, k), lambda i, j: (i, 0)),
                  pl.BlockSpec((k, tn), lambda i, j: (0, j))],
        out_specs=pl.BlockSpec((tm, tn), lambda i, j: (i, j)),
        out_shape=jax.ShapeDtypeStruct((n, wt), F32),
        compiler_params=_params("parallel", "parallel"),
        name="proj",
    )(x2d, w)


def _mixer_a_kernel(q0_ref, q1_ref, k0_ref, k1_ref, v0_ref, v1_ref, bias_ref, o_ref, op_ref, lse_ref,
                    *, seq, dils):
    q_refs, k_refs, v_refs = (q0_ref, q1_ref), (k0_ref, k1_ref), (v0_ref, v1_ref)
    ab = A_BLOCK
    lane = _iota((ab, LANES), 1)
    upper = lane >= HEAD_DIM
    col = _iota((ab, 2 * ab), 1)
    scale = HEAD_DIM ** -0.5

    for p, d in enumerate(dils):
        per_res = seq // (d * ab)

        def body(idx, carry, p=p, d=d, per_res=per_res):
            r = idx // per_res
            b = idx % per_res
            start = b * (ab * d) + r
            pstart = jnp.maximum(start - ab * d, r)
            if d == 1:
                rows, prow = pl.ds(pl.multiple_of(start, ab), ab), pl.ds(pl.multiple_of(pstart, ab), ab)
            else:
                rows, prow = pl.ds(start, ab, stride=d), pl.ds(pstart, ab, stride=d)
            first_pen = jnp.where(b == 0, NEG, 0.0)
            pen = jnp.where(col < ab, first_pen, 0.0)
            for hp in range(2):
                q_ref, k_ref, v_ref = q_refs[hp], k_refs[hp], v_refs[hp]
                q = q_ref[rows, :]
                kb = jnp.concatenate([k_ref[prow, :], k_ref[rows, :]], 0).astype(BF16)
                vb = jnp.concatenate([v_ref[prow, :], v_ref[rows, :]], 0).astype(BF16)
                outs, lses = [], []
                for hh in range(2):
                    hm = upper if hh else jnp.logical_not(upper)
                    qm = jnp.where(hm, q, 0.0).astype(BF16)
                    s = _dotg(qm, kb, NT) * scale + bias_ref[p, 2 * hp + hh] + pen
                    mx = jnp.max(s, -1, keepdims=True)
                    e = jnp.exp(s - mx)
                    den = jnp.sum(e, -1, keepdims=True)
                    outs.append(_dot(e.astype(BF16), vb) / den)
                    lses.append(mx + jnp.log(den))
                op_ref[2 * p + hp, rows, :] = jnp.where(upper, outs[1], outs[0])
                lse_ref[2 * p + hp, rows, :] = jnp.where(upper, lses[1], lses[0])
            return carry

        def body_group(t, carry, body=body):
            for k in range(A_UNROLL):
                carry = body(A_UNROLL * t + k, carry)
            return carry

        lax.fori_loop(0, seq // (A_UNROLL * ab), body_group, 0)

    cr = 256

    def combine(c, carry):
        rows = pl.ds(pl.multiple_of(c * cr, cr), cr)
        for hp in range(2):
            ls = [lse_ref[2 * p + hp, rows, :] for p in range(len(dils))]
            mx = functools.reduce(jnp.maximum, ls)
            es = [jnp.exp(l - mx) for l in ls]
            num = functools.reduce(lambda a, b: a + b,
                                   [e * op_ref[2 * p + hp, rows, :] for p, e in enumerate(es)])
            o_ref[rows, hp * LANES:(hp + 1) * LANES] = num / functools.reduce(lambda a, b: a + b, es)
        return carry

    lax.fori_loop(0, seq // cr, combine, 0)


def _mixer_a(h, bias, bsz, seq):
    dils = tuple(d for _, d in A_PATTERNS)
    for w, d in A_PATTERNS:
        assert w // d == A_BLOCK and seq % (d * A_BLOCK) == 0
    npat = len(dils)

    def pieces(name):
        off, w = _PIECE_OFF[name]
        return [pl.BlockSpec((seq, LANES), lambda b, o=off + u: (b, o)) for u in range(w)]

    return pl.pallas_call(
        functools.partial(_mixer_a_kernel, seq=seq, dils=dils),
        grid=(bsz,),
        in_specs=pieces("a_q") + pieces("a_k") + pieces("a_v")
        + [pl.BlockSpec(bias.shape, lambda b: (0, 0, 0, 0))],
        out_specs=pl.BlockSpec((seq, BRANCH_WIDTH), lambda b: (b, 0)),
        out_shape=jax.ShapeDtypeStruct((bsz * seq, BRANCH_WIDTH), F32),
        scratch_shapes=[pltpu.VMEM((2 * npat, seq, LANES), F32),
                        pltpu.VMEM((2 * npat, seq, LANES), F32)],
        compiler_params=_params("parallel"),
        name="mixer_a",
    )(h, h, h, h, h, h, bias)


def _mixer_b_kernel(*refs, seq, keep):
    def query_block(i, carry):
        _mixer_b_block(i, *refs, seq=seq, keep=keep)
        return carry

    lax.fori_loop(0, seq // B_QUERIES, query_block, 0)


def _mixer_b_block(i, q_ref, iq_ref, mk_ref, gain_ref, wuvt_ref, bias_ref, o_ref,
                   ckvn_ref, ckvt_ref, ikb_ref, sc_ref, sc16_ref, lg_ref, iqs_ref, qs_ref, *, seq, keep):
    qb = B_QUERIES
    sb = B_KEYS
    bb = Q_BLOCK
    iw_lane = B_LATENT + IDX_DIM

    @pl.when(i == 0)
    def _():
        src = _iota((LANES, BRANCH_WIDTH), 0)
        dst = _iota((LANES, BRANCH_WIDTH), 1)
        is_lat = src < B_LATENT
        rep_lat = jnp.where(jnp.logical_and(is_lat, dst % B_LATENT == src), 1.0, 0.0).astype(BF16)
        avg_lat = jnp.where(is_lat, 1.0 / B_LATENT, 0.0).astype(BF16)
        is_ik = jnp.logical_and(src >= B_LATENT, src < iw_lane)
        rep_ik = jnp.where(jnp.logical_and(is_ik, dst % IDX_DIM == src - B_LATENT), 1.0, 0.0).astype(BF16)

        def prep(c, carry):
            rows = pl.ds(pl.multiple_of(c * sb, sb), sb)
            x = mk_ref[rows, :]
            hi, lo = _split2(x)
            lo2 = (x - hi.astype(F32) - lo.astype(F32)).astype(BF16)
            lat = _dot(hi, rep_lat) + (_dot(lo, rep_lat) + _dot(lo2, rep_lat))
            ms = _dot_rhs_exact(x * x, avg_lat)
            xn = lat * lax.rsqrt(ms + 1e-6) * gain_ref[...]
            ckvn_ref[rows, :] = xn.astype(BF16)
            ckvt_ref[c] = xn.T[0:B_LATENT, :].astype(BF16)
            ikb_ref[rows, :] = _dot(hi, rep_ik).astype(BF16)
            return carry

        lax.fori_loop(0, seq // sb, prep, 0)

    lane = _iota((qb, BRANCH_WIDTH), 1)
    qrows = pl.ds(pl.multiple_of(i * qb, qb), qb)
    iq = iq_ref[qrows, :] * (IDX_DIM ** -0.5)
    for h in range(IDX_HEADS):
        iqs_ref[h * qb:(h + 1) * qb, :] = jnp.where(lane // IDX_DIM == h, iq, 0.0).astype(BF16)
    q = q_ref[qrows, :] * (B_LATENT ** -0.5)
    for h in range(B_HEADS):
        qs_ref[h * qb:(h + 1) * qb, :] = jnp.where(lane // B_LATENT == h, q, 0.0).astype(BF16)
    iw_t = (mk_ref[qrows, :] * (IDX_HEADS ** -0.5)).T[iw_lane:iw_lane + IDX_HEADS, :]
    krow = _iota((sb, qb), 0)
    qcol = _iota((sb, qb), 1)
    nsb = ((i + 1) * qb + sb - 1) // sb

    def fold(x, op):
        parts = [x[r:r + 8] for r in range(0, sb, 8)]
        while len(parts) > 1:
            parts = [op(parts[k], parts[k + 1]) for k in range(0, len(parts), 2)]
        return parts[0]

    def for_steps(body, init):
        def pair(p, carry):
            return body(2 * p + 1, body(2 * p, carry))

        carry = lax.fori_loop(0, nsb // 2, pair, init)
        return lax.cond(nsb % 2 == 1, lambda c: body(nsb - 1, c), lambda c: c, carry)

    def score_body(j, carry):
        kr = pl.ds(pl.multiple_of(j * sb, sb), sb)
        rel = _dotg(ikb_ref[kr, :], iqs_ref[...], NT)
        sc = jnp.maximum(rel[:, 0:qb], 0.0) * iw_t[0:1, :]
        for h in range(1, IDX_HEADS):
            sc = sc + jnp.maximum(rel[:, h * qb:(h + 1) * qb], 0.0) * iw_t[h:h + 1, :]
        sc = jnp.where(krow <= qcol + (i * qb - j * sb), sc, -jnp.inf)
        sc_ref[j] = sc
        sc16_ref[j] = pltpu.bitcast(pltpu.bitcast(sc, I32) & jnp.int32(-65536), F32).astype(BF16)
        logits = _dotg(ckvn_ref[kr, :], qs_ref[...], NT)
        d0 = i * (qb // bb) - j * (sb // bb)
        for h in range(B_HEADS):
            for a in range(sb // bb):
                for b in range(qb // bb):
                    bias = bias_ref[jnp.maximum(d0 + b - a, 0), h]
                    lg_ref[h, j, a * bb:(a + 1) * bb, b * bb:(b + 1) * bb] = (
                        logits[a * bb:(a + 1) * bb, h * qb + b * bb:h * qb + (b + 1) * bb] + bias)
        return carry

    for_steps(score_body, 0)

    sign = jnp.int32(-2 ** 31)

    def key_to_float(u):
        k = u ^ sign
        bits = k ^ ((k >> 31) & jnp.int32(0x7FFFFFFF))
        return pltpu.bitcast(bits, F32)

    def count(pred):
        def cnt(j, acc):
            return acc + fold(pred(j), jnp.add)
        acc = for_steps(cnt, jnp.zeros((8, qb), F32))
        return jnp.sum(acc, 0, keepdims=True)

    pk = 16

    def count_ge16(cf):
        cf = pltpu.bitcast(pltpu.bitcast(cf, I32) & jnp.int32(-65536), F32)
        c16 = jnp.broadcast_to(cf, (pk, qb)).astype(BF16)
        one, zero = jnp.ones((pk, qb), BF16), jnp.zeros((pk, qb), BF16)

        def cnt(j, acc):
            x = sc16_ref[j]
            parts = [jnp.where(x[r:r + pk] >= c16, one, zero) for r in range(0, sb, pk)]
            while len(parts) > 1:
                parts = [parts[k] + parts[k + 1] for k in range(0, len(parts), 2)]
            return acc + parts[0].astype(F32)

        acc = for_steps(cnt, jnp.zeros((pk, qb), F32))
        return jnp.sum(acc, 0, keepdims=True)

    def bit_body(t, state, packed=False):
        u, n_u = state
        cand = u | jnp.left_shift(jnp.int32(1), 31 - t)
        cf = key_to_float(cand)
        c = count_ge16(cf) if packed else count(lambda j: jnp.where(sc_ref[j] >= cf, 1.0, 0.0))
        take = c >= keep
        return jnp.where(take, cand, u), jnp.where(take, c, n_u)

    def bit_body16(t, state):
        return bit_body(t, state, packed=True)

    first_bits, bit_group = 24, 4
    state = (jnp.zeros((1, qb), I32), (jnp.zeros((1, qb), I32) + nsb * sb).astype(F32))
    state = bit_body16(0, state)
    n_pos = count(lambda j: jnp.where(sc_ref[j] > 0.0, 1.0, 0.0))
    zero_thr = jnp.logical_and(state[0] != 0, n_pos < keep)
    state = lax.fori_loop(1, 16, bit_body16, state)
    u, n_ge = lax.fori_loop(16, first_bits, bit_body, state)

    def more_bits(s):
        t, _, n_u = s
        open_rows = jnp.where(zero_thr, 0.0, jnp.abs(n_u - keep))
        return jnp.logical_and(t < 32, jnp.max(open_rows) > 0.0)

    def bit_group_body(s):
        t, u, n_u = s
        for k in range(bit_group):
            u, n_u = bit_body(t + k, (u, n_u))
        return t + bit_group, u, n_u

    _, u, n_ge = lax.while_loop(more_bits, bit_group_body, (jnp.int32(first_bits), u, n_ge))
    u = jnp.maximum(u ^ sign, jnp.int32(0x007FFFFF - 2 ** 31)) ^ sign
    thr = key_to_float(u)
    n_gt = count(lambda j: jnp.where(sc_ref[j] > thr, 1.0, 0.0))
    need = keep - n_gt

    nbits = int(math.ceil(math.log2(seq))) + 1

    def cut_search():
        def cut_body(t, cut):
            cand = cut | jnp.left_shift(jnp.int32(1), nbits - 1 - t)
            c = count(lambda j: jnp.where(sc_ref[j] == thr,
                                          jnp.where(krow < cand - j * sb, 1.0, 0.0), 0.0))
            return jnp.where(c <= need, cand, cut)

        return lax.fori_loop(0, nbits, cut_body, jnp.zeros((1, qb), I32))

    surplus = jnp.max(n_ge) > keep
    cut = lax.cond(surplus, cut_search, lambda: jnp.full((1, qb), 2 ** nbits - 1, I32))
    cut = jnp.minimum(cut, i * qb + 1 + _iota((1, qb), 1))

    def mask_body(j, mx):
        sc = sc_ref[j]
        sel = jnp.where(sc > thr, 1.0, jnp.where(sc == thr, jnp.where(krow < cut - j * sb, 1.0, 0.0), 0.0))
        out = []
        for h in range(B_HEADS):
            s = jnp.where(sel > 0.5, lg_ref[h, j], NEG)
            lg_ref[h, j] = s
            out.append(jnp.maximum(mx[h], fold(s, jnp.maximum)))
        return tuple(out)

    mx = for_steps(mask_body, tuple(jnp.full((8, qb), NEG, F32) for _ in range(B_HEADS)))
    ms = [jnp.max(m, 0, keepdims=True) for m in mx]

    def att_body(j, carry):
        ls, acc = carry
        ps, new_ls = [], []
        for h in range(B_HEADS):
            pr = jnp.exp(lg_ref[h, j] - ms[h])
            new_ls.append(ls[h] + fold(pr, jnp.add))
            ps.append(pr.astype(BF16))
        upd = _dot(ckvt_ref[j], jnp.concatenate(ps, 1))
        return tuple(new_ls), acc + upd

    init = (tuple(jnp.zeros((8, qb), F32) for _ in range(B_HEADS)),
            jnp.zeros((B_LATENT, B_HEADS * qb), F32))
    ls, acc = for_steps(att_body, init)
    ls = [jnp.sum(l, 0, keepdims=True) for l in ls]
    o_t = (acc / jnp.concatenate(ls, 1)).astype(BF16)
    y_t = jnp.concatenate([_dot(wuvt_ref[h], o_t[:, h * qb:(h + 1) * qb]) for h in range(B_HEADS)], 0)
    o_ref[qrows, :] = y_t.T


def _mixer_b(h, gain4, wuv_t, bias, bsz, seq):
    qb, sb = B_QUERIES, B_KEYS
    assert seq % qb == 0 and seq % sb == 0 and qb % Q_BLOCK == 0 and sb % Q_BLOCK == 0
    nstep = seq // sb
    keep = min(TOPK_MAX, seq // TOPK_DIV)

    def kpiece(name):
        off, w = _PIECE_OFF[name]
        return pl.BlockSpec((seq, w * LANES), lambda b, o=off // w: (b, o))

    return pl.pallas_call(
        functools.partial(_mixer_b_kernel, seq=seq, keep=keep),
        grid=(bsz,),
        in_specs=[kpiece("b_q"), kpiece("b_iq"), kpiece("b_misc"),
                  pl.BlockSpec(gain4.shape, lambda b: (0, 0)),
                  pl.BlockSpec(wuv_t.shape, lambda b: (0, 0, 0)),
                  pl.BlockSpec(bias.shape, lambda b: (0, 0, 0, 0))],
        out_specs=pl.BlockSpec((seq, BRANCH_WIDTH), lambda b: (b, 0)),
        out_shape=jax.ShapeDtypeStruct((bsz * seq, BRANCH_WIDTH), F32),
        scratch_shapes=[pltpu.VMEM((seq, BRANCH_WIDTH), BF16),
                        pltpu.VMEM((nstep, B_LATENT, sb), BF16),
                        pltpu.VMEM((seq, BRANCH_WIDTH), BF16),
                        pltpu.VMEM((nstep, sb, qb), F32),
                        pltpu.VMEM((nstep, sb, qb), BF16),
                        pltpu.VMEM((B_HEADS, nstep, sb, qb), F32),
                        pltpu.VMEM((IDX_HEADS * qb, BRANCH_WIDTH), BF16),
                        pltpu.VMEM((B_HEADS * qb, BRANCH_WIDTH), BF16)],
        compiler_params=_params("parallel"),
        name="mixer_b",
    )(h, h, h, gain4, wuv_t, bias)


def _mixer_c_kernel(q_ref, k_ref, v_ref, g_ref, a_ref, aup_ref, abias_ref, gain_ref, o_ref, *, seq):
    ch = C_CHUNK
    kw = C_HEADS * C_KEY_DIM
    vw = C_HEADS * C_VAL_DIM
    tri = jnp.where(_iota((ch, ch), 1) <= _iota((ch, ch), 0), 1.0, 0.0).astype(BF16)
    causal = _iota((C_HEADS * ch, ch), 1) <= (_iota((C_HEADS * ch, ch), 0) % ch)
    klane_head = _iota((C_HEADS * ch, kw), 1) // C_KEY_DIM
    krow_head = _iota((C_HEADS * ch, kw), 0) // ch
    vlane_head = _iota((ch, vw), 1) // C_VAL_DIM
    st_mask = (_iota((vw, kw), 0) // C_VAL_DIM) == (_iota((vw, kw), 1) // C_KEY_DIM)
    gmat = _group_matrix(vw, C_VAL_DIM, 1.0 / C_VAL_DIM)
    aup = aup_ref[...]

    def body(c, st):
        rows = pl.ds(pl.multiple_of(c * ch, ch), ch)
        qc = q_ref[rows, :] * (C_KEY_DIM ** -0.5)
        kc = k_ref[rows, :]
        vc = v_ref[rows, :]
        z = _dot3(a_ref[rows, :], aup) + abias_ref[...]
        log_a = -_softplus(-z) / C_GATE_TAU
        cum = _dot_lhs_exact(tri, log_a)
        last = cum[ch - 1:ch, :]
        q_dec = qc * jnp.exp(cum)
        k_inv = (kc * jnp.exp(-cum)).astype(BF16)
        k_dec = (kc * jnp.exp(last - cum)).astype(BF16)
        vb = vc.astype(BF16)
        qd_b = q_dec.astype(BF16)
        q_stack = jnp.where(klane_head == krow_head, jnp.concatenate([q_dec] * C_HEADS, 0), 0.0)
        att = jnp.where(causal, _dotg(q_stack.astype(BF16), k_inv, NT), 0.0)
        full = _dot(att.astype(BF16), vb)
        o = _dotg(qd_b, st.astype(BF16), NT)
        for h in range(C_HEADS):
            o = o + jnp.where(vlane_head == h, full[h * ch:(h + 1) * ch], 0.0)
        upd = _dotg(vb, k_dec, TN)
        st = st * jnp.exp(last) + jnp.where(st_mask, upd, 0.0)
        ms = _dot_rhs_exact(o * o, gmat)
        o = o * lax.rsqrt(ms + 1e-6) * gain_ref[...]
        g = g_ref[rows, :]
        o_ref[rows, :] = g * _sigmoid(g) * o
        return st

    def body_group(t, st):
        for k in range(C_UNROLL):
            st = body(C_UNROLL * t + k, st)
        return st

    lax.fori_loop(0, seq // (C_UNROLL * ch), body_group, jnp.zeros((vw, kw), F32))


def _mixer_c(h, aup_pad, abias, gain, bsz, seq):
    assert seq % C_CHUNK == 0

    def piece(name):
        off, w = _PIECE_OFF[name]
        return pl.BlockSpec((seq, w * LANES), lambda b, o=off // w: (b, o))

    def const(a):
        return pl.BlockSpec(a.shape, lambda b: (0, 0))

    return pl.pallas_call(
        functools.partial(_mixer_c_kernel, seq=seq),
        grid=(bsz,),
        in_specs=[piece("c_q"), piece("c_k"), piece("c_v"), piece("c_g"), piece("c_a"),
                  const(aup_pad), const(abias), const(gain)],
        out_specs=pl.BlockSpec((seq, BRANCH_WIDTH), lambda b: (b, 0)),
        out_shape=jax.ShapeDtypeStruct((bsz * seq, BRANCH_WIDTH), F32),
        compiler_params=_params("parallel"),
        name="mixer_c",
    )(h, h, h, h, h, aup_pad, abias, gain)


def _mixer_d_kernel(r_ref, k_ref, v_ref, low_ref, mu_ref, mulow_ref, w2_ref, a2_ref, g2_ref, vec_ref,
                    o_ref, *, seq, nsub):
    ch = D_CHUNK
    bw = BRANCH_WIDTH
    nh = D_HEADS
    hs = D_HEAD_SIZE
    r_i = _iota((bw, bw), 0)
    c_i = _iota((bw, bw), 1)
    same = (r_i // hs) == (c_i // hs)
    strict = c_i < r_i
    incl = c_i <= r_i
    eye = r_i == c_i
    ones_bd = _group_matrix(bw, hs, 1.0)
    avg_bd = _group_matrix(bw, hs, 1.0 / hs)
    rows = nsub * ch
    tr_i = _iota((rows, rows), 0)
    tc_i = _iota((rows, rows), 1)
    tri = jnp.where(jnp.logical_and(tr_i // ch == tc_i // ch, tc_i <= tr_i), 1.0, 0.0).astype(BF16)
    row0 = _iota((rows, bw), 0) == 0
    row0_low = _iota((rows, LANES), 0) == 0
    vec = vec_ref[...]
    w0, a0, k_k, k_a, r_k, gn_w, gn_b = (vec[n:n + 1, :] for n in range(7))
    mu = mu_ref[...]
    w2, a2, g2 = w2_ref[...], a2_ref[...], g2_ref[...]

    def wide(x):
        return jnp.where(same, jnp.concatenate([x] * nh, 0), 0.0)

    def shifted(ref, start, first_row, m):
        cur = ref[pl.ds(start, rows), :]
        last8 = ref[pl.ds(pl.multiple_of(jnp.maximum(start - 8, 0), 8), 8), :]
        prev_row = last8[7:8, :] * jnp.where(start > 0, 1.0, 0.0)
        prev = jnp.where(first_row, prev_row, pltpu.roll(cur, 1, 0))
        return cur + (prev - cur) * m

    def body(c, st):
        start = pl.multiple_of(c * rows, rows)
        r = shifted(r_ref, start, row0, mu[0:1, :])
        k = shifted(k_ref, start, row0, mu[1:2, :])
        v = shifted(v_ref, start, row0, mu[2:3, :])
        low = shifted(low_ref, start, row0_low, mulow_ref[...])
        w_raw = -_softplus(-(w0 + _dot3(jnp.tanh(low), w2))) - 0.5
        lw = -jnp.exp(w_raw)
        a = _sigmoid(a0 + _dot3(low, a2))
        g = _dot3(_sigmoid(low), g2)
        kk = k * k_k
        kk = kk / jnp.maximum(jnp.sqrt(_dot_rhs_exact(kk * kk, ones_bd)), 1e-12)
        k2 = k * (1.0 + (a - 1.0) * k_a)
        bonus = _dot_rhs_exact(r * k2 * r_k, ones_bd) * v
        b = kk * a

        cum = _dot_lhs_exact(tri, lw)
        e_in = jnp.exp(cum)
        e_inv = jnp.exp(-cum)
        a_dec = -kk * jnp.exp(cum - lw)
        r_dec = r * e_in
        b_inv = b * e_inv
        k_inv = k2 * e_inv
        local = [chunk_terms(*(z[s * ch:(s + 1) * ch] for z in (a_dec, r_dec, b_inv, k_inv, v, e_in)))
                 for s in range(nsub)]
        ys = []
        for atp, vp, xr, m_rb, y_kv, bh, h_kv, g_col in local:
            st16 = st.astype(BF16)
            u16 = (_dot(atp, st16) + vp).astype(BF16)
            y_w = _dot(xr, st16) + _dot(m_rb, u16) + y_kv
            st = g_col * st + _dotg(bh, u16, TN) + h_kv
            y = y_w[0:ch]
            for h in range(1, nh):
                y = y + y_w[h * ch:(h + 1) * ch]
            ys.append(y)
        y = jnp.concatenate(ys, 0)
        mean = _dot_rhs_exact(y, avg_bd)
        yc = y - mean
        var = _dot_rhs_exact(yc * yc, avg_bd)
        yn = yc * lax.rsqrt(var + D_GN_EPS) * gn_w + gn_b
        o_ref[pl.ds(start, rows), :] = (yn + bonus) * g
        return st

    def chunk_terms(a_dec, r_dec, b_inv, k_inv, v, e_in):
        g_last = e_in[ch - 1:ch, :]
        xa = wide(a_dec).astype(BF16)
        xr = wide(r_dec).astype(BF16)
        yb = wide(b_inv).astype(BF16)
        yk = wide(k_inv).astype(BF16)
        bh = wide(b_inv * g_last).astype(BF16)
        kh = wide(k_inv * g_last).astype(BF16)
        vw = wide(v).astype(BF16)

        a_ab = jnp.where(strict, _dotg(xa, yb, NT), 0.0)
        a_ak = jnp.where(strict, _dotg(xa, yk, NT), 0.0).astype(BF16)
        m_rb = jnp.where(incl, _dotg(xr, yb, NT), 0.0).astype(BF16)
        m_rk = jnp.where(incl, _dotg(xr, yk, NT), 0.0).astype(BF16)

        t_inv = jnp.where(eye, 1.0, a_ab)
        apow = a_ab
        for _ in range(int(math.log2(ch)) - 1):
            ab16 = apow.astype(BF16)
            apow = _dot(ab16, ab16)
            t_inv = t_inv + _dot(t_inv.astype(BF16), apow.astype(BF16))
        t16 = t_inv.astype(BF16)

        atp = _dot(t16, xa).astype(BF16)
        vp = _dot(t16, _dot(a_ak, vw).astype(BF16))
        y_kv = _dot(m_rk, vw)
        h_kv = _dotg(kh, vw, TN)
        g_col = jnp.sum(jnp.where(eye, g_last, 0.0), -1, keepdims=True)
        return atp, vp, xr, m_rb, y_kv, bh, h_kv, g_col

    lax.fori_loop(0, seq // rows, body, jnp.zeros((bw, bw), F32))


def _mixer_d(h, mu3, mulow, w2p, a2p, g2p, vec, bsz, seq, nsub=4):
    assert seq % (nsub * D_CHUNK) == 0 and D_CHUNK == D_HEAD_SIZE

    def piece(name):
        off, w = _PIECE_OFF[name]
        return pl.BlockSpec((seq, w * LANES), lambda b, o=off // w: (b, o))

    def const(a):
        return pl.BlockSpec(a.shape, lambda b: (0, 0))

    return pl.pallas_call(
        functools.partial(_mixer_d_kernel, seq=seq, nsub=nsub),
        grid=(bsz,),
        in_specs=[piece("d_r"), piece("d_k"), piece("d_v"), piece("d_low"),
                  const(mu3), const(mulow), const(w2p), const(a2p), const(g2p), const(vec)],
        out_specs=pl.BlockSpec((seq, BRANCH_WIDTH), lambda b: (b, 0)),
        out_shape=jax.ShapeDtypeStruct((bsz * seq, BRANCH_WIDTH), F32),
        compiler_params=_params("parallel"),
        name="mixer_d",
    )(h, h, h, h, mu3, mulow, w2p, a2p, g2p, vec)


def _merge_kernel(x_ref, ya_ref, yb_ref, yc_ref, yd_ref, wg_ref, wb_ref, wo_ref, g_ref, b_ref, o_ref,
                  *, alpha):
    x = x_ref[...]
    xb = x.astype(BF16)
    merged = None
    for n, y_ref in enumerate((ya_ref, yb_ref, yc_ref, yd_ref)):
        gate = _sigmoid(_dot(xb, wg_ref[:, n * D_MODEL:(n + 1) * D_MODEL]))
        term = gate * _dot(y_ref[...].astype(BF16), wb_ref[n])
        merged = term if merged is None else merged + term
    z = alpha * x + _dot(merged.astype(BF16), wo_ref[...])
    o_ref[...] = _layer_norm(z, g_ref[...], b_ref[...])


def _merge(x2d, ys, wg, wb, wo, g, b, alpha, tm):
    n, d = x2d.shape

    def const(a):
        nd = a.ndim
        return pl.BlockSpec(a.shape, lambda i: (0,) * nd, pipeline_mode=pl.Buffered(1))

    yspec = pl.BlockSpec((tm, BRANCH_WIDTH), lambda i: (i, 0))
    return pl.pallas_call(
        functools.partial(_merge_kernel, alpha=alpha),
        grid=(n // tm,),
        in_specs=[pl.BlockSpec((tm, d), lambda i: (i, 0)), yspec, yspec, yspec, yspec,
                  const(wg), const(wb), const(wo), const(g), const(b)],
        out_specs=pl.BlockSpec((tm, d), lambda i: (i, 0)),
        out_shape=jax.ShapeDtypeStruct((n, d), F32),
        compiler_params=_params("parallel"),
        name="merge",
    )(x2d, *ys, wg, wb, wo, g, b)


def _moe_kernel(x_ref, ltri_ref, wr_ref, br_ref, wg_ref, wu_ref, wd_ref, g_ref, b_ref, o_ref,
                xs_ref, gs_ref, ys_ref, pt_ref, seg_ref, *, alpha):
    e = pl.program_id(1)
    tm = x_ref.shape[0]
    nrow = xs_ref.shape[0]
    ck = MOE_CHUNK
    lane = _iota((tm, LANES), 1)

    @pl.when(e == 0)
    def _():
        x = x_ref[...]
        xb = x.astype(BF16)
        ys_ref[...] = jnp.zeros_like(ys_ref)
        x_lo = (x - xb.astype(F32)).astype(BF16)
        cross = _dot(jnp.concatenate([xb, x_lo], 0), wr_ref[...])
        logits = (cross[0:tm, 0:LANES] + (cross[0:tm, LANES:] + cross[tm:, 0:LANES])) + br_ref[...]
        is_g = jnp.logical_and(lane >= MOE_EXPERTS, lane < MOE_EXPERTS + MOE_GROUPS)
        lg = jnp.where(is_g, logits, -jnp.inf)
        gmax = jnp.max(lg, -1, keepdims=True)
        ptop = 1.0 / jnp.sum(jnp.exp(lg - gmax), -1, keepdims=True)
        gsel = jnp.min(jnp.where(lg == gmax, lane, 2 * LANES), -1, keepdims=True) - MOE_EXPERTS
        in_group = jnp.logical_and(lane < MOE_EXPERTS, lane // MOE_PER_GROUP == gsel)
        le = jnp.where(in_group, logits, -jnp.inf)
        v1 = jnp.max(le, -1, keepdims=True)
        i1 = jnp.min(jnp.where(le == v1, lane, 2 * LANES), -1, keepdims=True)
        le2 = jnp.where(lane == i1, -jnp.inf, le)
        v2 = jnp.max(le2, -1, keepdims=True)
        i2 = jnp.min(jnp.where(le2 == v2, lane, 2 * LANES), -1, keepdims=True)
        e2 = jnp.exp(v2 - v1)
        w1 = ptop / (1.0 + e2)
        gate = jnp.where(lane == i1, w1, jnp.where(lane == i2, w1 * e2, 0.0))

        onehot = jnp.where(lane == gsel, 1.0, 0.0)
        rank = _dot(ltri_ref[...], onehot.astype(BF16))
        cnt = jnp.sum(onehot, 0, keepdims=True).astype(I32)
        nch = jnp.right_shift(cnt + (ck - 1), int(math.log2(ck)))
        lane1 = _iota((1, LANES), 1)
        off = jnp.int32(0)
        off_vec = jnp.zeros((1, LANES), I32)
        for grp in range(MOE_GROUPS):
            n_g = nch[0, grp]
            seg_ref[grp] = off
            seg_ref[MOE_GROUPS + grp] = n_g
            off_vec = jnp.where(lane1 == grp, off, off_vec)
            off = off + n_g * ck
        dest = jnp.sum(onehot * (rank + off_vec.astype(F32)), -1, keepdims=True)
        pt_ref[...] = jnp.where(_iota((tm, nrow), 1) == dest.astype(I32), 1.0, 0.0).astype(BF16)
        dest_row = jnp.broadcast_to(dest, (tm, LANES)).T[0:1, :].astype(I32)
        perm = jnp.where(_iota((nrow, tm), 0) == dest_row, 1.0, 0.0).astype(BF16)
        gate_hi, gate_lo = _split2(gate)
        moved = _dot(perm, jnp.concatenate([xb, gate_hi, gate_lo], 1))
        d = x_ref.shape[1]
        xs_ref[...] = moved[:, 0:d].astype(BF16)
        gs_ref[...] = moved[:, d:d + LANES] + moved[:, d + LANES:]

    nse = MOE_STEP_EXPERTS
    grp = e // (MOE_PER_GROUP // nse)
    seg_off = seg_ref[grp]
    seg_chunks = seg_ref[MOE_GROUPS + grp]

    def run_experts(start, size):
        rows = pl.ds(pl.multiple_of(start, ck), size)
        xc = xs_ref[rows, :]
        gs = gs_ref[rows, :]
        lane_s = _iota((size, LANES), 1)
        hidden = []
        for k in range(nse):
            gk = jnp.sum(jnp.where(lane_s == nse * e + k, gs, 0.0), -1, keepdims=True)
            hg = _dot(xc, wg_ref[k])
            hu = _dot(xc, wu_ref[k])
            hidden.append((hg * _sigmoid(hg) * hu * gk).astype(BF16))
        wd = wd_ref[...].reshape(nse * MOE_HIDDEN, wd_ref.shape[2])
        ys_ref[rows, :] += _dot(jnp.concatenate(hidden, 1), wd)

    big = 4

    def quad(c, carry):
        run_experts(seg_off + c * (big * ck), big * ck)
        return carry

    lax.fori_loop(0, seg_chunks // big, quad, 0)
    rem = seg_chunks % big
    rem_start = seg_off + (seg_chunks - rem) * ck
    for r in range(1, big):
        @pl.when(rem == r)
        def _(r=r):
            run_experts(rem_start, r * ck)

    @pl.when(e == pl.num_programs(1) - 1)
    def _():
        z = alpha * x_ref[...] + _dot(pt_ref[...], ys_ref[...].astype(BF16))
        o_ref[...] = _layer_norm(z, g_ref[...], b_ref[...])


def _moe(x2d, wr, br, wg, wu, wd, g, b, alpha, tm):
    n, d = x2d.shape
    ne = wg.shape[0]
    hid = wg.shape[2]
    nse = MOE_STEP_EXPERTS
    assert MOE_PER_GROUP % nse == 0 and ne % nse == 0
    nrow = tm + MOE_GROUPS * MOE_CHUNK
    ltri = jnp.tril(jnp.ones((tm, tm), BF16), -1)

    def const(a):
        return pl.BlockSpec(a.shape, lambda i, e: (0, 0), pipeline_mode=pl.Buffered(1))

    return pl.pallas_call(
        functools.partial(_moe_kernel, alpha=alpha),
        grid=(n // tm, ne // nse),
        in_specs=[pl.BlockSpec((tm, d), lambda i, e: (i, 0)), const(ltri), const(wr), const(br),
                  pl.BlockSpec((nse, d, hid), lambda i, e: (e, 0, 0)),
                  pl.BlockSpec((nse, d, hid), lambda i, e: (e, 0, 0)),
                  pl.BlockSpec((nse, hid, d), lambda i, e: (e, 0, 0)),
                  const(g), const(b)],
        out_specs=pl.BlockSpec((tm, d), lambda i, e: (i, 0)),
        out_shape=jax.ShapeDtypeStruct((n, d), F32),
        scratch_shapes=[pltpu.VMEM((nrow, d), BF16), pltpu.VMEM((nrow, LANES), F32),
                        pltpu.VMEM((nrow, d), F32), pltpu.VMEM((tm, nrow), BF16),
                        pltpu.SMEM((2 * MOE_GROUPS,), I32)],
        compiler_params=_params("parallel", "arbitrary"),
        name="moe",
    )(x2d, ltri, wr, br, wg, wu, wd, g, b)


def _t5_bucket(dist):
    exact = RPB_BUCKETS // 2
    d = jnp.maximum(dist, 0)
    df = jnp.maximum(d, 1).astype(F32)
    large = exact + (jnp.log(df / exact) / math.log(RPB_MAX_DIST / exact)
                     * (RPB_BUCKETS - exact)).astype(I32)
    large = jnp.minimum(large, RPB_BUCKETS - 1)
    return jnp.where(d < exact, d, large)


def _rpb_lookup(tab, dist):
    bucket = _t5_bucket(dist)[..., None]
    out = jnp.zeros(dist.shape + (tab.shape[1],), F32)
    for k in range(RPB_BUCKETS):
        out = jnp.where(bucket == k, tab[k], out)
    return out


def _bias_a(tab):
    qi = jnp.arange(A_BLOCK)[:, None]
    kj = jnp.arange(2 * A_BLOCK)[None, :]
    rdist = qi + A_BLOCK - kj
    out = []
    for window, dilation in A_PATTERNS:
        in_band = (rdist >= 0) & (rdist <= window // dilation)
        bias = jnp.transpose(_rpb_lookup(tab, rdist * dilation), (2, 0, 1))
        out.append(jnp.where(in_band[None], bias, NEG))
    return jnp.stack(out).astype(F32)


def _bias_b(tab, nblk):
    qi = jnp.arange(Q_BLOCK)[:, None]
    kj = jnp.arange(Q_BLOCK)[None, :]
    delta = jnp.arange(nblk)[:, None, None] * Q_BLOCK
    bias = _rpb_lookup(tab, delta + qi - kj)
    return jnp.transpose(bias, (0, 3, 2, 1)).astype(F32)


def _in_offsets():
    splits = (("a_q", 256), ("a_k", 256), ("a_v", 256), ("b_q", 256), ("b_ckv", 64), ("b_iq", 256),
              ("b_ik", 32), ("b_iw", 8), ("c_q", 128), ("c_k", 128), ("c_v", 256), ("c_a", 16),
              ("c_g", 256), ("d", 832), ("gate", 4096))
    off, out = 0, {}
    for name, w in splits:
        out[name] = (off, w)
        off += w
    return out


def _proj_columns():
    src = _in_offsets()
    d0 = src["d"][0]
    named = {
        "a_q": np.arange(256) + src["a_q"][0], "a_k": np.arange(256) + src["a_k"][0],
        "a_v": np.arange(256) + src["a_v"][0], "b_q": np.arange(256) + src["b_q"][0],
        "b_iq": np.arange(256) + src["b_iq"][0],
        "b_misc": np.concatenate([np.arange(B_LATENT) + src["b_ckv"][0], np.arange(IDX_DIM) + src["b_ik"][0],
                                  np.arange(IDX_HEADS) + src["b_iw"][0]]),
        "c_v": np.arange(256) + src["c_v"][0], "c_g": np.arange(256) + src["c_g"][0],
        "d_r": np.arange(256) + d0, "d_k": np.arange(256) + d0 + 256, "d_v": np.arange(256) + d0 + 512,
        "c_q": np.arange(128) + src["c_q"][0], "c_k": np.arange(128) + src["c_k"][0],
        "c_a": np.arange(16) + src["c_a"][0], "d_low": np.arange(64) + d0 + 768,
    }
    cols = []
    for name, w in _PIECES:
        c = named[name]
        cols.append(np.concatenate([c, -np.ones(w * LANES - len(c), np.int64)]))
    return np.concatenate(cols)


def _proj_weight(w):
    cols = _proj_columns()
    cuts = [0] + [k for k in range(1, len(cols))
                  if (cols[k] < 0) != (cols[k - 1] < 0) or (cols[k] >= 0 and cols[k] != cols[k - 1] + 1)]
    cuts.append(len(cols))
    parts = []
    for a, b in zip(cuts[:-1], cuts[1:]):
        if cols[a] < 0:
            parts.append(jnp.zeros((w.shape[0], b - a), BF16))
        else:
            parts.append(w[:, int(cols[a]):int(cols[a]) + (b - a)].astype(BF16))
    return jnp.concatenate(parts, 1)


def _pad_rows(w, first, total):
    return jnp.zeros((total, w.shape[1]), w.dtype).at[first:first + w.shape[0]].set(w)


def kernel(x, rpb_table, w_in, b_kv_gain, b_w_uv, c_a_up, c_a_bias, c_norm_gain, d_mu, d_w0, d_w2,
           d_a0, d_a2, d_g2, d_k_k, d_k_a, d_r_k, d_gn_w, d_gn_b, w_branch, w_out, ln_g, ln_b,
           router_g, router_g_bias, router_e, router_e_bias, moe_w_gate, moe_w_up, moe_w_down):
    bsz, seq, d_model = x.shape
    depth = w_in.shape[0]
    n = bsz * seq
    alpha = (2 * depth) ** 0.25
    gate_off = _in_offsets()["gate"][0]
    bias_a = _bias_a(rpb_table[:, :A_HEADS])
    bias_b = _bias_b(rpb_table[:, A_HEADS:], seq // Q_BLOCK)
    tm_proj = math.gcd(n, 512)
    tm_merge = math.gcd(n, 1024)
    tm_moe = math.gcd(n, 1024)

    x2d = x.reshape(n, d_model)
    for l in range(depth):
        h = _project(x2d, _proj_weight(w_in[l]), tm_proj, PROJ_WIDTH)

        ya = _mixer_a(h, bias_a, bsz, seq)

        gain4 = jnp.tile(b_kv_gain[l], B_HEADS)[None, :]
        wuv_t = jnp.transpose(b_w_uv[l], (0, 2, 1)).astype(BF16)
        yb = _mixer_b(h, gain4, wuv_t, bias_b, bsz, seq)

        aup_pad = _pad_rows(c_a_up[l], 0, LANES)
        yc = _mixer_c(h, aup_pad, c_a_bias[l][None, :], c_norm_gain[l][None, :], bsz, seq)

        mu = d_mu[l]
        mu3 = mu[:3 * BRANCH_WIDTH].reshape(3, BRANCH_WIDTH)
        mulow = jnp.zeros((1, LANES), F32).at[0, :D_DECAY_RANK + D_ICLR_RANK + D_GATE_RANK].set(
            mu[3 * BRANCH_WIDTH:])
        w2p = _pad_rows(d_w2[l], 0, LANES)
        a2p = _pad_rows(d_a2[l], D_DECAY_RANK, LANES)
        g2p = _pad_rows(d_g2[l], D_DECAY_RANK + D_ICLR_RANK, LANES)
        vec = jnp.stack([d_w0[l], d_a0[l], d_k_k[l], d_k_a[l], d_r_k[l], d_gn_w[l], d_gn_b[l],
                         jnp.zeros_like(d_w0[l])])
        yd = _mixer_d(h, mu3, mulow, w2p, a2p, g2p, vec, bsz, seq)

        wg = w_in[l][:, gate_off:gate_off + N_BRANCHES * d_model].astype(BF16)
        x2d = _merge(x2d, (ya, yb, yc, yd), wg, w_branch[l].astype(BF16), w_out[l].astype(BF16),
                     ln_g[l, 0][None, :], ln_b[l, 0][None, :], alpha, tm_merge)

        wr = jnp.zeros((d_model, LANES), F32)
        wr = wr.at[:, :MOE_EXPERTS].set(router_e[l]).at[:, MOE_EXPERTS:MOE_EXPERTS + MOE_GROUPS].set(
            router_g[l])
        br = jnp.zeros((1, LANES), F32)
        br = br.at[0, :MOE_EXPERTS].set(router_e_bias[l]).at[0, MOE_EXPERTS:MOE_EXPERTS + MOE_GROUPS].set(
            router_g_bias[l])
        wr = jnp.concatenate(_split2(wr), 1)
        x2d = _moe(x2d, wr, br, moe_w_gate[l].astype(BF16), moe_w_up[l].astype(BF16),
                   moe_w_down[l].astype(BF16), ln_g[l, 1][None, :], ln_b[l, 1][None, :], alpha, tm_moe)
    return x2d.reshape(bsz, seq, d_model)
```

```python
import functools
import math

import jax
import jax.numpy as jnp
import numpy as np
from jax import lax
from jax.experimental import pallas as pl
from jax.experimental.pallas import tpu as pltpu

F32 = jnp.float32
BF16 = jnp.bfloat16
I32 = jnp.int32

D_MODEL = 1024
N_BRANCHES = 4
BRANCH_WIDTH = 256
HEAD_DIM = 64
A_HEADS = 4
A_PATTERNS = ((128, 1), (512, 4), (2048, 16))
A_BLOCK = 128
A_UNROLL = 4
B_HEADS = 4
B_LATENT = 64
IDX_HEADS = 8
IDX_DIM = 32
TOPK_MAX = 256
TOPK_DIV = 4
Q_BLOCK = 128
B_QUERIES = 512
B_KEYS = 256
C_HEADS = 4
C_KEY_DIM = 32
C_VAL_DIM = 64
C_GATE_RANK = 16
C_GATE_TAU = 16.0
C_CHUNK = 64
C_UNROLL = 4
D_HEADS = 4
D_HEAD_SIZE = 64
D_DECAY_RANK = 16
D_ICLR_RANK = 16
D_GATE_RANK = 32
D_GN_EPS = 64e-5
D_CHUNK = 64
RPB_BUCKETS = 32
RPB_MAX_DIST = 2048
MOE_GROUPS = 4
MOE_PER_GROUP = 8
MOE_EXPERTS = 32
MOE_HIDDEN = 256
MOE_CHUNK = 128
MOE_STEP_EXPERTS = 4
LN_EPS = 1e-5

LANES = 128
NEG = -1e30
VMEM_LIMIT = 56 * 1024 * 1024

NT = (((1,), (1,)), ((), ()))
TN = (((0,), (0,)), ((), ()))

_PIECES = (
    ("a_q", 2), ("a_k", 2), ("a_v", 2), ("b_q", 2), ("b_iq", 2),
    ("c_v", 2), ("c_g", 2), ("d_r", 2), ("d_k", 2), ("d_v", 2),
    ("c_q", 1), ("c_k", 1), ("b_misc", 1), ("c_a", 1), ("d_low", 1),
)
_PIECE_OFF = {}
_off = 0
for _n, _w in _PIECES:
    _PIECE_OFF[_n] = (_off, _w)
    _off += _w
PROJ_WIDTH = _off * LANES


def _dot(a, b):
    return jnp.dot(a, b, preferred_element_type=F32)


def _dotg(a, b, dims):
    return lax.dot_general(a, b, dims, preferred_element_type=F32)


def _split2(x):
    hi = x.astype(BF16)
    lo = (x - hi.astype(F32)).astype(BF16)
    return hi, lo


def _dot_rhs_exact(x, m):
    hi, lo = _split2(x)
    return _dot(hi, m) + _dot(lo, m)


def _dot_lhs_exact(m, x):
    hi, lo = _split2(x)
    return _dot(m, hi) + _dot(m, lo)


def _dot3(x, w):
    xh, xl = _split2(x)
    wh, wl = _split2(w)
    return _dot(xh, wh) + (_dot(xl, wh) + _dot(xh, wl))


def _sigmoid(x):
    return 1.0 / (1.0 + jnp.exp(-x))


def _softplus(x):
    return jnp.maximum(x, 0.0) + jnp.log(1.0 + jnp.exp(-jnp.abs(x)))


def _iota(shape, axis):
    return lax.broadcasted_iota(I32, shape, axis)


def _group_matrix(n, group, value, dtype=BF16):
    r = _iota((n, n), 0) // group
    c = _iota((n, n), 1) // group
    return jnp.where(r == c, value, 0.0).astype(dtype)


def _layer_norm(z, g, b):
    mu = jnp.mean(z, -1, keepdims=True)
    zc = z - mu
    var = jnp.mean(zc * zc, -1, keepdims=True)
    return zc * lax.rsqrt(var + LN_EPS) * g + b


def _params(*sem):
    return pltpu.CompilerParams(dimension_semantics=sem, vmem_limit_bytes=VMEM_LIMIT)


def _proj_kernel(x_ref, w_ref, o_ref):
    o_ref[...] = _dot(x_ref[...].astype(BF16), w_ref[...])


def _project(x2d, w, tm, tn):
    n, k = x2d.shape
    wt = w.shape[1]
    return pl.pallas_call(
        _proj_kernel,
        grid=(n // tm, wt // tn),
        in_specs=[pl.BlockSpec((tm, k), lambda i, j: (i, 0)),
                  pl.BlockSpec((k, tn), lambda i, j: (0, j))],
        out_specs=pl.BlockSpec((tm, tn), lambda i, j: (i, j)),
        out_shape=jax.ShapeDtypeStruct((n, wt), F32),
        compiler_params=_params("parallel", "parallel"),
        name="proj",
    )(x2d, w)


def _mixer_a_kernel(q0_ref, q1_ref, k0_ref, k1_ref, v0_ref, v1_ref, bias_ref, o_ref, op_ref, lse_ref,
                    *, seq, dils):
    q_refs, k_refs, v_refs = (q0_ref, q1_ref), (k0_ref, k1_ref), (v0_ref, v1_ref)
    ab = A_BLOCK
    lane = _iota((ab, LANES), 1)
    upper = lane >= HEAD_DIM
    col = _iota((ab, 2 * ab), 1)
    scale = HEAD_DIM ** -0.5

    for p, d in enumerate(dils):
        per_res = seq // (d * ab)

        def body(idx, carry, p=p, d=d, per_res=per_res):
            r = idx // per_res
            b = idx % per_res
            start = b * (ab * d) + r
            pstart = jnp.maximum(start - ab * d, r)
            if d == 1:
                rows, prow = pl.ds(pl.multiple_of(start, ab), ab), pl.ds(pl.multiple_of(pstart, ab), ab)
            else:
                rows, prow = pl.ds(start, ab, stride=d), pl.ds(pstart, ab, stride=d)
            first_pen = jnp.where(b == 0, NEG, 0.0)
            pen = jnp.where(col < ab, first_pen, 0.0)
            for hp in range(2):
                q_ref, k_ref, v_ref = q_refs[hp], k_refs[hp], v_refs[hp]
                q = q_ref[rows, :]
                kb = jnp.concatenate([k_ref[prow, :], k_ref[rows, :]], 0).astype(BF16)
                vb = jnp.concatenate([v_ref[prow, :], v_ref[rows, :]], 0).astype(BF16)
                outs, lses = [], []
                for hh in range(2):
                    hm = upper if hh else jnp.logical_not(upper)
                    qm = jnp.where(hm, q, 0.0).astype(BF16)
                    s = _dotg(qm, kb, NT) * scale + bias_ref[p, 2 * hp + hh] + pen
                    mx = jnp.max(s, -1, keepdims=True)
                    e = jnp.exp(s - mx)
                    den = jnp.sum(e, -1, keepdims=True)
                    outs.append(_dot(e.astype(BF16), vb) / den)
                    lses.append(mx + jnp.log(den))
                op_ref[2 * p + hp, rows, :] = jnp.where(upper, outs[1], outs[0])
                lse_ref[2 * p + hp, rows, :] = jnp.where(upper, lses[1], lses[0])
            return carry

        def body_group(t, carry, body=body):
            for k in range(A_UNROLL):
                carry = body(A_UNROLL * t + k, carry)
            return carry

        lax.fori_loop(0, seq // (A_UNROLL * ab), body_group, 0)

    cr = 256

    def combine(c, carry):
        rows = pl.ds(pl.multiple_of(c * cr, cr), cr)
        for hp in range(2):
            ls = [lse_ref[2 * p + hp, rows, :] for p in range(len(dils))]
            mx = functools.reduce(jnp.maximum, ls)
            es = [jnp.exp(l - mx) for l in ls]
            num = functools.reduce(lambda a, b: a + b,
                                   [e * op_ref[2 * p + hp, rows, :] for p, e in enumerate(es)])
            o_ref[rows, hp * LANES:(hp + 1) * LANES] = num / functools.reduce(lambda a, b: a + b, es)
        return carry

    lax.fori_loop(0, seq // cr, combine, 0)


def _mixer_a(h, bias, bsz, seq):
    dils = tuple(d for _, d in A_PATTERNS)
    for w, d in A_PATTERNS:
        assert w // d == A_BLOCK and seq % (d * A_BLOCK) == 0
    npat = len(dils)

    def pieces(name):
        off, w = _PIECE_OFF[name]
        return [pl.BlockSpec((seq, LANES), lambda b, o=off + u: (b, o)) for u in range(w)]

    return pl.pallas_call(
        functools.partial(_mixer_a_kernel, seq=seq, dils=dils),
        grid=(bsz,),
        in_specs=pieces("a_q") + pieces("a_k") + pieces("a_v")
        + [pl.BlockSpec(bias.shape, lambda b: (0, 0, 0, 0))],
        out_specs=pl.BlockSpec((seq, BRANCH_WIDTH), lambda b: (b, 0)),
        out_shape=jax.ShapeDtypeStruct((bsz * seq, BRANCH_WIDTH), F32),
        scratch_shapes=[pltpu.VMEM((2 * npat, seq, LANES), F32),
                        pltpu.VMEM((2 * npat, seq, LANES), F32)],
        compiler_params=_params("parallel"),
        name="mixer_a",
    )(h, h, h, h, h, h, bias)


def _mixer_b_kernel(*refs, seq, keep):
    def query_block(i, carry):
        _mixer_b_block(i, *refs, seq=seq, keep=keep)
        return carry

    lax.fori_loop(0, seq // B_QUERIES, query_block, 0)


def _mixer_b_block(i, q_ref, iq_ref, mk_ref, gain_ref, wuvt_ref, bias_ref, o_ref,
                   ckvn_ref, ckvt_ref, ikb_ref, sc_ref, sc16_ref, lg_ref, iqs_ref, qs_ref, *, seq, keep):
    qb = B_QUERIES
    sb = B_KEYS
    bb = Q_BLOCK
    iw_lane = B_LATENT + IDX_DIM

    @pl.when(i == 0)
    def _():
        src = _iota((LANES, BRANCH_WIDTH), 0)
        dst = _iota((LANES, BRANCH_WIDTH), 1)
        is_lat = src < B_LATENT
        rep_lat = jnp.where(jnp.logical_and(is_lat, dst % B_LATENT == src), 1.0, 0.0).astype(BF16)
        avg_lat = jnp.where(is_lat, 1.0 / B_LATENT, 0.0).astype(BF16)
        is_ik = jnp.logical_and(src >= B_LATENT, src < iw_lane)
        rep_ik = jnp.where(jnp.logical_and(is_ik, dst % IDX_DIM == src - B_LATENT), 1.0, 0.0).astype(BF16)

        def prep(c, carry):
            rows = pl.ds(pl.multiple_of(c * sb, sb), sb)
            x = mk_ref[rows, :]
            hi, lo = _split2(x)
            lo2 = (x - hi.astype(F32) - lo.astype(F32)).astype(BF16)
            lat = _dot(hi, rep_lat) + (_dot(lo, rep_lat) + _dot(lo2, rep_lat))
            ms = _dot_rhs_exact(x * x, avg_lat)
            xn = lat * lax.rsqrt(ms + 1e-6) * gain_ref[...]
            ckvn_ref[rows, :] = xn.astype(BF16)
            ckvt_ref[c] = xn.T[0:B_LATENT, :].astype(BF16)
            ikb_ref[rows, :] = _dot(hi, rep_ik).astype(BF16)
            return carry

        lax.fori_loop(0, seq // sb, prep, 0)

    lane = _iota((qb, BRANCH_WIDTH), 1)
    qrows = pl.ds(pl.multiple_of(i * qb, qb), qb)
    iq = iq_ref[qrows, :] * (IDX_DIM ** -0.5)
    for h in range(IDX_HEADS):
        iqs_ref[h * qb:(h + 1) * qb, :] = jnp.where(lane // IDX_DIM == h, iq, 0.0).astype(BF16)
    q = q_ref[qrows, :] * (B_LATENT ** -0.5)
    for h in range(B_HEADS):
        qs_ref[h * qb:(h + 1) * qb, :] = jnp.where(lane // B_LATENT == h, q, 0.0).astype(BF16)
    iw_t = (mk_ref[qrows, :] * (IDX_HEADS ** -0.5)).T[iw_lane:iw_lane + IDX_HEADS, :]
    krow = _iota((sb, qb), 0)
    qcol = _iota((sb, qb), 1)
    nsb = ((i + 1) * qb + sb - 1) // sb

    def fold(x, op):
        parts = [x[r:r + 8] for r in range(0, sb, 8)]
        while len(parts) > 1:
            parts = [op(parts[k], parts[k + 1]) for k in range(0, len(parts), 2)]
        return parts[0]

    def for_steps(body, init):
        def pair(p, carry):
            return body(2 * p + 1, body(2 * p, carry))

        carry = lax.fori_loop(0, nsb // 2, pair, init)
        return lax.cond(nsb % 2 == 1, lambda c: body(nsb - 1, c), lambda c: c, carry)

    def score_body(j, carry):
        kr = pl.ds(pl.multiple_of(j * sb, sb), sb)
        rel = _dotg(ikb_ref[kr, :], iqs_ref[...], NT)
        sc = jnp.maximum(rel[:, 0:qb], 0.0) * iw_t[0:1, :]
        for h in range(1, IDX_HEADS):
            sc = sc + jnp.maximum(rel[:, h * qb:(h + 1) * qb], 0.0) * iw_t[h:h + 1, :]
        sc = jnp.where(krow <= qcol + (i * qb - j * sb), sc, -jnp.inf)
        sc_ref[j] = sc
        sc16_ref[j] = pltpu.bitcast(pltpu.bitcast(sc, I32) & jnp.int32(-65536), F32).astype(BF16)
        logits = _dotg(ckvn_ref[kr, :], qs_ref[...], NT)
        d0 = i * (qb // bb) - j * (sb // bb)
        for h in range(B_HEADS):
            for a in range(sb // bb):
                for b in range(qb // bb):
                    bias = bias_ref[jnp.maximum(d0 + b - a, 0), h]
                    lg_ref[h, j, a * bb:(a + 1) * bb, b * bb:(b + 1) * bb] = (
                        logits[a * bb:(a + 1) * bb, h * qb + b * bb:h * qb + (b + 1) * bb] + bias)
        return carry

    for_steps(score_body, 0)

    sign = jnp.int32(-2 ** 31)

    def key_to_float(u):
        k = u ^ sign
        bits = k ^ ((k >> 31) & jnp.int32(0x7FFFFFFF))
        return pltpu.bitcast(bits, F32)

    def count(pred):
        def cnt(j, acc):
            return acc + fold(pred(j), jnp.add)
        acc = for_steps(cnt, jnp.zeros((8, qb), F32))
        return jnp.sum(acc, 0, keepdims=True)

    pk = 16

    def count_ge16(cf):
        cf = pltpu.bitcast(pltpu.bitcast(cf, I32) & jnp.int32(-65536), F32)
        c16 = jnp.broadcast_to(cf, (pk, qb)).astype(BF16)
        one, zero = jnp.ones((pk, qb), BF16), jnp.zeros((pk, qb), BF16)

        def cnt(j, acc):
            x = sc16_ref[j]
            parts = [jnp.where(x[r:r + pk] >= c16, one, zero) for r in range(0, sb, pk)]
            while len(parts) > 1:
                parts = [parts[k] + parts[k + 1] for k in range(0, len(parts), 2)]
            return acc + parts[0].astype(F32)

        acc = for_steps(cnt, jnp.zeros((pk, qb), F32))
        return jnp.sum(acc, 0, keepdims=True)

    def bit_body(t, state, packed=False):
        u, n_u = state
        cand = u | jnp.left_shift(jnp.int32(1), 31 - t)
        cf = key_to_float(cand)
        c = count_ge16(cf) if packed else count(lambda j: jnp.where(sc_ref[j] >= cf, 1.0, 0.0))
        take = c >= keep
        return jnp.where(take, cand, u), jnp.where(take, c, n_u)

    def bit_body16(t, state):
        return bit_body(t, state, packed=True)

    first_bits, bit_group = 24, 4
    state = (jnp.zeros((1, qb), I32), (jnp.zeros((1, qb), I32) + nsb * sb).astype(F32))
    state = bit_body16(0, state)
    n_pos = count(lambda j: jnp.where(sc_ref[j] > 0.0, 1.0, 0.0))
    zero_thr = jnp.logical_and(state[0] != 0, n_pos < keep)
    state = lax.fori_loop(1, 16, bit_body16, state)
    u, n_ge = lax.fori_loop(16, first_bits, bit_body, state)

    def more_bits(s):
        t, _, n_u = s
        open_rows = jnp.where(zero_thr, 0.0, jnp.abs(n_u - keep))
        return jnp.logical_and(t < 32, jnp.max(open_rows) > 0.0)

    def bit_group_body(s):
        t, u, n_u = s
        for k in range(bit_group):
            u, n_u = bit_body(t + k, (u, n_u))
        return t + bit_group, u, n_u

    _, u, n_ge = lax.while_loop(more_bits, bit_group_body, (jnp.int32(first_bits), u, n_ge))
    u = jnp.maximum(u ^ sign, jnp.int32(0x007FFFFF - 2 ** 31)) ^ sign
    thr = key_to_float(u)
    n_gt = count(lambda j: jnp.where(sc_ref[j] > thr, 1.0, 0.0))
    need = keep - n_gt

    nbits = int(math.ceil(math.log2(seq))) + 1

    def cut_search():
        def cut_body(t, cut):
            cand = cut | jnp.left_shift(jnp.int32(1), nbits - 1 - t)
            c = count(lambda j: jnp.where(sc_ref[j] == thr,
                                          jnp.where(krow < cand - j * sb, 1.0, 0.0), 0.0))
            return jnp.where(c <= need, cand, cut)

        return lax.fori_loop(0, nbits, cut_body, jnp.zeros((1, qb), I32))

    surplus = jnp.max(n_ge) > keep
    cut = lax.cond(surplus, cut_search, lambda: jnp.full((1, qb), 2 ** nbits - 1, I32))
    cut = jnp.minimum(cut, i * qb + 1 + _iota((1, qb), 1))

    def mask_body(j, mx):
        sc = sc_ref[j]
        sel = jnp.where(sc > thr, 1.0, jnp.where(sc == thr, jnp.where(krow < cut - j * sb, 1.0, 0.0), 0.0))
        out = []
        for h in range(B_HEADS):
            s = jnp.where(sel > 0.5, lg_ref[h, j], NEG)
            lg_ref[h, j] = s
            out.append(jnp.maximum(mx[h], fold(s, jnp.maximum)))
        return tuple(out)

    mx = for_steps(mask_body, tuple(jnp.full((8, qb), NEG, F32) for _ in range(B_HEADS)))
    ms = [jnp.max(m, 0, keepdims=True) for m in mx]

    def att_body(j, carry):
        ls, acc = carry
        ps, new_ls = [], []
        for h in range(B_HEADS):
            pr = jnp.exp(lg_ref[h, j] - ms[h])
            new_ls.append(ls[h] + fold(pr, jnp.add))
            ps.append(pr.astype(BF16))
        upd = _dot(ckvt_ref[j], jnp.concatenate(ps, 1))
        return tuple(new_ls), acc + upd

    init = (tuple(jnp.zeros((8, qb), F32) for _ in range(B_HEADS)),
            jnp.zeros((B_LATENT, B_HEADS * qb), F32))
    ls, acc = for_steps(att_body, init)
    ls = [jnp.sum(l, 0, keepdims=True) for l in ls]
    o_t = (acc / jnp.concatenate(ls, 1)).astype(BF16)
    y_t = jnp.concatenate([_dot(wuvt_ref[h], o_t[:, h * qb:(h + 1) * qb]) for h in range(B_HEADS)], 0)
    o_ref[qrows, :] = y_t.T


def _mixer_b(h, gain4, wuv_t, bias, bsz, seq):
    qb, sb = B_QUERIES, B_KEYS
    assert seq % qb == 0 and seq % sb == 0 and qb % Q_BLOCK == 0 and sb % Q_BLOCK == 0
    nstep = seq // sb
    keep = min(TOPK_MAX, seq // TOPK_DIV)

    def kpiece(name):
        off, w = _PIECE_OFF[name]
        return pl.BlockSpec((seq, w * LANES), lambda b, o=off // w: (b, o))

    return pl.pallas_call(
        functools.partial(_mixer_b_kernel, seq=seq, keep=keep),
        grid=(bsz,),
        in_specs=[kpiece("b_q"), kpiece("b_iq"), kpiece("b_misc"),
                  pl.BlockSpec(gain4.shape, lambda b: (0, 0)),
                  pl.BlockSpec(wuv_t.shape, lambda b: (0, 0, 0)),
                  pl.BlockSpec(bias.shape, lambda b: (0, 0, 0, 0))],
        out_specs=pl.BlockSpec((seq, BRANCH_WIDTH), lambda b: (b, 0)),
        out_shape=jax.ShapeDtypeStruct((bsz * seq, BRANCH_WIDTH), F32),
        scratch_shapes=[pltpu.VMEM((seq, BRANCH_WIDTH), BF16),
                        pltpu.VMEM((nstep, B_LATENT, sb), BF16),
                        pltpu.VMEM((seq, BRANCH_WIDTH), BF16),
                        pltpu.VMEM((nstep, sb, qb), F32),
                        pltpu.VMEM((nstep, sb, qb), BF16),
                        pltpu.VMEM((B_HEADS, nstep, sb, qb), F32),
                        pltpu.VMEM((IDX_HEADS * qb, BRANCH_WIDTH), BF16),
                        pltpu.VMEM((B_HEADS * qb, BRANCH_WIDTH), BF16)],
        compiler_params=_params("parallel"),
        name="mixer_b",
    )(h, h, h, gain4, wuv_t, bias)


def _mixer_c_kernel(q_ref, k_ref, v_ref, g_ref, a_ref, aup_ref, abias_ref, gain_ref, o_ref, *, seq):
    ch = C_CHUNK
    kw = C_HEADS * C_KEY_DIM
    vw = C_HEADS * C_VAL_DIM
    tri = jnp.where(_iota((ch, ch), 1) <= _iota((ch, ch), 0), 1.0, 0.0).astype(BF16)
    causal = _iota((C_HEADS * ch, ch), 1) <= (_iota((C_HEADS * ch, ch), 0) % ch)
    klane_head = _iota((C_HEADS * ch, kw), 1) // C_KEY_DIM
    krow_head = _iota((C_HEADS * ch, kw), 0) // ch
    vlane_head = _iota((ch, vw), 1) // C_VAL_DIM
    st_mask = (_iota((vw, kw), 0) // C_VAL_DIM) == (_iota((vw, kw), 1) // C_KEY_DIM)
    gmat = _group_matrix(vw, C_VAL_DIM, 1.0 / C_VAL_DIM)
    aup = aup_ref[...]

    def body(c, st):
        rows = pl.ds(pl.multiple_of(c * ch, ch), ch)
        qc = q_ref[rows, :] * (C_KEY_DIM ** -0.5)
        kc = k_ref[rows, :]
        vc = v_ref[rows, :]
        z = _dot3(a_ref[rows, :], aup) + abias_ref[...]
        log_a = -_softplus(-z) / C_GATE_TAU
        cum = _dot_lhs_exact(tri, log_a)
        last = cum[ch - 1:ch, :]
        q_dec = qc * jnp.exp(cum)
        k_inv = (kc * jnp.exp(-cum)).astype(BF16)
        k_dec = (kc * jnp.exp(last - cum)).astype(BF16)
        vb = vc.astype(BF16)
        qd_b = q_dec.astype(BF16)
        q_stack = jnp.where(klane_head == krow_head, jnp.concatenate([q_dec] * C_HEADS, 0), 0.0)
        att = jnp.where(causal, _dotg(q_stack.astype(BF16), k_inv, NT), 0.0)
        full = _dot(att.astype(BF16), vb)
        o = _dotg(qd_b, st.astype(BF16), NT)
        for h in range(C_HEADS):
            o = o + jnp.where(vlane_head == h, full[h * ch:(h + 1) * ch], 0.0)
        upd = _dotg(vb, k_dec, TN)
        st = st * jnp.exp(last) + jnp.where(st_mask, upd, 0.0)
        ms = _dot_rhs_exact(o * o, gmat)
        o = o * lax.rsqrt(ms + 1e-6) * gain_ref[...]
        g = g_ref[rows, :]
        o_ref[rows, :] = g * _sigmoid(g) * o
        return st

    def body_group(t, st):
        for k in range(C_UNROLL):
            st = body(C_UNROLL * t + k, st)
        return st

    lax.fori_loop(0, seq // (C_UNROLL * ch), body_group, jnp.zeros((vw, kw), F32))


def _mixer_c(h, aup_pad, abias, gain, bsz, seq):
    assert seq % C_CHUNK == 0

    def piece(name):
        off, w = _PIECE_OFF[name]
        return pl.BlockSpec((seq, w * LANES), lambda b, o=off // w: (b, o))

    def const(a):
        return pl.BlockSpec(a.shape, lambda b: (0, 0))

    return pl.pallas_call(
        functools.partial(_mixer_c_kernel, seq=seq),
        grid=(bsz,),
        in_specs=[piece("c_q"), piece("c_k"), piece("c_v"), piece("c_g"), piece("c_a"),
                  const(aup_pad), const(abias), const(gain)],
        out_specs=pl.BlockSpec((seq, BRANCH_WIDTH), lambda b: (b, 0)),
        out_shape=jax.ShapeDtypeStruct((bsz * seq, BRANCH_WIDTH), F32),
        compiler_params=_params("parallel"),
        name="mixer_c",
    )(h, h, h, h, h, aup_pad, abias, gain)


def _mixer_d_kernel(r_ref, k_ref, v_ref, low_ref, mu_ref, mulow_ref, w2_ref, a2_ref, g2_ref, vec_ref,
                    o_ref, *, seq, nsub):
    ch = D_CHUNK
    bw = BRANCH_WIDTH
    nh = D_HEADS
    hs = D_HEAD_SIZE
    r_i = _iota((bw, bw), 0)
    c_i = _iota((bw, bw), 1)
    same = (r_i // hs) == (c_i // hs)
    strict = c_i < r_i
    incl = c_i <= r_i
    eye = r_i == c_i
    ones_bd = _group_matrix(bw, hs, 1.0)
    avg_bd = _group_matrix(bw, hs, 1.0 / hs)
    rows = nsub * ch
    tr_i = _iota((rows, rows), 0)
    tc_i = _iota((rows, rows), 1)
    tri = jnp.where(jnp.logical_and(tr_i // ch == tc_i // ch, tc_i <= tr_i), 1.0, 0.0).astype(BF16)
    row0 = _iota((rows, bw), 0) == 0
    row0_low = _iota((rows, LANES), 0) == 0
    vec = vec_ref[...]
    w0, a0, k_k, k_a, r_k, gn_w, gn_b = (vec[n:n + 1, :] for n in range(7))
    mu = mu_ref[...]
    w2, a2, g2 = w2_ref[...], a2_ref[...], g2_ref[...]

    def wide(x):
        return jnp.where(same, jnp.concatenate([x] * nh, 0), 0.0)

    def shifted(ref, start, first_row, m):
        cur = ref[pl.ds(start, rows), :]
        last8 = ref[pl.ds(pl.multiple_of(jnp.maximum(start - 8, 0), 8), 8), :]
        prev_row = last8[7:8, :] * jnp.where(start > 0, 1.0, 0.0)
        prev = jnp.where(first_row, prev_row, pltpu.roll(cur, 1, 0))
        return cur + (prev - cur) * m

    def body(c, st):
        start = pl.multiple_of(c * rows, rows)
        r = shifted(r_ref, start, row0, mu[0:1, :])
        k = shifted(k_ref, start, row0, mu[1:2, :])
        v = shifted(v_ref, start, row0, mu[2:3, :])
        low = shifted(low_ref, start, row0_low, mulow_ref[...])
        w_raw = -_softplus(-(w0 + _dot3(jnp.tanh(low), w2))) - 0.5
        lw = -jnp.exp(w_raw)
        a = _sigmoid(a0 + _dot3(low, a2))
        g = _dot3(_sigmoid(low), g2)
        kk = k * k_k
        kk = kk / jnp.maximum(jnp.sqrt(_dot_rhs_exact(kk * kk, ones_bd)), 1e-12)
        k2 = k * (1.0 + (a - 1.0) * k_a)
        bonus = _dot_rhs_exact(r * k2 * r_k, ones_bd) * v
        b = kk * a

        cum = _dot_lhs_exact(tri, lw)
        e_in = jnp.exp(cum)
        e_inv = jnp.exp(-cum)
        a_dec = -kk * jnp.exp(cum - lw)
        r_dec = r * e_in
        b_inv = b * e_inv
        k_inv = k2 * e_inv
        local = [chunk_terms(*(z[s * ch:(s + 1) * ch] for z in (a_dec, r_dec, b_inv, k_inv, v, e_in)))
                 for s in range(nsub)]
        ys = []
        for atp, vp, xr, m_rb, y_kv, bh, h_kv, g_col in local:
            st16 = st.astype(BF16)
            u16 = (_dot(atp, st16) + vp).astype(BF16)
            y_w = _dot(xr, st16) + _dot(m_rb, u16) + y_kv
            st = g_col * st + _dotg(bh, u16, TN) + h_kv
            y = y_w[0:ch]
            for h in range(1, nh):
                y = y + y_w[h * ch:(h + 1) * ch]
            ys.append(y)
        y = jnp.concatenate(ys, 0)
        mean = _dot_rhs_exact(y, avg_bd)
        yc = y - mean
        var = _dot_rhs_exact(yc * yc, avg_bd)
        yn = yc * lax.rsqrt(var + D_GN_EPS) * gn_w + gn_b
        o_ref[pl.ds(start, rows), :] = (yn + bonus) * g
        return st

    def chunk_terms(a_dec, r_dec, b_inv, k_inv, v, e_in):
        g_last = e_in[ch - 1:ch, :]
        xa = wide(a_dec).astype(BF16)
        xr = wide(r_dec).astype(BF16)
        yb = wide(b_inv).astype(BF16)
        yk = wide(k_inv).astype(BF16)
        bh = wide(b_inv * g_last).astype(BF16)
        kh = wide(k_inv * g_last).astype(BF16)
        vw = wide(v).astype(BF16)

        a_ab = jnp.where(strict, _dotg(xa, yb, NT), 0.0)
        a_ak = jnp.where(strict, _dotg(xa, yk, NT), 0.0).astype(BF16)
        m_rb = jnp.where(incl, _dotg(xr, yb, NT), 0.0).astype(BF16)
        m_rk = jnp.where(incl, _dotg(xr, yk, NT), 0.0).astype(BF16)

        t_inv = jnp.where(eye, 1.0, a_ab)
        apow = a_ab
        for _ in range(int(math.log2(ch)) - 1):
            ab16 = apow.astype(BF16)
            apow = _dot(ab16, ab16)
            t_inv = t_inv + _dot(t_inv.astype(BF16), apow.astype(BF16))
        t16 = t_inv.astype(BF16)

        atp = _dot(t16, xa).astype(BF16)
        vp = _dot(t16, _dot(a_ak, vw).astype(BF16))
        y_kv = _dot(m_rk, vw)
        h_kv = _dotg(kh, vw, TN)
        g_col = jnp.sum(jnp.where(eye, g_last, 0.0), -1, keepdims=True)
        return atp, vp, xr, m_rb, y_kv, bh, h_kv, g_col

    lax.fori_loop(0, seq // rows, body, jnp.zeros((bw, bw), F32))


def _mixer_d(h, mu3, mulow, w2p, a2p, g2p, vec, bsz, seq, nsub=4):
    assert seq % (nsub * D_CHUNK) == 0 and D_CHUNK == D_HEAD_SIZE

    def piece(name):
        off, w = _PIECE_OFF[name]
        return pl.BlockSpec((seq, w * LANES), lambda b, o=off // w: (b, o))

    def const(a):
        return pl.BlockSpec(a.shape, lambda b: (0, 0))

    return pl.pallas_call(
        functools.partial(_mixer_d_kernel, seq=seq, nsub=nsub),
        grid=(bsz,),
        in_specs=[piece("d_r"), piece("d_k"), piece("d_v"), piece("d_low"),
                  const(mu3), const(mulow), const(w2p), const(a2p), const(g2p), const(vec)],
        out_specs=pl.BlockSpec((seq, BRANCH_WIDTH), lambda b: (b, 0)),
        out_shape=jax.ShapeDtypeStruct((bsz * seq, BRANCH_WIDTH), F32),
        compiler_params=_params("parallel"),
        name="mixer_d",
    )(h, h, h, h, mu3, mulow, w2p, a2p, g2p, vec)


def _merge_kernel(x_ref, ya_ref, yb_ref, yc_ref, yd_ref, wg_ref, wb_ref, wo_ref, g_ref, b_ref, o_ref,
                  *, alpha):
    x = x_ref[...]
    xb = x.astype(BF16)
    merged = None
    for n, y_ref in enumerate((ya_ref, yb_ref, yc_ref, yd_ref)):
        gate = _sigmoid(_dot(xb, wg_ref[:, n * D_MODEL:(n + 1) * D_MODEL]))
        term = gate * _dot(y_ref[...].astype(BF16), wb_ref[n])
        merged = term if merged is None else merged + term
    z = alpha * x + _dot(merged.astype(BF16), wo_ref[...])
    o_ref[...] = _layer_norm(z, g_ref[...], b_ref[...])


def _merge(x2d, ys, wg, wb, wo, g, b, alpha, tm):
    n, d = x2d.shape

    def const(a):
        nd = a.ndim
        return pl.BlockSpec(a.shape, lambda i: (0,) * nd, pipeline_mode=pl.Buffered(1))

    yspec = pl.BlockSpec((tm, BRANCH_WIDTH), lambda i: (i, 0))
    return pl.pallas_call(
        functools.partial(_merge_kernel, alpha=alpha),
        grid=(n // tm,),
        in_specs=[pl.BlockSpec((tm, d), lambda i: (i, 0)), yspec, yspec, yspec, yspec,
                  const(wg), const(wb), const(wo), const(g), const(b)],
        out_specs=pl.BlockSpec((tm, d), lambda i: (i, 0)),
        out_shape=jax.ShapeDtypeStruct((n, d), F32),
        compiler_params=_params("parallel"),
        name="merge",
    )(x2d, *ys, wg, wb, wo, g, b)


def _moe_kernel(x_ref, ltri_ref, wr_ref, br_ref, wg_ref, wu_ref, wd_ref, g_ref, b_ref, o_ref,
                xs_ref, gs_ref, ys_ref, pt_ref, seg_ref, *, alpha):
    e = pl.program_id(1)
    tm = x_ref.shape[0]
    nrow = xs_ref.shape[0]
    ck = MOE_CHUNK
    lane = _iota((tm, LANES), 1)

    @pl.when(e == 0)
    def _():
        x = x_ref[...]
        xb = x.astype(BF16)
        ys_ref[...] = jnp.zeros_like(ys_ref)
        x_lo = (x - xb.astype(F32)).astype(BF16)
        cross = _dot(jnp.concatenate([xb, x_lo], 0), wr_ref[...])
        logits = (cross[0:tm, 0:LANES] + (cross[0:tm, LANES:] + cross[tm:, 0:LANES])) + br_ref[...]
        is_g = jnp.logical_and(lane >= MOE_EXPERTS, lane < MOE_EXPERTS + MOE_GROUPS)
        lg = jnp.where(is_g, logits, -jnp.inf)
        gmax = jnp.max(lg, -1, keepdims=True)
        ptop = 1.0 / jnp.sum(jnp.exp(lg - gmax), -1, keepdims=True)
        gsel = jnp.min(jnp.where(lg == gmax, lane, 2 * LANES), -1, keepdims=True) - MOE_EXPERTS
        in_group = jnp.logical_and(lane < MOE_EXPERTS, lane // MOE_PER_GROUP == gsel)
        le = jnp.where(in_group, logits, -jnp.inf)
        v1 = jnp.max(le, -1, keepdims=True)
        i1 = jnp.min(jnp.where(le == v1, lane, 2 * LANES), -1, keepdims=True)
        le2 = jnp.where(lane == i1, -jnp.inf, le)
        v2 = jnp.max(le2, -1, keepdims=True)
        i2 = jnp.min(jnp.where(le2 == v2, lane, 2 * LANES), -1, keepdims=True)
        e2 = jnp.exp(v2 - v1)
        w1 = ptop / (1.0 + e2)
        gate = jnp.where(lane == i1, w1, jnp.where(lane == i2, w1 * e2, 0.0))

        onehot = jnp.where(lane == gsel, 1.0, 0.0)
        rank = _dot(ltri_ref[...], onehot.astype(BF16))
        cnt = jnp.sum(onehot, 0, keepdims=True).astype(I32)
        nch = jnp.right_shift(cnt + (ck - 1), int(math.log2(ck)))
        lane1 = _iota((1, LANES), 1)
        off = jnp.int32(0)
        off_vec = jnp.zeros((1, LANES), I32)
        for grp in range(MOE_GROUPS):
            n_g = nch[0, grp]
            seg_ref[grp] = off
            seg_ref[MOE_GROUPS + grp] = n_g
            off_vec = jnp.where(lane1 == grp, off, off_vec)
            off = off + n_g * ck
        dest = jnp.sum(onehot * (rank + off_vec.astype(F32)), -1, keepdims=True)
        pt_ref[...] = jnp.where(_iota((tm, nrow), 1) == dest.astype(I32), 1.0, 0.0).astype(BF16)
        dest_row = jnp.broadcast_to(dest, (tm, LANES)).T[0:1, :].astype(I32)
        perm = jnp.where(_iota((nrow, tm), 0) == dest_row, 1.0, 0.0).astype(BF16)
        gate_hi, gate_lo = _split2(gate)
        moved = _dot(perm, jnp.concatenate([xb, gate_hi, gate_lo], 1))
        d = x_ref.shape[1]
        xs_ref[...] = moved[:, 0:d].astype(BF16)
        gs_ref[...] = moved[:, d:d + LANES] + moved[:, d + LANES:]

    nse = MOE_STEP_EXPERTS
    grp = e // (MOE_PER_GROUP // nse)
    seg_off = seg_ref[grp]
    seg_chunks = seg_ref[MOE_GROUPS + grp]

    def run_experts(start, size):
        rows = pl.ds(pl.multiple_of(start, ck), size)
        xc = xs_ref[rows, :]
        gs = gs_ref[rows, :]
        lane_s = _iota((size, LANES), 1)
        hidden = []
        for k in range(nse):
            gk = jnp.sum(jnp.where(lane_s == nse * e + k, gs, 0.0), -1, keepdims=True)
            hg = _dot(xc, wg_ref[k])
            hu = _dot(xc, wu_ref[k])
            hidden.append((hg * _sigmoid(hg) * hu * gk).astype(BF16))
        wd = wd_ref[...].reshape(nse * MOE_HIDDEN, wd_ref.shape[2])
        ys_ref[rows, :] += _dot(jnp.concatenate(hidden, 1), wd)

    big = 4

    def quad(c, carry):
        run_experts(seg_off + c * (big * ck), big * ck)
        return carry

    lax.fori_loop(0, seg_chunks // big, quad, 0)
    rem = seg_chunks % big
    rem_start = seg_off + (seg_chunks - rem) * ck
    for r in range(1, big):
        @pl.when(rem == r)
        def _(r=r):
            run_experts(rem_start, r * ck)

    @pl.when(e == pl.num_programs(1) - 1)
    def _():
        z = alpha * x_ref[...] + _dot(pt_ref[...], ys_ref[...].astype(BF16))
        o_ref[...] = _layer_norm(z, g_ref[...], b_ref[...])


def _moe(x2d, wr, br, wg, wu, wd, g, b, alpha, tm):
    n, d = x2d.shape
    ne = wg.shape[0]
    hid = wg.shape[2]
    nse = MOE_STEP_EXPERTS
    assert MOE_PER_GROUP % nse == 0 and ne % nse == 0
    nrow = tm + MOE_GROUPS * MOE_CHUNK
    ltri = jnp.tril(jnp.ones((tm, tm), BF16), -1)

    def const(a):
        return pl.BlockSpec(a.shape, lambda i, e: (0, 0), pipeline_mode=pl.Buffered(1))

    return pl.pallas_call(
        functools.partial(_moe_kernel, alpha=alpha),
        grid=(n // tm, ne // nse),
        in_specs=[pl.BlockSpec((tm, d), lambda i, e: (i, 0)), const(ltri), const(wr), const(br),
                  pl.BlockSpec((nse, d, hid), lambda i, e: (e, 0, 0)),
                  pl.BlockSpec((nse, d, hid), lambda i, e: (e, 0, 0)),
                  pl.BlockSpec((nse, hid, d), lambda i, e: (e, 0, 0)),
                  const(g), const(b)],
        out_specs=pl.BlockSpec((tm, d), lambda i, e: (i, 0)),
        out_shape=jax.ShapeDtypeStruct((n, d), F32),
        scratch_shapes=[pltpu.VMEM((nrow, d), BF16), pltpu.VMEM((nrow, LANES), F32),
                        pltpu.VMEM((nrow, d), F32), pltpu.VMEM((tm, nrow), BF16),
                        pltpu.SMEM((2 * MOE_GROUPS,), I32)],
        compiler_params=_params("parallel", "arbitrary"),
        name="moe",
    )(x2d, ltri, wr, br, wg, wu, wd, g, b)


def _t5_bucket(dist):
    exact = RPB_BUCKETS // 2
    d = jnp.maximum(dist, 0)
    df = jnp.maximum(d, 1).astype(F32)
    large = exact + (jnp.log(df / exact) / math.log(RPB_MAX_DIST / exact)
                     * (RPB_BUCKETS - exact)).astype(I32)
    large = jnp.minimum(large, RPB_BUCKETS - 1)
    return jnp.where(d < exact, d, large)


def _rpb_lookup(tab, dist):
    bucket = _t5_bucket(dist)[..., None]
    out = jnp.zeros(dist.shape + (tab.shape[1],), F32)
    for k in range(RPB_BUCKETS):
        out = jnp.where(bucket == k, tab[k], out)
    return out


def _bias_a(tab):
    qi = jnp.arange(A_BLOCK)[:, None]
    kj = jnp.arange(2 * A_BLOCK)[None, :]
    rdist = qi + A_BLOCK - kj
    out = []
    for window, dilation in A_PATTERNS:
        in_band = (rdist >= 0) & (rdist <= window // dilation)
        bias = jnp.transpose(_rpb_lookup(tab, rdist * dilation), (2, 0, 1))
        out.append(jnp.where(in_band[None], bias, NEG))
    return jnp.stack(out).astype(F32)


def _bias_b(tab, nblk):
    qi = jnp.arange(Q_BLOCK)[:, None]
    kj = jnp.arange(Q_BLOCK)[None, :]
    delta = jnp.arange(nblk)[:, None, None] * Q_BLOCK
    bias = _rpb_lookup(tab, delta + qi - kj)
    return jnp.transpose(bias, (0, 3, 2, 1)).astype(F32)


def _in_offsets():
    splits = (("a_q", 256), ("a_k", 256), ("a_v", 256), ("b_q", 256), ("b_ckv", 64), ("b_iq", 256),
              ("b_ik", 32), ("b_iw", 8), ("c_q", 128), ("c_k", 128), ("c_v", 256), ("c_a", 16),
              ("c_g", 256), ("d", 832), ("gate", 4096))
    off, out = 0, {}
    for name, w in splits:
        out[name] = (off, w)
        off += w
    return out


def _proj_columns():
    src = _in_offsets()
    d0 = src["d"][0]
    named = {
        "a_q": np.arange(256) + src["a_q"][0], "a_k": np.arange(256) + src["a_k"][0],
        "a_v": np.arange(256) + src["a_v"][0], "b_q": np.arange(256) + src["b_q"][0],
        "b_iq": np.arange(256) + src["b_iq"][0],
        "b_misc": np.concatenate([np.arange(B_LATENT) + src["b_ckv"][0], np.arange(IDX_DIM) + src["b_ik"][0],
                                  np.arange(IDX_HEADS) + src["b_iw"][0]]),
        "c_v": np.arange(256) + src["c_v"][0], "c_g": np.arange(256) + src["c_g"][0],
        "d_r": np.arange(256) + d0, "d_k": np.arange(256) + d0 + 256, "d_v": np.arange(256) + d0 + 512,
        "c_q": np.arange(128) + src["c_q"][0], "c_k": np.arange(128) + src["c_k"][0],
        "c_a": np.arange(16) + src["c_a"][0], "d_low": np.arange(64) + d0 + 768,
    }
    cols = []
    for name, w in _PIECES:
        c = named[name]
        cols.append(np.concatenate([c, -np.ones(w * LANES - len(c), np.int64)]))
    return np.concatenate(cols)


def _proj_weight(w):
    cols = _proj_columns()
    cuts = [0] + [k for k in range(1, len(cols))
                  if (cols[k] < 0) != (cols[k - 1] < 0) or (cols[k] >= 0 and cols[k] != cols[k - 1] + 1)]
    cuts.append(len(cols))
    parts = []
    for a, b in zip(cuts[:-1], cuts[1:]):
        if cols[a] < 0:
            parts.append(jnp.zeros((w.shape[0], b - a), BF16))
        else:
            parts.append(w[:, int(cols[a]):int(cols[a]) + (b - a)].astype(BF16))
    return jnp.concatenate(parts, 1)


def _pad_rows(w, first, total):
    return jnp.zeros((total, w.shape[1]), w.dtype).at[first:first + w.shape[0]].set(w)


def kernel(x, rpb_table, w_in, b_kv_gain, b_w_uv, c_a_up, c_a_bias, c_norm_gain, d_mu, d_w0, d_w2,
           d_a0, d_a2, d_g2, d_k_k, d_k_a, d_r_k, d_gn_w, d_gn_b, w_branch, w_out, ln_g, ln_b,
           router_g, router_g_bias, router_e, router_e_bias, moe_w_gate, moe_w_up, moe_w_down):
    bsz, seq, d_model = x.shape
    depth = w_in.shape[0]
    n = bsz * seq
    alpha = (2 * depth) ** 0.25
    gate_off = _in_offsets()["gate"][0]
    bias_a = _bias_a(rpb_table[:, :A_HEADS])
    bias_b = _bias_b(rpb_table[:, A_HEADS:], seq // Q_BLOCK)
    tm_proj = math.gcd(n, 512)
    tm_merge = math.gcd(n, 1024)
    tm_moe = math.gcd(n, 1024)

    x2d = x.reshape(n, d_model)
    for l in range(depth):
        h = _project(x2d, _proj_weight(w_in[l]), tm_proj, PROJ_WIDTH)

        ya = _mixer_a(h, bias_a, bsz, seq)

        gain4 = jnp.tile(b_kv_gain[l], B_HEADS)[None, :]
        wuv_t = jnp.transpose(b_w_uv[l], (0, 2, 1)).astype(BF16)
        yb = _mixer_b(h, gain4, wuv_t, bias_b, bsz, seq)

        aup_pad = _pad_rows(c_a_up[l], 0, LANES)
        yc = _mixer_c(h, aup_pad, c_a_bias[l][None, :], c_norm_gain[l][None, :], bsz, seq)

        mu = d_mu[l]
        mu3 = mu[:3 * BRANCH_WIDTH].reshape(3, BRANCH_WIDTH)
        mulow = jnp.zeros((1, LANES), F32).at[0, :D_DECAY_RANK + D_ICLR_RANK + D_GATE_RANK].set(
            mu[3 * BRANCH_WIDTH:])
        w2p = _pad_rows(d_w2[l], 0, LANES)
        a2p = _pad_rows(d_a2[l], D_DECAY_RANK, LANES)
        g2p = _pad_rows(d_g2[l], D_DECAY_RANK + D_ICLR_RANK, LANES)
        vec = jnp.stack([d_w0[l], d_a0[l], d_k_k[l], d_k_a[l], d_r_k[l], d_gn_w[l], d_gn_b[l],
                         jnp.zeros_like(d_w0[l])])
        yd = _mixer_d(h, mu3, mulow, w2p, a2p, g2p, vec, bsz, seq)

        wg = w_in[l][:, gate_off:gate_off + N_BRANCHES * d_model].astype(BF16)
        x2d = _merge(x2d, (ya, yb, yc, yd), wg, w_branch[l].astype(BF16), w_out[l].astype(BF16),
                     ln_g[l, 0][None, :], ln_b[l, 0][None, :], alpha, tm_merge)

        wr = jnp.zeros((d_model, LANES), F32)
        wr = wr.at[:, :MOE_EXPERTS].set(router_e[l]).at[:, MOE_EXPERTS:MOE_EXPERTS + MOE_GROUPS].set(
            router_g[l])
        br = jnp.zeros((1, LANES), F32)
        br = br.at[0, :MOE_EXPERTS].set(router_e_bias[l]).at[0, MOE_EXPERTS:MOE_EXPERTS + MOE_GROUPS].set(
            router_g_bias[l])
        wr = jnp.concatenate(_split2(wr), 1)
        x2d = _moe(x2d, wr, br, moe_w_gate[l].astype(BF16), moe_w_up[l].astype(BF16),
                   moe_w_down[l].astype(BF16), ln_g[l, 1][None, :], ln_b[l, 1][None, :], alpha, tm_moe)
    return x2d.reshape(bsz, seq, d_model)
```
